```python
import jax, jax.numpy as jnp
from jax import lax
import numpy as np

D_MODEL = 1024
BATCH = 4
SEQ = 8192
DEPTH = 1
DEC_BATCH = 16
DEC_SEQ = 16
PAST_LEN = 1024

CHUNK = 64
D_POOL = 512
POOL_WINDOWS = (2, 4, 8, 16)
N_POOL_GROUPS = len(POOL_WINDOWS)
POOL_GROUP = D_POOL // N_POOL_GROUPS
POOL_BUF = max(POOL_WINDOWS) - 1
N_HG_HEADS = 4
HG_KEY = 128
HG_VAL = 128
D_HG_K = N_HG_HEADS * HG_KEY
D_HG_V = N_HG_HEADS * HG_VAL
D_MIX = D_POOL + D_HG_V
D_IN = 2 * D_POOL + 2 * D_HG_K + 2 * D_HG_V
SCAN_BLK = CHUNK // 4
D_PLE = 256
EPS = 1e-6

kernel_name = "hybrid_pool_hgrn2_stream_step"


def rmsnorm(x, g):
    xf = x.astype(jnp.float32)
    y = xf * lax.rsqrt(jnp.mean(xf * xf, axis=-1, keepdims=True) + EPS)
    return (y * g.astype(jnp.float32)).astype(x.dtype)


def pool_mixer(u, buf, start_pos, w_grp, scale):
    b, t, _ = u.shape
    ext = jnp.concatenate([buf.astype(jnp.float32), u.astype(jnp.float32)], axis=1)
    cs = jnp.concatenate([jnp.zeros((b, 1, D_POOL), jnp.float32), jnp.cumsum(ext, axis=1)], axis=1)
    pos = start_pos + jnp.arange(t)
    outs = []
    for gi, w in enumerate(POOL_WINDOWS):
        sl = slice(gi * POOL_GROUP, (gi + 1) * POOL_GROUP)
        s = cs[:, POOL_BUF + 1:, sl] - cs[:, POOL_BUF + 1 - w:POOL_BUF + 1 - w + t, sl]
        cnt = jnp.minimum(w, pos + 1).astype(jnp.float32)[None, :, None]
        outs.append(s / cnt - ext[:, POOL_BUF:, sl])
    pooled = jnp.stack(outs, axis=2).astype(u.dtype)
    mixed = jnp.einsum('btgc,gcd->btgd', pooled, w_grp).reshape(b, t, D_POOL)
    new_buf = ext[:, -POOL_BUF:].astype(u.dtype)
    return mixed * scale, new_buf


def hgrn2(q, f_logit, v, lb, s0):
    f32 = jnp.float32
    b, t = q.shape[:2]
    q = q.astype(f32).reshape(b, t, N_HG_HEADS, HG_KEY)
    fl = f_logit.astype(f32).reshape(b, t, N_HG_HEADS, HG_KEY)
    v = v.astype(f32).reshape(b, t, N_HG_HEADS, HG_VAL)
    lbh = lb.reshape(N_HG_HEADS, HG_KEY)
    f = lbh + (1.0 - lbh) * jax.nn.sigmoid(fl)
    g = jnp.log(f)
    k = 1.0 - f
    n_blk = -(-t // SCAN_BLK)
    tp = n_blk * SCAN_BLK
    pad = ((0, 0), (0, tp - t), (0, 0), (0, 0))
    q, k, g, v = [jnp.pad(arr, pad).reshape(b, n_blk, SCAN_BLK, N_HG_HEADS, -1) for arr in (q, k, g, v)]
    bc = jnp.cumsum(g, axis=2)
    bl = bc[:, :, -1:]
    qe = q * jnp.exp(bc)
    ke = k * jnp.exp(-bc)
    kd = k * jnp.exp(bl - bc)
    mask = jnp.tril(jnp.ones((SCAN_BLK, SCAN_BLK), bool))
    att = jnp.where(mask, jnp.einsum('bnthk,bnshk->bnhts', qe, ke), 0.0)
    o_intra = jnp.einsum('bnhts,bnshv->bnthv', att, v)
    decay = jnp.exp(bl[:, :, 0])

    def step(s, xs):
        qe_n, kd_n, v_n, dec_n = xs
        o_n = jnp.einsum('bthk,bhkv->bthv', qe_n, s)
        s = dec_n[..., None] * s + jnp.einsum('bthk,bthv->bhkv', kd_n, v_n)
        return s, o_n

    xs = tuple(jnp.moveaxis(arr, 1, 0) for arr in (qe, kd, v, decay))
    s_fin, o_inter = lax.scan(step, s0.astype(f32), xs)
    o = o_intra + jnp.moveaxis(o_inter, 0, 1)
    o = o.reshape(b, tp, N_HG_HEADS, HG_VAL)[:, :t]
    return o, s_fin


def layer(x, p, pool_buf, s0, start_pos, lb, w_in, w_pool, pool_scale, hg_norm,
          w_out, norm_pre, norm_post, w_ple, w_ple_gate):
    b, t, _ = x.shape
    h = rmsnorm(x, norm_pre)
    z = h @ w_in
    u, g_pool, q, f_logit, v_in, g_hg = jnp.split(
        z, [D_POOL, 2 * D_POOL, 2 * D_POOL + D_HG_K, 2 * D_POOL + 2 * D_HG_K,
            2 * D_POOL + 2 * D_HG_K + D_HG_V], axis=-1)
    pool_out, new_buf = pool_mixer(u, pool_buf, start_pos, w_pool, pool_scale)
    pool_out = pool_out * jax.nn.silu(g_pool)
    o, s_fin = hgrn2(q, f_logit, v_in, lb, s0)
    o = rmsnorm(o, hg_norm).reshape(b, t, D_HG_V).astype(x.dtype) * jax.nn.silu(g_hg)
    y = jnp.concatenate([pool_out, o], axis=-1) @ w_out
    x = x + rmsnorm(y, norm_post)
    x = x + jax.nn.sigmoid(x @ w_ple_gate) * (p @ w_ple)
    return x, new_buf, s_fin.astype(x.dtype)


def setup_inputs(seed: int = 0) -> dict:
    key = jax.random.key(seed)
    ks = jax.random.split(key, 20)
    nrm = jax.random.normal
    f32 = jnp.float32
    return {
        "x_prompt": nrm(ks[0], (BATCH, SEQ, D_MODEL), f32),
        "x_sample": nrm(ks[1], (DEC_BATCH, DEC_SEQ, D_MODEL), f32),
        "cache_pool": nrm(ks[2], (DEPTH, DEC_BATCH, POOL_BUF, D_POOL), f32),
        "state_hgrn": 0.5 * nrm(ks[3], (DEPTH, DEC_BATCH, N_HG_HEADS, HG_KEY, HG_VAL), f32),
        "p_prompt": nrm(ks[4], (DEPTH, BATCH, SEQ, D_PLE), f32),
        "p_sample": nrm(ks[5], (DEPTH, DEC_BATCH, DEC_SEQ, D_PLE), f32),
        "lb_logits": 0.5 * nrm(ks[6], (DEPTH + 1, D_HG_K), f32),
        "w_in": nrm(ks[7], (DEPTH, D_MODEL, D_IN), f32) * D_MODEL ** -0.5,
        "w_pool": nrm(ks[8], (DEPTH, N_POOL_GROUPS, POOL_GROUP, POOL_GROUP), f32) * POOL_GROUP ** -0.5,
        "pool_scale": 1.0 + 0.1 * nrm(ks[9], (DEPTH, D_POOL), f32),
        "hg_norm": 1.0 + 0.1 * nrm(ks[10], (DEPTH, HG_VAL), f32),
        "w_out": nrm(ks[11], (DEPTH, D_MIX, D_MODEL), f32) * D_MIX ** -0.5,
        "norm_pre": 1.0 + 0.1 * nrm(ks[12], (DEPTH, D_MODEL), f32),
        "norm_post": 1.0 + 0.1 * nrm(ks[13], (DEPTH, D_MODEL), f32),
        "w_ple": nrm(ks[14], (DEPTH, D_PLE, D_MODEL), f32) * D_PLE ** -0.5,
        "w_ple_gate": nrm(ks[15], (DEPTH, D_MODEL, D_MODEL), f32) * D_MODEL ** -0.5,
    }


def reference(x_prompt, x_sample, cache_pool, state_hgrn, p_prompt, p_sample, lb_logits,
              w_in, w_pool, pool_scale, hg_norm, w_out, norm_pre, norm_post, w_ple, w_ple_gate):
    lb_all = jnp.cumsum(jax.nn.softmax(lb_logits.astype(jnp.float32), axis=0), axis=0)
    xp, xs = x_prompt, x_sample
    pool_p, hg_p, pool_s, hg_s = [], [], [], []
    for l in range(DEPTH):
        wl = (lb_all[l], w_in[l], w_pool[l], pool_scale[l], hg_norm[l], w_out[l],
              norm_pre[l], norm_post[l], w_ple[l], w_ple_gate[l])
        zero_buf = jnp.zeros((xp.shape[0], POOL_BUF, D_POOL), xp.dtype)
        zero_s = jnp.zeros((xp.shape[0], N_HG_HEADS, HG_KEY, HG_VAL), jnp.float32)
        xp, bp, sp = layer(xp, p_prompt[l], zero_buf, zero_s, 0, *wl)
        xs, bs, ss = layer(xs, p_sample[l], cache_pool[l], state_hgrn[l], PAST_LEN, *wl)
        pool_p.append(bp); hg_p.append(sp); pool_s.append(bs); hg_s.append(ss)
    new_pool_prompt = jnp.stack(pool_p)
    new_hgrn_prompt = jnp.stack(hg_p)
    new_pool_sample = jnp.stack(pool_s)
    new_hgrn_sample = jnp.stack(hg_s)
    return (xp, xs, new_pool_prompt, new_hgrn_prompt, new_pool_sample, new_hgrn_sample)
```

```python
import functools

import jax
import jax.numpy as jnp
from jax import lax
from jax.experimental import pallas as pl
from jax.experimental.pallas import tpu as pltpu

D_MODEL = 1024
D_POOL = 512
POOL_WINDOWS = (2, 4, 8, 16)
POOL_GROUP = 128
POOL_BUF = 15
N_HEADS = 4
HEAD = 128
D_HG = N_HEADS * HEAD
D_IN = 2 * D_POOL + 4 * D_HG
D_PLE = 256
EPS = 1e-6
BLK = 16
CHUNK = 128
HDR = 16
PROMPT_TILE = 256
VMEM_LIMIT_BYTES = 48 * 1024 * 1024

C_U, C_GP, C_Q, C_F, C_V, C_GH = 0, 512, 1024, 1536, 2048, 2560

_NT = (((1,), (1,)), ((), ()))
_TN = (((0,), (0,)), ((), ()))


def _dot(a, b):
    return jnp.dot(a, b, preferred_element_type=jnp.float32)


def _dot_nt(a, b):
    return lax.dot_general(a, b, _NT, preferred_element_type=jnp.float32)


def _dot_tn(a, b):
    return lax.dot_general(a, b, _TN, preferred_element_type=jnp.float32)


def _bf(x):
    return x.astype(jnp.bfloat16)


def _sigmoid(x):
    return 1.0 / (1.0 + jnp.exp(-x))


def _rms_scale(x):
    return lax.rsqrt(jnp.mean(x * x, axis=-1, keepdims=True) + EPS)


def _lower_bound(lb_logits):
    l0 = lb_logits[0:1, :]
    l1 = lb_logits[1:2, :]
    m = jnp.maximum(l0, l1)
    e0 = jnp.exp(l0 - m)
    e1 = jnp.exp(l1 - m)
    return e0 / (e0 + e1)


def _window_sum(e, w):
    s = e
    d = 1
    while d < w:
        s = s + pltpu.roll(s, d, axis=0)
        d *= 2
    return s


def _pool_group(e, w, first_pos):
    s = _window_sum(e, w)[HDR:]
    u = e[HDR:]
    rows = lax.broadcasted_iota(jnp.int32, (HDR, POOL_GROUP), 0)
    cnt = jnp.minimum(w, first_pos + rows + 1).astype(jnp.float32)
    head = s[:HDR] / cnt - u[:HDR]
    if s.shape[0] == HDR:
        return head
    tail = s[HDR:] * (1.0 / w) - u[HDR:]
    return jnp.concatenate([head, tail], axis=0)


def _block_cumprod(f):
    row = lax.broadcasted_iota(jnp.int32, f.shape, 0) & (BLK - 1)
    x = f
    for d in (1, 2, 4, 8):
        x = x * jnp.where(row >= d, pltpu.roll(x, d, axis=0), 1.0)
    return x


def _gates(q, fl, lb):
    f = lb + (1.0 - lb) * _sigmoid(fl)
    k = 1.0 - f
    F = _block_cumprod(f)
    return q * F, k / F, F


def _stack_blocks(pieces, n_blocks):
    zero = jnp.zeros((BLK, HEAD), jnp.float32)
    return jnp.concatenate([pieces.get(j, zero) for j in range(n_blocks)], axis=0)


def _hgrn_chunk(q, fl, v, lb, st, mask_d, mask_p):
    c = q.shape[0]
    nb = c // BLK
    qe, ke, F = _gates(q, fl, lb)
    T = [F[BLK * j + BLK - 1:BLK * j + BLK, :] for j in range(nb)]
    qe_b = [qe[BLK * j:BLK * (j + 1)] for j in range(nb)]
    kd_b = [ke[BLK * j:BLK * (j + 1)] * T[j] for j in range(nb)]
    kd = jnp.concatenate(kd_b, axis=0)

    r1 = _dot_nt(_bf(qe), _bf(jnp.concatenate([ke, kd], axis=0)))
    p = jnp.where(mask_d, r1[:, :c], jnp.where(mask_p, r1[:, c:], 0.0))

    q_slots, k_slots = [], []
    gs = 4
    while gs <= nb:
        for a in range(0, nb, gs):
            mid = a + gs // 2
            qs, ks = {}, {}
            dec = None
            for j in range(mid, a + gs):
                qs[j] = qe_b[j] if dec is None else qe_b[j] * dec
                dec = T[j] if dec is None else dec * T[j]
            dec = None
            for j in range(mid - 1, a - 1, -1):
                ks[j] = kd_b[j] if dec is None else kd_b[j] * dec
                dec = T[j] if dec is None else dec * T[j]
            q_slots.append(_stack_blocks(qs, nb))
            k_slots.append(_stack_blocks(ks, nb))
        gs *= 2
    if q_slots:
        p = p + _dot_nt(_bf(jnp.concatenate(q_slots, axis=1)), _bf(jnp.concatenate(k_slots, axis=1)))

    e_in = [None] * nb
    dec = None
    for j in range(nb):
        e_in[j] = dec
        dec = T[j] if dec is None else dec * T[j]
    e_tot = dec
    d_out = [None] * nb
    dec = None
    for j in range(nb - 1, -1, -1):
        d_out[j] = dec
        dec = T[j] if dec is None else dec * T[j]
    q_in = jnp.concatenate([qe_b[j] if e_in[j] is None else qe_b[j] * e_in[j] for j in range(nb)], axis=0)
    k_out = jnp.concatenate([kd_b[j] if d_out[j] is None else kd_b[j] * d_out[j] for j in range(nb)], axis=0)
    v_bf = _bf(v)
    o = _dot(_bf(p), v_bf) + _dot_nt(_bf(q_in), _bf(st))
    st_new = st * e_tot + _dot_tn(v_bf, _bf(k_out))
    return o, st_new


def _head_out(o, gate, hg_norm):
    y = o * _rms_scale(o) * hg_norm
    return y * (gate * _sigmoid(gate))


def _finish(x, mix_bf, p_bf, w_out, w_ple, w_gate, norm_post):
    y = _dot(mix_bf, w_out)
    x1 = x + y * _rms_scale(y) * norm_post
    gate = _sigmoid(_dot(_bf(x1), w_gate))
    return x1 + gate * _dot(p_bf, w_ple)


def _chunk_masks(c):
    t = lax.broadcasted_iota(jnp.int32, (c, c), 0)
    s = lax.broadcasted_iota(jnp.int32, (c, c), 1)
    tb = t >> 4
    sb = s >> 4
    mask_d = (tb == sb) & (s <= t)
    mask_p = ((tb & 1) == 1) & (sb == tb - 1)
    return mask_d, mask_p


def _prompt_kernel(x_ref, p_ref, lbl_ref, w_in_ref, w_pool_ref, scale_ref, hgn_ref, w_out_ref,
                   npre_ref, npost_ref, w_ple_ref, w_gate_ref,
                   y_ref, pool_ref, hg_ref,
                   uext_ref, z_ref, mix_ref, st_ref):
    t = pl.program_id(1)
    nt = pl.num_programs(1)
    tt = x_ref.shape[0]

    @pl.when(t == 0)
    def _():
        uext_ref[0:HDR, :] = jnp.zeros((HDR, D_POOL), jnp.float32)
        st_ref[...] = jnp.zeros(st_ref.shape, jnp.float32)

    x = x_ref[...]
    h = _bf(x * _rms_scale(x) * npre_ref[...])
    uext_ref[HDR:HDR + tt, :] = _dot(h, w_in_ref[:, C_U:C_U + D_POOL])
    z_ref[...] = _dot(h, w_in_ref[:, C_GP:])

    first_pos = t * tt
    for gi, w in enumerate(POOL_WINDOWS):
        cs = slice(gi * POOL_GROUP, (gi + 1) * POOL_GROUP)
        pooled = _pool_group(uext_ref[:, cs], w, first_pos)
        mixed = _dot(_bf(pooled), w_pool_ref[gi]) * scale_ref[:, cs]
        gp = z_ref[:, cs]
        mix_ref[:, cs] = _bf(mixed * (gp * _sigmoid(gp)))

    lb = _lower_bound(lbl_ref[...])
    mask_d, mask_p = _chunk_masks(CHUNK)
    for hd in range(N_HEADS):
        hs = slice(hd * HEAD, (hd + 1) * HEAD)
        st = st_ref[hd]
        for c0 in range(0, tt, CHUNK):
            rs = slice(c0, c0 + CHUNK)
            q = z_ref[rs, C_Q - C_GP + hd * HEAD:C_Q - C_GP + (hd + 1) * HEAD]
            fl = z_ref[rs, C_F - C_GP + hd * HEAD:C_F - C_GP + (hd + 1) * HEAD]
            v = z_ref[rs, C_V - C_GP + hd * HEAD:C_V - C_GP + (hd + 1) * HEAD]
            gh = z_ref[rs, C_GH - C_GP + hd * HEAD:C_GH - C_GP + (hd + 1) * HEAD]
            o, st = _hgrn_chunk(q, fl, v, lb[:, hs], st, mask_d, mask_p)
            mix_ref[rs, D_POOL + hd * HEAD:D_POOL + (hd + 1) * HEAD] = _bf(_head_out(o, gh, hgn_ref[...]))
        st_ref[hd] = st

    y_ref[...] = _finish(x, mix_ref[...], _bf(p_ref[...]), w_out_ref[...], w_ple_ref[...],
                         w_gate_ref[...], npost_ref[...])

    @pl.when(t == nt - 1)
    def _():
        pool_ref[...] = uext_ref[tt + HDR - POOL_BUF:tt + HDR, :]
        for hd in range(N_HEADS):
            hg_ref[hd] = st_ref[hd].T

    @pl.when(t < nt - 1)
    def _():
        uext_ref[0:HDR, :] = uext_ref[tt:tt + HDR, :]


def _const_spec(shape):
    nd = len(shape)
    return pl.BlockSpec(shape, lambda *_: (0,) * nd, pipeline_mode=pl.Buffered(1))


def _prompt_call(x, p, lbl, w_in, w_pool, scale, hgn, w_out, npre, npost, w_ple, w_gate):
    b, t, _ = x.shape
    tt = PROMPT_TILE
    assert t % tt == 0 and tt % CHUNK == 0
    weights = (lbl, w_in, w_pool, scale, hgn, w_out, npre, npost, w_ple, w_gate)
    return pl.pallas_call(
        _prompt_kernel,
        grid=(b, t // tt),
        in_specs=[pl.BlockSpec((None, tt, D_MODEL), lambda i, j: (i, j, 0)),
                  pl.BlockSpec((None, tt, D_PLE), lambda i, j: (i, j, 0))]
                 + [_const_spec(w.shape) for w in weights],
        out_specs=[pl.BlockSpec((None, tt, D_MODEL), lambda i, j: (i, j, 0)),
                   pl.BlockSpec((None, POOL_BUF, D_POOL), lambda i, j: (i, 0, 0)),
                   pl.BlockSpec((None, N_HEADS, HEAD, HEAD), lambda i, j: (i, 0, 0, 0))],
        out_shape=[jax.ShapeDtypeStruct((b, t, D_MODEL), jnp.float32),
                   jax.ShapeDtypeStruct((b, POOL_BUF, D_POOL), jnp.float32),
                   jax.ShapeDtypeStruct((b, N_HEADS, HEAD, HEAD), jnp.float32)],
        scratch_shapes=[pltpu.VMEM((HDR + tt, D_POOL), jnp.float32),
                        pltpu.VMEM((tt, D_IN - D_POOL), jnp.float32),
                        pltpu.VMEM((tt, D_MODEL), jnp.bfloat16),
                        pltpu.VMEM((N_HEADS, HEAD, HEAD), jnp.float32)],
        compiler_params=pltpu.CompilerParams(dimension_semantics=("arbitrary", "arbitrary"),
                                             vmem_limit_bytes=VMEM_LIMIT_BYTES),
        name="prompt_layer",
    )(x, p, *weights)


def _sample_kernel(start_pos, x_ref, p_ref, cache_ref, s0_ref, lbl_ref, w_in_ref, w_pool_ref, scale_ref,
                   hgn_ref, w_out_ref, npre_ref, npost_ref, w_ple_ref, w_gate_ref,
                   y_ref, pool_ref, hg_ref,
                   z_ref, ext_ref, mix_ref, tpad_ref):
    b = pl.program_id(0)
    nb = pl.num_programs(0)
    ts = pool_ref.shape[0] + 1
    rows = pl.ds(pl.multiple_of(b * ts, ts), ts)

    @pl.when(b == 0)
    def _():
        x = x_ref[...]
        h = _bf(x * _rms_scale(x) * npre_ref[...])
        z_ref[...] = _dot(h, w_in_ref[...])
        tpad_ref[...] = jnp.zeros(tpad_ref.shape, jnp.float32)

    u = z_ref[rows, C_U:C_U + D_POOL]
    ext_ref[0:HDR, :] = jnp.zeros((HDR, D_POOL), jnp.float32)
    ext_ref[HDR - POOL_BUF:HDR, :] = cache_ref[...]
    ext_ref[HDR:HDR + ts, :] = u
    pool_ref[...] = u[ts - POOL_BUF:ts, :]
    for gi, w in enumerate(POOL_WINDOWS):
        cs = slice(gi * POOL_GROUP, (gi + 1) * POOL_GROUP)
        pooled = _pool_group(ext_ref[:, cs], w, start_pos)
        mixed = _dot(_bf(pooled), w_pool_ref[gi]) * scale_ref[:, cs]
        gp = z_ref[rows, C_GP + gi * POOL_GROUP:C_GP + (gi + 1) * POOL_GROUP]
        mix_ref[rows, cs] = _bf(mixed * (gp * _sigmoid(gp)))

    lb = _lower_bound(lbl_ref[...])
    tri_t = lax.broadcasted_iota(jnp.int32, (ts, ts), 0)
    tri_s = lax.broadcasted_iota(jnp.int32, (ts, ts), 1)
    gated = []
    for hd in range(N_HEADS):
        hs = slice(hd * HEAD, (hd + 1) * HEAD)
        q = z_ref[rows, C_Q + hd * HEAD:C_Q + (hd + 1) * HEAD]
        fl = z_ref[rows, C_F + hd * HEAD:C_F + (hd + 1) * HEAD]
        qe, ke, F = _gates(q, fl, lb[:, hs])
        gated.append((qe, ke, F))
        tpad_ref[0:1, hs] = F[ts - 1:ts, :]
    for hd in range(N_HEADS):
        hs = slice(hd * HEAD, (hd + 1) * HEAD)
        qe, ke, F = gated[hd]
        v = _bf(z_ref[rows, C_V + hd * HEAD:C_V + (hd + 1) * HEAD])
        gh = z_ref[rows, C_GH + hd * HEAD:C_GH + (hd + 1) * HEAD]
        s0 = s0_ref[hd]
        att = jnp.where(tri_s <= tri_t, _dot_nt(_bf(qe), _bf(ke)), 0.0)
        o = _dot(_bf(att), v) + _dot(_bf(qe), _bf(s0))
        mix_ref[rows, D_POOL + hd * HEAD:D_POOL + (hd + 1) * HEAD] = _bf(_head_out(o, gh, hgn_ref[...]))
        t_col = tpad_ref[:, hs].T[:, 0:1]
        kd = ke * F[ts - 1:ts, :]
        hg_ref[hd] = s0 * t_col + _dot_tn(_bf(kd), v)

    @pl.when(b == nb - 1)
    def _():
        y_ref[...] = _finish(x_ref[...], mix_ref[...], _bf(p_ref[...]), w_out_ref[...], w_ple_ref[...],
                             w_gate_ref[...], npost_ref[...])


def _sample_call(start_pos, x, p, cache, s0, lbl, w_in, w_pool, scale, hgn, w_out, npre, npost, w_ple, w_gate):
    b, ts, _ = x.shape
    assert ts == BLK and ts == POOL_BUF + 1
    n = b * ts
    weights = (lbl, w_in, w_pool, scale, hgn, w_out, npre, npost, w_ple, w_gate)
    y, pool, hg = pl.pallas_call(
        functools.partial(_sample_kernel, start_pos),
        grid=(b,),
        in_specs=[_const_spec((n, D_MODEL)), _const_spec((n, D_PLE)),
                  pl.BlockSpec((None, POOL_BUF, D_POOL), lambda i: (i, 0, 0)),
                  pl.BlockSpec((None, N_HEADS, HEAD, HEAD), lambda i: (i, 0, 0, 0))]
                 + [_const_spec(w.shape) for w in weights],
        out_specs=[pl.BlockSpec((n, D_MODEL), lambda i: (0, 0)),
                   pl.BlockSpec((None, POOL_BUF, D_POOL), lambda i: (i, 0, 0)),
                   pl.BlockSpec((None, N_HEADS, HEAD, HEAD), lambda i: (i, 0, 0, 0))],
        out_shape=[jax.ShapeDtypeStruct((n, D_MODEL), jnp.float32),
                   jax.ShapeDtypeStruct((b, POOL_BUF, D_POOL), jnp.float32),
                   jax.ShapeDtypeStruct((b, N_HEADS, HEAD, HEAD), jnp.float32)],
        scratch_shapes=[pltpu.VMEM((n, D_IN), jnp.float32),
                        pltpu.VMEM((HDR + ts, D_POOL), jnp.float32),
                        pltpu.VMEM((n, D_MODEL), jnp.bfloat16),
                        pltpu.VMEM((HEAD, D_HG), jnp.float32)],
        compiler_params=pltpu.CompilerParams(dimension_semantics=("arbitrary",),
                                             vmem_limit_bytes=VMEM_LIMIT_BYTES),
        name="sample_layer",
    )(x.reshape(n, D_MODEL), p.reshape(n, D_PLE), cache, s0, *weights)
    return y.reshape(b, ts, D_MODEL), pool, hg


def kernel(x_prompt, x_sample, cache_pool, state_hgrn, p_prompt, p_sample, lb_logits, w_in, w_pool, pool_scale,
           hg_norm, w_out, norm_pre, norm_post, w_ple, w_ple_gate):
    depth = w_in.shape[0]
    assert depth == 1 and lb_logits.shape[0] == 2
    past_len = 1024
    weights = (lb_logits, _bf(w_in[0]), _bf(w_pool[0]), pool_scale, hg_norm, _bf(w_out[0]),
               norm_pre, norm_post, _bf(w_ple[0]), _bf(w_ple_gate[0]))
    y_p, pool_p, hg_p = _prompt_call(x_prompt, p_prompt[0], *weights)
    y_s, pool_s, hg_s = _sample_call(past_len, x_sample, p_sample[0], cache_pool[0], state_hgrn[0], *weights)
    return (y_p, y_s, pool_p[None], hg_p[None], pool_s[None], hg_s[None])
```

```python
import functools

import jax
import jax.numpy as jnp
from jax import lax
from jax.experimental import pallas as pl
from jax.experimental.pallas import tpu as pltpu

D_MODEL = 1024
D_POOL = 512
POOL_WINDOWS = (2, 4, 8, 16)
POOL_GROUP = 128
POOL_BUF = 15
N_HEADS = 4
HEAD = 128
D_HG = N_HEADS * HEAD
D_IN = 2 * D_POOL + 4 * D_HG
D_PLE = 256
EPS = 1e-6
LOG2_E = 1.4426950408889634
COLS = 512
BLK = 16
CHUNK = 128
HDR = 16
PROMPT_TILE = 256
VMEM_LIMIT_BYTES = 48 * 1024 * 1024
PROMPT_SCHEDULER_FLAGS = None

C_U, C_GP, C_Q, C_F, C_V, C_GH = 0, 512, 1024, 1536, 2048, 2560

_NT = (((1,), (1,)), ((), ()))
_TN = (((0,), (0,)), ((), ()))


def _dot(a, b):
    return jnp.dot(a, b, preferred_element_type=jnp.float32)


def _dot_nt(a, b):
    return lax.dot_general(a, b, _NT, preferred_element_type=jnp.float32)


def _dot_tn(a, b):
    return lax.dot_general(a, b, _TN, preferred_element_type=jnp.float32)


def _bf(x):
    return x.astype(jnp.bfloat16)


def _sigmoid(x):
    return 1.0 / (1.0 + jnp.exp2(x * (-LOG2_E)))


def _rms_scale(x):
    return lax.rsqrt(jnp.mean(x * x, axis=-1, keepdims=True) + EPS)


def _lower_bound(lb_logits):
    l0 = lb_logits[0:1, :]
    l1 = lb_logits[1:2, :]
    m = jnp.maximum(l0, l1)
    e0 = jnp.exp(l0 - m)
    e1 = jnp.exp(l1 - m)
    return e0 / (e0 + e1)


def _window_sum(e, w):
    s = e
    d = 1
    while d < w:
        s = s + pltpu.roll(s, d, axis=0)
        d *= 2
    return s


def _pool_group(e, w, first_pos):
    s = _window_sum(e, w)[HDR:]
    u = e[HDR:]
    rows = lax.broadcasted_iota(jnp.int32, (HDR, POOL_GROUP), 0)
    cnt = jnp.minimum(w, first_pos + rows + 1).astype(jnp.float32)
    head = s[:HDR] / cnt - u[:HDR]
    if s.shape[0] == HDR:
        return head
    tail = s[HDR:] * (1.0 / w) - u[HDR:]
    return jnp.concatenate([head, tail], axis=0)


def _block_cumprod(f):
    row = lax.broadcasted_iota(jnp.int32, f.shape, 0) & (BLK - 1)
    x = f
    for d in (1, 2, 4, 8):
        x = x * jnp.where(row >= d, pltpu.roll(x, d, axis=0), 1.0)
    return x


def _gates(q, fl, lb):
    f = lb + (1.0 - lb) * _sigmoid(fl)
    k = 1.0 - f
    F = _block_cumprod(f)
    return q * F, k / F, F


def _stack_blocks(pieces, n_blocks):
    zero = jnp.zeros((BLK, HEAD), jnp.float32)
    return jnp.concatenate([pieces.get(j, zero) for j in range(n_blocks)], axis=0)


def _hgrn_chunk(q, fl, v, lb, st, mask_d, mask_p):
    c = q.shape[0]
    nb = c // BLK
    qe, ke, F = _gates(q, fl, lb)
    T = [F[BLK * j + BLK - 1:BLK * j + BLK, :] for j in range(nb)]
    qe_b = [qe[BLK * j:BLK * (j + 1)] for j in range(nb)]
    kd_b = [ke[BLK * j:BLK * (j + 1)] * T[j] for j in range(nb)]
    kd = jnp.concatenate(kd_b, axis=0)

    r1 = _dot_nt(_bf(qe), _bf(jnp.concatenate([ke, kd], axis=0)))
    p = jnp.where(mask_d, r1[:, :c], jnp.where(mask_p, r1[:, c:], 0.0))

    q_slots, k_slots = [], []
    gs = 4
    while gs <= nb:
        for a in range(0, nb, gs):
            mid = a + gs // 2
            qs, ks = {}, {}
            dec = None
            for j in range(mid, a + gs):
                qs[j] = qe_b[j] if dec is None else qe_b[j] * dec
                dec = T[j] if dec is None else dec * T[j]
            dec = None
            for j in range(mid - 1, a - 1, -1):
                ks[j] = kd_b[j] if dec is None else kd_b[j] * dec
                dec = T[j] if dec is None else dec * T[j]
            q_slots.append(_stack_blocks(qs, nb))
            k_slots.append(_stack_blocks(ks, nb))
        gs *= 2
    if q_slots:
        p = p + _dot_nt(_bf(jnp.concatenate(q_slots, axis=1)), _bf(jnp.concatenate(k_slots, axis=1)))

    e_in = [None] * nb
    dec = None
    for j in range(nb):
        e_in[j] = dec
        dec = T[j] if dec is None else dec * T[j]
    e_tot = dec
    d_out = [None] * nb
    dec = None
    for j in range(nb - 1, -1, -1):
        d_out[j] = dec
        dec = T[j] if dec is None else dec * T[j]
    q_in = jnp.concatenate([qe_b[j] if e_in[j] is None else qe_b[j] * e_in[j] for j in range(nb)], axis=0)
    k_out = jnp.concatenate([kd_b[j] if d_out[j] is None else kd_b[j] * d_out[j] for j in range(nb)], axis=0)
    v_bf = _bf(v)
    o = _dot(_bf(p), v_bf) + _dot_nt(_bf(q_in), _bf(st))
    st_new = st * e_tot + _dot_tn(v_bf, _bf(k_out))
    return o, st_new


def _head_out(o, gate, hg_norm):
    y = o * _rms_scale(o) * hg_norm
    return y * (gate * _sigmoid(gate))


def _post_norm(x, y_blocks, norm_post):
    ms = sum(jnp.sum(y * y, axis=-1, keepdims=True) for y in y_blocks) / D_MODEL
    r = lax.rsqrt(ms + EPS)
    return [x[:, j * COLS:(j + 1) * COLS] + y * r * norm_post[:, j * COLS:(j + 1) * COLS]
            for j, y in enumerate(y_blocks)]


def _finish(x, mix_bf, p_bf, w_out_ref, w_ple_ref, w_gate_ref, norm_post):
    nblk = D_MODEL // COLS
    x1 = _post_norm(x, [_dot(mix_bf, w_out_ref[j]) for j in range(nblk)], norm_post)
    x1_bf = _bf(jnp.concatenate(x1, axis=1))
    out = [x1[j] + _sigmoid(_dot(x1_bf, w_gate_ref[j])) * _dot(p_bf, w_ple_ref[j]) for j in range(nblk)]
    return jnp.concatenate(out, axis=1)


def _chunk_masks(c):
    t = lax.broadcasted_iota(jnp.int32, (c, c), 0)
    s = lax.broadcasted_iota(jnp.int32, (c, c), 1)
    tb = t >> 4
    sb = s >> 4
    mask_d = (tb == sb) & (s <= t)
    mask_p = ((tb & 1) == 1) & (sb == tb - 1)
    return mask_d, mask_p


def _prompt_step(first_pos, xa_ref, xc_ref, p_ref, lbl_ref, w_in_ref, w_pool_ref, scale_ref, hgn_ref, w_out_ref,
                 npre_ref, npost_ref, w_ple_ref, w_gate_ref, y_ref, carry_ref, st_ref,
                 u_w, z_w, mix_w, u_r, z_r, mix_r):
    tt = xa_ref.shape[0]
    lb = _lower_bound(lbl_ref[...])
    mask_d, mask_p = _chunk_masks(CHUNK)
    states = [st_ref[hd] for hd in range(N_HEADS)]

    def in_proj(h, c0):
        blk = _dot(h, w_in_ref[c0 // COLS])
        if c0 == C_U:
            u_w[HDR:HDR + tt, :] = blk
        else:
            z_w[:, c0 - C_GP:c0 - C_GP + D_POOL] = blk

    def pool_mixer():
        u_r[0:HDR, :] = carry_ref[...]
        for gi, w in enumerate(POOL_WINDOWS):
            cs = slice(gi * POOL_GROUP, (gi + 1) * POOL_GROUP)
            pooled = _pool_group(u_r[:, cs], w, first_pos)
            mixed = _dot(_bf(pooled), w_pool_ref[gi]) * scale_ref[:, cs]
            gp = z_r[:, cs]
            mix_w[:, cs] = _bf(mixed * (gp * _sigmoid(gp)))
        carry_ref[...] = u_r[tt:tt + HDR, :]

    def hgrn(hd, c0):
        hs = slice(hd * HEAD, (hd + 1) * HEAD)
        rs = slice(c0, c0 + CHUNK)
        q = z_r[rs, C_Q - C_GP + hd * HEAD:C_Q - C_GP + (hd + 1) * HEAD]
        fl = z_r[rs, C_F - C_GP + hd * HEAD:C_F - C_GP + (hd + 1) * HEAD]
        v = z_r[rs, C_V - C_GP + hd * HEAD:C_V - C_GP + (hd + 1) * HEAD]
        gh = z_r[rs, C_GH - C_GP + hd * HEAD:C_GH - C_GP + (hd + 1) * HEAD]
        o, states[hd] = _hgrn_chunk(q, fl, v, lb[:, hs], states[hd], mask_d, mask_p)
        mix_w[rs, D_POOL + hd * HEAD:D_POOL + (hd + 1) * HEAD] = _bf(_head_out(o, gh, hgn_ref[...]))

    todo = [(hd, c0) for c0 in range(0, tt, CHUNK) for hd in range(N_HEADS)]
    p_bf = _bf(p_ref[...])
    ple = [_dot(p_bf, w_ple_ref[j]) for j in range(2)]
    xa = xa_ref[...]
    h = _bf(xa * _rms_scale(xa) * npre_ref[...])
    mix_prev = mix_r[...]
    y_lo = _dot(mix_prev, w_out_ref[0])
    pool_mixer()
    y_hi = _dot(mix_prev, w_out_ref[1])
    hgrn(*todo.pop(0))
    in_proj(h, C_U)
    x1 = _post_norm(xc_ref[...], [y_lo, y_hi], npost_ref[...])
    x1_bf = _bf(jnp.concatenate(x1, axis=1))
    hgrn(*todo.pop(0))
    for c0 in (C_GP, C_Q, C_F, C_V, C_GH):
        in_proj(h, c0)
        hgrn(*todo.pop(0))
    gate_lo = _dot(x1_bf, w_gate_ref[0])
    hgrn(*todo.pop(0))
    assert not todo
    gate_hi = _dot(x1_bf, w_gate_ref[1])
    y_ref[:, :COLS] = x1[0] + _sigmoid(gate_lo) * ple[0]
    y_ref[:, COLS:] = x1[1] + _sigmoid(gate_hi) * ple[1]
    for hd in range(N_HEADS):
        st_ref[hd] = states[hd]


def _prompt_kernel(tiles_per_stream, xa_ref, xc_ref, p_ref, lbl_ref, w_in_ref, w_pool_ref, scale_ref, hgn_ref,
                   w_out_ref, npre_ref, npost_ref, w_ple_ref, w_gate_ref,
                   y_ref, pool_ref, hg_ref,
                   u_a, z_a, mix_a, u_b, z_b, mix_b, carry_ref, st_ref):
    s = pl.program_id(0)
    tt = xa_ref.shape[0]
    tile = lax.rem(s - 1 + tiles_per_stream, tiles_per_stream)

    @pl.when(s == 0)
    def _():
        for ref in (u_a, z_a, mix_a, u_b, z_b, mix_b, carry_ref, st_ref):
            ref[...] = jnp.zeros(ref.shape, ref.dtype)

    @pl.when(tile == 0)
    def _():
        carry_ref[...] = jnp.zeros(carry_ref.shape, jnp.float32)
        st_ref[...] = jnp.zeros(st_ref.shape, jnp.float32)

    step = functools.partial(_prompt_step, tile * tt, xa_ref, xc_ref, p_ref, lbl_ref, w_in_ref, w_pool_ref,
                             scale_ref, hgn_ref, w_out_ref, npre_ref, npost_ref, w_ple_ref, w_gate_ref,
                             y_ref, carry_ref, st_ref)

    @pl.when((s & 1) == 0)
    def _():
        step(u_a, z_a, mix_a, u_b, z_b, mix_b)

    @pl.when((s & 1) == 1)
    def _():
        step(u_b, z_b, mix_b, u_a, z_a, mix_a)

    @pl.when((tile == tiles_per_stream - 1) & (s > 0))
    def _():
        pool_ref[...] = carry_ref[HDR - POOL_BUF:HDR, :]
        for hd in range(N_HEADS):
            hg_ref[hd] = st_ref[hd].T


def _const_spec(shape):
    nd = len(shape)
    return pl.BlockSpec(shape, lambda *_: (0,) * nd, pipeline_mode=pl.Buffered(1))


def _prompt_call(x, p, lbl, w_in, w_pool, scale, hgn, w_out, npre, npost, w_ple, w_gate):
    b, t, _ = x.shape
    tt = PROMPT_TILE
    assert t % tt == 0 and tt % CHUNK == 0
    nt = t // tt
    g = b * nt
    weights = (lbl, w_in, w_pool, scale, hgn, w_out, npre, npost, w_ple, w_gate)
    x2 = x.reshape(b * t, D_MODEL)
    p2 = p.reshape(b * t, D_PLE)
    ahead = lambda s: (jnp.minimum(s, g - 1), 0)
    behind = lambda s: (jnp.clip(s - 2, 0, g - 1), 0)
    stream = lambda s: (jnp.clip((s - 1) // nt, 0, b - 1),)
    y, pool, hg = pl.pallas_call(
        functools.partial(_prompt_kernel, nt),
        grid=(g + 2,),
        in_specs=[pl.BlockSpec((tt, D_MODEL), ahead),
                  pl.BlockSpec((tt, D_MODEL), behind),
                  pl.BlockSpec((tt, D_PLE), behind)]
                 + [_const_spec(w.shape) for w in weights],
        out_specs=[pl.BlockSpec((tt, D_MODEL), behind),
                   pl.BlockSpec((None, POOL_BUF, D_POOL), lambda s: stream(s) + (0, 0)),
                   pl.BlockSpec((None, N_HEADS, HEAD, HEAD), lambda s: stream(s) + (0, 0, 0))],
        out_shape=[jax.ShapeDtypeStruct((b * t, D_MODEL), jnp.float32),
                   jax.ShapeDtypeStruct((b, POOL_BUF, D_POOL), jnp.float32),
                   jax.ShapeDtypeStruct((b, N_HEADS, HEAD, HEAD), jnp.float32)],
        scratch_shapes=[pltpu.VMEM((HDR + tt, D_POOL), jnp.float32),
                        pltpu.VMEM((tt, D_IN - D_POOL), jnp.float32),
                        pltpu.VMEM((tt, D_MODEL), jnp.bfloat16)] * 2
                       + [pltpu.VMEM((HDR, D_POOL), jnp.float32),
                          pltpu.VMEM((N_HEADS, HEAD, HEAD), jnp.float32)],
        compiler_params=pltpu.CompilerParams(dimension_semantics=("arbitrary",),
                                             vmem_limit_bytes=VMEM_LIMIT_BYTES,
                                             flags=PROMPT_SCHEDULER_FLAGS),
        name="prompt_layer",
    )(x2, x2, p2, *weights)
    return y.reshape(b, t, D_MODEL), pool, hg


def _sample_kernel(start_pos, x_ref, p_ref, cache_ref, s0_ref, lbl_ref, w_in_ref, w_pool_ref, scale_ref,
                   hgn_ref, w_out_ref, npre_ref, npost_ref, w_ple_ref, w_gate_ref,
                   y_ref, pool_ref, hg_ref,
                   z_ref, ext_ref, mix_ref, tpad_ref):
    b = pl.program_id(0)
    nb = pl.num_programs(0)
    ts = pool_ref.shape[0] + 1
    rows = pl.ds(pl.multiple_of(b * ts, ts), ts)

    @pl.when(b == 0)
    def _():
        x = x_ref[...]
        h = _bf(x * _rms_scale(x) * npre_ref[...])
        for j in range(D_IN // COLS):
            z_ref[:, j * COLS:(j + 1) * COLS] = _dot(h, w_in_ref[j])
        tpad_ref[...] = jnp.zeros(tpad_ref.shape, jnp.float32)

    u = z_ref[rows, C_U:C_U + D_POOL]
    ext_ref[0:HDR, :] = jnp.zeros((HDR, D_POOL), jnp.float32)
    ext_ref[HDR - POOL_BUF:HDR, :] = cache_ref[...]
    ext_ref[HDR:HDR + ts, :] = u
    pool_ref[...] = u[ts - POOL_BUF:ts, :]
    for gi, w in enumerate(POOL_WINDOWS):
        cs = slice(gi * POOL_GROUP, (gi + 1) * POOL_GROUP)
        pooled = _pool_group(ext_ref[:, cs], w, start_pos)
        mixed = _dot(_bf(pooled), w_pool_ref[gi]) * scale_ref[:, cs]
        gp = z_ref[rows, C_GP + gi * POOL_GROUP:C_GP + (gi + 1) * POOL_GROUP]
        mix_ref[rows, cs] = _bf(mixed * (gp * _sigmoid(gp)))

    lb = _lower_bound(lbl_ref[...])
    tri_t = lax.broadcasted_iota(jnp.int32, (ts, ts), 0)
    tri_s = lax.broadcasted_iota(jnp.int32, (ts, ts), 1)
    gated = []
    for hd in range(N_HEADS):
        hs = slice(hd * HEAD, (hd + 1) * HEAD)
        q = z_ref[rows, C_Q + hd * HEAD:C_Q + (hd + 1) * HEAD]
        fl = z_ref[rows, C_F + hd * HEAD:C_F + (hd + 1) * HEAD]
        qe, ke, F = _gates(q, fl, lb[:, hs])
        gated.append((qe, ke, F))
        tpad_ref[0:1, hs] = F[ts - 1:ts, :]
    for hd in range(N_HEADS):
        hs = slice(hd * HEAD, (hd + 1) * HEAD)
        qe, ke, F = gated[hd]
        v = _bf(z_ref[rows, C_V + hd * HEAD:C_V + (hd + 1) * HEAD])
        gh = z_ref[rows, C_GH + hd * HEAD:C_GH + (hd + 1) * HEAD]
        s0 = s0_ref[hd]
        att = jnp.where(tri_s <= tri_t, _dot_nt(_bf(qe), _bf(ke)), 0.0)
        o = _dot(_bf(att), v) + _dot(_bf(qe), _bf(s0))
        mix_ref[rows, D_POOL + hd * HEAD:D_POOL + (hd + 1) * HEAD] = _bf(_head_out(o, gh, hgn_ref[...]))
        t_col = tpad_ref[:, hs].T[:, 0:1]
        kd = ke * F[ts - 1:ts, :]
        hg_ref[hd] = s0 * t_col + _dot_tn(_bf(kd), v)

    @pl.when(b == nb - 1)
    def _():
        y_ref[...] = _finish(x_ref[...], mix_ref[...], _bf(p_ref[...]), w_out_ref, w_ple_ref, w_gate_ref,
                             npost_ref[...])


def _sample_call(start_pos, x, p, cache, s0, lbl, w_in, w_pool, scale, hgn, w_out, npre, npost, w_ple, w_gate):
    b, ts, _ = x.shape
    assert ts == BLK and ts == POOL_BUF + 1
    n = b * ts
    weights = (lbl, w_in, w_pool, scale, hgn, w_out, npre, npost, w_ple, w_gate)
    y, pool, hg = pl.pallas_call(
        functools.partial(_sample_kernel, start_pos),
        grid=(b,),
        in_specs=[_const_spec((n, D_MODEL)), _const_spec((n, D_PLE)),
                  pl.BlockSpec((None, POOL_BUF, D_POOL), lambda i: (i, 0, 0)),
                  pl.BlockSpec((None, N_HEADS, HEAD, HEAD), lambda i: (i, 0, 0, 0))]
                 + [_const_spec(w.shape) for w in weights],
        out_specs=[pl.BlockSpec((n, D_MODEL), lambda i: (0, 0)),
                   pl.BlockSpec((None, POOL_BUF, D_POOL), lambda i: (i, 0, 0)),
                   pl.BlockSpec((None, N_HEADS, HEAD, HEAD), lambda i: (i, 0, 0, 0))],
        out_shape=[jax.ShapeDtypeStruct((n, D_MODEL), jnp.float32),
                   jax.ShapeDtypeStruct((b, POOL_BUF, D_POOL), jnp.float32),
                   jax.ShapeDtypeStruct((b, N_HEADS, HEAD, HEAD), jnp.float32)],
        scratch_shapes=[pltpu.VMEM((n, D_IN), jnp.float32),
                        pltpu.VMEM((HDR + ts, D_POOL), jnp.float32),
                        pltpu.VMEM((n, D_MODEL), jnp.bfloat16),
                        pltpu.VMEM((HEAD, D_HG), jnp.float32)],
        compiler_params=pltpu.CompilerParams(dimension_semantics=("arbitrary",),
                                             vmem_limit_bytes=VMEM_LIMIT_BYTES),
        name="sample_layer",
    )(x.reshape(n, D_MODEL), p.reshape(n, D_PLE), cache, s0, *weights)
    return y.reshape(b, ts, D_MODEL), pool, hg


def _col_blocks(w):
    k, n = w.shape
    return _bf(w.reshape(k, n // COLS, COLS).transpose(1, 0, 2))


def kernel(x_prompt, x_sample, cache_pool, state_hgrn, p_prompt, p_sample, lb_logits, w_in, w_pool, pool_scale,
           hg_norm, w_out, norm_pre, norm_post, w_ple, w_ple_gate):
    depth = w_in.shape[0]
    assert depth == 1 and lb_logits.shape[0] == 2
    past_len = 1024
    weights = (lb_logits, _col_blocks(w_in[0]), _bf(w_pool[0]), pool_scale, hg_norm, _col_blocks(w_out[0]),
               norm_pre, norm_post, _col_blocks(w_ple[0]), _col_blocks(w_ple_gate[0]))
    y_p, pool_p, hg_p = _prompt_call(x_prompt, p_prompt[0], *weights)
    y_s, pool_s, hg_s = _sample_call(past_len, x_sample, p_sample[0], cache_pool[0], state_hgrn[0], *weights)
    return (y_p, y_s, pool_p[None], hg_p[None], pool_s[None], hg_s[None])
```

```python
import functools

import jax
import jax.numpy as jnp
from jax import lax
from jax.experimental import pallas as pl
from jax.experimental.pallas import tpu as pltpu

D_MODEL = 1024
D_POOL = 512
POOL_WINDOWS = (2, 4, 8, 16)
POOL_GROUP = 128
POOL_BUF = 15
N_HEADS = 4
HEAD = 128
D_HG = N_HEADS * HEAD
D_IN = 2 * D_POOL + 4 * D_HG
D_PLE = 256
EPS = 1e-6
LOG2_E = 1.4426950408889634
COLS = 512
BLK = 16
CHUNK = 128
HDR = 16
PROMPT_TILE = 256
VMEM_LIMIT_BYTES = 48 * 1024 * 1024
PROMPT_SCHEDULER_FLAGS = None

C_U, C_GP, C_Q, C_F, C_V, C_GH = 0, 512, 1024, 1536, 2048, 2560

_NT = (((1,), (1,)), ((), ()))
_TN = (((0,), (0,)), ((), ()))


def _dot(a, b):
    return jnp.dot(a, b, preferred_element_type=jnp.float32)


def _dot_nt(a, b):
    return lax.dot_general(a, b, _NT, preferred_element_type=jnp.float32)


def _dot_tn(a, b):
    return lax.dot_general(a, b, _TN, preferred_element_type=jnp.float32)


def _bf(x):
    return x.astype(jnp.bfloat16)


def _sigmoid(x):
    return 1.0 / (1.0 + jnp.exp2(x * (-LOG2_E)))


def _rms_scale(x):
    return lax.rsqrt(jnp.mean(x * x, axis=-1, keepdims=True) + EPS)


def _lower_bound(lb_logits):
    l0 = lb_logits[0:1, :]
    l1 = lb_logits[1:2, :]
    m = jnp.maximum(l0, l1)
    e0 = jnp.exp(l0 - m)
    e1 = jnp.exp(l1 - m)
    return e0 / (e0 + e1)


def _window_sum(e, w):
    s = e
    d = 1
    while d < w:
        s = s + pltpu.roll(s, d, axis=0)
        d *= 2
    return s


def _pool_group(e, w, first_pos):
    s = _window_sum(e, w)[HDR:]
    u = e[HDR:]
    rows = lax.broadcasted_iota(jnp.int32, (HDR, POOL_GROUP), 0)
    cnt = jnp.minimum(w, first_pos + rows + 1).astype(jnp.float32)
    head = s[:HDR] / cnt - u[:HDR]
    if s.shape[0] == HDR:
        return head
    tail = s[HDR:] * (1.0 / w) - u[HDR:]
    return jnp.concatenate([head, tail], axis=0)


def _block_cumprod(f):
    row = lax.broadcasted_iota(jnp.int32, f.shape, 0) & (BLK - 1)
    x = f
    for d in (1, 2, 4, 8):
        x = x * jnp.where(row >= d, pltpu.roll(x, d, axis=0), 1.0)
    return x


def _gates(q, fl, lb):
    f = lb + (1.0 - lb) * _sigmoid(fl)
    k = 1.0 - f
    F = _block_cumprod(f)
    return q * F, k / F, F


def _stack_blocks(pieces, n_blocks):
    zero = jnp.zeros((BLK, HEAD), jnp.float32)
    return jnp.concatenate([pieces.get(j, zero) for j in range(n_blocks)], axis=0)


class _HgrnChunk:
    def __init__(self, load, lb):
        self.load, self.lb = load, lb

    def prepare(self):
        q, fl, v = self.load()
        c = self.c = q.shape[0]
        nb = c // BLK
        qe, ke, F = _gates(q, fl, self.lb)
        T = [F[BLK * j + BLK - 1:BLK * j + BLK, :] for j in range(nb)]
        qe_b = [qe[BLK * j:BLK * (j + 1)] for j in range(nb)]
        kd_b = [ke[BLK * j:BLK * (j + 1)] * T[j] for j in range(nb)]
        kd = jnp.concatenate(kd_b, axis=0)
        self.qe = _bf(qe)
        self.ke_kd = _bf(jnp.concatenate([ke, kd], axis=0))

        q_slots, k_slots = [], []
        gs = 4
        while gs <= nb:
            for a in range(0, nb, gs):
                mid = a + gs // 2
                qs, ks = {}, {}
                dec = None
                for j in range(mid, a + gs):
                    qs[j] = qe_b[j] if dec is None else qe_b[j] * dec
                    dec = T[j] if dec is None else dec * T[j]
                dec = None
                for j in range(mid - 1, a - 1, -1):
                    ks[j] = kd_b[j] if dec is None else kd_b[j] * dec
                    dec = T[j] if dec is None else dec * T[j]
                q_slots.append(_stack_blocks(qs, nb))
                k_slots.append(_stack_blocks(ks, nb))
            gs *= 2
        self.q_far = _bf(jnp.concatenate(q_slots, axis=1)) if q_slots else None
        self.k_far = _bf(jnp.concatenate(k_slots, axis=1)) if k_slots else None

        e_in = [None] * nb
        dec = None
        for j in range(nb):
            e_in[j] = dec
            dec = T[j] if dec is None else dec * T[j]
        self.e_tot = dec
        d_out = [None] * nb
        dec = None
        for j in range(nb - 1, -1, -1):
            d_out[j] = dec
            dec = T[j] if dec is None else dec * T[j]
        self.q_in = _bf(jnp.concatenate(
            [qe_b[j] if e_in[j] is None else qe_b[j] * e_in[j] for j in range(nb)], axis=0))
        self.k_out = _bf(jnp.concatenate(
            [kd_b[j] if d_out[j] is None else kd_b[j] * d_out[j] for j in range(nb)], axis=0))
        self.v_bf = _bf(v)

    def issue(self):
        self.r1 = _dot_nt(self.qe, self.ke_kd)
        self.far = _dot_nt(self.q_far, self.k_far) if self.q_far is not None else None
        self.upd = _dot_tn(self.v_bf, self.k_out)

    def combine(self, st, mask_d, mask_p):
        c = self.c
        p = jnp.where(mask_d, self.r1[:, :c], jnp.where(mask_p, self.r1[:, c:], 0.0))
        if self.far is not None:
            p = p + self.far
        self.o = _dot(_bf(p), self.v_bf) + _dot_nt(self.q_in, _bf(st))
        return st * self.e_tot + self.upd


def _head_out(o, gate, hg_norm):
    y = o * _rms_scale(o) * hg_norm
    return y * (gate * _sigmoid(gate))


def _post_norm(x, y_blocks, norm_post):
    ms = sum(jnp.sum(y * y, axis=-1, keepdims=True) for y in y_blocks) / D_MODEL
    r = lax.rsqrt(ms + EPS)
    return [x[:, j * COLS:(j + 1) * COLS] + y * r * norm_post[:, j * COLS:(j + 1) * COLS]
            for j, y in enumerate(y_blocks)]


def _finish(x, mix_bf, p_bf, w_out_ref, w_ple_ref, w_gate_ref, norm_post):
    nblk = D_MODEL // COLS
    x1 = _post_norm(x, [_dot(mix_bf, w_out_ref[j]) for j in range(nblk)], norm_post)
    x1_bf = _bf(jnp.concatenate(x1, axis=1))
    out = [x1[j] + _sigmoid(_dot(x1_bf, w_gate_ref[j])) * _dot(p_bf, w_ple_ref[j]) for j in range(nblk)]
    return jnp.concatenate(out, axis=1)


def _chunk_masks(c):
    t = lax.broadcasted_iota(jnp.int32, (c, c), 0)
    s = lax.broadcasted_iota(jnp.int32, (c, c), 1)
    tb = t >> 4
    sb = s >> 4
    mask_d = (tb == sb) & (s <= t)
    mask_p = ((tb & 1) == 1) & (sb == tb - 1)
    return mask_d, mask_p


def _prompt_step(first_pos, xa_ref, xc_ref, p_ref, lbl_ref, w_in_ref, w_pool_ref, scale_ref, hgn_ref, w_out_ref,
                 npre_ref, npost_ref, w_ple_ref, w_gate_ref, y_ref, carry_ref, st_ref,
                 u_w, z_w, mix_w, u_r, z_r, mix_r):
    tt = xa_ref.shape[0]
    lb = _lower_bound(lbl_ref[...])
    mask_d, mask_p = _chunk_masks(CHUNK)
    states = [st_ref[hd] for hd in range(N_HEADS)]

    def in_proj(h, c0):
        blk = _dot(h, w_in_ref[c0 // COLS])
        if c0 == C_U:
            u_w[HDR:HDR + tt, :] = blk
        else:
            z_w[:, c0 - C_GP:c0 - C_GP + D_POOL] = blk

    def pool_group(gi):
        cs = slice(gi * POOL_GROUP, (gi + 1) * POOL_GROUP)
        pooled = _pool_group(u_r[:, cs], POOL_WINDOWS[gi], first_pos)
        mixed = _dot(_bf(pooled), w_pool_ref[gi]) * scale_ref[:, cs]
        gp = z_r[:, cs]
        mix_w[:, cs] = _bf(mixed * (gp * _sigmoid(gp)))

    def hgrn_unit(hd, c0):
        hs = slice(hd * HEAD, (hd + 1) * HEAD)
        rs = slice(c0, c0 + CHUNK)
        col = lambda base: slice(base - C_GP + hd * HEAD, base - C_GP + (hd + 1) * HEAD)
        unit = _HgrnChunk(lambda: (z_r[rs, col(C_Q)], z_r[rs, col(C_F)], z_r[rs, col(C_V)]), lb[:, hs])
        unit.head, unit.rows, unit.gate_cols = hd, rs, col(C_GH)
        return unit

    def hgrn_store(unit):
        gh = z_r[unit.rows, unit.gate_cols]
        mix_w[unit.rows, D_POOL + unit.head * HEAD:D_POOL + (unit.head + 1) * HEAD] = _bf(
            _head_out(unit.o, gh, hgn_ref[...]))

    units = [hgrn_unit(hd, c0) for c0 in range(0, tt, CHUNK) for hd in range(N_HEADS)]
    dense = [C_U, C_GP, C_Q, C_F, C_V, C_GH]
    p_bf = _bf(p_ref[...])
    ple = [_dot(p_bf, w_ple_ref[j]) for j in range(2)]
    xa = xa_ref[...]
    h = _bf(xa * _rms_scale(xa) * npre_ref[...])
    units[0].prepare()
    mix_prev = mix_r[...]
    y_lo = _dot(mix_prev, w_out_ref[0])
    y_hi = _dot(mix_prev, w_out_ref[1])
    u_r[0:HDR, :] = carry_ref[...]
    x1 = x1_bf = gate_lo = gate_hi = None
    last = len(units) - 1
    for i, unit in enumerate(units):
        unit.issue()
        if i < last:
            units[i + 1].prepare()
        if dense:
            in_proj(h, dense.pop(0))
        elif gate_lo is None:
            gate_lo = _dot(x1_bf, w_gate_ref[0])
        states[unit.head] = unit.combine(states[unit.head], mask_d, mask_p)
        if i == 0:
            x1 = _post_norm(xc_ref[...], [y_lo, y_hi], npost_ref[...])
            x1_bf = _bf(jnp.concatenate(x1, axis=1))
        else:
            hgrn_store(units[i - 1])
        if i < len(POOL_WINDOWS):
            pool_group(i)
    carry_ref[...] = u_r[tt:tt + HDR, :]
    gate_hi = _dot(x1_bf, w_gate_ref[1])
    y_ref[:, :COLS] = x1[0] + _sigmoid(gate_lo) * ple[0]
    hgrn_store(units[last])
    y_ref[:, COLS:] = x1[1] + _sigmoid(gate_hi) * ple[1]
    for hd in range(N_HEADS):
        st_ref[hd] = states[hd]


def _prompt_kernel(tiles_per_stream, xa_ref, xc_ref, p_ref, lbl_ref, w_in_ref, w_pool_ref, scale_ref, hgn_ref,
                   w_out_ref, npre_ref, npost_ref, w_ple_ref, w_gate_ref,
                   y_ref, pool_ref, hg_ref,
                   u_a, z_a, mix_a, u_b, z_b, mix_b, carry_ref, st_ref):
    s = pl.program_id(0)
    tt = xa_ref.shape[0]
    tile = lax.rem(s - 1 + tiles_per_stream, tiles_per_stream)

    @pl.when(s == 0)
    def _():
        for ref in (u_a, z_a, mix_a, u_b, z_b, mix_b, carry_ref, st_ref):
            ref[...] = jnp.zeros(ref.shape, ref.dtype)

    @pl.when(tile == 0)
    def _():
        carry_ref[...] = jnp.zeros(carry_ref.shape, jnp.float32)
        st_ref[...] = jnp.zeros(st_ref.shape, jnp.float32)

    step = functools.partial(_prompt_step, tile * tt, xa_ref, xc_ref, p_ref, lbl_ref, w_in_ref, w_pool_ref,
                             scale_ref, hgn_ref, w_out_ref, npre_ref, npost_ref, w_ple_ref, w_gate_ref,
                             y_ref, carry_ref, st_ref)

    @pl.when((s & 1) == 0)
    def _():
        step(u_a, z_a, mix_a, u_b, z_b, mix_b)

    @pl.when((s & 1) == 1)
    def _():
        step(u_b, z_b, mix_b, u_a, z_a, mix_a)

    @pl.when((tile == tiles_per_stream - 1) & (s > 0))
    def _():
        pool_ref[...] = carry_ref[HDR - POOL_BUF:HDR, :]
        for hd in range(N_HEADS):
            hg_ref[hd] = st_ref[hd].T


def _const_spec(shape):
    nd = len(shape)
    return pl.BlockSpec(shape, lambda *_: (0,) * nd, pipeline_mode=pl.Buffered(1))


def _prompt_call(x, p, lbl, w_in, w_pool, scale, hgn, w_out, npre, npost, w_ple, w_gate):
    b, t, _ = x.shape
    tt = PROMPT_TILE
    assert t % tt == 0 and tt % CHUNK == 0
    nt = t // tt
    g = b * nt
    weights = (lbl, w_in, w_pool, scale, hgn, w_out, npre, npost, w_ple, w_gate)
    x2 = x.reshape(b * t, D_MODEL)
    p2 = p.reshape(b * t, D_PLE)
    ahead = lambda s: (jnp.minimum(s, g - 1), 0)
    behind = lambda s: (jnp.clip(s - 2, 0, g - 1), 0)
    stream = lambda s: (jnp.clip((s - 1) // nt, 0, b - 1),)
    y, pool, hg = pl.pallas_call(
        functools.partial(_prompt_kernel, nt),
        grid=(g + 2,),
        in_specs=[pl.BlockSpec((tt, D_MODEL), ahead),
                  pl.BlockSpec((tt, D_MODEL), behind),
                  pl.BlockSpec((tt, D_PLE), behind)]
                 + [_const_spec(w.shape) for w in weights],
        out_specs=[pl.BlockSpec((tt, D_MODEL), behind),
                   pl.BlockSpec((None, POOL_BUF, D_POOL), lambda s: stream(s) + (0, 0)),
                   pl.BlockSpec((None, N_HEADS, HEAD, HEAD), lambda s: stream(s) + (0, 0, 0))],
        out_shape=[jax.ShapeDtypeStruct((b * t, D_MODEL), jnp.float32),
                   jax.ShapeDtypeStruct((b, POOL_BUF, D_POOL), jnp.float32),
                   jax.ShapeDtypeStruct((b, N_HEADS, HEAD, HEAD), jnp.float32)],
        scratch_shapes=[pltpu.VMEM((HDR + tt, D_POOL), jnp.float32),
                        pltpu.VMEM((tt, D_IN - D_POOL), jnp.float32),
                        pltpu.VMEM((tt, D_MODEL), jnp.bfloat16)] * 2
                       + [pltpu.VMEM((HDR, D_POOL), jnp.float32),
                          pltpu.VMEM((N_HEADS, HEAD, HEAD), jnp.float32)],
        compiler_params=pltpu.CompilerParams(dimension_semantics=("arbitrary",),
                                             vmem_limit_bytes=VMEM_LIMIT_BYTES,
                                             flags=PROMPT_SCHEDULER_FLAGS),
        name="prompt_layer",
    )(x2, x2, p2, *weights)
    return y.reshape(b, t, D_MODEL), pool, hg


def _sample_kernel(start_pos, x_ref, p_ref, cache_ref, s0_ref, lbl_ref, w_in_ref, w_pool_ref, scale_ref,
                   hgn_ref, w_out_ref, npre_ref, npost_ref, w_ple_ref, w_gate_ref,
                   y_ref, pool_ref, hg_ref,
                   z_ref, ext_ref, mix_ref, tpad_ref):
    b = pl.program_id(0)
    nb = pl.num_programs(0)
    ts = pool_ref.shape[0] + 1
    rows = pl.ds(pl.multiple_of(b * ts, ts), ts)

    @pl.when(b == 0)
    def _():
        x = x_ref[...]
        h = _bf(x * _rms_scale(x) * npre_ref[...])
        for j in range(D_IN // COLS):
            z_ref[:, j * COLS:(j + 1) * COLS] = _dot(h, w_in_ref[j])
        tpad_ref[...] = jnp.zeros(tpad_ref.shape, jnp.float32)

    u = z_ref[rows, C_U:C_U + D_POOL]
    ext_ref[0:HDR, :] = jnp.zeros((HDR, D_POOL), jnp.float32)
    ext_ref[HDR - POOL_BUF:HDR, :] = cache_ref[...]
    ext_ref[HDR:HDR + ts, :] = u
    pool_ref[...] = u[ts - POOL_BUF:ts, :]
    for gi, w in enumerate(POOL_WINDOWS):
        cs = slice(gi * POOL_GROUP, (gi + 1) * POOL_GROUP)
        pooled = _pool_group(ext_ref[:, cs], w, start_pos)
        mixed = _dot(_bf(pooled), w_pool_ref[gi]) * scale_ref[:, cs]
        gp = z_ref[rows, C_GP + gi * POOL_GROUP:C_GP + (gi + 1) * POOL_GROUP]
        mix_ref[rows, cs] = _bf(mixed * (gp * _sigmoid(gp)))

    lb = _lower_bound(lbl_ref[...])
    tri_t = lax.broadcasted_iota(jnp.int32, (ts, ts), 0)
    tri_s = lax.broadcasted_iota(jnp.int32, (ts, ts), 1)
    gated = []
    for hd in range(N_HEADS):
        hs = slice(hd * HEAD, (hd + 1) * HEAD)
        q = z_ref[rows, C_Q + hd * HEAD:C_Q + (hd + 1) * HEAD]
        fl = z_ref[rows, C_F + hd * HEAD:C_F + (hd + 1) * HEAD]
        qe, ke, F = _gates(q, fl, lb[:, hs])
        gated.append((qe, ke, F))
        tpad_ref[0:1, hs] = F[ts - 1:ts, :]
    for hd in range(N_HEADS):
        hs = slice(hd * HEAD, (hd + 1) * HEAD)
        qe, ke, F = gated[hd]
        v = _bf(z_ref[rows, C_V + hd * HEAD:C_V + (hd + 1) * HEAD])
        gh = z_ref[rows, C_GH + hd * HEAD:C_GH + (hd + 1) * HEAD]
        s0 = s0_ref[hd]
        att = jnp.where(tri_s <= tri_t, _dot_nt(_bf(qe), _bf(ke)), 0.0)
        o = _dot(_bf(att), v) + _dot(_bf(qe), _bf(s0))
        mix_ref[rows, D_POOL + hd * HEAD:D_POOL + (hd + 1) * HEAD] = _bf(_head_out(o, gh, hgn_ref[...]))
        t_col = tpad_ref[:, hs].T[:, 0:1]
        kd = ke * F[ts - 1:ts, :]
        hg_ref[hd] = s0 * t_col + _dot_tn(_bf(kd), v)

    @pl.when(b == nb - 1)
    def _():
        y_ref[...] = _finish(x_ref[...], mix_ref[...], _bf(p_ref[...]), w_out_ref, w_ple_ref, w_gate_ref,
                             npost_ref[...])


def _sample_call(start_pos, x, p, cache, s0, lbl, w_in, w_pool, scale, hgn, w_out, npre, npost, w_ple, w_gate):
    b, ts, _ = x.shape
    assert ts == BLK and ts == POOL_BUF + 1
    n = b * ts
    weights = (lbl, w_in, w_pool, scale, hgn, w_out, npre, npost, w_ple, w_gate)
    y, pool, hg = pl.pallas_call(
        functools.partial(_sample_kernel, start_pos),
        grid=(b,),
        in_specs=[_const_spec((n, D_MODEL)), _const_spec((n, D_PLE)),
                  pl.BlockSpec((None, POOL_BUF, D_POOL), lambda i: (i, 0, 0)),
                  pl.BlockSpec((None, N_HEADS, HEAD, HEAD), lambda i: (i, 0, 0, 0))]
                 + [_const_spec(w.shape) for w in weights],
        out_specs=[pl.BlockSpec((n, D_MODEL), lambda i: (0, 0)),
                   pl.BlockSpec((None, POOL_BUF, D_POOL), lambda i: (i, 0, 0)),
                   pl.BlockSpec((None, N_HEADS, HEAD, HEAD), lambda i: (i, 0, 0, 0))],
        out_shape=[jax.ShapeDtypeStruct((n, D_MODEL), jnp.float32),
                   jax.ShapeDtypeStruct((b, POOL_BUF, D_POOL), jnp.float32),
                   jax.ShapeDtypeStruct((b, N_HEADS, HEAD, HEAD), jnp.float32)],
        scratch_shapes=[pltpu.VMEM((n, D_IN), jnp.float32),
                        pltpu.VMEM((HDR + ts, D_POOL), jnp.float32),
                        pltpu.VMEM((n, D_MODEL), jnp.bfloat16),
                        pltpu.VMEM((HEAD, D_HG), jnp.float32)],
        compiler_params=pltpu.CompilerParams(dimension_semantics=("arbitrary",),
                                             vmem_limit_bytes=VMEM_LIMIT_BYTES),
        name="sample_layer",
    )(x.reshape(n, D_MODEL), p.reshape(n, D_PLE), cache, s0, *weights)
    return y.reshape(b, ts, D_MODEL), pool, hg


def _col_blocks(w):
    k, n = w.shape
    return _bf(w.reshape(k, n // COLS, COLS).transpose(1, 0, 2))


def kernel(x_prompt, x_sample, cache_pool, state_hgrn, p_prompt, p_sample, lb_logits, w_in, w_pool, pool_scale,
           hg_norm, w_out, norm_pre, norm_post, w_ple, w_ple_gate):
    depth = w_in.shape[0]
    assert depth == 1 and lb_logits.shape[0] == 2
    past_len = 1024
    weights = (lb_logits, _col_blocks(w_in[0]), _bf(w_pool[0]), pool_scale, hg_norm, _col_blocks(w_out[0]),
               norm_pre, norm_post, _col_blocks(w_ple[0]), _col_blocks(w_ple_gate[0]))
    y_p, pool_p, hg_p = _prompt_call(x_prompt, p_prompt[0], *weights)
    y_s, pool_s, hg_s = _sample_call(past_len, x_sample, p_sample[0], cache_pool[0], state_hgrn[0], *weights)
    return (y_p, y_s, pool_p[None], hg_p[None], pool_s[None], hg_s[None])
```

```python
import functools

import jax
import jax.numpy as jnp
from jax import lax
from jax.experimental import pallas as pl
from jax.experimental.pallas import tpu as pltpu

D_MODEL = 1024
D_POOL = 512
POOL_WINDOWS = (2, 4, 8, 16)
POOL_GROUP = 128
POOL_BUF = 15
N_HEADS = 4
HEAD = 128
D_HG = N_HEADS * HEAD
D_IN = 2 * D_POOL + 4 * D_HG
D_PLE = 256
EPS = 1e-6
LOG2_E = 1.4426950408889634
COLS = 512
BLK = 16
CHUNK = 128
HDR = 16
PROMPT_TILE = 256
VMEM_LIMIT_BYTES = 48 * 1024 * 1024
PROMPT_SCHEDULER_FLAGS = None

C_U, C_GP, C_Q, C_F, C_V, C_GH = 0, 512, 1024, 1536, 2048, 2560

_NT = (((1,), (1,)), ((), ()))
_TN = (((0,), (0,)), ((), ()))


def _dot(a, b):
    return jnp.dot(a, b, preferred_element_type=jnp.float32)


def _dot_nt(a, b):
    return lax.dot_general(a, b, _NT, preferred_element_type=jnp.float32)


def _dot_tn(a, b):
    return lax.dot_general(a, b, _TN, preferred_element_type=jnp.float32)


def _bf(x):
    return x.astype(jnp.bfloat16)


def _wblk(w_ref, j):
    return w_ref[j]


def _sigmoid(x):
    return 1.0 / (1.0 + jnp.exp2(x * (-LOG2_E)))


def _rms_scale(x):
    return lax.rsqrt(jnp.mean(x * x, axis=-1, keepdims=True) + EPS)


def _lower_bound(lb_logits):
    l0 = lb_logits[0:1, :]
    l1 = lb_logits[1:2, :]
    m = jnp.maximum(l0, l1)
    e0 = jnp.exp(l0 - m)
    e1 = jnp.exp(l1 - m)
    return e0 / (e0 + e1)


def _window_sum(e, w):
    s = e
    d = 1
    while d < w:
        s = s + pltpu.roll(s, d, axis=0)
        d *= 2
    return s


def _pool_group(e, w, first_pos):
    s = _window_sum(e, w)[HDR:]
    u = e[HDR:]
    rows = lax.broadcasted_iota(jnp.int32, (HDR, POOL_GROUP), 0)
    cnt = jnp.minimum(w, first_pos + rows + 1).astype(jnp.float32)
    head = s[:HDR] / cnt - u[:HDR]
    if s.shape[0] == HDR:
        return head
    tail = s[HDR:] * (1.0 / w) - u[HDR:]
    return jnp.concatenate([head, tail], axis=0)


def _block_cumprod(f):
    row = lax.broadcasted_iota(jnp.int32, f.shape, 0) & (BLK - 1)
    x = f
    for d in (1, 2, 4, 8):
        x = x * jnp.where(row >= d, pltpu.roll(x, d, axis=0), 1.0)
    return x


def _gates(q, fl, lb):
    f = lb + (1.0 - lb) * _sigmoid(fl)
    k = 1.0 - f
    F = _block_cumprod(f)
    return q * F, k / F, F


def _stack_blocks(pieces, n_blocks):
    zero = jnp.zeros((BLK, HEAD), jnp.float32)
    return jnp.concatenate([pieces.get(j, zero) for j in range(n_blocks)], axis=0)


class _HgrnChunk:
    def __init__(self, load, lb):
        self.load, self.lb = load, lb

    def prepare(self):
        q, fl, v = self.load()
        c = self.c = q.shape[0]
        nb = c // BLK
        qe, ke, F = _gates(q, fl, self.lb)
        T = [F[BLK * j + BLK - 1:BLK * j + BLK, :] for j in range(nb)]
        qe_b = [qe[BLK * j:BLK * (j + 1)] for j in range(nb)]
        kd_b = [ke[BLK * j:BLK * (j + 1)] * T[j] for j in range(nb)]
        kd = jnp.concatenate(kd_b, axis=0)
        self.qe = _bf(qe)
        self.ke_kd = _bf(jnp.concatenate([ke, kd], axis=0))

        q_slots, k_slots = [], []
        gs = 4
        while gs <= nb:
            for a in range(0, nb, gs):
                mid = a + gs // 2
                qs, ks = {}, {}
                dec = None
                for j in range(mid, a + gs):
                    qs[j] = qe_b[j] if dec is None else qe_b[j] * dec
                    dec = T[j] if dec is None else dec * T[j]
                dec = None
                for j in range(mid - 1, a - 1, -1):
                    ks[j] = kd_b[j] if dec is None else kd_b[j] * dec
                    dec = T[j] if dec is None else dec * T[j]
                q_slots.append(_stack_blocks(qs, nb))
                k_slots.append(_stack_blocks(ks, nb))
            gs *= 2
        self.q_far = _bf(jnp.concatenate(q_slots, axis=1)) if q_slots else None
        self.k_far = _bf(jnp.concatenate(k_slots, axis=1)) if k_slots else None

        e_in = [None] * nb
        dec = None
        for j in range(nb):
            e_in[j] = dec
            dec = T[j] if dec is None else dec * T[j]
        self.e_tot = dec
        d_out = [None] * nb
        dec = None
        for j in range(nb - 1, -1, -1):
            d_out[j] = dec
            dec = T[j] if dec is None else dec * T[j]
        self.q_in = _bf(jnp.concatenate(
            [qe_b[j] if e_in[j] is None else qe_b[j] * e_in[j] for j in range(nb)], axis=0))
        self.k_out = _bf(jnp.concatenate(
            [kd_b[j] if d_out[j] is None else kd_b[j] * d_out[j] for j in range(nb)], axis=0))
        self.v_bf = _bf(v)

    def issue(self):
        self.r1 = _dot_nt(self.qe, self.ke_kd)
        self.far = _dot_nt(self.q_far, self.k_far) if self.q_far is not None else None
        self.upd = _dot_tn(self.v_bf, self.k_out)

    def combine(self, st, mask_d, mask_p):
        c = self.c
        p = jnp.where(mask_d, self.r1[:, :c], jnp.where(mask_p, self.r1[:, c:], 0.0))
        if self.far is not None:
            p = p + self.far
        self.o = _dot(_bf(p), self.v_bf) + _dot_nt(self.q_in, _bf(st))
        return st * self.e_tot + self.upd


def _head_out(o, gate, hg_norm):
    y = o * _rms_scale(o) * hg_norm
    return y * (gate * _sigmoid(gate))


def _post_norm(x, y_blocks, norm_post):
    ms = sum(jnp.sum(y * y, axis=-1, keepdims=True) for y in y_blocks) / D_MODEL
    r = lax.rsqrt(ms + EPS)
    return [x[:, j * COLS:(j + 1) * COLS] + y * r * norm_post[:, j * COLS:(j + 1) * COLS]
            for j, y in enumerate(y_blocks)]


def _finish(x, mix_bf, p_bf, w_out_ref, w_ple_ref, w_gate_ref, norm_post):
    nblk = D_MODEL // COLS
    x1 = _post_norm(x, [_dot(mix_bf, _wblk(w_out_ref, j)) for j in range(nblk)], norm_post)
    x1_bf = _bf(jnp.concatenate(x1, axis=1))
    out = [x1[j] + _sigmoid(_dot(x1_bf, _wblk(w_gate_ref, j))) * _dot(p_bf, _wblk(w_ple_ref, j)) for j in range(nblk)]
    return jnp.concatenate(out, axis=1)


def _chunk_masks(c):
    t = lax.broadcasted_iota(jnp.int32, (c, c), 0)
    s = lax.broadcasted_iota(jnp.int32, (c, c), 1)
    tb = t >> 4
    sb = s >> 4
    mask_d = (tb == sb) & (s <= t)
    mask_p = ((tb & 1) == 1) & (sb == tb - 1)
    return mask_d, mask_p


def _prompt_step(first_pos, xa_ref, xc_ref, p_ref, lbl_ref, w_in_ref, w_pool_ref, scale_ref, hgn_ref, w_out_ref,
                 npre_ref, npost_ref, w_ple_ref, w_gate_ref, y_ref, carry_ref, st_ref,
                 u_w, z_w, mix_w, u_r, z_r, mix_r):
    tt = xa_ref.shape[0]
    lb = _lower_bound(lbl_ref[...])
    mask_d, mask_p = _chunk_masks(CHUNK)
    states = [st_ref[hd] for hd in range(N_HEADS)]

    def in_proj(h, c0):
        blk = _dot(h, _wblk(w_in_ref, c0 // COLS))
        if c0 == C_U:
            u_w[HDR:HDR + tt, :] = blk
        else:
            z_w[:, c0 - C_GP:c0 - C_GP + D_POOL] = blk

    def pool_group(gi):
        cs = slice(gi * POOL_GROUP, (gi + 1) * POOL_GROUP)
        pooled = _pool_group(u_r[:, cs], POOL_WINDOWS[gi], first_pos)
        mixed = _dot(_bf(pooled), _bf(w_pool_ref[gi])) * scale_ref[:, cs]
        gp = z_r[:, cs]
        mix_w[:, cs] = _bf(mixed * (gp * _sigmoid(gp)))

    def hgrn_unit(hd, c0):
        hs = slice(hd * HEAD, (hd + 1) * HEAD)
        rs = slice(c0, c0 + CHUNK)
        col = lambda base: slice(base - C_GP + hd * HEAD, base - C_GP + (hd + 1) * HEAD)
        unit = _HgrnChunk(lambda: (z_r[rs, col(C_Q)], z_r[rs, col(C_F)], z_r[rs, col(C_V)]), lb[:, hs])
        unit.head, unit.rows, unit.gate_cols = hd, rs, col(C_GH)
        return unit

    def hgrn_store(unit):
        gh = z_r[unit.rows, unit.gate_cols]
        mix_w[unit.rows, D_POOL + unit.head * HEAD:D_POOL + (unit.head + 1) * HEAD] = _bf(
            _head_out(unit.o, gh, hgn_ref[...]))

    units = [hgrn_unit(hd, c0) for c0 in range(0, tt, CHUNK) for hd in range(N_HEADS)]
    dense = [C_U, C_GP, C_Q, C_F, C_V, C_GH]
    p_bf = _bf(p_ref[...])
    ple = [_dot(p_bf, _wblk(w_ple_ref, j)) for j in range(2)]
    xa = xa_ref[...]
    h = _bf(xa * _rms_scale(xa) * npre_ref[...])
    units[0].prepare()
    mix_prev = mix_r[...]
    y_lo = _dot(mix_prev, _wblk(w_out_ref, 0))
    y_hi = _dot(mix_prev, _wblk(w_out_ref, 1))
    u_r[0:HDR, :] = carry_ref[...]
    x1 = x1_bf = gate_lo = gate_hi = None
    last = len(units) - 1
    for i, unit in enumerate(units):
        unit.issue()
        if i < last:
            units[i + 1].prepare()
        if dense:
            in_proj(h, dense.pop(0))
        elif gate_lo is None:
            gate_lo = _dot(x1_bf, _wblk(w_gate_ref, 0))
        states[unit.head] = unit.combine(states[unit.head], mask_d, mask_p)
        if i == 0:
            x1 = _post_norm(xc_ref[...], [y_lo, y_hi], npost_ref[...])
            x1_bf = _bf(jnp.concatenate(x1, axis=1))
        else:
            hgrn_store(units[i - 1])
        if i < len(POOL_WINDOWS):
            pool_group(i)
    carry_ref[...] = u_r[tt:tt + HDR, :]
    gate_hi = _dot(x1_bf, _wblk(w_gate_ref, 1))
    y_ref[:, :COLS] = x1[0] + _sigmoid(gate_lo) * ple[0]
    hgrn_store(units[last])
    y_ref[:, COLS:] = x1[1] + _sigmoid(gate_hi) * ple[1]
    for hd in range(N_HEADS):
        st_ref[hd] = states[hd]


def _prompt_kernel(tiles_per_stream, xa_ref, xc_ref, p_ref, lbl_ref, w_in_ref, w_pool_ref, scale_ref, hgn_ref,
                   w_out_ref, npre_ref, npost_ref, w_ple_ref, w_gate_ref,
                   y_ref, pool_ref, hg_ref,
                   u_a, z_a, mix_a, u_b, z_b, mix_b, carry_ref, st_ref):
    s = pl.program_id(0)
    tt = xa_ref.shape[0]
    tile = lax.rem(s - 1 + tiles_per_stream, tiles_per_stream)

    @pl.when(s == 0)
    def _():
        for ref in (u_a, z_a, mix_a, u_b, z_b, mix_b, carry_ref, st_ref):
            ref[...] = jnp.zeros(ref.shape, ref.dtype)

    @pl.when(tile == 0)
    def _():
        carry_ref[...] = jnp.zeros(carry_ref.shape, jnp.float32)
        st_ref[...] = jnp.zeros(st_ref.shape, jnp.float32)

    step = functools.partial(_prompt_step, tile * tt, xa_ref, xc_ref, p_ref, lbl_ref, w_in_ref, w_pool_ref,
                             scale_ref, hgn_ref, w_out_ref, npre_ref, npost_ref, w_ple_ref, w_gate_ref,
                             y_ref, carry_ref, st_ref)

    @pl.when((s & 1) == 0)
    def _():
        step(u_a, z_a, mix_a, u_b, z_b, mix_b)

    @pl.when((s & 1) == 1)
    def _():
        step(u_b, z_b, mix_b, u_a, z_a, mix_a)

    @pl.when((tile == tiles_per_stream - 1) & (s > 0))
    def _():
        pool_ref[...] = carry_ref[HDR - POOL_BUF:HDR, :]
        for hd in range(N_HEADS):
            hg_ref[hd] = st_ref[hd].T


def _const_spec(shape):
    nd = len(shape)
    return pl.BlockSpec(shape, lambda *_: (0,) * nd, pipeline_mode=pl.Buffered(1))


def _prompt_call(x, p, lbl, w_in, w_pool, scale, hgn, w_out, npre, npost, w_ple, w_gate):
    b, t, _ = x.shape
    tt = PROMPT_TILE
    assert t % tt == 0 and tt % CHUNK == 0
    nt = t // tt
    g = b * nt
    weights = (lbl, w_in, w_pool, scale, hgn, w_out, npre, npost, w_ple, w_gate)
    x2 = x.reshape(b * t, D_MODEL)
    p2 = p.reshape(b * t, D_PLE)
    ahead = lambda s: (jnp.minimum(s, g - 1), 0)
    behind = lambda s: (jnp.clip(s - 2, 0, g - 1), 0)
    stream = lambda s: (jnp.clip((s - 1) // nt, 0, b - 1),)
    y, pool, hg = pl.pallas_call(
        functools.partial(_prompt_kernel, nt),
        grid=(g + 2,),
        in_specs=[pl.BlockSpec((tt, D_MODEL), ahead),
                  pl.BlockSpec((tt, D_MODEL), behind),
                  pl.BlockSpec((tt, D_PLE), behind)]
                 + [_const_spec(w.shape) for w in weights],
        out_specs=[pl.BlockSpec((tt, D_MODEL), behind),
                   pl.BlockSpec((None, POOL_BUF, D_POOL), lambda s: stream(s) + (0, 0)),
                   pl.BlockSpec((None, N_HEADS, HEAD, HEAD), lambda s: stream(s) + (0, 0, 0))],
        out_shape=[jax.ShapeDtypeStruct((b * t, D_MODEL), jnp.float32),
                   jax.ShapeDtypeStruct((b, POOL_BUF, D_POOL), jnp.float32),
                   jax.ShapeDtypeStruct((b, N_HEADS, HEAD, HEAD), jnp.float32)],
        scratch_shapes=[pltpu.VMEM((HDR + tt, D_POOL), jnp.float32),
                        pltpu.VMEM((tt, D_IN - D_POOL), jnp.float32),
                        pltpu.VMEM((tt, D_MODEL), jnp.bfloat16)] * 2
                       + [pltpu.VMEM((HDR, D_POOL), jnp.float32),
                          pltpu.VMEM((N_HEADS, HEAD, HEAD), jnp.float32)],
        compiler_params=pltpu.CompilerParams(dimension_semantics=("arbitrary",),
                                             vmem_limit_bytes=VMEM_LIMIT_BYTES,
                                             flags=PROMPT_SCHEDULER_FLAGS),
        name="prompt_layer",
    )(x2, x2, p2, *weights)
    return y.reshape(b, t, D_MODEL), pool, hg


def _sample_kernel(start_pos, x_ref, p_ref, cache_ref, s0_ref, lbl_ref, w_in_ref, w_pool_ref, scale_ref,
                   hgn_ref, w_out_ref, npre_ref, npost_ref, w_ple_ref, w_gate_ref,
                   y_ref, pool_ref, hg_ref,
                   z_ref, ext_ref, mix_ref, tpad_ref):
    b = pl.program_id(0)
    nb = pl.num_programs(0)
    ts = pool_ref.shape[0] + 1
    rows = pl.ds(pl.multiple_of(b * ts, ts), ts)

    @pl.when(b == 0)
    def _():
        x = x_ref[...]
        h = _bf(x * _rms_scale(x) * npre_ref[...])
        for j in range(D_IN // COLS):
            z_ref[:, j * COLS:(j + 1) * COLS] = _dot(h, _wblk(w_in_ref, j))
        tpad_ref[...] = jnp.zeros(tpad_ref.shape, jnp.float32)

    u = z_ref[rows, C_U:C_U + D_POOL]
    ext_ref[0:HDR, :] = jnp.zeros((HDR, D_POOL), jnp.float32)
    ext_ref[HDR - POOL_BUF:HDR, :] = cache_ref[...]
    ext_ref[HDR:HDR + ts, :] = u
    pool_ref[...] = u[ts - POOL_BUF:ts, :]
    for gi, w in enumerate(POOL_WINDOWS):
        cs = slice(gi * POOL_GROUP, (gi + 1) * POOL_GROUP)
        pooled = _pool_group(ext_ref[:, cs], w, start_pos)
        mixed = _dot(_bf(pooled), _bf(w_pool_ref[gi])) * scale_ref[:, cs]
        gp = z_ref[rows, C_GP + gi * POOL_GROUP:C_GP + (gi + 1) * POOL_GROUP]
        mix_ref[rows, cs] = _bf(mixed * (gp * _sigmoid(gp)))

    lb = _lower_bound(lbl_ref[...])
    tri_t = lax.broadcasted_iota(jnp.int32, (ts, ts), 0)
    tri_s = lax.broadcasted_iota(jnp.int32, (ts, ts), 1)
    gated = []
    for hd in range(N_HEADS):
        hs = slice(hd * HEAD, (hd + 1) * HEAD)
        q = z_ref[rows, C_Q + hd * HEAD:C_Q + (hd + 1) * HEAD]
        fl = z_ref[rows, C_F + hd * HEAD:C_F + (hd + 1) * HEAD]
        qe, ke, F = _gates(q, fl, lb[:, hs])
        gated.append((qe, ke, F))
        tpad_ref[0:1, hs] = F[ts - 1:ts, :]
    for hd in range(N_HEADS):
        hs = slice(hd * HEAD, (hd + 1) * HEAD)
        qe, ke, F = gated[hd]
        v = _bf(z_ref[rows, C_V + hd * HEAD:C_V + (hd + 1) * HEAD])
        gh = z_ref[rows, C_GH + hd * HEAD:C_GH + (hd + 1) * HEAD]
        s0 = s0_ref[hd]
        att = jnp.where(tri_s <= tri_t, _dot_nt(_bf(qe), _bf(ke)), 0.0)
        o = _dot(_bf(att), v) + _dot(_bf(qe), _bf(s0))
        mix_ref[rows, D_POOL + hd * HEAD:D_POOL + (hd + 1) * HEAD] = _bf(_head_out(o, gh, hgn_ref[...]))
        t_col = tpad_ref[:, hs].T[:, 0:1]
        kd = ke * F[ts - 1:ts, :]
        hg_ref[hd] = s0 * t_col + _dot_tn(_bf(kd), v)

    @pl.when(b == nb - 1)
    def _():
        y_ref[...] = _finish(x_ref[...], mix_ref[...], _bf(p_ref[...]), w_out_ref, w_ple_ref, w_gate_ref,
                             npost_ref[...])


def _sample_call(start_pos, x, p, cache, s0, lbl, w_in, w_pool, scale, hgn, w_out, npre, npost, w_ple, w_gate):
    b, ts, _ = x.shape
    assert ts == BLK and ts == POOL_BUF + 1
    n = b * ts
    weights = (lbl, w_in, w_pool, scale, hgn, w_out, npre, npost, w_ple, w_gate)
    y, pool, hg = pl.pallas_call(
        functools.partial(_sample_kernel, start_pos),
        grid=(b,),
        in_specs=[_const_spec((n, D_MODEL)), _const_spec((n, D_PLE)),
                  pl.BlockSpec((None, POOL_BUF, D_POOL), lambda i: (i, 0, 0)),
                  pl.BlockSpec((None, N_HEADS, HEAD, HEAD), lambda i: (i, 0, 0, 0))]
                 + [_const_spec(w.shape) for w in weights],
        out_specs=[pl.BlockSpec((n, D_MODEL), lambda i: (0, 0)),
                   pl.BlockSpec((None, POOL_BUF, D_POOL), lambda i: (i, 0, 0)),
                   pl.BlockSpec((None, N_HEADS, HEAD, HEAD), lambda i: (i, 0, 0, 0))],
        out_shape=[jax.ShapeDtypeStruct((n, D_MODEL), jnp.float32),
                   jax.ShapeDtypeStruct((b, POOL_BUF, D_POOL), jnp.float32),
                   jax.ShapeDtypeStruct((b, N_HEADS, HEAD, HEAD), jnp.float32)],
        scratch_shapes=[pltpu.VMEM((n, D_IN), jnp.float32),
                        pltpu.VMEM((HDR + ts, D_POOL), jnp.float32),
                        pltpu.VMEM((n, D_MODEL), jnp.bfloat16),
                        pltpu.VMEM((HEAD, D_HG), jnp.float32)],
        compiler_params=pltpu.CompilerParams(dimension_semantics=("arbitrary",),
                                             vmem_limit_bytes=VMEM_LIMIT_BYTES),
        name="sample_layer",
    )(x.reshape(n, D_MODEL), p.reshape(n, D_PLE), cache, s0, *weights)
    return y.reshape(b, ts, D_MODEL), pool, hg


def _cast_kernel(w_ref, o_ref):
    o_ref[...] = _bf(w_ref[...])


def _col_blocks(w):
    k, n = w.shape
    return pl.pallas_call(
        _cast_kernel,
        grid=(n // COLS,),
        in_specs=[pl.BlockSpec((k, COLS), lambda j: (0, j))],
        out_specs=pl.BlockSpec((None, k, COLS), lambda j: (j, 0, 0)),
        out_shape=jax.ShapeDtypeStruct((n // COLS, k, COLS), jnp.bfloat16),
        compiler_params=pltpu.CompilerParams(dimension_semantics=("arbitrary",)),
        name="weight_blocks",
    )(w)


def kernel(x_prompt, x_sample, cache_pool, state_hgrn, p_prompt, p_sample, lb_logits, w_in, w_pool, pool_scale,
           hg_norm, w_out, norm_pre, norm_post, w_ple, w_ple_gate):
    depth = w_in.shape[0]
    assert depth == 1 and lb_logits.shape[0] == 2
    past_len = 1024
    weights = (lb_logits, _col_blocks(w_in[0]), w_pool[0], pool_scale, hg_norm, _col_blocks(w_out[0]),
               norm_pre, norm_post, _col_blocks(w_ple[0]), _col_blocks(w_ple_gate[0]))
    y_p, pool_p, hg_p = _prompt_call(x_prompt, p_prompt[0], *weights)
    y_s, pool_s, hg_s = _sample_call(past_len, x_sample, p_sample[0], cache_pool[0], state_hgrn[0], *weights)
    return (y_p, y_s, pool_p[None], hg_p[None], pool_s[None], hg_s[None])
```

```python
import functools

import jax
import jax.numpy as jnp
from jax import lax
from jax.experimental import pallas as pl
from jax.experimental.pallas import tpu as pltpu

D_MODEL = 1024
D_POOL = 512
POOL_WINDOWS = (2, 4, 8, 16)
POOL_GROUP = 128
POOL_BUF = 15
N_HEADS = 4
HEAD = 128
D_HG = N_HEADS * HEAD
D_IN = 2 * D_POOL + 4 * D_HG
D_PLE = 256
EPS = 1e-6
LOG2_E = 1.4426950408889634
COLS = 512
BLK = 16
CHUNK = 128
HDR = 16
PROMPT_TILE = 256
VMEM_LIMIT_BYTES = 48 * 1024 * 1024
PROMPT_SCHEDULER_FLAGS = None

C_U, C_GP, C_Q, C_F, C_V, C_GH = 0, 512, 1024, 1536, 2048, 2560

_NT = (((1,), (1,)), ((), ()))
_TN = (((0,), (0,)), ((), ()))


def _dot(a, b):
    return jnp.dot(a, b, preferred_element_type=jnp.float32)


def _dot_nt(a, b):
    return lax.dot_general(a, b, _NT, preferred_element_type=jnp.float32)


def _dot_tn(a, b):
    return lax.dot_general(a, b, _TN, preferred_element_type=jnp.float32)


def _bf(x):
    return x.astype(jnp.bfloat16)


def _wblk(w_ref, j):
    return w_ref[j]


def _sigmoid(x):
    return 1.0 / (1.0 + jnp.exp2(x * (-LOG2_E)))


def _rms_scale(x):
    return lax.rsqrt(jnp.mean(x * x, axis=-1, keepdims=True) + EPS)


def _lower_bound(lb_logits):
    l0 = lb_logits[0:1, :]
    l1 = lb_logits[1:2, :]
    m = jnp.maximum(l0, l1)
    e0 = jnp.exp(l0 - m)
    e1 = jnp.exp(l1 - m)
    return e0 / (e0 + e1)


def _window_sum(e, w):
    s = e
    d = 1
    while d < w:
        s = s + pltpu.roll(s, d, axis=0)
        d *= 2
    return s


def _pool_group(e, w, first_pos):
    s = _window_sum(e, w)[HDR:]
    u = e[HDR:]
    rows = lax.broadcasted_iota(jnp.int32, (HDR, POOL_GROUP), 0)
    cnt = jnp.minimum(w, first_pos + rows + 1).astype(jnp.float32)
    head = s[:HDR] / cnt - u[:HDR]
    if s.shape[0] == HDR:
        return head
    tail = s[HDR:] * (1.0 / w) - u[HDR:]
    return jnp.concatenate([head, tail], axis=0)


def _block_cumprod(f):
    row = lax.broadcasted_iota(jnp.int32, f.shape, 0) & (BLK - 1)
    x = f
    for d in (1, 2, 4, 8):
        x = x * jnp.where(row >= d, pltpu.roll(x, d, axis=0), 1.0)
    return x


def _gates(q, fl, lb):
    f = lb + (1.0 - lb) * _sigmoid(fl)
    k = 1.0 - f
    F = _block_cumprod(f)
    return q * F, k / F, F


def _stack_blocks(pieces, n_blocks):
    zero = jnp.zeros((BLK, HEAD), jnp.float32)
    return jnp.concatenate([pieces.get(j, zero) for j in range(n_blocks)], axis=0)


class _HgrnChunk:
    def __init__(self, load, lb):
        self.load, self.lb = load, lb

    def prepare(self):
        q, fl, v = self.load()
        c = self.c = q.shape[0]
        nb = c // BLK
        qe, ke, F = _gates(q, fl, self.lb)
        T = [F[BLK * j + BLK - 1:BLK * j + BLK, :] for j in range(nb)]
        qe_b = [qe[BLK * j:BLK * (j + 1)] for j in range(nb)]
        kd_b = [ke[BLK * j:BLK * (j + 1)] * T[j] for j in range(nb)]
        kd = jnp.concatenate(kd_b, axis=0)
        self.qe = _bf(qe)
        self.ke_kd = _bf(jnp.concatenate([ke, kd], axis=0))

        q_slots, k_slots = [], []
        gs = 4
        while gs <= nb:
            for a in range(0, nb, gs):
                mid = a + gs // 2
                qs, ks = {}, {}
                dec = None
                for j in range(mid, a + gs):
                    qs[j] = qe_b[j] if dec is None else qe_b[j] * dec
                    dec = T[j] if dec is None else dec * T[j]
                dec = None
                for j in range(mid - 1, a - 1, -1):
                    ks[j] = kd_b[j] if dec is None else kd_b[j] * dec
                    dec = T[j] if dec is None else dec * T[j]
                q_slots.append(_stack_blocks(qs, nb))
                k_slots.append(_stack_blocks(ks, nb))
            gs *= 2
        self.q_far = _bf(jnp.concatenate(q_slots, axis=1)) if q_slots else None
        self.k_far = _bf(jnp.concatenate(k_slots, axis=1)) if k_slots else None

        e_in = [None] * nb
        dec = None
        for j in range(nb):
            e_in[j] = dec
            dec = T[j] if dec is None else dec * T[j]
        self.e_tot = dec
        d_out = [None] * nb
        dec = None
        for j in range(nb - 1, -1, -1):
            d_out[j] = dec
            dec = T[j] if dec is None else dec * T[j]
        self.q_in = _bf(jnp.concatenate(
            [qe_b[j] if e_in[j] is None else qe_b[j] * e_in[j] for j in range(nb)], axis=0))
        self.k_out = _bf(jnp.concatenate(
            [kd_b[j] if d_out[j] is None else kd_b[j] * d_out[j] for j in range(nb)], axis=0))
        self.v_bf = _bf(v)

    def issue(self):
        self.r1 = _dot_nt(self.qe, self.ke_kd)
        self.far = _dot_nt(self.q_far, self.k_far) if self.q_far is not None else None
        self.upd = _dot_tn(self.v_bf, self.k_out)

    def combine(self, st, mask_d, mask_p):
        c = self.c
        p = jnp.where(mask_d, self.r1[:, :c], jnp.where(mask_p, self.r1[:, c:], 0.0))
        if self.far is not None:
            p = p + self.far
        self.o = _dot(_bf(p), self.v_bf) + _dot_nt(self.q_in, _bf(st))
        return st * self.e_tot + self.upd


def _head_out(o, gate, hg_norm):
    y = o * _rms_scale(o) * hg_norm
    return y * (gate * _sigmoid(gate))


def _post_norm(x, y_blocks, norm_post):
    ms = sum(jnp.sum(y * y, axis=-1, keepdims=True) for y in y_blocks) / D_MODEL
    r = lax.rsqrt(ms + EPS)
    return [x[:, j * COLS:(j + 1) * COLS] + y * r * norm_post[:, j * COLS:(j + 1) * COLS]
            for j, y in enumerate(y_blocks)]


def _finish(x, mix_bf, p_bf, w_out_ref, w_ple_ref, w_gate_ref, norm_post):
    nblk = D_MODEL // COLS
    x1 = _post_norm(x, [_dot(mix_bf, _wblk(w_out_ref, j)) for j in range(nblk)], norm_post)
    x1_bf = _bf(jnp.concatenate(x1, axis=1))
    out = [x1[j] + _sigmoid(_dot(x1_bf, _wblk(w_gate_ref, j))) * _dot(p_bf, _wblk(w_ple_ref, j)) for j in range(nblk)]
    return jnp.concatenate(out, axis=1)


def _chunk_masks(c):
    t = lax.broadcasted_iota(jnp.int32, (c, c), 0)
    s = lax.broadcasted_iota(jnp.int32, (c, c), 1)
    tb = t >> 4
    sb = s >> 4
    mask_d = (tb == sb) & (s <= t)
    mask_p = ((tb & 1) == 1) & (sb == tb - 1)
    return mask_d, mask_p


def _prompt_step(first_pos, xa_ref, xc_ref, p_ref, lbl_ref, w_in_ref, w_pool_ref, scale_ref, hgn_ref, w_out_ref,
                 npre_ref, npost_ref, w_ple_ref, w_gate_ref, y_ref, carry_ref, st_ref,
                 u_w, z_w, mix_w, u_r, z_r, mix_r):
    tt = xa_ref.shape[0]
    lb = _lower_bound(lbl_ref[...])
    mask_d, mask_p = _chunk_masks(CHUNK)
    states = [st_ref[hd] for hd in range(N_HEADS)]

    def in_proj(h, c0):
        blk = _dot(h, _wblk(w_in_ref, c0 // COLS))
        if c0 == C_U:
            u_w[HDR:HDR + tt, :] = blk
        else:
            z_w[:, c0 - C_GP:c0 - C_GP + D_POOL] = blk

    def pool_group(gi):
        cs = slice(gi * POOL_GROUP, (gi + 1) * POOL_GROUP)
        pooled = _pool_group(u_r[:, cs], POOL_WINDOWS[gi], first_pos)
        mixed = _dot(_bf(pooled), _bf(w_pool_ref[gi])) * scale_ref[:, cs]
        gp = z_r[:, cs]
        mix_w[:, cs] = _bf(mixed * (gp * _sigmoid(gp)))

    def hgrn_unit(hd, c0):
        hs = slice(hd * HEAD, (hd + 1) * HEAD)
        rs = slice(c0, c0 + CHUNK)
        col = lambda base: slice(base - C_GP + hd * HEAD, base - C_GP + (hd + 1) * HEAD)
        unit = _HgrnChunk(lambda: (z_r[rs, col(C_Q)], z_r[rs, col(C_F)], z_r[rs, col(C_V)]), lb[:, hs])
        unit.head, unit.rows, unit.gate_cols = hd, rs, col(C_GH)
        return unit

    def hgrn_store(unit):
        gh = z_r[unit.rows, unit.gate_cols]
        mix_w[unit.rows, D_POOL + unit.head * HEAD:D_POOL + (unit.head + 1) * HEAD] = _bf(
            _head_out(unit.o, gh, hgn_ref[...]))

    units = [hgrn_unit(hd, c0) for c0 in range(0, tt, CHUNK) for hd in range(N_HEADS)]
    dense = [("out", 1), ("in", C_U), ("gate", 0), ("gate", 1), ("in", C_GP), ("in", C_Q), ("in", C_F),
             ("in", C_V)]
    pool_at = {1: (0,), 4: (1,), 6: (2,), 7: (3,)}
    final_at = {5: (0,), 6: (1,)}
    p_bf = _bf(p_ref[...])
    ple = [_dot(p_bf, _wblk(w_ple_ref, j)) for j in range(2)]
    xa = xa_ref[...]
    h = _bf(xa * _rms_scale(xa) * npre_ref[...])
    units[0].prepare()
    mix_prev = mix_r[...]
    y = {0: _dot(mix_prev, _wblk(w_out_ref, 0))}
    u_r[0:HDR, :] = carry_ref[...]
    x1 = x1_bf = None
    gate = {}
    last = len(units) - 1
    assert len(units) == len(dense)
    for i, unit in enumerate(units):
        unit.issue()
        if i < last:
            units[i + 1].prepare()
        kind, arg = dense[i]
        if kind == "in":
            in_proj(h, arg)
        elif kind == "out":
            y[arg] = _dot(mix_prev, _wblk(w_out_ref, arg))
        else:
            gate[arg] = _dot(x1_bf, _wblk(w_gate_ref, arg))
        states[unit.head] = unit.combine(states[unit.head], mask_d, mask_p)
        if i == 0:
            x1 = _post_norm(xc_ref[...], [y[0], y[1]], npost_ref[...])
            x1_bf = _bf(jnp.concatenate(x1, axis=1))
        else:
            hgrn_store(units[i - 1])
        for gi in pool_at.get(i, ()):
            pool_group(gi)
        for j in final_at.get(i, ()):
            y_ref[:, j * COLS:(j + 1) * COLS] = x1[j] + _sigmoid(gate[j]) * ple[j]
    in_proj(h, C_GH)
    hgrn_store(units[last])
    carry_ref[...] = u_r[tt:tt + HDR, :]
    for hd in range(N_HEADS):
        st_ref[hd] = states[hd]


def _prompt_kernel(tiles_per_stream, xa_ref, xc_ref, p_ref, lbl_ref, w_in_ref, w_pool_ref, scale_ref, hgn_ref,
                   w_out_ref, npre_ref, npost_ref, w_ple_ref, w_gate_ref,
                   y_ref, pool_ref, hg_ref,
                   u_a, z_a, mix_a, u_b, z_b, mix_b, carry_ref, st_ref):
    s = pl.program_id(0)
    tt = xa_ref.shape[0]
    tile = lax.rem(s - 1 + tiles_per_stream, tiles_per_stream)

    @pl.when(s == 0)
    def _():
        for ref in (u_a, z_a, mix_a, u_b, z_b, mix_b, carry_ref, st_ref):
            ref[...] = jnp.zeros(ref.shape, ref.dtype)

    @pl.when(tile == 0)
    def _():
        carry_ref[...] = jnp.zeros(carry_ref.shape, jnp.float32)
        st_ref[...] = jnp.zeros(st_ref.shape, jnp.float32)

    step = functools.partial(_prompt_step, tile * tt, xa_ref, xc_ref, p_ref, lbl_ref, w_in_ref, w_pool_ref,
                             scale_ref, hgn_ref, w_out_ref, npre_ref, npost_ref, w_ple_ref, w_gate_ref,
                             y_ref, carry_ref, st_ref)

    @pl.when((s & 1) == 0)
    def _():
        step(u_a, z_a, mix_a, u_b, z_b, mix_b)

    @pl.when((s & 1) == 1)
    def _():
        step(u_b, z_b, mix_b, u_a, z_a, mix_a)

    @pl.when((tile == tiles_per_stream - 1) & (s > 0))
    def _():
        pool_ref[...] = carry_ref[HDR - POOL_BUF:HDR, :]
        for hd in range(N_HEADS):
            hg_ref[hd] = st_ref[hd].T


def _const_spec(shape):
    nd = len(shape)
    return pl.BlockSpec(shape, lambda *_: (0,) * nd, pipeline_mode=pl.Buffered(1))


def _prompt_call(x, p, lbl, w_in, w_pool, scale, hgn, w_out, npre, npost, w_ple, w_gate):
    b, t, _ = x.shape
    tt = PROMPT_TILE
    assert t % tt == 0 and tt % CHUNK == 0
    nt = t // tt
    g = b * nt
    weights = (lbl, w_in, w_pool, scale, hgn, w_out, npre, npost, w_ple, w_gate)
    x2 = x.reshape(b * t, D_MODEL)
    p2 = p.reshape(b * t, D_PLE)
    ahead = lambda s: (jnp.minimum(s, g - 1), 0)
    behind = lambda s: (jnp.clip(s - 2, 0, g - 1), 0)
    stream = lambda s: (jnp.clip((s - 1) // nt, 0, b - 1),)
    y, pool, hg = pl.pallas_call(
        functools.partial(_prompt_kernel, nt),
        grid=(g + 2,),
        in_specs=[pl.BlockSpec((tt, D_MODEL), ahead),
                  pl.BlockSpec((tt, D_MODEL), behind),
                  pl.BlockSpec((tt, D_PLE), behind)]
                 + [_const_spec(w.shape) for w in weights],
        out_specs=[pl.BlockSpec((tt, D_MODEL), behind),
                   pl.BlockSpec((None, POOL_BUF, D_POOL), lambda s: stream(s) + (0, 0)),
                   pl.BlockSpec((None, N_HEADS, HEAD, HEAD), lambda s: stream(s) + (0, 0, 0))],
        out_shape=[jax.ShapeDtypeStruct((b * t, D_MODEL), jnp.float32),
                   jax.ShapeDtypeStruct((b, POOL_BUF, D_POOL), jnp.float32),
                   jax.ShapeDtypeStruct((b, N_HEADS, HEAD, HEAD), jnp.float32)],
        scratch_shapes=[pltpu.VMEM((HDR + tt, D_POOL), jnp.float32),
                        pltpu.VMEM((tt, D_IN - D_POOL), jnp.float32),
                        pltpu.VMEM((tt, D_MODEL), jnp.bfloat16)] * 2
                       + [pltpu.VMEM((HDR, D_POOL), jnp.float32),
                          pltpu.VMEM((N_HEADS, HEAD, HEAD), jnp.float32)],
        compiler_params=pltpu.CompilerParams(dimension_semantics=("arbitrary",),
                                             vmem_limit_bytes=VMEM_LIMIT_BYTES,
                                             flags=PROMPT_SCHEDULER_FLAGS),
        name="prompt_layer",
    )(x2, x2, p2, *weights)
    return y.reshape(b, t, D_MODEL), pool, hg


def _sample_kernel(start_pos, x_ref, p_ref, cache_ref, s0_ref, lbl_ref, w_in_ref, w_pool_ref, scale_ref,
                   hgn_ref, w_out_ref, npre_ref, npost_ref, w_ple_ref, w_gate_ref,
                   y_ref, pool_ref, hg_ref,
                   z_ref, ext_ref, mix_ref, tpad_ref):
    b = pl.program_id(0)
    nb = pl.num_programs(0)
    ts = pool_ref.shape[0] + 1
    rows = pl.ds(pl.multiple_of(b * ts, ts), ts)

    @pl.when(b == 0)
    def _():
        x = x_ref[...]
        h = _bf(x * _rms_scale(x) * npre_ref[...])
        for j in range(D_IN // COLS):
            z_ref[:, j * COLS:(j + 1) * COLS] = _dot(h, _wblk(w_in_ref, j))
        tpad_ref[...] = jnp.zeros(tpad_ref.shape, jnp.float32)

    u = z_ref[rows, C_U:C_U + D_POOL]
    ext_ref[0:HDR, :] = jnp.zeros((HDR, D_POOL), jnp.float32)
    ext_ref[HDR - POOL_BUF:HDR, :] = cache_ref[...]
    ext_ref[HDR:HDR + ts, :] = u
    pool_ref[...] = u[ts - POOL_BUF:ts, :]
    for gi, w in enumerate(POOL_WINDOWS):
        cs = slice(gi * POOL_GROUP, (gi + 1) * POOL_GROUP)
        pooled = _pool_group(ext_ref[:, cs], w, start_pos)
        mixed = _dot(_bf(pooled), _bf(w_pool_ref[gi])) * scale_ref[:, cs]
        gp = z_ref[rows, C_GP + gi * POOL_GROUP:C_GP + (gi + 1) * POOL_GROUP]
        mix_ref[rows, cs] = _bf(mixed * (gp * _sigmoid(gp)))

    lb = _lower_bound(lbl_ref[...])
    tri_t = lax.broadcasted_iota(jnp.int32, (ts, ts), 0)
    tri_s = lax.broadcasted_iota(jnp.int32, (ts, ts), 1)
    gated = []
    for hd in range(N_HEADS):
        hs = slice(hd * HEAD, (hd + 1) * HEAD)
        q = z_ref[rows, C_Q + hd * HEAD:C_Q + (hd + 1) * HEAD]
        fl = z_ref[rows, C_F + hd * HEAD:C_F + (hd + 1) * HEAD]
        qe, ke, F = _gates(q, fl, lb[:, hs])
        gated.append((qe, ke, F))
        tpad_ref[0:1, hs] = F[ts - 1:ts, :]
    for hd in range(N_HEADS):
        hs = slice(hd * HEAD, (hd + 1) * HEAD)
        qe, ke, F = gated[hd]
        v = _bf(z_ref[rows, C_V + hd * HEAD:C_V + (hd + 1) * HEAD])
        gh = z_ref[rows, C_GH + hd * HEAD:C_GH + (hd + 1) * HEAD]
        s0 = s0_ref[hd]
        att = jnp.where(tri_s <= tri_t, _dot_nt(_bf(qe), _bf(ke)), 0.0)
        o = _dot(_bf(att), v) + _dot(_bf(qe), _bf(s0))
        mix_ref[rows, D_POOL + hd * HEAD:D_POOL + (hd + 1) * HEAD] = _bf(_head_out(o, gh, hgn_ref[...]))
        t_col = tpad_ref[:, hs].T[:, 0:1]
        kd = ke * F[ts - 1:ts, :]
        hg_ref[hd] = s0 * t_col + _dot_tn(_bf(kd), v)

    @pl.when(b == nb - 1)
    def _():
        y_ref[...] = _finish(x_ref[...], mix_ref[...], _bf(p_ref[...]), w_out_ref, w_ple_ref, w_gate_ref,
                             npost_ref[...])


def _sample_call(start_pos, x, p, cache, s0, lbl, w_in, w_pool, scale, hgn, w_out, npre, npost, w_ple, w_gate):
    b, ts, _ = x.shape
    assert ts == BLK and ts == POOL_BUF + 1
    n = b * ts
    weights = (lbl, w_in, w_pool, scale, hgn, w_out, npre, npost, w_ple, w_gate)
    y, pool, hg = pl.pallas_call(
        functools.partial(_sample_kernel, start_pos),
        grid=(b,),
        in_specs=[_const_spec((n, D_MODEL)), _const_spec((n, D_PLE)),
                  pl.BlockSpec((None, POOL_BUF, D_POOL), lambda i: (i, 0, 0)),
                  pl.BlockSpec((None, N_HEADS, HEAD, HEAD), lambda i: (i, 0, 0, 0))]
                 + [_const_spec(w.shape) for w in weights],
        out_specs=[pl.BlockSpec((n, D_MODEL), lambda i: (0, 0)),
                   pl.BlockSpec((None, POOL_BUF, D_POOL), lambda i: (i, 0, 0)),
                   pl.BlockSpec((None, N_HEADS, HEAD, HEAD), lambda i: (i, 0, 0, 0))],
        out_shape=[jax.ShapeDtypeStruct((n, D_MODEL), jnp.float32),
                   jax.ShapeDtypeStruct((b, POOL_BUF, D_POOL), jnp.float32),
                   jax.ShapeDtypeStruct((b, N_HEADS, HEAD, HEAD), jnp.float32)],
        scratch_shapes=[pltpu.VMEM((n, D_IN), jnp.float32),
                        pltpu.VMEM((HDR + ts, D_POOL), jnp.float32),
                        pltpu.VMEM((n, D_MODEL), jnp.bfloat16),
                        pltpu.VMEM((HEAD, D_HG), jnp.float32)],
        compiler_params=pltpu.CompilerParams(dimension_semantics=("arbitrary",),
                                             vmem_limit_bytes=VMEM_LIMIT_BYTES),
        name="sample_layer",
    )(x.reshape(n, D_MODEL), p.reshape(n, D_PLE), cache, s0, *weights)
    return y.reshape(b, ts, D_MODEL), pool, hg


def _cast_kernel(w_ref, o_ref):
    o_ref[...] = _bf(w_ref[...])


def _col_blocks(w):
    k, n = w.shape
    return pl.pallas_call(
        _cast_kernel,
        grid=(n // COLS,),
        in_specs=[pl.BlockSpec((k, COLS), lambda j: (0, j))],
        out_specs=pl.BlockSpec((None, k, COLS), lambda j: (j, 0, 0)),
        out_shape=jax.ShapeDtypeStruct((n // COLS, k, COLS), jnp.bfloat16),
        compiler_params=pltpu.CompilerParams(dimension_semantics=("arbitrary",)),
        name="weight_blocks",
    )(w)


def kernel(x_prompt, x_sample, cache_pool, state_hgrn, p_prompt, p_sample, lb_logits, w_in, w_pool, pool_scale,
           hg_norm, w_out, norm_pre, norm_post, w_ple, w_ple_gate):
    depth = w_in.shape[0]
    assert depth == 1 and lb_logits.shape[0] == 2
    past_len = 1024
    weights = (lb_logits, _col_blocks(w_in[0]), w_pool[0], pool_scale, hg_norm, _col_blocks(w_out[0]),
               norm_pre, norm_post, _col_blocks(w_ple[0]), _col_blocks(w_ple_gate[0]))
    y_p, pool_p, hg_p = _prompt_call(x_prompt, p_prompt[0], *weights)
    y_s, pool_s, hg_s = _sample_call(past_len, x_sample, p_sample[0], cache_pool[0], state_hgrn[0], *weights)
    return (y_p, y_s, pool_p[None], hg_p[None], pool_s[None], hg_s[None])
```

```python
import functools

import jax
import jax.numpy as jnp
from jax import lax
from jax.experimental import pallas as pl
from jax.experimental.pallas import tpu as pltpu

D_MODEL = 1024
D_POOL = 512
POOL_WINDOWS = (2, 4, 8, 16)
POOL_GROUP = 128
POOL_BUF = 15
N_HEADS = 4
HEAD = 128
D_HG = N_HEADS * HEAD
D_IN = 2 * D_POOL + 4 * D_HG
D_PLE = 256
EPS = 1e-6
LOG2_E = 1.4426950408889634
COLS = 512
BLK = 16
CHUNK = 128
HDR = 16
PROMPT_TILE = 256
VMEM_LIMIT_BYTES = 48 * 1024 * 1024
PROMPT_SCHEDULER_FLAGS = None

C_U, C_GP, C_Q, C_F, C_V, C_GH = 0, 512, 1024, 1536, 2048, 2560

_NT = (((1,), (1,)), ((), ()))
_TN = (((0,), (0,)), ((), ()))


def _dot(a, b):
    return jnp.dot(a, b, preferred_element_type=jnp.float32)


def _dot_nt(a, b):
    return lax.dot_general(a, b, _NT, preferred_element_type=jnp.float32)


def _dot_tn(a, b):
    return lax.dot_general(a, b, _TN, preferred_element_type=jnp.float32)


def _bf(x):
    return x.astype(jnp.bfloat16)


def _wblk(w_ref, j):
    return w_ref[j]


def _sigmoid(x):
    return 1.0 / (1.0 + jnp.exp2(x * (-LOG2_E)))


def _rms_scale(x):
    return lax.rsqrt(jnp.mean(x * x, axis=-1, keepdims=True) + EPS)


def _lower_bound(lb_logits):
    l0 = lb_logits[0:1, :]
    l1 = lb_logits[1:2, :]
    m = jnp.maximum(l0, l1)
    e0 = jnp.exp(l0 - m)
    e1 = jnp.exp(l1 - m)
    return e0 / (e0 + e1)


def _window_sum(e, w):
    s = e
    d = 1
    while d < w:
        s = s + pltpu.roll(s, d, axis=0)
        d *= 2
    return s


def _pool_group(e, w, first_pos):
    s = _window_sum(e, w)[HDR:]
    u = e[HDR:]
    rows = lax.broadcasted_iota(jnp.int32, (HDR, POOL_GROUP), 0)
    cnt = jnp.minimum(w, first_pos + rows + 1).astype(jnp.float32)
    head = s[:HDR] / cnt - u[:HDR]
    if s.shape[0] == HDR:
        return head
    tail = s[HDR:] * (1.0 / w) - u[HDR:]
    return jnp.concatenate([head, tail], axis=0)


def _block_cumprod(f):
    row = lax.broadcasted_iota(jnp.int32, f.shape, 0) & (BLK - 1)
    x = f
    for d in (1, 2, 4, 8):
        x = x * jnp.where(row >= d, pltpu.roll(x, d, axis=0), 1.0)
    return x


def _gates(q, fl, lb):
    f = lb + (1.0 - lb) * _sigmoid(fl)
    k = 1.0 - f
    F = _block_cumprod(f)
    return q * F, k / F, F


def _as_column(row):
    tile = jnp.concatenate([row, jnp.zeros((7, row.shape[1]), row.dtype)], axis=0)
    return tile.T[:, 0:1]


def _stack_blocks(pieces, n_blocks):
    zero = jnp.zeros((BLK, HEAD), jnp.float32)
    return jnp.concatenate([pieces.get(j, zero) for j in range(n_blocks)], axis=0)


class _HgrnChunk:
    def __init__(self, load, lb):
        self.load, self.lb = load, lb

    def prepare(self):
        q, fl, v = self.load()
        c = self.c = q.shape[0]
        nb = c // BLK
        qe, ke, F = _gates(q, fl, self.lb)
        T = [F[BLK * j + BLK - 1:BLK * j + BLK, :] for j in range(nb)]
        qe_b = [qe[BLK * j:BLK * (j + 1)] for j in range(nb)]
        kd_b = [ke[BLK * j:BLK * (j + 1)] * T[j] for j in range(nb)]
        kd = jnp.concatenate(kd_b, axis=0)
        self.qe = _bf(qe)
        self.ke_kd = _bf(jnp.concatenate([ke, kd], axis=0))

        q_slots, k_slots = [], []
        gs = 4
        while gs <= nb:
            for a in range(0, nb, gs):
                mid = a + gs // 2
                qs, ks = {}, {}
                dec = None
                for j in range(mid, a + gs):
                    qs[j] = qe_b[j] if dec is None else qe_b[j] * dec
                    dec = T[j] if dec is None else dec * T[j]
                dec = None
                for j in range(mid - 1, a - 1, -1):
                    ks[j] = kd_b[j] if dec is None else kd_b[j] * dec
                    dec = T[j] if dec is None else dec * T[j]
                q_slots.append(_stack_blocks(qs, nb))
                k_slots.append(_stack_blocks(ks, nb))
            gs *= 2
        self.q_far = _bf(jnp.concatenate(q_slots, axis=1)) if q_slots else None
        self.k_far = _bf(jnp.concatenate(k_slots, axis=1)) if k_slots else None

        e_in = [None] * nb
        dec = None
        for j in range(nb):
            e_in[j] = dec
            dec = T[j] if dec is None else dec * T[j]
        self.e_tot = dec
        d_out = [None] * nb
        dec = None
        for j in range(nb - 1, -1, -1):
            d_out[j] = dec
            dec = T[j] if dec is None else dec * T[j]
        self.q_in = _bf(jnp.concatenate(
            [qe_b[j] if e_in[j] is None else qe_b[j] * e_in[j] for j in range(nb)], axis=0))
        self.k_out = _bf(jnp.concatenate(
            [kd_b[j] if d_out[j] is None else kd_b[j] * d_out[j] for j in range(nb)], axis=0))
        self.v_bf = _bf(v)

    def issue(self):
        self.r1 = _dot_nt(self.qe, self.ke_kd)
        self.far = _dot_nt(self.q_far, self.k_far) if self.q_far is not None else None
        self.upd = _dot_tn(self.v_bf, self.k_out)

    def combine(self, st, mask_d, mask_p):
        c = self.c
        p = jnp.where(mask_d, self.r1[:, :c], jnp.where(mask_p, self.r1[:, c:], 0.0))
        if self.far is not None:
            p = p + self.far
        self.o = _dot(_bf(p), self.v_bf) + _dot_nt(self.q_in, _bf(st))
        return st * self.e_tot + self.upd


def _head_out(o, gate, hg_norm):
    y = o * _rms_scale(o) * hg_norm
    return y * (gate * _sigmoid(gate))


def _post_norm(x, y_blocks, norm_post):
    ms = sum(jnp.sum(y * y, axis=-1, keepdims=True) for y in y_blocks) / D_MODEL
    r = lax.rsqrt(ms + EPS)
    return [x[:, j * COLS:(j + 1) * COLS] + y * r * norm_post[:, j * COLS:(j + 1) * COLS]
            for j, y in enumerate(y_blocks)]


def _finish(x, mix_bf, p_bf, w_out_ref, w_ple_ref, w_gate_ref, norm_post):
    nblk = D_MODEL // COLS
    x1 = _post_norm(x, [_dot(mix_bf, _wblk(w_out_ref, j)) for j in range(nblk)], norm_post)
    x1_bf = _bf(jnp.concatenate(x1, axis=1))
    out = [x1[j] + _sigmoid(_dot(x1_bf, _wblk(w_gate_ref, j))) * _dot(p_bf, _wblk(w_ple_ref, j)) for j in range(nblk)]
    return jnp.concatenate(out, axis=1)


def _chunk_masks(c):
    t = lax.broadcasted_iota(jnp.int32, (c, c), 0)
    s = lax.broadcasted_iota(jnp.int32, (c, c), 1)
    tb = t >> 4
    sb = s >> 4
    mask_d = (tb == sb) & (s <= t)
    mask_p = ((tb & 1) == 1) & (sb == tb - 1)
    return mask_d, mask_p


def _prompt_step(first_pos, xa_ref, xc_ref, p_ref, lbl_ref, w_in_ref, w_pool_ref, scale_ref, hgn_ref, w_out_ref,
                 npre_ref, npost_ref, w_ple_ref, w_gate_ref, y_ref, carry_ref, st_ref,
                 u_w, z_w, mix_w, u_r, z_r, mix_r):
    tt = xa_ref.shape[0]
    lb = _lower_bound(lbl_ref[...])
    mask_d, mask_p = _chunk_masks(CHUNK)
    states = [st_ref[hd] for hd in range(N_HEADS)]

    def in_proj(h, c0):
        blk = _dot(h, _wblk(w_in_ref, c0 // COLS))
        if c0 == C_U:
            u_w[HDR:HDR + tt, :] = blk
        else:
            z_w[:, c0 - C_GP:c0 - C_GP + D_POOL] = blk

    def pool_group(gi):
        cs = slice(gi * POOL_GROUP, (gi + 1) * POOL_GROUP)
        pooled = _pool_group(u_r[:, cs], POOL_WINDOWS[gi], first_pos)
        mixed = _dot(_bf(pooled), _bf(w_pool_ref[gi])) * scale_ref[:, cs]
        gp = z_r[:, cs]
        mix_w[:, cs] = _bf(mixed * (gp * _sigmoid(gp)))

    def hgrn_unit(hd, c0):
        hs = slice(hd * HEAD, (hd + 1) * HEAD)
        rs = slice(c0, c0 + CHUNK)
        col = lambda base: slice(base - C_GP + hd * HEAD, base - C_GP + (hd + 1) * HEAD)
        unit = _HgrnChunk(lambda: (z_r[rs, col(C_Q)], z_r[rs, col(C_F)], z_r[rs, col(C_V)]), lb[:, hs])
        unit.head, unit.rows, unit.gate_cols = hd, rs, col(C_GH)
        return unit

    def hgrn_store(unit):
        gh = z_r[unit.rows, unit.gate_cols]
        mix_w[unit.rows, D_POOL + unit.head * HEAD:D_POOL + (unit.head + 1) * HEAD] = _bf(
            _head_out(unit.o, gh, hgn_ref[...]))

    units = [hgrn_unit(hd, c0) for c0 in range(0, tt, CHUNK) for hd in range(N_HEADS)]
    dense = [("out", 1), ("in", C_U), ("gate", 0), ("gate", 1), ("in", C_GP), ("in", C_Q), ("in", C_F),
             ("in", C_V)]
    pool_at = {1: (0,), 4: (1,), 6: (2,), 7: (3,)}
    final_at = {5: (0,), 6: (1,)}
    p_bf = _bf(p_ref[...])
    ple = [_dot(p_bf, _wblk(w_ple_ref, j)) for j in range(2)]
    xa = xa_ref[...]
    h = _bf(xa * _rms_scale(xa) * npre_ref[...])
    units[0].prepare()
    mix_prev = mix_r[...]
    y = {0: _dot(mix_prev, _wblk(w_out_ref, 0))}
    u_r[0:HDR, :] = carry_ref[...]
    x1 = x1_bf = None
    gate = {}
    last = len(units) - 1
    assert len(units) == len(dense)
    for i, unit in enumerate(units):
        unit.issue()
        if i < last:
            units[i + 1].prepare()
        kind, arg = dense[i]
        if kind == "in":
            in_proj(h, arg)
        elif kind == "out":
            y[arg] = _dot(mix_prev, _wblk(w_out_ref, arg))
        else:
            gate[arg] = _dot(x1_bf, _wblk(w_gate_ref, arg))
        states[unit.head] = unit.combine(states[unit.head], mask_d, mask_p)
        if i == 0:
            x1 = _post_norm(xc_ref[...], [y[0], y[1]], npost_ref[...])
            x1_bf = _bf(jnp.concatenate(x1, axis=1))
        else:
            hgrn_store(units[i - 1])
        for gi in pool_at.get(i, ()):
            pool_group(gi)
        for j in final_at.get(i, ()):
            y_ref[:, j * COLS:(j + 1) * COLS] = x1[j] + _sigmoid(gate[j]) * ple[j]
    in_proj(h, C_GH)
    hgrn_store(units[last])
    carry_ref[...] = u_r[tt:tt + HDR, :]
    for hd in range(N_HEADS):
        st_ref[hd] = states[hd]


def _prompt_kernel(tiles_per_stream, xa_ref, xc_ref, p_ref, lbl_ref, w_in_ref, w_pool_ref, scale_ref, hgn_ref,
                   w_out_ref, npre_ref, npost_ref, w_ple_ref, w_gate_ref,
                   y_ref, pool_ref, hg_ref,
                   u_a, z_a, mix_a, u_b, z_b, mix_b, carry_ref, st_ref):
    s = pl.program_id(0)
    tt = xa_ref.shape[0]
    tile = lax.rem(s - 1 + tiles_per_stream, tiles_per_stream)

    @pl.when(s == 0)
    def _():
        for ref in (u_a, z_a, mix_a, u_b, z_b, mix_b, carry_ref, st_ref):
            ref[...] = jnp.zeros(ref.shape, ref.dtype)

    @pl.when(tile == 0)
    def _():
        carry_ref[...] = jnp.zeros(carry_ref.shape, jnp.float32)
        st_ref[...] = jnp.zeros(st_ref.shape, jnp.float32)

    step = functools.partial(_prompt_step, tile * tt, xa_ref, xc_ref, p_ref, lbl_ref, w_in_ref, w_pool_ref,
                             scale_ref, hgn_ref, w_out_ref, npre_ref, npost_ref, w_ple_ref, w_gate_ref,
                             y_ref, carry_ref, st_ref)

    @pl.when((s & 1) == 0)
    def _():
        step(u_a, z_a, mix_a, u_b, z_b, mix_b)

    @pl.when((s & 1) == 1)
    def _():
        step(u_b, z_b, mix_b, u_a, z_a, mix_a)

    @pl.when((tile == tiles_per_stream - 1) & (s > 0))
    def _():
        pool_ref[...] = carry_ref[HDR - POOL_BUF:HDR, :]
        for hd in range(N_HEADS):
            hg_ref[hd] = st_ref[hd].T


def _const_spec(shape):
    nd = len(shape)
    return pl.BlockSpec(shape, lambda *_: (0,) * nd, pipeline_mode=pl.Buffered(1))


def _prompt_call(x, p, lbl, w_in, w_pool, scale, hgn, w_out, npre, npost, w_ple, w_gate):
    b, t, _ = x.shape
    tt = PROMPT_TILE
    assert t % tt == 0 and tt % CHUNK == 0
    nt = t // tt
    g = b * nt
    weights = (lbl, w_in, w_pool, scale, hgn, w_out, npre, npost, w_ple, w_gate)
    x2 = x.reshape(b * t, D_MODEL)
    p2 = p.reshape(b * t, D_PLE)
    ahead = lambda s: (jnp.minimum(s, g - 1), 0)
    behind = lambda s: (jnp.clip(s - 2, 0, g - 1), 0)
    stream = lambda s: (jnp.clip((s - 1) // nt, 0, b - 1),)
    y, pool, hg = pl.pallas_call(
        functools.partial(_prompt_kernel, nt),
        grid=(g + 2,),
        in_specs=[pl.BlockSpec((tt, D_MODEL), ahead),
                  pl.BlockSpec((tt, D_MODEL), behind),
                  pl.BlockSpec((tt, D_PLE), behind)]
                 + [_const_spec(w.shape) for w in weights],
        out_specs=[pl.BlockSpec((tt, D_MODEL), behind),
                   pl.BlockSpec((None, POOL_BUF, D_POOL), lambda s: stream(s) + (0, 0)),
                   pl.BlockSpec((None, N_HEADS, HEAD, HEAD), lambda s: stream(s) + (0, 0, 0))],
        out_shape=[jax.ShapeDtypeStruct((b * t, D_MODEL), jnp.float32),
                   jax.ShapeDtypeStruct((b, POOL_BUF, D_POOL), jnp.float32),
                   jax.ShapeDtypeStruct((b, N_HEADS, HEAD, HEAD), jnp.float32)],
        scratch_shapes=[pltpu.VMEM((HDR + tt, D_POOL), jnp.float32),
                        pltpu.VMEM((tt, D_IN - D_POOL), jnp.float32),
                        pltpu.VMEM((tt, D_MODEL), jnp.bfloat16)] * 2
                       + [pltpu.VMEM((HDR, D_POOL), jnp.float32),
                          pltpu.VMEM((N_HEADS, HEAD, HEAD), jnp.float32)],
        compiler_params=pltpu.CompilerParams(dimension_semantics=("arbitrary",),
                                             vmem_limit_bytes=VMEM_LIMIT_BYTES,
                                             flags=PROMPT_SCHEDULER_FLAGS),
        name="prompt_layer",
    )(x2, x2, p2, *weights)
    return y.reshape(b, t, D_MODEL), pool, hg


STREAMS_PER_ITER = 8
SAMPLE_VMEM_LIMIT_BYTES = 58 * 1024 * 1024


def _sample_kernel(start_pos, x_ref, p_ref, cache_ref, s0_ref, lbl_ref, w_in_ref, w_pool_ref, scale_ref,
                   hgn_ref, w_out_ref, npre_ref, npost_ref, w_ple_ref, w_gate_ref,
                   y_ref, pool_ref, hg_ref, w_in_b, w_out_b, w_ple_b, w_gate_b,
                   z_ref, ext_ref, mix_ref):
    n_streams, ts = pool_ref.shape[0], pool_ref.shape[1] + 1

    for src, dst in ((w_in_ref, w_in_b), (w_out_ref, w_out_b), (w_ple_ref, w_ple_b), (w_gate_ref, w_gate_b)):
        for j in range(dst.shape[0]):
            dst[j] = _bf(src[:, j * COLS:(j + 1) * COLS])

    x = x_ref[...]
    h = _bf(x * _rms_scale(x) * npre_ref[...])
    for j in range(D_IN // COLS):
        z_ref[:, j * COLS:(j + 1) * COLS] = _dot(h, w_in_b[j])

    lb = _lower_bound(lbl_ref[...])
    causal = (lax.broadcasted_iota(jnp.int32, (ts, ts), 1) <= lax.broadcasted_iota(jnp.int32, (ts, ts), 0))
    w_pool = [_bf(w_pool_ref[gi]) for gi in range(len(POOL_WINDOWS))]

    def prepare(b, slot):
        c = {"b": b, "rows": pl.ds(pl.multiple_of(b * ts, ts), ts)}
        rows = c["rows"]
        ext = ext_ref.at[slot]
        u = z_ref[rows, C_U:C_U + D_POOL]
        ext[0:HDR, :] = jnp.zeros((HDR, D_POOL), jnp.float32)
        ext[HDR - POOL_BUF:HDR, :] = cache_ref[b]
        ext[HDR:HDR + ts, :] = u
        pool_ref[b] = u[ts - POOL_BUF:ts, :]
        c["pooled"] = [_bf(_pool_group(ext[:, gi * POOL_GROUP:(gi + 1) * POOL_GROUP], w, start_pos))
                       for gi, w in enumerate(POOL_WINDOWS)]
        gated = []
        for hd in range(N_HEADS):
            q = z_ref[rows, C_Q + hd * HEAD:C_Q + (hd + 1) * HEAD]
            fl = z_ref[rows, C_F + hd * HEAD:C_F + (hd + 1) * HEAD]
            gated.append(_gates(q, fl, lb[:, hd * HEAD:(hd + 1) * HEAD]))
        decay_rows = [F[ts - 1:ts, :] for _, _, F in gated]
        c["decay_cols"] = jnp.concatenate(
            decay_rows + [jnp.zeros((8 - N_HEADS, HEAD), jnp.float32)], axis=0).T
        c["qe"] = [_bf(qe) for qe, _, _ in gated]
        c["ke"] = [_bf(ke) for _, ke, _ in gated]
        c["kd"] = [_bf(ke * decay_rows[hd]) for hd, (_, ke, _) in enumerate(gated)]
        c["v"] = [_bf(z_ref[rows, C_V + hd * HEAD:C_V + (hd + 1) * HEAD]) for hd in range(N_HEADS)]
        return c

    def issue(c):
        c["mixed"] = [_dot(c["pooled"][gi], w_pool[gi]) for gi in range(len(POOL_WINDOWS))]
        c["att"] = [_dot_nt(c["qe"][hd], c["ke"][hd]) for hd in range(N_HEADS)]
        c["upd"] = [_dot_tn(c["kd"][hd], c["v"][hd]) for hd in range(N_HEADS)]

    def combine(c):
        b, rows = c["b"], c["rows"]
        for gi in range(len(POOL_WINDOWS)):
            cs = slice(gi * POOL_GROUP, (gi + 1) * POOL_GROUP)
            gp = z_ref[rows, C_GP + gi * POOL_GROUP:C_GP + (gi + 1) * POOL_GROUP]
            mix_ref[rows, cs] = _bf(c["mixed"][gi] * scale_ref[:, cs] * (gp * _sigmoid(gp)))
        c["o"] = []
        for hd in range(N_HEADS):
            s0 = s0_ref[b, hd]
            att = _bf(jnp.where(causal, c["att"][hd], 0.0))
            c["o"].append(_dot(att, c["v"][hd]) + _dot(c["qe"][hd], _bf(s0)))
            hg_ref[b, hd] = s0 * c["decay_cols"][:, hd:hd + 1] + c["upd"][hd]

    def store(c):
        rows = c["rows"]
        for hd in range(N_HEADS):
            gh = z_ref[rows, C_GH + hd * HEAD:C_GH + (hd + 1) * HEAD]
            mix_ref[rows, D_POOL + hd * HEAD:D_POOL + (hd + 1) * HEAD] = _bf(
                _head_out(c["o"][hd], gh, hgn_ref[...]))

    def stream_group(g, carry):
        ctx = [prepare(g * STREAMS_PER_ITER + slot, slot) for slot in range(STREAMS_PER_ITER)]
        for phase in (issue, combine, store):
            for c in ctx:
                phase(c)
        return carry

    lax.fori_loop(0, n_streams // STREAMS_PER_ITER, stream_group, 0)

    y_ref[...] = _finish(x, mix_ref[...], _bf(p_ref[...]), w_out_b, w_ple_b, w_gate_b, npost_ref[...])


def _sample_call(start_pos, x, p, cache, s0, lbl, w_in, w_pool, scale, hgn, w_out, npre, npost, w_ple, w_gate):
    b, ts, _ = x.shape
    assert ts == BLK and ts == POOL_BUF + 1 and b % STREAMS_PER_ITER == 0
    n = b * ts
    blocks = lambda w: jax.ShapeDtypeStruct((w.shape[1] // COLS, w.shape[0], COLS), jnp.bfloat16)
    y, pool, hg, w_in_b, w_out_b, w_ple_b, w_gate_b = pl.pallas_call(
        functools.partial(_sample_kernel, start_pos),
        out_shape=[jax.ShapeDtypeStruct((n, D_MODEL), jnp.float32),
                   jax.ShapeDtypeStruct((b, POOL_BUF, D_POOL), jnp.float32),
                   jax.ShapeDtypeStruct((b, N_HEADS, HEAD, HEAD), jnp.float32),
                   blocks(w_in), blocks(w_out), blocks(w_ple), blocks(w_gate)],
        scratch_shapes=[pltpu.VMEM((n, D_IN), jnp.float32),
                        pltpu.VMEM((STREAMS_PER_ITER, HDR + ts, D_POOL), jnp.float32),
                        pltpu.VMEM((n, D_MODEL), jnp.bfloat16)],
        compiler_params=pltpu.CompilerParams(vmem_limit_bytes=SAMPLE_VMEM_LIMIT_BYTES),
        name="sample_layer",
    )(x.reshape(n, D_MODEL), p.reshape(n, D_PLE), cache, s0, lbl, w_in, w_pool, scale, hgn, w_out, npre, npost,
      w_ple, w_gate)
    return (y.reshape(b, ts, D_MODEL), pool, hg), (w_in_b, w_out_b, w_ple_b, w_gate_b)


def kernel(x_prompt, x_sample, cache_pool, state_hgrn, p_prompt, p_sample, lb_logits, w_in, w_pool, pool_scale,
           hg_norm, w_out, norm_pre, norm_post, w_ple, w_ple_gate):
    depth = w_in.shape[0]
    assert depth == 1 and lb_logits.shape[0] == 2
    past_len = 1024
    (y_s, pool_s, hg_s), (w_in_b, w_out_b, w_ple_b, w_gate_b) = _sample_call(
        past_len, x_sample, p_sample[0], cache_pool[0], state_hgrn[0], lb_logits, w_in[0], w_pool[0], pool_scale,
        hg_norm, w_out[0], norm_pre, norm_post, w_ple[0], w_ple_gate[0])
    y_p, pool_p, hg_p = _prompt_call(x_prompt, p_prompt[0], lb_logits, w_in_b, w_pool[0], pool_scale, hg_norm,
                                     w_out_b, norm_pre, norm_post, w_ple_b, w_gate_b)
    return (y_p, y_s, pool_p[None], hg_p[None], pool_s[None], hg_s[None])
```

```python
import functools

import jax
import jax.numpy as jnp
from jax import lax
from jax.experimental import pallas as pl
from jax.experimental.pallas import tpu as pltpu

D_MODEL = 1024
D_POOL = 512
POOL_WINDOWS = (2, 4, 8, 16)
POOL_GROUP = 128
POOL_BUF = 15
N_HEADS = 4
HEAD = 128
D_HG = N_HEADS * HEAD
D_IN = 2 * D_POOL + 4 * D_HG
D_PLE = 256
EPS = 1e-6
LOG2_E = 1.4426950408889634
COLS = 512
BLK = 16
CHUNK = 128
HDR = 16
PROMPT_TILE = 256
VMEM_LIMIT_BYTES = 60 * 1024 * 1024
PROMPT_SCHEDULER_FLAGS = None

C_U, C_GP, C_Q, C_F, C_V, C_GH = 0, 512, 1024, 1536, 2048, 2560

_NT = (((1,), (1,)), ((), ()))
_TN = (((0,), (0,)), ((), ()))


def _dot(a, b):
    return jnp.dot(a, b, preferred_element_type=jnp.float32)


def _dot_nt(a, b):
    return lax.dot_general(a, b, _NT, preferred_element_type=jnp.float32)


def _dot_tn(a, b):
    return lax.dot_general(a, b, _TN, preferred_element_type=jnp.float32)


def _bf(x):
    return x.astype(jnp.bfloat16)


def _wblk(w_ref, j):
    return w_ref[j]


def _sigmoid(x):
    return 1.0 / (1.0 + jnp.exp2(x * (-LOG2_E)))


def _rms_scale(x):
    return lax.rsqrt(jnp.mean(x * x, axis=-1, keepdims=True) + EPS)


def _lower_bound(lb_logits):
    l0 = lb_logits[0:1, :]
    l1 = lb_logits[1:2, :]
    m = jnp.maximum(l0, l1)
    e0 = jnp.exp(l0 - m)
    e1 = jnp.exp(l1 - m)
    return e0 / (e0 + e1)


def _window_sum(e, w):
    s = e
    d = 1
    while d < w:
        s = s + pltpu.roll(s, d, axis=0)
        d *= 2
    return s


def _pool_group(e, w, first_pos):
    s = _window_sum(e, w)[HDR:]
    u = e[HDR:]
    rows = lax.broadcasted_iota(jnp.int32, (HDR, POOL_GROUP), 0)
    cnt = jnp.minimum(w, first_pos + rows + 1).astype(jnp.float32)
    head = s[:HDR] / cnt - u[:HDR]
    if s.shape[0] == HDR:
        return head
    tail = s[HDR:] * (1.0 / w) - u[HDR:]
    return jnp.concatenate([head, tail], axis=0)


def _block_cumprod(f):
    row = lax.broadcasted_iota(jnp.int32, f.shape, 0) & (BLK - 1)
    x = f
    for d in (1, 2, 4, 8):
        x = x * jnp.where(row >= d, pltpu.roll(x, d, axis=0), 1.0)
    return x


def _gates(q, fl, lb):
    f = lb + (1.0 - lb) * _sigmoid(fl)
    k = 1.0 - f
    F = _block_cumprod(f)
    return q * F, k / F, F


def _as_column(row):
    tile = jnp.concatenate([row, jnp.zeros((7, row.shape[1]), row.dtype)], axis=0)
    return tile.T[:, 0:1]


def _stack_blocks(pieces, n_blocks):
    zero = jnp.zeros((BLK, HEAD), jnp.float32)
    return jnp.concatenate([pieces.get(j, zero) for j in range(n_blocks)], axis=0)


class _HgrnChunk:
    def __init__(self, load, lb):
        self.load, self.lb = load, lb

    def prepare(self):
        q, fl, v = self.load()
        c = self.c = q.shape[0]
        nb = c // BLK
        qe, ke, F = _gates(q, fl, self.lb)
        T = [F[BLK * j + BLK - 1:BLK * j + BLK, :] for j in range(nb)]
        qe_b = [qe[BLK * j:BLK * (j + 1)] for j in range(nb)]
        kd_b = [ke[BLK * j:BLK * (j + 1)] * T[j] for j in range(nb)]
        kd = jnp.concatenate(kd_b, axis=0)
        self.qe = _bf(qe)
        self.ke_kd = _bf(jnp.concatenate([ke, kd], axis=0))

        q_slots, k_slots = [], []
        gs = 4
        while gs <= nb:
            for a in range(0, nb, gs):
                mid = a + gs // 2
                qs, ks = {}, {}
                dec = None
                for j in range(mid, a + gs):
                    qs[j] = qe_b[j] if dec is None else qe_b[j] * dec
                    dec = T[j] if dec is None else dec * T[j]
                dec = None
                for j in range(mid - 1, a - 1, -1):
                    ks[j] = kd_b[j] if dec is None else kd_b[j] * dec
                    dec = T[j] if dec is None else dec * T[j]
                q_slots.append(_stack_blocks(qs, nb))
                k_slots.append(_stack_blocks(ks, nb))
            gs *= 2
        self.q_far = _bf(jnp.concatenate(q_slots, axis=1)) if q_slots else None
        self.k_far = _bf(jnp.concatenate(k_slots, axis=1)) if k_slots else None

        e_in = [None] * nb
        dec = None
        for j in range(nb):
            e_in[j] = dec
            dec = T[j] if dec is None else dec * T[j]
        self.e_tot = dec
        d_out = [None] * nb
        dec = None
        for j in range(nb - 1, -1, -1):
            d_out[j] = dec
            dec = T[j] if dec is None else dec * T[j]
        self.q_in = _bf(jnp.concatenate(
            [qe_b[j] if e_in[j] is None else qe_b[j] * e_in[j] for j in range(nb)], axis=0))
        self.k_out = _bf(jnp.concatenate(
            [kd_b[j] if d_out[j] is None else kd_b[j] * d_out[j] for j in range(nb)], axis=0))
        self.v_bf = _bf(v)

    def issue(self):
        self.r1 = _dot_nt(self.qe, self.ke_kd)
        self.far = _dot_nt(self.q_far, self.k_far) if self.q_far is not None else None
        self.upd = _dot_tn(self.v_bf, self.k_out)

    def combine(self, st, mask_d, mask_p):
        c = self.c
        p = jnp.where(mask_d, self.r1[:, :c], jnp.where(mask_p, self.r1[:, c:], 0.0))
        if self.far is not None:
            p = p + self.far
        self.o = _dot(_bf(p), self.v_bf) + _dot_nt(self.q_in, _bf(st))
        return st * self.e_tot + self.upd


def _head_out(o, gate, hg_norm):
    y = o * _rms_scale(o) * hg_norm
    return y * (gate * _sigmoid(gate))


def _post_norm(x, y_blocks, norm_post):
    ms = sum(jnp.sum(y * y, axis=-1, keepdims=True) for y in y_blocks) / D_MODEL
    r = lax.rsqrt(ms + EPS)
    return [x[:, j * COLS:(j + 1) * COLS] + y * r * norm_post[:, j * COLS:(j + 1) * COLS]
            for j, y in enumerate(y_blocks)]


def _finish(x, mix_bf, p_bf, w_out_ref, w_ple_ref, w_gate_ref, norm_post):
    nblk = D_MODEL // COLS
    x1 = _post_norm(x, [_dot(mix_bf, _wblk(w_out_ref, j)) for j in range(nblk)], norm_post)
    x1_bf = _bf(jnp.concatenate(x1, axis=1))
    out = [x1[j] + _sigmoid(_dot(x1_bf, _wblk(w_gate_ref, j))) * _dot(p_bf, _wblk(w_ple_ref, j)) for j in range(nblk)]
    return jnp.concatenate(out, axis=1)


def _chunk_masks(c):
    t = lax.broadcasted_iota(jnp.int32, (c, c), 0)
    s = lax.broadcasted_iota(jnp.int32, (c, c), 1)
    tb = t >> 4
    sb = s >> 4
    mask_d = (tb == sb) & (s <= t)
    mask_p = ((tb & 1) == 1) & (sb == tb - 1)
    return mask_d, mask_p


def _prompt_step(first_pos, xa_ref, xc_ref, p_ref, lbl_ref, w_in_ref, w_pool_ref, scale_ref, hgn_ref, w_out_ref,
                 npre_ref, npost_ref, w_ple_ref, w_gate_ref, y_ref, carry_ref, st_ref,
                 u_w, z_w, mix_w, u_r, z_r, mix_r):
    tt = xa_ref.shape[0]
    lb = _lower_bound(lbl_ref[...])
    mask_d, mask_p = _chunk_masks(CHUNK)
    states = [st_ref[hd] for hd in range(N_HEADS)]

    def in_proj(h, c0):
        blk = _dot(h, _wblk(w_in_ref, c0 // COLS))
        if c0 == C_U:
            u_w[HDR:HDR + tt, :] = blk
        else:
            z_w[:, c0 - C_GP:c0 - C_GP + D_POOL] = blk

    def pool_group(gi):
        cs = slice(gi * POOL_GROUP, (gi + 1) * POOL_GROUP)
        pooled = _pool_group(u_r[:, cs], POOL_WINDOWS[gi], first_pos)
        mixed = _dot(_bf(pooled), _bf(w_pool_ref[gi])) * scale_ref[:, cs]
        gp = z_r[:, cs]
        mix_w[:, cs] = _bf(mixed * (gp * _sigmoid(gp)))

    def hgrn_unit(hd, c0):
        hs = slice(hd * HEAD, (hd + 1) * HEAD)
        rs = slice(c0, c0 + CHUNK)
        col = lambda base: slice(base - C_GP + hd * HEAD, base - C_GP + (hd + 1) * HEAD)
        unit = _HgrnChunk(lambda: (z_r[rs, col(C_Q)], z_r[rs, col(C_F)], z_r[rs, col(C_V)]), lb[:, hs])
        unit.head, unit.rows, unit.gate_cols = hd, rs, col(C_GH)
        return unit

    def hgrn_store(unit):
        gh = z_r[unit.rows, unit.gate_cols]
        mix_w[unit.rows, D_POOL + unit.head * HEAD:D_POOL + (unit.head + 1) * HEAD] = _bf(
            _head_out(unit.o, gh, hgn_ref[...]))

    units = [hgrn_unit(hd, c0) for c0 in range(0, tt, CHUNK) for hd in range(N_HEADS)]
    dense = [("out", 1), ("in", C_U), ("gate", 0), ("gate", 1), ("in", C_GP), ("in", C_Q), ("in", C_F),
             ("in", C_V)]
    pool_at = {1: (0,), 4: (1,), 6: (2,), 7: (3,)}
    final_at = {5: (0,), 6: (1,)}
    p_bf = _bf(p_ref[...])
    ple = [_dot(p_bf, _wblk(w_ple_ref, j)) for j in range(2)]
    xa = xa_ref[...]
    h = _bf(xa * _rms_scale(xa) * npre_ref[...])
    units[0].prepare()
    mix_prev = mix_r[...]
    y = {0: _dot(mix_prev, _wblk(w_out_ref, 0))}
    u_r[0:HDR, :] = carry_ref[...]
    x1 = x1_bf = None
    gate = {}
    last = len(units) - 1
    assert len(units) == len(dense)
    for i, unit in enumerate(units):
        unit.issue()
        if i < last:
            units[i + 1].prepare()
        kind, arg = dense[i]
        if kind == "in":
            in_proj(h, arg)
        elif kind == "out":
            y[arg] = _dot(mix_prev, _wblk(w_out_ref, arg))
        else:
            gate[arg] = _dot(x1_bf, _wblk(w_gate_ref, arg))
        states[unit.head] = unit.combine(states[unit.head], mask_d, mask_p)
        if i == 0:
            x1 = _post_norm(xc_ref[...], [y[0], y[1]], npost_ref[...])
            x1_bf = _bf(jnp.concatenate(x1, axis=1))
        else:
            hgrn_store(units[i - 1])
        for gi in pool_at.get(i, ()):
            pool_group(gi)
        for j in final_at.get(i, ()):
            y_ref[:, j * COLS:(j + 1) * COLS] = x1[j] + _sigmoid(gate[j]) * ple[j]
    in_proj(h, C_GH)
    hgrn_store(units[last])
    carry_ref[...] = u_r[tt:tt + HDR, :]
    for hd in range(N_HEADS):
        st_ref[hd] = states[hd]


def _prompt_kernel(tiles_per_stream, xa_ref, xc_ref, p_ref, lbl_ref, w_in_ref, w_pool_ref, scale_ref, hgn_ref,
                   w_out_ref, npre_ref, npost_ref, w_ple_ref, w_gate_ref,
                   y_ref, pool_ref, hg_ref,
                   u_a, z_a, mix_a, u_b, z_b, mix_b, carry_ref, st_ref):
    s = pl.program_id(0)
    tt = xa_ref.shape[0]
    tile = lax.rem(s - 1 + tiles_per_stream, tiles_per_stream)

    @pl.when(s == 0)
    def _():
        for ref in (u_a, z_a, mix_a, u_b, z_b, mix_b, carry_ref, st_ref):
            ref[...] = jnp.zeros(ref.shape, ref.dtype)

    @pl.when(tile == 0)
    def _():
        carry_ref[...] = jnp.zeros(carry_ref.shape, jnp.float32)
        st_ref[...] = jnp.zeros(st_ref.shape, jnp.float32)

    step = functools.partial(_prompt_step, tile * tt, xa_ref, xc_ref, p_ref, lbl_ref, w_in_ref, w_pool_ref,
                             scale_ref, hgn_ref, w_out_ref, npre_ref, npost_ref, w_ple_ref, w_gate_ref,
                             y_ref, carry_ref, st_ref)

    @pl.when((s & 1) == 0)
    def _():
        step(u_a, z_a, mix_a, u_b, z_b, mix_b)

    @pl.when((s & 1) == 1)
    def _():
        step(u_b, z_b, mix_b, u_a, z_a, mix_a)

    @pl.when((tile == tiles_per_stream - 1) & (s > 0))
    def _():
        pool_ref[...] = carry_ref[HDR - POOL_BUF:HDR, :]
        for hd in range(N_HEADS):
            hg_ref[hd] = st_ref[hd].T


def _const_spec(shape):
    nd = len(shape)
    return pl.BlockSpec(shape, lambda *_: (0,) * nd, pipeline_mode=pl.Buffered(1))


def _prompt_call(x, p, lbl, w_in, w_pool, scale, hgn, w_out, npre, npost, w_ple, w_gate):
    b, t, _ = x.shape
    tt = PROMPT_TILE
    assert t % tt == 0 and tt % CHUNK == 0
    nt = t // tt
    g = b * nt
    weights = (lbl, w_in, w_pool, scale, hgn, w_out, npre, npost, w_ple, w_gate)
    x2 = x.reshape(b * t, D_MODEL)
    p2 = p.reshape(b * t, D_PLE)
    ahead = lambda s: (jnp.minimum(s, g - 1), 0)
    behind = lambda s: (jnp.clip(s - 2, 0, g - 1), 0)
    stream = lambda s: (jnp.clip((s - 1) // nt, 0, b - 1),)
    y, pool, hg = pl.pallas_call(
        functools.partial(_prompt_kernel, nt),
        grid=(g + 2,),
        in_specs=[pl.BlockSpec((tt, D_MODEL), ahead),
                  pl.BlockSpec((tt, D_MODEL), behind),
                  pl.BlockSpec((tt, D_PLE), behind)]
                 + [_const_spec(w.shape) for w in weights],
        out_specs=[pl.BlockSpec((tt, D_MODEL), behind),
                   pl.BlockSpec((None, POOL_BUF, D_POOL), lambda s: stream(s) + (0, 0)),
                   pl.BlockSpec((None, N_HEADS, HEAD, HEAD), lambda s: stream(s) + (0, 0, 0))],
        out_shape=[jax.ShapeDtypeStruct((b * t, D_MODEL), jnp.float32),
                   jax.ShapeDtypeStruct((b, POOL_BUF, D_POOL), jnp.float32),
                   jax.ShapeDtypeStruct((b, N_HEADS, HEAD, HEAD), jnp.float32)],
        scratch_shapes=[pltpu.VMEM((HDR + tt, D_POOL), jnp.float32),
                        pltpu.VMEM((tt, D_IN - D_POOL), jnp.float32),
                        pltpu.VMEM((tt, D_MODEL), jnp.bfloat16)] * 2
                       + [pltpu.VMEM((HDR, D_POOL), jnp.float32),
                          pltpu.VMEM((N_HEADS, HEAD, HEAD), jnp.float32)],
        compiler_params=pltpu.CompilerParams(dimension_semantics=("arbitrary",),
                                             vmem_limit_bytes=VMEM_LIMIT_BYTES,
                                             flags=PROMPT_SCHEDULER_FLAGS),
        name="prompt_layer",
    )(x2, x2, p2, *weights)
    return y.reshape(b, t, D_MODEL), pool, hg


STREAMS_PER_ITER = 8
SAMPLE_VMEM_LIMIT_BYTES = 58 * 1024 * 1024


def _sample_kernel(start_pos, x_ref, p_ref, cache_ref, s0_ref, lbl_ref, w_in_ref, w_pool_ref, scale_ref,
                   hgn_ref, w_out_ref, npre_ref, npost_ref, w_ple_ref, w_gate_ref,
                   y_ref, pool_ref, hg_ref, w_in_b, w_out_b, w_ple_b, w_gate_b,
                   z_ref, ext_ref, mix_ref):
    n_streams, ts = pool_ref.shape[0], pool_ref.shape[1] + 1

    for src, dst in ((w_in_ref, w_in_b), (w_out_ref, w_out_b), (w_ple_ref, w_ple_b), (w_gate_ref, w_gate_b)):
        for j in range(dst.shape[0]):
            dst[j] = _bf(src[:, j * COLS:(j + 1) * COLS])

    x = x_ref[...]
    h = _bf(x * _rms_scale(x) * npre_ref[...])
    for j in range(D_IN // COLS):
        z_ref[:, j * COLS:(j + 1) * COLS] = _dot(h, w_in_b[j])

    lb = _lower_bound(lbl_ref[...])
    causal = (lax.broadcasted_iota(jnp.int32, (ts, ts), 1) <= lax.broadcasted_iota(jnp.int32, (ts, ts), 0))
    w_pool = [_bf(w_pool_ref[gi]) for gi in range(len(POOL_WINDOWS))]

    def prepare(b, slot):
        c = {"b": b, "rows": pl.ds(pl.multiple_of(b * ts, ts), ts)}
        rows = c["rows"]
        ext = ext_ref.at[slot]
        u = z_ref[rows, C_U:C_U + D_POOL]
        ext[0:HDR, :] = jnp.zeros((HDR, D_POOL), jnp.float32)
        ext[HDR - POOL_BUF:HDR, :] = cache_ref[b]
        ext[HDR:HDR + ts, :] = u
        pool_ref[b] = u[ts - POOL_BUF:ts, :]
        c["pooled"] = [_bf(_pool_group(ext[:, gi * POOL_GROUP:(gi + 1) * POOL_GROUP], w, start_pos))
                       for gi, w in enumerate(POOL_WINDOWS)]
        gated = []
        for hd in range(N_HEADS):
            q = z_ref[rows, C_Q + hd * HEAD:C_Q + (hd + 1) * HEAD]
            fl = z_ref[rows, C_F + hd * HEAD:C_F + (hd + 1) * HEAD]
            gated.append(_gates(q, fl, lb[:, hd * HEAD:(hd + 1) * HEAD]))
        decay_rows = [F[ts - 1:ts, :] for _, _, F in gated]
        c["decay_cols"] = jnp.concatenate(
            decay_rows + [jnp.zeros((8 - N_HEADS, HEAD), jnp.float32)], axis=0).T
        c["qe"] = [_bf(qe) for qe, _, _ in gated]
        c["ke"] = [_bf(ke) for _, ke, _ in gated]
        c["kd"] = [_bf(ke * decay_rows[hd]) for hd, (_, ke, _) in enumerate(gated)]
        c["v"] = [_bf(z_ref[rows, C_V + hd * HEAD:C_V + (hd + 1) * HEAD]) for hd in range(N_HEADS)]
        return c

    def issue(c):
        c["mixed"] = [_dot(c["pooled"][gi], w_pool[gi]) for gi in range(len(POOL_WINDOWS))]
        c["att"] = [_dot_nt(c["qe"][hd], c["ke"][hd]) for hd in range(N_HEADS)]
        c["upd"] = [_dot_tn(c["kd"][hd], c["v"][hd]) for hd in range(N_HEADS)]

    def combine(c):
        b, rows = c["b"], c["rows"]
        for gi in range(len(POOL_WINDOWS)):
            cs = slice(gi * POOL_GROUP, (gi + 1) * POOL_GROUP)
            gp = z_ref[rows, C_GP + gi * POOL_GROUP:C_GP + (gi + 1) * POOL_GROUP]
            mix_ref[rows, cs] = _bf(c["mixed"][gi] * scale_ref[:, cs] * (gp * _sigmoid(gp)))
        c["o"] = []
        for hd in range(N_HEADS):
            s0 = s0_ref[b, hd]
            att = _bf(jnp.where(causal, c["att"][hd], 0.0))
            c["o"].append(_dot(att, c["v"][hd]) + _dot(c["qe"][hd], _bf(s0)))
            hg_ref[b, hd] = s0 * c["decay_cols"][:, hd:hd + 1] + c["upd"][hd]

    def store(c):
        rows = c["rows"]
        for hd in range(N_HEADS):
            gh = z_ref[rows, C_GH + hd * HEAD:C_GH + (hd + 1) * HEAD]
            mix_ref[rows, D_POOL + hd * HEAD:D_POOL + (hd + 1) * HEAD] = _bf(
                _head_out(c["o"][hd], gh, hgn_ref[...]))

    def stream_group(g, carry):
        ctx = [prepare(g * STREAMS_PER_ITER + slot, slot) for slot in range(STREAMS_PER_ITER)]
        for phase in (issue, combine, store):
            for c in ctx:
                phase(c)
        return carry

    lax.fori_loop(0, n_streams // STREAMS_PER_ITER, stream_group, 0)

    y_ref[...] = _finish(x, mix_ref[...], _bf(p_ref[...]), w_out_b, w_ple_b, w_gate_b, npost_ref[...])


def _sample_call(start_pos, x, p, cache, s0, lbl, w_in, w_pool, scale, hgn, w_out, npre, npost, w_ple, w_gate):
    b, ts, _ = x.shape
    assert ts == BLK and ts == POOL_BUF + 1 and b % STREAMS_PER_ITER == 0
    n = b * ts
    blocks = lambda w: jax.ShapeDtypeStruct((w.shape[1] // COLS, w.shape[0], COLS), jnp.bfloat16)
    y, pool, hg, w_in_b, w_out_b, w_ple_b, w_gate_b = pl.pallas_call(
        functools.partial(_sample_kernel, start_pos),
        out_shape=[jax.ShapeDtypeStruct((n, D_MODEL), jnp.float32),
                   jax.ShapeDtypeStruct((b, POOL_BUF, D_POOL), jnp.float32),
                   jax.ShapeDtypeStruct((b, N_HEADS, HEAD, HEAD), jnp.float32),
                   blocks(w_in), blocks(w_out), blocks(w_ple), blocks(w_gate)],
        scratch_shapes=[pltpu.VMEM((n, D_IN), jnp.float32),
                        pltpu.VMEM((STREAMS_PER_ITER, HDR + ts, D_POOL), jnp.float32),
                        pltpu.VMEM((n, D_MODEL), jnp.bfloat16)],
        compiler_params=pltpu.CompilerParams(vmem_limit_bytes=SAMPLE_VMEM_LIMIT_BYTES),
        name="sample_layer",
    )(x.reshape(n, D_MODEL), p.reshape(n, D_PLE), cache, s0, lbl, w_in, w_pool, scale, hgn, w_out, npre, npost,
      w_ple, w_gate)
    return (y.reshape(b, ts, D_MODEL), pool, hg), (w_in_b, w_out_b, w_ple_b, w_gate_b)


def kernel(x_prompt, x_sample, cache_pool, state_hgrn, p_prompt, p_sample, lb_logits, w_in, w_pool, pool_scale,
           hg_norm, w_out, norm_pre, norm_post, w_ple, w_ple_gate):
    depth = w_in.shape[0]
    assert depth == 1 and lb_logits.shape[0] == 2
    past_len = 1024
    (y_s, pool_s, hg_s), (w_in_b, w_out_b, w_ple_b, w_gate_b) = _sample_call(
        past_len, x_sample, p_sample[0], cache_pool[0], state_hgrn[0], lb_logits, w_in[0], w_pool[0], pool_scale,
        hg_norm, w_out[0], norm_pre, norm_post, w_ple[0], w_ple_gate[0])
    y_p, pool_p, hg_p = _prompt_call(x_prompt, p_prompt[0], lb_logits, w_in_b, w_pool[0], pool_scale, hg_norm,
                                     w_out_b, norm_pre, norm_post, w_ple_b, w_gate_b)
    return (y_p, y_s, pool_p[None], hg_p[None], pool_s[None], hg_s[None])
```

```python
import functools

import jax
import jax.numpy as jnp
from jax import lax
from jax.experimental import pallas as pl
from jax.experimental.pallas import tpu as pltpu

D_MODEL = 1024
D_POOL = 512
POOL_WINDOWS = (2, 4, 8, 16)
POOL_GROUP = 128
POOL_BUF = 15
N_HEADS = 4
HEAD = 128
D_HG = N_HEADS * HEAD
D_IN = 2 * D_POOL + 4 * D_HG
D_PLE = 256
EPS = 1e-6
LOG2_E = 1.4426950408889634
COLS = 512
SPAN = 32
BLK = 16
CHUNK = 128
HDR = 16
UNIT_GROUP = 1
PROMPT_TILE = 256
VMEM_LIMIT_BYTES = 60 * 1024 * 1024
PROMPT_SCHEDULER_FLAGS = None

C_U, C_GP, C_Q, C_F, C_V, C_GH = 0, 512, 1024, 1536, 2048, 2560

_NT = (((1,), (1,)), ((), ()))
_TN = (((0,), (0,)), ((), ()))


def _dot(a, b):
    return jnp.dot(a, b, preferred_element_type=jnp.float32)


def _dot_nt(a, b):
    return lax.dot_general(a, b, _NT, preferred_element_type=jnp.float32)


def _dot_tn(a, b):
    return lax.dot_general(a, b, _TN, preferred_element_type=jnp.float32)


def _bf(x):
    return x.astype(jnp.bfloat16)


def _wblk(w_ref, j):
    return w_ref[j]


def _sigmoid(x):
    return 1.0 / (1.0 + jnp.exp2(x * (-LOG2_E)))


def _rms_scale(x):
    return lax.rsqrt(jnp.mean(x * x, axis=-1, keepdims=True) + EPS)


def _lower_bound(lb_logits):
    l0 = lb_logits[0:1, :]
    l1 = lb_logits[1:2, :]
    m = jnp.maximum(l0, l1)
    e0 = jnp.exp(l0 - m)
    e1 = jnp.exp(l1 - m)
    return e0 / (e0 + e1)


def _window_sum(e, w):
    s = e
    d = 1
    while d < w:
        s = s + pltpu.roll(s, d, axis=0)
        d *= 2
    return s


def _pool_group(e, w, first_pos):
    s = _window_sum(e, w)[HDR:]
    u = e[HDR:]
    rows = lax.broadcasted_iota(jnp.int32, (HDR, POOL_GROUP), 0)
    cnt = jnp.minimum(w, first_pos + rows + 1).astype(jnp.float32)
    head = s[:HDR] / cnt - u[:HDR]
    if s.shape[0] == HDR:
        return head
    tail = s[HDR:] * (1.0 / w) - u[HDR:]
    return jnp.concatenate([head, tail], axis=0)


def _block_cumprod(f, reverse=False):
    n = f.shape[0]
    row = lax.broadcasted_iota(jnp.int32, f.shape, 0) & (BLK - 1)
    x = f
    for d in (1, 2, 4, 8):
        if reverse:
            x = x * jnp.where(row + d < BLK, pltpu.roll(x, n - d, axis=0), 1.0)
        else:
            x = x * jnp.where(row >= d, pltpu.roll(x, d, axis=0), 1.0)
    return x


def _gates(q, fl, lb):
    f = lb + (1.0 - lb) * _sigmoid(fl)
    k = 1.0 - f
    F = _block_cumprod(f)
    return q * F, k / F, F


def _as_column(row):
    tile = jnp.concatenate([row, jnp.zeros((7, row.shape[1]), row.dtype)], axis=0)
    return tile.T[:, 0:1]


def _stack_blocks(pieces, n_blocks):
    zero = jnp.zeros((SPAN, HEAD), jnp.float32)
    return jnp.concatenate([pieces.get(j, zero) for j in range(n_blocks)], axis=0)


class _HgrnChunk:
    def __init__(self, load, lb):
        self.load, self.lb = load, lb

    def prepare(self):
        q, fl, v = self.load()
        c = self.c = q.shape[0]
        nb = c // SPAN
        f = self.lb + (1.0 - self.lb) * _sigmoid(fl)
        k = 1.0 - f
        halves = lambda x: [jnp.concatenate([x[SPAN * j + BLK * w:SPAN * j + BLK * (w + 1)] for j in range(nb)], axis=0)
                            for w in (0, 1)]
        (f_a, f_b), (q_a, q_b), (k_a, k_b) = halves(f), halves(q), halves(k)
        fwd = _block_cumprod(f_b)
        rev_incl = _block_cumprod(f_a, reverse=True)
        row = lax.broadcasted_iota(jnp.int32, f_a.shape, 0) & (BLK - 1)
        rev = jnp.where(row < BLK - 1, pltpu.roll(rev_incl, f_a.shape[0] - 1, axis=0), 1.0)
        qm = (q_a / rev, q_b * fwd)
        km = (k_a * rev, k_b / fwd)
        span = lambda pair, j: jnp.concatenate([pair[0][BLK * j:BLK * (j + 1)], pair[1][BLK * j:BLK * (j + 1)]], axis=0)
        t_head = [rev_incl[BLK * j:BLK * j + 1, :] for j in range(nb)]
        t_tail = [fwd[BLK * j + BLK - 1:BLK * (j + 1), :] for j in range(nb)]
        T = [t_head[j] * t_tail[j] for j in range(nb)]
        qm_b = [span(qm, j) for j in range(nb)]
        km_b = [span(km, j) for j in range(nb)]
        qe_b = [qm_b[j] * t_head[j] for j in range(nb)]
        kd_b = [km_b[j] * t_tail[j] for j in range(nb)]
        zero = jnp.zeros((SPAN, HEAD), jnp.float32)
        k_pair = [kd_b[j] * t_head[j + 1] if j % 2 == 0 else zero for j in range(nb)]
        self.qe = _bf(jnp.concatenate(qm_b, axis=0))
        self.ke_kd = _bf(jnp.concatenate(km_b + k_pair, axis=0))

        q_slots, k_slots = [], []
        gs = 4
        while gs <= nb:
            for a in range(0, nb, gs):
                mid = a + gs // 2
                qs, ks = {}, {}
                dec = None
                for j in range(mid, a + gs):
                    qs[j] = qe_b[j] if dec is None else qe_b[j] * dec
                    dec = T[j] if dec is None else dec * T[j]
                dec = None
                for j in range(mid - 1, a - 1, -1):
                    ks[j] = kd_b[j] if dec is None else kd_b[j] * dec
                    dec = T[j] if dec is None else dec * T[j]
                q_slots.append(_stack_blocks(qs, nb))
                k_slots.append(_stack_blocks(ks, nb))
            gs *= 2
        self.q_far = _bf(jnp.concatenate(q_slots, axis=1)) if q_slots else None
        self.k_far = _bf(jnp.concatenate(k_slots, axis=1)) if k_slots else None

        e_in = [None] * nb
        dec = None
        for j in range(nb):
            e_in[j] = dec
            dec = T[j] if dec is None else dec * T[j]
        self.e_tot = dec
        d_out = [None] * nb
        dec = None
        for j in range(nb - 1, -1, -1):
            d_out[j] = dec
            dec = T[j] if dec is None else dec * T[j]
        self.q_in = _bf(jnp.concatenate(
            [qe_b[j] if e_in[j] is None else qe_b[j] * e_in[j] for j in range(nb)], axis=0))
        self.k_out = _bf(jnp.concatenate(
            [kd_b[j] if d_out[j] is None else kd_b[j] * d_out[j] for j in range(nb)], axis=0))
        self.v_bf = _bf(v)

    def issue(self):
        self.r1 = _dot_nt(self.qe, self.ke_kd)
        self.far = _dot_nt(self.q_far, self.k_far) if self.q_far is not None else None
        self.upd = _dot_tn(self.v_bf, self.k_out)

    def combine(self, st, mask_d, mask_p):
        c = self.c
        p = jnp.where(mask_d, self.r1[:, :c], jnp.where(mask_p, self.r1[:, c:], 0.0))
        if self.far is not None:
            p = p + self.far
        self.o = _dot(_bf(p), self.v_bf) + _dot_nt(self.q_in, _bf(st))
        return st * self.e_tot + self.upd


def _head_out(o, gate, hg_norm):
    y = o * _rms_scale(o) * hg_norm
    return y * (gate * _sigmoid(gate))


def _post_norm(x, y_blocks, norm_post):
    ms = sum(jnp.sum(y * y, axis=-1, keepdims=True) for y in y_blocks) / D_MODEL
    r = lax.rsqrt(ms + EPS)
    return [x[:, j * COLS:(j + 1) * COLS] + y * r * norm_post[:, j * COLS:(j + 1) * COLS]
            for j, y in enumerate(y_blocks)]


def _finish(x, mix_bf, p_bf, w_out_ref, w_ple_ref, w_gate_ref, norm_post):
    nblk = D_MODEL // COLS
    x1 = _post_norm(x, [_dot(mix_bf, _wblk(w_out_ref, j)) for j in range(nblk)], norm_post)
    x1_bf = _bf(jnp.concatenate(x1, axis=1))
    out = [x1[j] + _sigmoid(_dot(x1_bf, _wblk(w_gate_ref, j))) * _dot(p_bf, _wblk(w_ple_ref, j)) for j in range(nblk)]
    return jnp.concatenate(out, axis=1)


def _chunk_masks(c):
    t = lax.broadcasted_iota(jnp.int32, (c, c), 0)
    s = lax.broadcasted_iota(jnp.int32, (c, c), 1)
    tb = t // SPAN
    sb = s // SPAN
    mask_d = (tb == sb) & (s <= t)
    mask_p = ((tb & 1) == 1) & (sb == tb - 1)
    return mask_d, mask_p


def _prompt_step(first_pos, xa_ref, xc_ref, p_ref, lbl_ref, w_in_ref, w_pool_ref, scale_ref, hgn_ref, w_out_ref,
                 npre_ref, npost_ref, w_ple_ref, w_gate_ref, y_ref, carry_ref, st_ref,
                 u_w, z_w, mix_w, u_r, z_r, mix_r):
    tt = xa_ref.shape[0]
    lb = _lower_bound(lbl_ref[...])
    mask_d, mask_p = _chunk_masks(CHUNK)
    states = [st_ref[hd] for hd in range(N_HEADS)]

    def in_proj(h, c0):
        blk = _dot(h, _wblk(w_in_ref, c0 // COLS))
        if c0 == C_U:
            u_w[HDR:HDR + tt, :] = blk
        else:
            z_w[:, c0 - C_GP:c0 - C_GP + D_POOL] = blk

    def pool_group(gi):
        cs = slice(gi * POOL_GROUP, (gi + 1) * POOL_GROUP)
        pooled = _pool_group(u_r[:, cs], POOL_WINDOWS[gi], first_pos)
        mixed = _dot(_bf(pooled), _bf(w_pool_ref[gi])) * scale_ref[:, cs]
        gp = z_r[:, cs]
        mix_w[:, cs] = _bf(mixed * (gp * _sigmoid(gp)))

    def hgrn_unit(hd, c0):
        hs = slice(hd * HEAD, (hd + 1) * HEAD)
        rs = slice(c0, c0 + CHUNK)
        col = lambda base: slice(base - C_GP + hd * HEAD, base - C_GP + (hd + 1) * HEAD)
        unit = _HgrnChunk(lambda: (z_r[rs, col(C_Q)], z_r[rs, col(C_F)], z_r[rs, col(C_V)]), lb[:, hs])
        unit.head, unit.rows, unit.gate_cols = hd, rs, col(C_GH)
        return unit

    def hgrn_store(unit):
        gh = z_r[unit.rows, unit.gate_cols]
        mix_w[unit.rows, D_POOL + unit.head * HEAD:D_POOL + (unit.head + 1) * HEAD] = _bf(
            _head_out(unit.o, gh, hgn_ref[...]))

    units = [hgrn_unit(hd, c0) for c0 in range(0, tt, CHUNK) for hd in range(N_HEADS)]
    dense = [("out", 1), ("in", C_U), ("gate", 0), ("gate", 1), ("in", C_GP), ("in", C_Q), ("in", C_F),
             ("in", C_V)]
    pool_at = {1: (0,), 4: (1,), 6: (2,), 7: (3,)}
    final_at = {5: (0,), 6: (1,)}
    p_bf = _bf(p_ref[...])
    ple = [_dot(p_bf, _wblk(w_ple_ref, j)) for j in range(2)]
    xa = xa_ref[...]
    h = _bf(xa * _rms_scale(xa) * npre_ref[...])
    groups = [units[i:i + UNIT_GROUP] for i in range(0, len(units), UNIT_GROUP)]
    for unit in groups[0]:
        unit.prepare()
    mix_prev = mix_r[...]
    y = {0: _dot(mix_prev, _wblk(w_out_ref, 0))}
    u_r[0:HDR, :] = carry_ref[...]
    x1 = x1_bf = None
    gate = {}
    assert len(units) == len(dense)
    for k, group in enumerate(groups):
        for unit in group:
            unit.issue()
        if k + 1 < len(groups):
            for unit in groups[k + 1]:
                unit.prepare()
        for kind, arg in dense[k * UNIT_GROUP:(k + 1) * UNIT_GROUP]:
            if kind == "in":
                in_proj(h, arg)
            elif kind == "out":
                y[arg] = _dot(mix_prev, _wblk(w_out_ref, arg))
            else:
                gate[arg] = _dot(x1_bf, _wblk(w_gate_ref, arg))
        for unit in group:
            states[unit.head] = unit.combine(states[unit.head], mask_d, mask_p)
        if k == 0:
            x1 = _post_norm(xc_ref[...], [y[0], y[1]], npost_ref[...])
            x1_bf = _bf(jnp.concatenate(x1, axis=1))
        else:
            for unit in groups[k - 1]:
                hgrn_store(unit)
        for gi in pool_at.get(k, ()):
            pool_group(gi)
        for j in final_at.get(k, ()):
            y_ref[:, j * COLS:(j + 1) * COLS] = x1[j] + _sigmoid(gate[j]) * ple[j]
    in_proj(h, C_GH)
    for unit in groups[-1]:
        hgrn_store(unit)
    carry_ref[...] = u_r[tt:tt + HDR, :]
    for hd in range(N_HEADS):
        st_ref[hd] = states[hd]


def _prompt_kernel(tiles_per_stream, xa_ref, xc_ref, p_ref, lbl_ref, w_in_ref, w_pool_ref, scale_ref, hgn_ref,
                   w_out_ref, npre_ref, npost_ref, w_ple_ref, w_gate_ref,
                   y_ref, pool_ref, hg_ref,
                   u_a, z_a, mix_a, u_b, z_b, mix_b, carry_ref, st_ref):
    s = pl.program_id(0)
    tt = xa_ref.shape[0]
    tile = lax.rem(s - 1 + tiles_per_stream, tiles_per_stream)

    @pl.when(s == 0)
    def _():
        for ref in (u_a, z_a, mix_a, u_b, z_b, mix_b, carry_ref, st_ref):
            ref[...] = jnp.zeros(ref.shape, ref.dtype)

    @pl.when(tile == 0)
    def _():
        carry_ref[...] = jnp.zeros(carry_ref.shape, jnp.float32)
        st_ref[...] = jnp.zeros(st_ref.shape, jnp.float32)

    step = functools.partial(_prompt_step, tile * tt, xa_ref, xc_ref, p_ref, lbl_ref, w_in_ref, w_pool_ref,
                             scale_ref, hgn_ref, w_out_ref, npre_ref, npost_ref, w_ple_ref, w_gate_ref,
                             y_ref, carry_ref, st_ref)

    @pl.when((s & 1) == 0)
    def _():
        step(u_a, z_a, mix_a, u_b, z_b, mix_b)

    @pl.when((s & 1) == 1)
    def _():
        step(u_b, z_b, mix_b, u_a, z_a, mix_a)

    @pl.when((tile == tiles_per_stream - 1) & (s > 0))
    def _():
        pool_ref[...] = carry_ref[HDR - POOL_BUF:HDR, :]
        for hd in range(N_HEADS):
            hg_ref[hd] = st_ref[hd].T


def _const_spec(shape):
    nd = len(shape)
    return pl.BlockSpec(shape, lambda *_: (0,) * nd, pipeline_mode=pl.Buffered(1))


def _prompt_call(x, p, lbl, w_in, w_pool, scale, hgn, w_out, npre, npost, w_ple, w_gate):
    b, t, _ = x.shape
    tt = PROMPT_TILE
    assert t % tt == 0 and tt % CHUNK == 0
    nt = t // tt
    g = b * nt
    weights = (lbl, w_in, w_pool, scale, hgn, w_out, npre, npost, w_ple, w_gate)
    x2 = x.reshape(b * t, D_MODEL)
    p2 = p.reshape(b * t, D_PLE)
    ahead = lambda s: (jnp.minimum(s, g - 1), 0)
    behind = lambda s: (jnp.clip(s - 2, 0, g - 1), 0)
    stream = lambda s: (jnp.clip((s - 1) // nt, 0, b - 1),)
    y, pool, hg = pl.pallas_call(
        functools.partial(_prompt_kernel, nt),
        grid=(g + 2,),
        in_specs=[pl.BlockSpec((tt, D_MODEL), ahead),
                  pl.BlockSpec((tt, D_MODEL), behind),
                  pl.BlockSpec((tt, D_PLE), behind)]
                 + [_const_spec(w.shape) for w in weights],
        out_specs=[pl.BlockSpec((tt, D_MODEL), behind),
                   pl.BlockSpec((None, POOL_BUF, D_POOL), lambda s: stream(s) + (0, 0)),
                   pl.BlockSpec((None, N_HEADS, HEAD, HEAD), lambda s: stream(s) + (0, 0, 0))],
        out_shape=[jax.ShapeDtypeStruct((b * t, D_MODEL), jnp.float32),
                   jax.ShapeDtypeStruct((b, POOL_BUF, D_POOL), jnp.float32),
                   jax.ShapeDtypeStruct((b, N_HEADS, HEAD, HEAD), jnp.float32)],
        scratch_shapes=[pltpu.VMEM((HDR + tt, D_POOL), jnp.float32),
                        pltpu.VMEM((tt, D_IN - D_POOL), jnp.float32),
                        pltpu.VMEM((tt, D_MODEL), jnp.bfloat16)] * 2
                       + [pltpu.VMEM((HDR, D_POOL), jnp.float32),
                          pltpu.VMEM((N_HEADS, HEAD, HEAD), jnp.float32)],
        compiler_params=pltpu.CompilerParams(dimension_semantics=("arbitrary",),
                                             vmem_limit_bytes=VMEM_LIMIT_BYTES,
                                             flags=PROMPT_SCHEDULER_FLAGS),
        name="prompt_layer",
    )(x2, x2, p2, *weights)
    return y.reshape(b, t, D_MODEL), pool, hg


STREAMS_PER_ITER = 8
SAMPLE_VMEM_LIMIT_BYTES = 58 * 1024 * 1024


def _sample_kernel(start_pos, x_ref, p_ref, cache_ref, s0_ref, lbl_ref, w_in_ref, w_pool_ref, scale_ref,
                   hgn_ref, w_out_ref, npre_ref, npost_ref, w_ple_ref, w_gate_ref,
                   y_ref, pool_ref, hg_ref, w_in_b, w_out_b, w_ple_b, w_gate_b,
                   z_ref, ext_ref, mix_ref):
    n_streams, ts = pool_ref.shape[0], pool_ref.shape[1] + 1

    for src, dst in ((w_in_ref, w_in_b), (w_out_ref, w_out_b), (w_ple_ref, w_ple_b), (w_gate_ref, w_gate_b)):
        for j in range(dst.shape[0]):
            dst[j] = _bf(src[:, j * COLS:(j + 1) * COLS])

    x = x_ref[...]
    h = _bf(x * _rms_scale(x) * npre_ref[...])
    for j in range(D_IN // COLS):
        z_ref[:, j * COLS:(j + 1) * COLS] = _dot(h, w_in_b[j])

    lb = _lower_bound(lbl_ref[...])
    causal = (lax.broadcasted_iota(jnp.int32, (ts, ts), 1) <= lax.broadcasted_iota(jnp.int32, (ts, ts), 0))
    w_pool = [_bf(w_pool_ref[gi]) for gi in range(len(POOL_WINDOWS))]

    def prepare(b, slot):
        c = {"b": b, "rows": pl.ds(pl.multiple_of(b * ts, ts), ts)}
        rows = c["rows"]
        ext = ext_ref.at[slot]
        u = z_ref[rows, C_U:C_U + D_POOL]
        ext[0:HDR, :] = jnp.zeros((HDR, D_POOL), jnp.float32)
        ext[HDR - POOL_BUF:HDR, :] = cache_ref[b]
        ext[HDR:HDR + ts, :] = u
        pool_ref[b] = u[ts - POOL_BUF:ts, :]
        c["pooled"] = [_bf(_pool_group(ext[:, gi * POOL_GROUP:(gi + 1) * POOL_GROUP], w, start_pos))
                       for gi, w in enumerate(POOL_WINDOWS)]
        gated = []
        for hd in range(N_HEADS):
            q = z_ref[rows, C_Q + hd * HEAD:C_Q + (hd + 1) * HEAD]
            fl = z_ref[rows, C_F + hd * HEAD:C_F + (hd + 1) * HEAD]
            gated.append(_gates(q, fl, lb[:, hd * HEAD:(hd + 1) * HEAD]))
        decay_rows = [F[ts - 1:ts, :] for _, _, F in gated]
        c["decay_cols"] = jnp.concatenate(
            decay_rows + [jnp.zeros((8 - N_HEADS, HEAD), jnp.float32)], axis=0).T
        c["qe"] = [_bf(qe) for qe, _, _ in gated]
        c["ke"] = [_bf(ke) for _, ke, _ in gated]
        c["kd"] = [_bf(ke * decay_rows[hd]) for hd, (_, ke, _) in enumerate(gated)]
        c["v"] = [_bf(z_ref[rows, C_V + hd * HEAD:C_V + (hd + 1) * HEAD]) for hd in range(N_HEADS)]
        return c

    def issue(c):
        c["mixed"] = [_dot(c["pooled"][gi], w_pool[gi]) for gi in range(len(POOL_WINDOWS))]
        c["att"] = [_dot_nt(c["qe"][hd], c["ke"][hd]) for hd in range(N_HEADS)]
        c["upd"] = [_dot_tn(c["kd"][hd], c["v"][hd]) for hd in range(N_HEADS)]

    def combine(c):
        b, rows = c["b"], c["rows"]
        for gi in range(len(POOL_WINDOWS)):
            cs = slice(gi * POOL_GROUP, (gi + 1) * POOL_GROUP)
            gp = z_ref[rows, C_GP + gi * POOL_GROUP:C_GP + (gi + 1) * POOL_GROUP]
            mix_ref[rows, cs] = _bf(c["mixed"][gi] * scale_ref[:, cs] * (gp * _sigmoid(gp)))
        c["o"] = []
        for hd in range(N_HEADS):
            s0 = s0_ref[b, hd]
            att = _bf(jnp.where(causal, c["att"][hd], 0.0))
            c["o"].append(_dot(att, c["v"][hd]) + _dot(c["qe"][hd], _bf(s0)))
            hg_ref[b, hd] = s0 * c["decay_cols"][:, hd:hd + 1] + c["upd"][hd]

    def store(c):
        rows = c["rows"]
        for hd in range(N_HEADS):
            gh = z_ref[rows, C_GH + hd * HEAD:C_GH + (hd + 1) * HEAD]
            mix_ref[rows, D_POOL + hd * HEAD:D_POOL + (hd + 1) * HEAD] = _bf(
                _head_out(c["o"][hd], gh, hgn_ref[...]))

    def stream_group(g, carry):
        ctx = [prepare(g * STREAMS_PER_ITER + slot, slot) for slot in range(STREAMS_PER_ITER)]
        for phase in (issue, combine, store):
            for c in ctx:
                phase(c)
        return carry

    lax.fori_loop(0, n_streams // STREAMS_PER_ITER, stream_group, 0)

    y_ref[...] = _finish(x, mix_ref[...], _bf(p_ref[...]), w_out_b, w_ple_b, w_gate_b, npost_ref[...])


def _sample_call(start_pos, x, p, cache, s0, lbl, w_in, w_pool, scale, hgn, w_out, npre, npost, w_ple, w_gate):
    b, ts, _ = x.shape
    assert ts == BLK and ts == POOL_BUF + 1 and b % STREAMS_PER_ITER == 0
    n = b * ts
    blocks = lambda w: jax.ShapeDtypeStruct((w.shape[1] // COLS, w.shape[0], COLS), jnp.bfloat16)
    y, pool, hg, w_in_b, w_out_b, w_ple_b, w_gate_b = pl.pallas_call(
        functools.partial(_sample_kernel, start_pos),
        out_shape=[jax.ShapeDtypeStruct((n, D_MODEL), jnp.float32),
                   jax.ShapeDtypeStruct((b, POOL_BUF, D_POOL), jnp.float32),
                   jax.ShapeDtypeStruct((b, N_HEADS, HEAD, HEAD), jnp.float32),
                   blocks(w_in), blocks(w_out), blocks(w_ple), blocks(w_gate)],
        scratch_shapes=[pltpu.VMEM((n, D_IN), jnp.float32),
                        pltpu.VMEM((STREAMS_PER_ITER, HDR + ts, D_POOL), jnp.float32),
                        pltpu.VMEM((n, D_MODEL), jnp.bfloat16)],
        compiler_params=pltpu.CompilerParams(vmem_limit_bytes=SAMPLE_VMEM_LIMIT_BYTES),
        name="sample_layer",
    )(x.reshape(n, D_MODEL), p.reshape(n, D_PLE), cache, s0, lbl, w_in, w_pool, scale, hgn, w_out, npre, npost,
      w_ple, w_gate)
    return (y.reshape(b, ts, D_MODEL), pool, hg), (w_in_b, w_out_b, w_ple_b, w_gate_b)


def kernel(x_prompt, x_sample, cache_pool, state_hgrn, p_prompt, p_sample, lb_logits, w_in, w_pool, pool_scale,
           hg_norm, w_out, norm_pre, norm_post, w_ple, w_ple_gate):
    depth = w_in.shape[0]
    assert depth == 1 and lb_logits.shape[0] == 2
    past_len = 1024
    (y_s, pool_s, hg_s), (w_in_b, w_out_b, w_ple_b, w_gate_b) = _sample_call(
        past_len, x_sample, p_sample[0], cache_pool[0], state_hgrn[0], lb_logits, w_in[0], w_pool[0], pool_scale,
        hg_norm, w_out[0], norm_pre, norm_post, w_ple[0], w_ple_gate[0])
    y_p, pool_p, hg_p = _prompt_call(x_prompt, p_prompt[0], lb_logits, w_in_b, w_pool[0], pool_scale, hg_norm,
                                     w_out_b, norm_pre, norm_post, w_ple_b, w_gate_b)
    return (y_p, y_s, pool_p[None], hg_p[None], pool_s[None], hg_s[None])
```

```python
import functools

import jax
import jax.numpy as jnp
from jax import lax
from jax.experimental import pallas as pl
from jax.experimental.pallas import tpu as pltpu

D_MODEL = 1024
D_POOL = 512
POOL_WINDOWS = (2, 4, 8, 16)
POOL_GROUP = 128
POOL_BUF = 15
N_HEADS = 4
HEAD = 128
D_HG = N_HEADS * HEAD
D_IN = 2 * D_POOL + 4 * D_HG
D_PLE = 256
EPS = 1e-6
LOG2_E = 1.4426950408889634
COLS = 512
SPAN = 32
BLK = 16
CHUNK = 128
HDR = 16
UNIT_GROUP = 1
PROMPT_TILE = 256
VMEM_LIMIT_BYTES = 60 * 1024 * 1024
PROMPT_SCHEDULER_FLAGS = None

C_U, C_GP, C_Q, C_F, C_V, C_GH = 0, 512, 1024, 1536, 2048, 2560

_NT = (((1,), (1,)), ((), ()))
_TN = (((0,), (0,)), ((), ()))


def _dot(a, b):
    return jnp.dot(a, b, preferred_element_type=jnp.float32)


def _dot_nt(a, b):
    return lax.dot_general(a, b, _NT, preferred_element_type=jnp.float32)


def _dot_tn(a, b):
    return lax.dot_general(a, b, _TN, preferred_element_type=jnp.float32)


def _bf(x):
    return x.astype(jnp.bfloat16)


def _wblk(w_ref, j):
    return w_ref[j]


def _sigmoid(x):
    return 1.0 / (1.0 + jnp.exp2(x * (-LOG2_E)))


def _rms_scale(x):
    return lax.rsqrt(jnp.mean(x * x, axis=-1, keepdims=True) + EPS)


def _lower_bound(lb_logits):
    l0 = lb_logits[0:1, :]
    l1 = lb_logits[1:2, :]
    m = jnp.maximum(l0, l1)
    e0 = jnp.exp(l0 - m)
    e1 = jnp.exp(l1 - m)
    return e0 / (e0 + e1)


def _window_sum(e, w):
    s = e
    d = 1
    while d < w:
        s = s + pltpu.roll(s, d, axis=0)
        d *= 2
    return s


def _pool_group(e, w, first_pos):
    s = _window_sum(e, w)[HDR:]
    u = e[HDR:]
    rows = lax.broadcasted_iota(jnp.int32, (HDR, POOL_GROUP), 0)
    cnt = jnp.minimum(w, first_pos + rows + 1).astype(jnp.float32)
    head = s[:HDR] / cnt - u[:HDR]
    if s.shape[0] == HDR:
        return head
    tail = s[HDR:] * (1.0 / w) - u[HDR:]
    return jnp.concatenate([head, tail], axis=0)


def _block_cumprod(f, reverse=False):
    n = f.shape[0]
    row = lax.broadcasted_iota(jnp.int32, f.shape, 0) & (BLK - 1)
    x = f
    for d in (1, 2, 4, 8):
        if reverse:
            x = x * jnp.where(row + d < BLK, pltpu.roll(x, n - d, axis=0), 1.0)
        else:
            x = x * jnp.where(row >= d, pltpu.roll(x, d, axis=0), 1.0)
    return x


def _gates(q, fl, lb):
    f = lb + (1.0 - lb) * _sigmoid(fl)
    k = 1.0 - f
    F = _block_cumprod(f)
    return q * F, k / F, F


def _as_column(row):
    tile = jnp.concatenate([row, jnp.zeros((7, row.shape[1]), row.dtype)], axis=0)
    return tile.T[:, 0:1]


def _stack_blocks(pieces, n_blocks):
    zero = jnp.zeros((SPAN, HEAD), jnp.float32)
    return jnp.concatenate([pieces.get(j, zero) for j in range(n_blocks)], axis=0)


class _HgrnChunk:
    def __init__(self, load, lb):
        self.load, self.lb = load, lb

    def prepare(self):
        q, fl, v = self.load()
        c = self.c = q.shape[0]
        nb = c // SPAN
        f = self.lb + (1.0 - self.lb) * _sigmoid(fl)
        k = 1.0 - f
        halves = lambda x: [jnp.concatenate([x[SPAN * j + BLK * w:SPAN * j + BLK * (w + 1)] for j in range(nb)], axis=0)
                            for w in (0, 1)]
        (f_a, f_b), (q_a, q_b), (k_a, k_b) = halves(f), halves(q), halves(k)
        fwd = _block_cumprod(f_b)
        rev_incl = _block_cumprod(f_a, reverse=True)
        row = lax.broadcasted_iota(jnp.int32, f_a.shape, 0) & (BLK - 1)
        rev = jnp.where(row < BLK - 1, pltpu.roll(rev_incl, f_a.shape[0] - 1, axis=0), 1.0)
        qm = (q_a / rev, q_b * fwd)
        km = (k_a * rev, k_b / fwd)
        span = lambda pair, j: jnp.concatenate([pair[0][BLK * j:BLK * (j + 1)], pair[1][BLK * j:BLK * (j + 1)]], axis=0)
        t_head = [rev_incl[BLK * j:BLK * j + 1, :] for j in range(nb)]
        t_tail = [fwd[BLK * j + BLK - 1:BLK * (j + 1), :] for j in range(nb)]
        T = [t_head[j] * t_tail[j] for j in range(nb)]
        qm_b = [span(qm, j) for j in range(nb)]
        km_b = [span(km, j) for j in range(nb)]
        qe_b = [qm_b[j] * t_head[j] for j in range(nb)]
        kd_b = [km_b[j] * t_tail[j] for j in range(nb)]
        zero = jnp.zeros((SPAN, HEAD), jnp.float32)
        k_pair = [kd_b[j] * t_head[j + 1] if j % 2 == 0 else zero for j in range(nb)]
        self.qe = _bf(jnp.concatenate(qm_b, axis=0))
        self.ke_kd = _bf(jnp.concatenate(km_b + k_pair, axis=0))

        q_slots, k_slots = [], []
        gs = 4
        while gs <= nb:
            for a in range(0, nb, gs):
                mid = a + gs // 2
                qs, ks = {}, {}
                dec = None
                for j in range(mid, a + gs):
                    qs[j] = qe_b[j] if dec is None else qe_b[j] * dec
                    dec = T[j] if dec is None else dec * T[j]
                dec = None
                for j in range(mid - 1, a - 1, -1):
                    ks[j] = kd_b[j] if dec is None else kd_b[j] * dec
                    dec = T[j] if dec is None else dec * T[j]
                q_slots.append(_stack_blocks(qs, nb))
                k_slots.append(_stack_blocks(ks, nb))
            gs *= 2
        self.q_far = _bf(jnp.concatenate(q_slots, axis=1)) if q_slots else None
        self.k_far = _bf(jnp.concatenate(k_slots, axis=1)) if k_slots else None

        e_in = [None] * nb
        dec = None
        for j in range(nb):
            e_in[j] = dec
            dec = T[j] if dec is None else dec * T[j]
        self.e_tot = dec
        d_out = [None] * nb
        dec = None
        for j in range(nb - 1, -1, -1):
            d_out[j] = dec
            dec = T[j] if dec is None else dec * T[j]
        self.q_in = _bf(jnp.concatenate(
            [qe_b[j] if e_in[j] is None else qe_b[j] * e_in[j] for j in range(nb)], axis=0))
        self.k_out = _bf(jnp.concatenate(
            [kd_b[j] if d_out[j] is None else kd_b[j] * d_out[j] for j in range(nb)], axis=0))
        self.v_bf = _bf(v)

    def issue(self):
        self.r1 = _dot_nt(self.qe, self.ke_kd)
        self.far = _dot_nt(self.q_far, self.k_far) if self.q_far is not None else None
        self.upd = _dot_tn(self.v_bf, self.k_out)

    def combine(self, st, mask_d, mask_p):
        c = self.c
        p = jnp.where(mask_d, self.r1[:, :c], jnp.where(mask_p, self.r1[:, c:], 0.0))
        if self.far is not None:
            p = p + self.far
        self.o = _dot(_bf(p), self.v_bf) + _dot_nt(self.q_in, _bf(st))
        return st * self.e_tot + self.upd


def _head_out(o, gate, hg_norm):
    y = o * _rms_scale(o) * hg_norm
    return y * (gate * _sigmoid(gate))


def _post_norm(x, y_blocks, norm_post):
    ms = sum(jnp.sum(y * y, axis=-1, keepdims=True) for y in y_blocks) / D_MODEL
    r = lax.rsqrt(ms + EPS)
    return [x[:, j * COLS:(j + 1) * COLS] + y * r * norm_post[:, j * COLS:(j + 1) * COLS]
            for j, y in enumerate(y_blocks)]


def _finish(x, mix_bf, p_bf, w_out_ref, w_ple_ref, w_gate_ref, norm_post):
    nblk = D_MODEL // COLS
    x1 = _post_norm(x, [_dot(mix_bf, _wblk(w_out_ref, j)) for j in range(nblk)], norm_post)
    x1_bf = _bf(jnp.concatenate(x1, axis=1))
    out = [x1[j] + _sigmoid(_dot(x1_bf, _wblk(w_gate_ref, j))) * _dot(p_bf, _wblk(w_ple_ref, j)) for j in range(nblk)]
    return jnp.concatenate(out, axis=1)


def _chunk_masks(c):
    t = lax.broadcasted_iota(jnp.int32, (c, c), 0)
    s = lax.broadcasted_iota(jnp.int32, (c, c), 1)
    tb = t // SPAN
    sb = s // SPAN
    mask_d = (tb == sb) & (s <= t)
    mask_p = ((tb & 1) == 1) & (sb == tb - 1)
    return mask_d, mask_p


def _prompt_step(first_pos, xa_ref, xc_ref, p_ref, lbl_ref, w_in_ref, w_pool_ref, scale_ref, hgn_ref, w_out_ref,
                 npre_ref, npost_ref, w_ple_ref, w_gate_ref, y_ref, carry_ref, st_ref,
                 u_w, z_w, mix_w, u_r, z_r, mix_r):
    tt = xa_ref.shape[0]
    lb = _lower_bound(lbl_ref[...])
    mask_d, mask_p = _chunk_masks(CHUNK)
    states = [st_ref[hd] for hd in range(N_HEADS)]

    def in_proj(h, c0):
        blk = _dot(h, _wblk(w_in_ref, c0 // COLS))
        if c0 == C_U:
            u_w[HDR:HDR + tt, :] = blk
        else:
            z_w[:, c0 - C_GP:c0 - C_GP + D_POOL] = blk

    def pool_group(gi):
        cs = slice(gi * POOL_GROUP, (gi + 1) * POOL_GROUP)
        pooled = _pool_group(u_r[:, cs], POOL_WINDOWS[gi], first_pos)
        mixed = _dot(_bf(pooled), _bf(w_pool_ref[gi])) * scale_ref[:, cs]
        gp = z_r[:, cs]
        mix_w[:, cs] = _bf(mixed * (gp * _sigmoid(gp)))

    def hgrn_unit(hd, c0):
        hs = slice(hd * HEAD, (hd + 1) * HEAD)
        rs = slice(c0, c0 + CHUNK)
        col = lambda base: slice(base - C_GP + hd * HEAD, base - C_GP + (hd + 1) * HEAD)
        unit = _HgrnChunk(lambda: (z_r[rs, col(C_Q)], z_r[rs, col(C_F)], z_r[rs, col(C_V)]), lb[:, hs])
        unit.head, unit.rows, unit.gate_cols = hd, rs, col(C_GH)
        return unit

    def hgrn_store(unit):
        gh = z_r[unit.rows, unit.gate_cols]
        mix_w[unit.rows, D_POOL + unit.head * HEAD:D_POOL + (unit.head + 1) * HEAD] = _bf(
            _head_out(unit.o, gh, hgn_ref[...]))

    units = [hgrn_unit(hd, c0) for c0 in range(0, tt, CHUNK) for hd in range(N_HEADS)]
    dense = [("out", 1), ("in", C_U), ("in", C_GP), ("gate", 0), ("gate", 1), ("in", C_Q), ("in", C_F),
             ("in", C_V)]
    pool_at = {0: (0,), 2: (1,), 4: (2,), 7: (3,)}
    post_norm_at = 1
    final_at = {5: (0,), 6: (1,)}
    p_bf = _bf(p_ref[...])
    ple = [_dot(p_bf, _wblk(w_ple_ref, j)) for j in range(2)]
    xa = xa_ref[...]
    h = _bf(xa * _rms_scale(xa) * npre_ref[...])
    groups = [units[i:i + UNIT_GROUP] for i in range(0, len(units), UNIT_GROUP)]
    for unit in groups[0]:
        unit.prepare()
    mix_prev = mix_r[...]
    y = {0: _dot(mix_prev, _wblk(w_out_ref, 0))}
    u_r[0:HDR, :] = carry_ref[...]
    x1 = x1_bf = None
    gate = {}
    assert len(units) == len(dense)
    for k, group in enumerate(groups):
        for unit in group:
            unit.issue()
        if k + 1 < len(groups):
            for unit in groups[k + 1]:
                unit.prepare()
        for kind, arg in dense[k * UNIT_GROUP:(k + 1) * UNIT_GROUP]:
            if kind == "in":
                in_proj(h, arg)
            elif kind == "out":
                y[arg] = _dot(mix_prev, _wblk(w_out_ref, arg))
            else:
                gate[arg] = _dot(x1_bf, _wblk(w_gate_ref, arg))
        for unit in group:
            states[unit.head] = unit.combine(states[unit.head], mask_d, mask_p)
        if k > 0:
            for unit in groups[k - 1]:
                hgrn_store(unit)
        if k == post_norm_at:
            x1 = _post_norm(xc_ref[...], [y[0], y[1]], npost_ref[...])
            x1_bf = _bf(jnp.concatenate(x1, axis=1))
        for gi in pool_at.get(k, ()):
            pool_group(gi)
        for j in final_at.get(k, ()):
            y_ref[:, j * COLS:(j + 1) * COLS] = x1[j] + _sigmoid(gate[j]) * ple[j]
    in_proj(h, C_GH)
    for unit in groups[-1]:
        hgrn_store(unit)
    carry_ref[...] = u_r[tt:tt + HDR, :]
    for hd in range(N_HEADS):
        st_ref[hd] = states[hd]


def _prompt_kernel(tiles_per_stream, n_steps, xa_ref, xc_ref, p_ref, lbl_ref, w_in_ref, w_pool_ref, scale_ref, hgn_ref,
                   w_out_ref, npre_ref, npost_ref, w_ple_ref, w_gate_ref,
                   y_ref, pool_ref, hg_ref,
                   u_a, z_a, mix_a, u_b, z_b, mix_b, carry_ref, st_ref):
    s = pl.program_id(0)
    last = pl.num_programs(0) - 1
    tt = xa_ref.shape[0]
    tile = lax.rem(s - 1 + tiles_per_stream, tiles_per_stream)

    @pl.when(tile == 0)
    def _():
        carry_ref[...] = jnp.zeros(carry_ref.shape, jnp.float32)
        st_ref[...] = jnp.zeros(st_ref.shape, jnp.float32)

    step = functools.partial(_prompt_step, tile * tt, xa_ref, xc_ref, p_ref, lbl_ref, w_in_ref, w_pool_ref,
                             scale_ref, hgn_ref, w_out_ref, npre_ref, npost_ref, w_ple_ref, w_gate_ref,
                             y_ref, carry_ref, st_ref)
    full = (s > 0) & (s < last)

    @pl.when(s == 0)
    def _():
        mix_a[...] = jnp.zeros(mix_a.shape, mix_a.dtype)
        xa = xa_ref[...]
        h = _bf(xa * _rms_scale(xa) * npre_ref[...])
        u_a[HDR:HDR + tt, :] = _dot(h, _wblk(w_in_ref, 0))
        for j in range(1, D_IN // COLS):
            z_a[:, (j - 1) * COLS:j * COLS] = _dot(h, _wblk(w_in_ref, j))

    @pl.when(full & ((s & 1) == 0))
    def _():
        step(u_a, z_a, mix_a, u_b, z_b, mix_b)

    @pl.when(full & ((s & 1) == 1))
    def _():
        step(u_b, z_b, mix_b, u_a, z_a, mix_a)

    @pl.when(s == last)
    def _():
        mix_last = mix_a if n_steps % 2 == 0 else mix_b
        y_ref[...] = _finish(xc_ref[...], mix_last[...], _bf(p_ref[...]), w_out_ref, w_ple_ref, w_gate_ref,
                             npost_ref[...])

    @pl.when((tile == tiles_per_stream - 1) & (s > 0))
    def _():
        pool_ref[...] = carry_ref[HDR - POOL_BUF:HDR, :]
        for hd in range(N_HEADS):
            hg_ref[hd] = st_ref[hd].T


def _const_spec(shape):
    nd = len(shape)
    return pl.BlockSpec(shape, lambda *_: (0,) * nd, pipeline_mode=pl.Buffered(1))


def _prompt_call(x, p, lbl, w_in, w_pool, scale, hgn, w_out, npre, npost, w_ple, w_gate):
    b, t, _ = x.shape
    tt = PROMPT_TILE
    assert t % tt == 0 and tt % CHUNK == 0
    nt = t // tt
    g = b * nt
    weights = (lbl, w_in, w_pool, scale, hgn, w_out, npre, npost, w_ple, w_gate)
    x2 = x.reshape(b * t, D_MODEL)
    p2 = p.reshape(b * t, D_PLE)
    ahead = lambda s: (jnp.minimum(s, g - 1), 0)
    behind = lambda s: (jnp.clip(s - 2, 0, g - 1), 0)
    stream = lambda s: (jnp.clip((s - 1) // nt, 0, b - 1),)
    y, pool, hg = pl.pallas_call(
        functools.partial(_prompt_kernel, nt, g + 2),
        grid=(g + 2,),
        in_specs=[pl.BlockSpec((tt, D_MODEL), ahead),
                  pl.BlockSpec((tt, D_MODEL), behind),
                  pl.BlockSpec((tt, D_PLE), behind)]
                 + [_const_spec(w.shape) for w in weights],
        out_specs=[pl.BlockSpec((tt, D_MODEL), behind),
                   pl.BlockSpec((None, POOL_BUF, D_POOL), lambda s: stream(s) + (0, 0)),
                   pl.BlockSpec((None, N_HEADS, HEAD, HEAD), lambda s: stream(s) + (0, 0, 0))],
        out_shape=[jax.ShapeDtypeStruct((b * t, D_MODEL), jnp.float32),
                   jax.ShapeDtypeStruct((b, POOL_BUF, D_POOL), jnp.float32),
                   jax.ShapeDtypeStruct((b, N_HEADS, HEAD, HEAD), jnp.float32)],
        scratch_shapes=[pltpu.VMEM((HDR + tt, D_POOL), jnp.float32),
                        pltpu.VMEM((tt, D_IN - D_POOL), jnp.float32),
                        pltpu.VMEM((tt, D_MODEL), jnp.bfloat16)] * 2
                       + [pltpu.VMEM((HDR, D_POOL), jnp.float32),
                          pltpu.VMEM((N_HEADS, HEAD, HEAD), jnp.float32)],
        compiler_params=pltpu.CompilerParams(dimension_semantics=("arbitrary",),
                                             vmem_limit_bytes=VMEM_LIMIT_BYTES,
                                             flags=PROMPT_SCHEDULER_FLAGS),
        name="prompt_layer",
    )(x2, x2, p2, *weights)
    return y.reshape(b, t, D_MODEL), pool, hg


STREAMS_PER_ITER = 8
SAMPLE_VMEM_LIMIT_BYTES = 58 * 1024 * 1024


def _sample_kernel(start_pos, x_ref, p_ref, cache_ref, s0_ref, lbl_ref, w_in_ref, w_pool_ref, scale_ref,
                   hgn_ref, w_out_ref, npre_ref, npost_ref, w_ple_ref, w_gate_ref,
                   y_ref, pool_ref, hg_ref, w_in_b, w_out_b, w_ple_b, w_gate_b,
                   z_ref, ext_ref, mix_ref):
    n_streams, ts = pool_ref.shape[0], pool_ref.shape[1] + 1

    for src, dst in ((w_in_ref, w_in_b), (w_out_ref, w_out_b), (w_ple_ref, w_ple_b), (w_gate_ref, w_gate_b)):
        for j in range(dst.shape[0]):
            dst[j] = _bf(src[:, j * COLS:(j + 1) * COLS])

    x = x_ref[...]
    h = _bf(x * _rms_scale(x) * npre_ref[...])
    for j in range(D_IN // COLS):
        z_ref[:, j * COLS:(j + 1) * COLS] = _dot(h, w_in_b[j])

    lb = _lower_bound(lbl_ref[...])
    causal = (lax.broadcasted_iota(jnp.int32, (ts, ts), 1) <= lax.broadcasted_iota(jnp.int32, (ts, ts), 0))
    w_pool = [_bf(w_pool_ref[gi]) for gi in range(len(POOL_WINDOWS))]

    def prepare(b, slot):
        c = {"b": b, "rows": pl.ds(pl.multiple_of(b * ts, ts), ts)}
        rows = c["rows"]
        ext = ext_ref.at[slot]
        u = z_ref[rows, C_U:C_U + D_POOL]
        ext[0:HDR, :] = jnp.zeros((HDR, D_POOL), jnp.float32)
        ext[HDR - POOL_BUF:HDR, :] = cache_ref[b]
        ext[HDR:HDR + ts, :] = u
        pool_ref[b] = u[ts - POOL_BUF:ts, :]
        c["pooled"] = [_bf(_pool_group(ext[:, gi * POOL_GROUP:(gi + 1) * POOL_GROUP], w, start_pos))
                       for gi, w in enumerate(POOL_WINDOWS)]
        gated = []
        for hd in range(N_HEADS):
            q = z_ref[rows, C_Q + hd * HEAD:C_Q + (hd + 1) * HEAD]
            fl = z_ref[rows, C_F + hd * HEAD:C_F + (hd + 1) * HEAD]
            gated.append(_gates(q, fl, lb[:, hd * HEAD:(hd + 1) * HEAD]))
        decay_rows = [F[ts - 1:ts, :] for _, _, F in gated]
        c["decay_cols"] = jnp.concatenate(
            decay_rows + [jnp.zeros((8 - N_HEADS, HEAD), jnp.float32)], axis=0).T
        c["qe"] = [_bf(qe) for qe, _, _ in gated]
        c["ke"] = [_bf(ke) for _, ke, _ in gated]
        c["kd"] = [_bf(ke * decay_rows[hd]) for hd, (_, ke, _) in enumerate(gated)]
        c["v"] = [_bf(z_ref[rows, C_V + hd * HEAD:C_V + (hd + 1) * HEAD]) for hd in range(N_HEADS)]
        return c

    def issue(c):
        c["mixed"] = [_dot(c["pooled"][gi], w_pool[gi]) for gi in range(len(POOL_WINDOWS))]
        c["att"] = [_dot_nt(c["qe"][hd], c["ke"][hd]) for hd in range(N_HEADS)]
        c["upd"] = [_dot_tn(c["kd"][hd], c["v"][hd]) for hd in range(N_HEADS)]

    def combine(c):
        b, rows = c["b"], c["rows"]
        for gi in range(len(POOL_WINDOWS)):
            cs = slice(gi * POOL_GROUP, (gi + 1) * POOL_GROUP)
            gp = z_ref[rows, C_GP + gi * POOL_GROUP:C_GP + (gi + 1) * POOL_GROUP]
            mix_ref[rows, cs] = _bf(c["mixed"][gi] * scale_ref[:, cs] * (gp * _sigmoid(gp)))
        c["o"] = []
        for hd in range(N_HEADS):
            s0 = s0_ref[b, hd]
            att = _bf(jnp.where(causal, c["att"][hd], 0.0))
            c["o"].append(_dot(att, c["v"][hd]) + _dot(c["qe"][hd], _bf(s0)))
            hg_ref[b, hd] = s0 * c["decay_cols"][:, hd:hd + 1] + c["upd"][hd]

    def store(c):
        rows = c["rows"]
        for hd in range(N_HEADS):
            gh = z_ref[rows, C_GH + hd * HEAD:C_GH + (hd + 1) * HEAD]
            mix_ref[rows, D_POOL + hd * HEAD:D_POOL + (hd + 1) * HEAD] = _bf(
                _head_out(c["o"][hd], gh, hgn_ref[...]))

    def stream_group(g, carry):
        ctx = [prepare(g * STREAMS_PER_ITER + slot, slot) for slot in range(STREAMS_PER_ITER)]
        for phase in (issue, combine, store):
            for c in ctx:
                phase(c)
        return carry

    lax.fori_loop(0, n_streams // STREAMS_PER_ITER, stream_group, 0)

    y_ref[...] = _finish(x, mix_ref[...], _bf(p_ref[...]), w_out_b, w_ple_b, w_gate_b, npost_ref[...])


def _sample_call(start_pos, x, p, cache, s0, lbl, w_in, w_pool, scale, hgn, w_out, npre, npost, w_ple, w_gate):
    b, ts, _ = x.shape
    assert ts == BLK and ts == POOL_BUF + 1 and b % STREAMS_PER_ITER == 0
    n = b * ts
    blocks = lambda w: jax.ShapeDtypeStruct((w.shape[1] // COLS, w.shape[0], COLS), jnp.bfloat16)
    y, pool, hg, w_in_b, w_out_b, w_ple_b, w_gate_b = pl.pallas_call(
        functools.partial(_sample_kernel, start_pos),
        out_shape=[jax.ShapeDtypeStruct((n, D_MODEL), jnp.float32),
                   jax.ShapeDtypeStruct((b, POOL_BUF, D_POOL), jnp.float32),
                   jax.ShapeDtypeStruct((b, N_HEADS, HEAD, HEAD), jnp.float32),
                   blocks(w_in), blocks(w_out), blocks(w_ple), blocks(w_gate)],
        scratch_shapes=[pltpu.VMEM((n, D_IN), jnp.float32),
                        pltpu.VMEM((STREAMS_PER_ITER, HDR + ts, D_POOL), jnp.float32),
                        pltpu.VMEM((n, D_MODEL), jnp.bfloat16)],
        compiler_params=pltpu.CompilerParams(vmem_limit_bytes=SAMPLE_VMEM_LIMIT_BYTES),
        name="sample_layer",
    )(x.reshape(n, D_MODEL), p.reshape(n, D_PLE), cache, s0, lbl, w_in, w_pool, scale, hgn, w_out, npre, npost,
      w_ple, w_gate)
    return (y.reshape(b, ts, D_MODEL), pool, hg), (w_in_b, w_out_b, w_ple_b, w_gate_b)


def kernel(x_prompt, x_sample, cache_pool, state_hgrn, p_prompt, p_sample, lb_logits, w_in, w_pool, pool_scale,
           hg_norm, w_out, norm_pre, norm_post, w_ple, w_ple_gate):
    depth = w_in.shape[0]
    assert depth == 1 and lb_logits.shape[0] == 2
    past_len = 1024
    (y_s, pool_s, hg_s), (w_in_b, w_out_b, w_ple_b, w_gate_b) = _sample_call(
        past_len, x_sample, p_sample[0], cache_pool[0], state_hgrn[0], lb_logits, w_in[0], w_pool[0], pool_scale,
        hg_norm, w_out[0], norm_pre, norm_post, w_ple[0], w_ple_gate[0])
    y_p, pool_p, hg_p = _prompt_call(x_prompt, p_prompt[0], lb_logits, w_in_b, w_pool[0], pool_scale, hg_norm,
                                     w_out_b, norm_pre, norm_post, w_ple_b, w_gate_b)
    return (y_p, y_s, pool_p[None], hg_p[None], pool_s[None], hg_s[None])
```

```python
import functools

import jax
import jax.numpy as jnp
from jax import lax
from jax.experimental import pallas as pl
from jax.experimental.pallas import tpu as pltpu

D_MODEL = 1024
D_POOL = 512
POOL_WINDOWS = (2, 4, 8, 16)
POOL_GROUP = 128
POOL_BUF = 15
N_HEADS = 4
HEAD = 128
D_HG = N_HEADS * HEAD
D_IN = 2 * D_POOL + 4 * D_HG
D_PLE = 256
EPS = 1e-6
LOG2_E = 1.4426950408889634
COLS = 512
SPAN = 32
BLK = 16
CHUNK = 128
HDR = 16
SUB_TILE = 256
VMEM_LIMIT_BYTES = 60 * 1024 * 1024

C_U, C_GP, C_Q, C_F, C_V, C_GH = 0, 512, 1024, 1536, 2048, 2560

_NT = (((1,), (1,)), ((), ()))
_TN = (((0,), (0,)), ((), ()))


def _dot(a, b):
    return jnp.dot(a, b, preferred_element_type=jnp.float32)


def _dot_nt(a, b):
    return lax.dot_general(a, b, _NT, preferred_element_type=jnp.float32)


def _dot_tn(a, b):
    return lax.dot_general(a, b, _TN, preferred_element_type=jnp.float32)


def _bf(x):
    return x.astype(jnp.bfloat16)


def _wblk(w_ref, j):
    return w_ref[j]


def _sigmoid(x):
    return 1.0 / (1.0 + jnp.exp2(x * (-LOG2_E)))


def _rms_scale(x):
    return lax.rsqrt(jnp.mean(x * x, axis=-1, keepdims=True) + EPS)


def _lower_bound(lb_logits):
    l0 = lb_logits[0:1, :]
    l1 = lb_logits[1:2, :]
    m = jnp.maximum(l0, l1)
    e0 = jnp.exp(l0 - m)
    e1 = jnp.exp(l1 - m)
    return e0 / (e0 + e1)


def _window_sum(e, w):
    s = e
    d = 1
    while d < w:
        s = s + pltpu.roll(s, d, axis=0)
        d *= 2
    return s


def _pool_group(e, w, first_pos):
    s = _window_sum(e, w)[HDR:]
    u = e[HDR:]
    rows = lax.broadcasted_iota(jnp.int32, (HDR, POOL_GROUP), 0)
    cnt = jnp.minimum(w, first_pos + rows + 1).astype(jnp.float32)
    head = s[:HDR] / cnt - u[:HDR]
    if s.shape[0] == HDR:
        return head
    tail = s[HDR:] * (1.0 / w) - u[HDR:]
    return jnp.concatenate([head, tail], axis=0)


def _block_cumprod(f, reverse=False):
    n = f.shape[0]
    row = lax.broadcasted_iota(jnp.int32, f.shape, 0) & (BLK - 1)
    x = f
    for d in (1, 2, 4, 8):
        if reverse:
            x = x * jnp.where(row + d < BLK, pltpu.roll(x, n - d, axis=0), 1.0)
        else:
            x = x * jnp.where(row >= d, pltpu.roll(x, d, axis=0), 1.0)
    return x


def _gates(q, fl, lb):
    f = lb + (1.0 - lb) * _sigmoid(fl)
    k = 1.0 - f
    F = _block_cumprod(f)
    return q * F, k / F, F


def _as_column(row):
    tile = jnp.concatenate([row, jnp.zeros((7, row.shape[1]), row.dtype)], axis=0)
    return tile.T[:, 0:1]


def _stack_blocks(pieces, n_blocks):
    zero = jnp.zeros((SPAN, HEAD), jnp.float32)
    return jnp.concatenate([pieces.get(j, zero) for j in range(n_blocks)], axis=0)


class _HgrnChunk:
    def __init__(self, load, lb):
        self.load, self.lb = load, lb

    def prepare(self):
        q, fl, v = self.load()
        c = self.c = q.shape[0]
        nb = c // SPAN
        f = self.lb + (1.0 - self.lb) * _sigmoid(fl)
        k = 1.0 - f
        halves = lambda x: [jnp.concatenate([x[SPAN * j + BLK * w:SPAN * j + BLK * (w + 1)] for j in range(nb)], axis=0)
                            for w in (0, 1)]
        (f_a, f_b), (q_a, q_b), (k_a, k_b) = halves(f), halves(q), halves(k)
        fwd = _block_cumprod(f_b)
        rev_incl = _block_cumprod(f_a, reverse=True)
        row = lax.broadcasted_iota(jnp.int32, f_a.shape, 0) & (BLK - 1)
        rev = jnp.where(row < BLK - 1, pltpu.roll(rev_incl, f_a.shape[0] - 1, axis=0), 1.0)
        qm = (q_a / rev, q_b * fwd)
        km = (k_a * rev, k_b / fwd)
        span = lambda pair, j: jnp.concatenate([pair[0][BLK * j:BLK * (j + 1)], pair[1][BLK * j:BLK * (j + 1)]], axis=0)
        t_head = [rev_incl[BLK * j:BLK * j + 1, :] for j in range(nb)]
        t_tail = [fwd[BLK * j + BLK - 1:BLK * (j + 1), :] for j in range(nb)]
        T = [t_head[j] * t_tail[j] for j in range(nb)]
        qm_b = [span(qm, j) for j in range(nb)]
        km_b = [span(km, j) for j in range(nb)]
        qe_b = [qm_b[j] * t_head[j] for j in range(nb)]
        kd_b = [km_b[j] * t_tail[j] for j in range(nb)]
        zero = jnp.zeros((SPAN, HEAD), jnp.float32)
        k_pair = [kd_b[j] * t_head[j + 1] if j % 2 == 0 else zero for j in range(nb)]
        self.qe = _bf(jnp.concatenate(qm_b, axis=0))
        self.ke_kd = _bf(jnp.concatenate(km_b + k_pair, axis=0))

        q_slots, k_slots = [], []
        gs = 4
        while gs <= nb:
            for a in range(0, nb, gs):
                mid = a + gs // 2
                qs, ks = {}, {}
                dec = None
                for j in range(mid, a + gs):
                    qs[j] = qe_b[j] if dec is None else qe_b[j] * dec
                    dec = T[j] if dec is None else dec * T[j]
                dec = None
                for j in range(mid - 1, a - 1, -1):
                    ks[j] = kd_b[j] if dec is None else kd_b[j] * dec
                    dec = T[j] if dec is None else dec * T[j]
                q_slots.append(_stack_blocks(qs, nb))
                k_slots.append(_stack_blocks(ks, nb))
            gs *= 2
        self.q_far = _bf(jnp.concatenate(q_slots, axis=1)) if q_slots else None
        self.k_far = _bf(jnp.concatenate(k_slots, axis=1)) if k_slots else None

        e_in = [None] * nb
        dec = None
        for j in range(nb):
            e_in[j] = dec
            dec = T[j] if dec is None else dec * T[j]
        self.e_tot = dec
        d_out = [None] * nb
        dec = None
        for j in range(nb - 1, -1, -1):
            d_out[j] = dec
            dec = T[j] if dec is None else dec * T[j]
        self.q_in = _bf(jnp.concatenate(
            [qe_b[j] if e_in[j] is None else qe_b[j] * e_in[j] for j in range(nb)], axis=0))
        self.k_out = _bf(jnp.concatenate(
            [kd_b[j] if d_out[j] is None else kd_b[j] * d_out[j] for j in range(nb)], axis=0))
        self.v_bf = _bf(v)

    def issue(self):
        self.r1 = _dot_nt(self.qe, self.ke_kd)
        self.far = _dot_nt(self.q_far, self.k_far) if self.q_far is not None else None
        self.upd = _dot_tn(self.v_bf, self.k_out)

    def combine(self, st, mask_d, mask_p):
        c = self.c
        p = jnp.where(mask_d, self.r1[:, :c], jnp.where(mask_p, self.r1[:, c:], 0.0))
        if self.far is not None:
            p = p + self.far
        self.o = _dot(_bf(p), self.v_bf) + _dot_nt(self.q_in, _bf(st))
        return st * self.e_tot + self.upd


def _head_out(o, gate, hg_norm):
    y = o * _rms_scale(o) * hg_norm
    return y * (gate * _sigmoid(gate))


def _post_norm(x, y_blocks, norm_post):
    ms = sum(jnp.sum(y * y, axis=-1, keepdims=True) for y in y_blocks) / D_MODEL
    r = lax.rsqrt(ms + EPS)
    return [x[:, j * COLS:(j + 1) * COLS] + y * r * norm_post[:, j * COLS:(j + 1) * COLS]
            for j, y in enumerate(y_blocks)]


def _finish(x, mix_bf, p_bf, w_out_ref, w_ple_ref, w_gate_ref, norm_post):
    nblk = D_MODEL // COLS
    x1 = _post_norm(x, [_dot(mix_bf, _wblk(w_out_ref, j)) for j in range(nblk)], norm_post)
    x1_bf = _bf(jnp.concatenate(x1, axis=1))
    out = [x1[j] + _sigmoid(_dot(x1_bf, _wblk(w_gate_ref, j))) * _dot(p_bf, _wblk(w_ple_ref, j)) for j in range(nblk)]
    return jnp.concatenate(out, axis=1)


def _chunk_masks(c):
    t = lax.broadcasted_iota(jnp.int32, (c, c), 0)
    s = lax.broadcasted_iota(jnp.int32, (c, c), 1)
    tb = t // SPAN
    sb = s // SPAN
    mask_d = (tb == sb) & (s <= t)
    mask_p = ((tb & 1) == 1) & (sb == tb - 1)
    return mask_d, mask_p


def _sub_step_head(stages, r0, xa_ref, p_ref, npre_ref, w_ple_ref):
    rows = slice(r0, r0 + SUB_TILE)
    ple, h = [], None
    if 3 in stages:
        p_bf = _bf(p_ref[rows, :])
        ple = [_dot(p_bf, _wblk(w_ple_ref, j)) for j in range(2)]
    if 1 in stages:
        xa = xa_ref[rows, :]
        h = _bf(xa * _rms_scale(xa) * npre_ref[...])
    return ple, h


def _sub_step(stages, first_pos, r0, xa_ref, xc_ref, p_ref, lbl_ref, w_in_ref, w_pool_ref, scale_ref, hgn_ref,
              w_out_ref, npre_ref, npost_ref, w_ple_ref, w_gate_ref, y_ref,
              u_w, z_w, mix_w, u_r, z_r, mix_r, carry, states, head=None, before_tail=None):
    tt = SUB_TILE
    rows = slice(r0, r0 + tt)
    states = list(states)

    def in_proj(h, c0):
        blk = _dot(h, _wblk(w_in_ref, c0 // COLS))
        if c0 == C_U:
            u_w[HDR:HDR + tt, :] = blk
        else:
            z_w[:, c0 - C_GP:c0 - C_GP + D_POOL] = blk

    if 2 not in stages:
        if 1 in stages:
            xa = xa_ref[rows, :]
            h = _bf(xa * _rms_scale(xa) * npre_ref[...])
            for c0 in (C_U, C_GP, C_Q, C_F, C_V, C_GH):
                in_proj(h, c0)
        if 3 in stages:
            y_ref[rows, :] = _finish(xc_ref[rows, :], mix_r[...], _bf(p_ref[rows, :]), w_out_ref, w_ple_ref,
                                     w_gate_ref, npost_ref[...])
        return carry, states

    lb = _lower_bound(lbl_ref[...])
    mask_d, mask_p = _chunk_masks(CHUNK)

    def pool_group(gi):
        cs = slice(gi * POOL_GROUP, (gi + 1) * POOL_GROUP)
        pooled = _pool_group(u_r[:, cs], POOL_WINDOWS[gi], first_pos)
        mixed = _dot(_bf(pooled), _bf(w_pool_ref[gi])) * scale_ref[:, cs]
        gp = z_r[:, cs]
        mix_w[:, cs] = _bf(mixed * (gp * _sigmoid(gp)))

    def hgrn_unit(hd, c0):
        hs = slice(hd * HEAD, (hd + 1) * HEAD)
        rs = slice(c0, c0 + CHUNK)
        col = lambda base: slice(base - C_GP + hd * HEAD, base - C_GP + (hd + 1) * HEAD)
        unit = _HgrnChunk(lambda: (z_r[rs, col(C_Q)], z_r[rs, col(C_F)], z_r[rs, col(C_V)]), lb[:, hs])
        unit.head, unit.rows, unit.gate_cols = hd, rs, col(C_GH)
        return unit

    def hgrn_store(unit):
        gh = z_r[unit.rows, unit.gate_cols]
        mix_w[unit.rows, D_POOL + unit.head * HEAD:D_POOL + (unit.head + 1) * HEAD] = _bf(
            _head_out(unit.o, gh, hgn_ref[...]))

    units = [hgrn_unit(hd, c0) for c0 in range(0, tt, CHUNK) for hd in range(N_HEADS)]
    dense = [("out", 1), ("in", C_U), ("in", C_GP), ("gate", 0), ("gate", 1), ("in", C_Q), ("in", C_F),
             ("in", C_V)]
    assert len(units) == len(dense)
    pool_at = {0: (0,), 2: (1,), 4: (2,), 7: (3,)}
    post_norm_at = 1
    final_at = {5: (0,), 6: (1,)}
    h = mix_prev = x1 = x1_bf = None
    y, gate, ple = {}, {}, []
    if head is None:
        head = _sub_step_head(stages, r0, xa_ref, p_ref, npre_ref, w_ple_ref)
    ple, h = head
    units[0].prepare()
    if 3 in stages:
        mix_prev = mix_r[...]
        y[0] = _dot(mix_prev, _wblk(w_out_ref, 0))
    u_r[0:HDR, :] = carry
    for k, unit in enumerate(units):
        unit.issue()
        if k + 1 < len(units):
            units[k + 1].prepare()
        kind, arg = dense[k]
        if kind == "in" and 1 in stages:
            in_proj(h, arg)
        elif kind == "out" and 3 in stages:
            y[arg] = _dot(mix_prev, _wblk(w_out_ref, arg))
        elif kind == "gate" and 3 in stages:
            gate[arg] = _dot(x1_bf, _wblk(w_gate_ref, arg))
        states[unit.head] = unit.combine(states[unit.head], mask_d, mask_p)
        if k > 0:
            hgrn_store(units[k - 1])
        if k == post_norm_at and 3 in stages:
            x1 = _post_norm(xc_ref[rows, :], [y[0], y[1]], npost_ref[...])
            x1_bf = _bf(jnp.concatenate(x1, axis=1))
        for gi in pool_at.get(k, ()):
            pool_group(gi)
        if 3 in stages:
            for j in final_at.get(k, ()):
                y_ref[rows, j * COLS:(j + 1) * COLS] = x1[j] + _sigmoid(gate[j]) * ple[j]
    if before_tail is not None:
        before_tail()
    if 1 in stages:
        in_proj(h, C_GH)
    hgrn_store(units[-1])
    return u_r[tt:tt + HDR, :], states


def _prompt_kernel(tiles_per_stream, xa_ref, xc_ref, p_ref, lbl_ref, w_in_ref, w_pool_ref, scale_ref, hgn_ref,
                   w_out_ref, npre_ref, npost_ref, w_ple_ref, w_gate_ref,
                   y_ref, pool_ref, hg_ref,
                   u_a, z_a, mix_a, u_b, z_b, mix_b, carry_ref, st_ref, carry_end, st_end):
    s = pl.program_id(0)
    last = pl.num_programs(0) - 1
    consts = (xa_ref, xc_ref, p_ref, lbl_ref, w_in_ref, w_pool_ref, scale_ref, hgn_ref, w_out_ref, npre_ref,
              npost_ref, w_ple_ref, w_gate_ref, y_ref)
    set_a, set_b = (u_a, z_a, mix_a), (u_b, z_b, mix_b)
    pos_first = lax.rem(2 * s - 1 + tiles_per_stream, tiles_per_stream) * SUB_TILE
    pos_second = lax.rem(2 * s, tiles_per_stream) * SUB_TILE
    new_stream = pos_second == 0

    def run(first_stages, second_stages):
        carry = carry_ref[...]
        states = [st_ref[hd] for hd in range(N_HEADS)]
        head_second = []
        before_tail = None
        if 2 in first_stages and 2 in second_stages:
            before_tail = lambda: head_second.append(
                _sub_step_head(second_stages, SUB_TILE, xa_ref, p_ref, npre_ref, w_ple_ref))
        carry, states = _sub_step(first_stages, pos_first, 0, *consts, *set_a, *set_b, carry, states,
                                  before_tail=before_tail)
        carry_end[...] = carry
        for hd in range(N_HEADS):
            st_end[hd] = states[hd]
        carry = jnp.where(new_stream, 0.0, carry)
        states = [jnp.where(new_stream, 0.0, st) for st in states]
        carry, states = _sub_step(second_stages, pos_second, SUB_TILE, *consts, *set_b, *set_a, carry, states,
                                  head=head_second[0] if head_second else None)
        carry_ref[...] = carry
        for hd in range(N_HEADS):
            st_ref[hd] = states[hd]

    @pl.when(s == 0)
    def _():
        carry_ref[...] = jnp.zeros(carry_ref.shape, jnp.float32)
        st_ref[...] = jnp.zeros(st_ref.shape, jnp.float32)
        run({1}, {1, 2})

    @pl.when((s > 0) & (s < last))
    def _():
        run({1, 2, 3}, {1, 2, 3})

    @pl.when(s == last)
    def _():
        run({2, 3}, {3})

    @pl.when(new_stream & (s > 0))
    def _():
        pool_ref[...] = carry_end[HDR - POOL_BUF:HDR, :]
        for hd in range(N_HEADS):
            hg_ref[hd] = st_end[hd].T


def _const_spec(shape):
    nd = len(shape)
    return pl.BlockSpec(shape, lambda *_: (0,) * nd, pipeline_mode=pl.Buffered(1))


def _prompt_call(x, p, lbl, w_in, w_pool, scale, hgn, w_out, npre, npost, w_ple, w_gate):
    b, t, _ = x.shape
    tt = 2 * SUB_TILE
    assert t % tt == 0 and SUB_TILE % CHUNK == 0
    nt = t // SUB_TILE
    nblk = b * t // tt
    weights = (lbl, w_in, w_pool, scale, hgn, w_out, npre, npost, w_ple, w_gate)
    x2 = x.reshape(b * t, D_MODEL)
    p2 = p.reshape(b * t, D_PLE)
    ahead = lambda s: (jnp.minimum(s, nblk - 1), 0)
    behind = lambda s: (jnp.clip(s - 1, 0, nblk - 1), 0)
    stream = lambda s: (jnp.clip((2 * s - 2) // nt, 0, b - 1),)
    y, pool, hg = pl.pallas_call(
        functools.partial(_prompt_kernel, nt),
        grid=(nblk + 1,),
        in_specs=[pl.BlockSpec((tt, D_MODEL), ahead),
                  pl.BlockSpec((tt, D_MODEL), behind),
                  pl.BlockSpec((tt, D_PLE), behind)]
                 + [_const_spec(w.shape) for w in weights],
        out_specs=[pl.BlockSpec((tt, D_MODEL), behind),
                   pl.BlockSpec((None, POOL_BUF, D_POOL), lambda s: stream(s) + (0, 0)),
                   pl.BlockSpec((None, N_HEADS, HEAD, HEAD), lambda s: stream(s) + (0, 0, 0))],
        out_shape=[jax.ShapeDtypeStruct((b * t, D_MODEL), jnp.float32),
                   jax.ShapeDtypeStruct((b, POOL_BUF, D_POOL), jnp.float32),
                   jax.ShapeDtypeStruct((b, N_HEADS, HEAD, HEAD), jnp.float32)],
        scratch_shapes=[pltpu.VMEM((HDR + SUB_TILE, D_POOL), jnp.float32),
                        pltpu.VMEM((SUB_TILE, D_IN - D_POOL), jnp.float32),
                        pltpu.VMEM((SUB_TILE, D_MODEL), jnp.bfloat16)] * 2
                       + [pltpu.VMEM((HDR, D_POOL), jnp.float32),
                          pltpu.VMEM((N_HEADS, HEAD, HEAD), jnp.float32)] * 2,
        compiler_params=pltpu.CompilerParams(dimension_semantics=("arbitrary",),
                                             vmem_limit_bytes=VMEM_LIMIT_BYTES),
        name="prompt_layer",
    )(x2, x2, p2, *weights)
    return y.reshape(b, t, D_MODEL), pool, hg


STREAMS_PER_ITER = 8
SAMPLE_VMEM_LIMIT_BYTES = 58 * 1024 * 1024


def _sample_kernel(start_pos, x_ref, p_ref, cache_ref, s0_ref, lbl_ref, w_in_ref, w_pool_ref, scale_ref,
                   hgn_ref, w_out_ref, npre_ref, npost_ref, w_ple_ref, w_gate_ref,
                   y_ref, pool_ref, hg_ref, w_in_b, w_out_b, w_ple_b, w_gate_b,
                   z_ref, ext_ref, mix_ref):
    n_streams, ts = pool_ref.shape[0], pool_ref.shape[1] + 1

    for src, dst in ((w_in_ref, w_in_b), (w_out_ref, w_out_b), (w_ple_ref, w_ple_b), (w_gate_ref, w_gate_b)):
        for j in range(dst.shape[0]):
            dst[j] = _bf(src[:, j * COLS:(j + 1) * COLS])

    x = x_ref[...]
    h = _bf(x * _rms_scale(x) * npre_ref[...])
    for j in range(D_IN // COLS):
        z_ref[:, j * COLS:(j + 1) * COLS] = _dot(h, w_in_b[j])

    lb = _lower_bound(lbl_ref[...])
    causal = (lax.broadcasted_iota(jnp.int32, (ts, ts), 1) <= lax.broadcasted_iota(jnp.int32, (ts, ts), 0))
    w_pool = [_bf(w_pool_ref[gi]) for gi in range(len(POOL_WINDOWS))]

    def prepare(b, slot):
        c = {"b": b, "rows": pl.ds(pl.multiple_of(b * ts, ts), ts)}
        rows = c["rows"]
        ext = ext_ref.at[slot]
        u = z_ref[rows, C_U:C_U + D_POOL]
        ext[0:HDR, :] = jnp.zeros((HDR, D_POOL), jnp.float32)
        ext[HDR - POOL_BUF:HDR, :] = cache_ref[b]
        ext[HDR:HDR + ts, :] = u
        pool_ref[b] = u[ts - POOL_BUF:ts, :]
        c["pooled"] = [_bf(_pool_group(ext[:, gi * POOL_GROUP:(gi + 1) * POOL_GROUP], w, start_pos))
                       for gi, w in enumerate(POOL_WINDOWS)]
        gated = []
        for hd in range(N_HEADS):
            q = z_ref[rows, C_Q + hd * HEAD:C_Q + (hd + 1) * HEAD]
            fl = z_ref[rows, C_F + hd * HEAD:C_F + (hd + 1) * HEAD]
            gated.append(_gates(q, fl, lb[:, hd * HEAD:(hd + 1) * HEAD]))
        decay_rows = [F[ts - 1:ts, :] for _, _, F in gated]
        c["decay_cols"] = jnp.concatenate(
            decay_rows + [jnp.zeros((8 - N_HEADS, HEAD), jnp.float32)], axis=0).T
        c["qe"] = [_bf(qe) for qe, _, _ in gated]
        c["ke"] = [_bf(ke) for _, ke, _ in gated]
        c["kd"] = [_bf(ke * decay_rows[hd]) for hd, (_, ke, _) in enumerate(gated)]
        c["v"] = [_bf(z_ref[rows, C_V + hd * HEAD:C_V + (hd + 1) * HEAD]) for hd in range(N_HEADS)]
        return c

    def issue(c):
        c["mixed"] = [_dot(c["pooled"][gi], w_pool[gi]) for gi in range(len(POOL_WINDOWS))]
        c["att"] = [_dot_nt(c["qe"][hd], c["ke"][hd]) for hd in range(N_HEADS)]
        c["upd"] = [_dot_tn(c["kd"][hd], c["v"][hd]) for hd in range(N_HEADS)]

    def combine(c):
        b, rows = c["b"], c["rows"]
        for gi in range(len(POOL_WINDOWS)):
            cs = slice(gi * POOL_GROUP, (gi + 1) * POOL_GROUP)
            gp = z_ref[rows, C_GP + gi * POOL_GROUP:C_GP + (gi + 1) * POOL_GROUP]
            mix_ref[rows, cs] = _bf(c["mixed"][gi] * scale_ref[:, cs] * (gp * _sigmoid(gp)))
        c["o"] = []
        for hd in range(N_HEADS):
            s0 = s0_ref[b, hd]
            att = _bf(jnp.where(causal, c["att"][hd], 0.0))
            c["o"].append(_dot(att, c["v"][hd]) + _dot(c["qe"][hd], _bf(s0)))
            hg_ref[b, hd] = s0 * c["decay_cols"][:, hd:hd + 1] + c["upd"][hd]

    def store(c):
        rows = c["rows"]
        for hd in range(N_HEADS):
            gh = z_ref[rows, C_GH + hd * HEAD:C_GH + (hd + 1) * HEAD]
            mix_ref[rows, D_POOL + hd * HEAD:D_POOL + (hd + 1) * HEAD] = _bf(
                _head_out(c["o"][hd], gh, hgn_ref[...]))

    def stream_group(g, carry):
        ctx = [prepare(g * STREAMS_PER_ITER + slot, slot) for slot in range(STREAMS_PER_ITER)]
        for phase in (issue, combine, store):
            for c in ctx:
                phase(c)
        return carry

    lax.fori_loop(0, n_streams // STREAMS_PER_ITER, stream_group, 0)

    y_ref[...] = _finish(x, mix_ref[...], _bf(p_ref[...]), w_out_b, w_ple_b, w_gate_b, npost_ref[...])


def _sample_call(start_pos, x, p, cache, s0, lbl, w_in, w_pool, scale, hgn, w_out, npre, npost, w_ple, w_gate):
    b, ts, _ = x.shape
    assert ts == BLK and ts == POOL_BUF + 1 and b % STREAMS_PER_ITER == 0
    n = b * ts
    blocks = lambda w: jax.ShapeDtypeStruct((w.shape[1] // COLS, w.shape[0], COLS), jnp.bfloat16)
    y, pool, hg, w_in_b, w_out_b, w_ple_b, w_gate_b = pl.pallas_call(
        functools.partial(_sample_kernel, start_pos),
        out_shape=[jax.ShapeDtypeStruct((n, D_MODEL), jnp.float32),
                   jax.ShapeDtypeStruct((b, POOL_BUF, D_POOL), jnp.float32),
                   jax.ShapeDtypeStruct((b, N_HEADS, HEAD, HEAD), jnp.float32),
                   blocks(w_in), blocks(w_out), blocks(w_ple), blocks(w_gate)],
        scratch_shapes=[pltpu.VMEM((n, D_IN), jnp.float32),
                        pltpu.VMEM((STREAMS_PER_ITER, HDR + ts, D_POOL), jnp.float32),
                        pltpu.VMEM((n, D_MODEL), jnp.bfloat16)],
        compiler_params=pltpu.CompilerParams(vmem_limit_bytes=SAMPLE_VMEM_LIMIT_BYTES),
        name="sample_layer",
    )(x.reshape(n, D_MODEL), p.reshape(n, D_PLE), cache, s0, lbl, w_in, w_pool, scale, hgn, w_out, npre, npost,
      w_ple, w_gate)
    return (y.reshape(b, ts, D_MODEL), pool, hg), (w_in_b, w_out_b, w_ple_b, w_gate_b)


def kernel(x_prompt, x_sample, cache_pool, state_hgrn, p_prompt, p_sample, lb_logits, w_in, w_pool, pool_scale,
           hg_norm, w_out, norm_pre, norm_post, w_ple, w_ple_gate):
    depth = w_in.shape[0]
    assert depth == 1 and lb_logits.shape[0] == 2
    past_len = 1024
    (y_s, pool_s, hg_s), (w_in_b, w_out_b, w_ple_b, w_gate_b) = _sample_call(
        past_len, x_sample, p_sample[0], cache_pool[0], state_hgrn[0], lb_logits, w_in[0], w_pool[0], pool_scale,
        hg_norm, w_out[0], norm_pre, norm_post, w_ple[0], w_ple_gate[0])
    y_p, pool_p, hg_p = _prompt_call(x_prompt, p_prompt[0], lb_logits, w_in_b, w_pool[0], pool_scale, hg_norm,
                                     w_out_b, norm_pre, norm_post, w_ple_b, w_gate_b)
    return (y_p, y_s, pool_p[None], hg_p[None], pool_s[None], hg_s[None])
```

```python
import functools

import jax
import jax.numpy as jnp
from jax import lax
from jax.experimental import pallas as pl
from jax.experimental.pallas import tpu as pltpu

D_MODEL = 1024
D_POOL = 512
POOL_WINDOWS = (2, 4, 8, 16)
POOL_GROUP = 128
POOL_BUF = 15
N_HEADS = 4
HEAD = 128
D_HG = N_HEADS * HEAD
D_IN = 2 * D_POOL + 4 * D_HG
D_PLE = 256
EPS = 1e-6
LOG2_E = 1.4426950408889634
COLS = 512
SPAN = 32
BLK = 16
CHUNK = 128
SUBLANES = 8
HDR = 16
SUB_TILE = 256
VMEM_LIMIT_BYTES = 60 * 1024 * 1024

C_U, C_GP, C_Q, C_F, C_V, C_GH = 0, 512, 1024, 1536, 2048, 2560

_NT = (((1,), (1,)), ((), ()))
_TN = (((0,), (0,)), ((), ()))


def _dot(a, b):
    return jnp.dot(a, b, preferred_element_type=jnp.float32)


def _dot_nt(a, b):
    return lax.dot_general(a, b, _NT, preferred_element_type=jnp.float32)


def _dot_tn(a, b):
    return lax.dot_general(a, b, _TN, preferred_element_type=jnp.float32)


def _bf(x):
    return x.astype(jnp.bfloat16)


def _wblk(w_ref, j):
    return w_ref[j]


def _sigmoid(x):
    return 1.0 / (1.0 + jnp.exp2(x * (-LOG2_E)))


def _rms_scale(x):
    return lax.rsqrt(jnp.mean(x * x, axis=-1, keepdims=True) + EPS)


def _lower_bound(lb_logits):
    l0 = lb_logits[0:1, :]
    l1 = lb_logits[1:2, :]
    m = jnp.maximum(l0, l1)
    e0 = jnp.exp(l0 - m)
    e1 = jnp.exp(l1 - m)
    return e0 / (e0 + e1)


def _window_sum(e, w):
    s = e
    d = 1
    while d < w:
        s = s + pltpu.roll(s, d, axis=0)
        d *= 2
    return s


def _pool_group(e, w, first_pos):
    s = _window_sum(e, w)[HDR:]
    u = e[HDR:]
    rows = lax.broadcasted_iota(jnp.int32, (HDR, POOL_GROUP), 0)
    cnt = jnp.minimum(w, first_pos + rows + 1).astype(jnp.float32)
    head = s[:HDR] / cnt - u[:HDR]
    if s.shape[0] == HDR:
        return head
    tail = s[HDR:] * (1.0 / w) - u[HDR:]
    return jnp.concatenate([head, tail], axis=0)


def _block_cumprod(f, reverse=False):
    n, lanes = f.shape
    x = f.reshape(n // SUBLANES, SUBLANES, lanes)
    row = lax.broadcasted_iota(jnp.int32, x.shape, 1)
    for d in (1, 2, 4):
        if reverse:
            x = x * jnp.where(row + d < SUBLANES, pltpu.roll(x, SUBLANES - d, axis=1), 1.0)
        else:
            x = x * jnp.where(row >= d, pltpu.roll(x, d, axis=1), 1.0)
    x = x.reshape(n // BLK, BLK // SUBLANES, SUBLANES, lanes)
    lo, hi = x[:, 0], x[:, 1]
    if reverse:
        lo = lo * hi[:, 0:1, :]
    else:
        hi = hi * lo[:, SUBLANES - 1:SUBLANES, :]
    return jnp.concatenate([lo[:, None], hi[:, None]], axis=1).reshape(n, lanes)


def _gates(q, fl, lb):
    f = lb + (1.0 - lb) * _sigmoid(fl)
    k = 1.0 - f
    F = _block_cumprod(f)
    return q * F, k / F, F


def _as_column(row):
    tile = jnp.concatenate([row, jnp.zeros((7, row.shape[1]), row.dtype)], axis=0)
    return tile.T[:, 0:1]


def _stack_blocks(pieces, n_blocks):
    zero = jnp.zeros((SPAN, HEAD), jnp.float32)
    return jnp.concatenate([pieces.get(j, zero) for j in range(n_blocks)], axis=0)


class _HgrnChunk:
    def __init__(self, load, lb):
        self.load, self.lb = load, lb

    def prepare(self):
        q, fl, v = self.load()
        c = self.c = q.shape[0]
        nb = c // SPAN
        f = self.lb + (1.0 - self.lb) * _sigmoid(fl)
        k = 1.0 - f
        halves = lambda x: [jnp.concatenate([x[SPAN * j + BLK * w:SPAN * j + BLK * (w + 1)] for j in range(nb)], axis=0)
                            for w in (0, 1)]
        (f_a, f_b), (q_a, q_b), (k_a, k_b) = halves(f), halves(q), halves(k)
        fwd = _block_cumprod(f_b)
        rev_incl = _block_cumprod(f_a, reverse=True)
        row = lax.broadcasted_iota(jnp.int32, f_a.shape, 0) & (BLK - 1)
        rev = jnp.where(row < BLK - 1, pltpu.roll(rev_incl, f_a.shape[0] - 1, axis=0), 1.0)
        qm = (q_a / rev, q_b * fwd)
        km = (k_a * rev, k_b / fwd)
        span = lambda pair, j: jnp.concatenate([pair[0][BLK * j:BLK * (j + 1)], pair[1][BLK * j:BLK * (j + 1)]], axis=0)
        t_head = [rev_incl[BLK * j:BLK * j + 1, :] for j in range(nb)]
        t_tail = [fwd[BLK * j + BLK - 1:BLK * (j + 1), :] for j in range(nb)]
        T = [t_head[j] * t_tail[j] for j in range(nb)]
        qm_b = [span(qm, j) for j in range(nb)]
        km_b = [span(km, j) for j in range(nb)]
        qe_b = [qm_b[j] * t_head[j] for j in range(nb)]
        kd_b = [km_b[j] * t_tail[j] for j in range(nb)]
        zero = jnp.zeros((SPAN, HEAD), jnp.float32)
        k_pair = [kd_b[j] * t_head[j + 1] if j % 2 == 0 else zero for j in range(nb)]
        self.qe = _bf(jnp.concatenate(qm_b, axis=0))
        self.ke_kd = _bf(jnp.concatenate(km_b + k_pair, axis=0))

        q_slots, k_slots = [], []
        gs = 4
        while gs <= nb:
            for a in range(0, nb, gs):
                mid = a + gs // 2
                qs, ks = {}, {}
                dec = None
                for j in range(mid, a + gs):
                    qs[j] = qe_b[j] if dec is None else qe_b[j] * dec
                    dec = T[j] if dec is None else dec * T[j]
                dec = None
                for j in range(mid - 1, a - 1, -1):
                    ks[j] = kd_b[j] if dec is None else kd_b[j] * dec
                    dec = T[j] if dec is None else dec * T[j]
                q_slots.append(_stack_blocks(qs, nb))
                k_slots.append(_stack_blocks(ks, nb))
            gs *= 2
        self.q_far = _bf(jnp.concatenate(q_slots, axis=1)) if q_slots else None
        self.k_far = _bf(jnp.concatenate(k_slots, axis=1)) if k_slots else None

        e_in = [None] * nb
        dec = None
        for j in range(nb):
            e_in[j] = dec
            dec = T[j] if dec is None else dec * T[j]
        self.e_tot = dec
        d_out = [None] * nb
        dec = None
        for j in range(nb - 1, -1, -1):
            d_out[j] = dec
            dec = T[j] if dec is None else dec * T[j]
        self.q_in = _bf(jnp.concatenate(
            [qe_b[j] if e_in[j] is None else qe_b[j] * e_in[j] for j in range(nb)], axis=0))
        self.k_out = _bf(jnp.concatenate(
            [kd_b[j] if d_out[j] is None else kd_b[j] * d_out[j] for j in range(nb)], axis=0))
        self.v_bf = _bf(v)

    def issue(self):
        self.r1 = _dot_nt(self.qe, self.ke_kd)
        self.far = _dot_nt(self.q_far, self.k_far) if self.q_far is not None else None
        self.upd = _dot_tn(self.v_bf, self.k_out)

    def combine(self, st, mask_d, mask_p):
        c = self.c
        p = jnp.where(mask_d, self.r1[:, :c], jnp.where(mask_p, self.r1[:, c:], 0.0))
        if self.far is not None:
            p = p + self.far
        self.o = _dot(_bf(p), self.v_bf) + _dot_nt(self.q_in, _bf(st))
        return st * self.e_tot + self.upd


def _head_out(o, gate, hg_norm):
    y = o * _rms_scale(o) * hg_norm
    return y * (gate * _sigmoid(gate))


def _post_norm(x, y_blocks, norm_post):
    ms = sum(jnp.sum(y * y, axis=-1, keepdims=True) for y in y_blocks) / D_MODEL
    r = lax.rsqrt(ms + EPS)
    return [x[:, j * COLS:(j + 1) * COLS] + y * r * norm_post[:, j * COLS:(j + 1) * COLS]
            for j, y in enumerate(y_blocks)]


def _finish(x, mix_bf, p_bf, w_out_ref, w_ple_ref, w_gate_ref, norm_post):
    nblk = D_MODEL // COLS
    x1 = _post_norm(x, [_dot(mix_bf, _wblk(w_out_ref, j)) for j in range(nblk)], norm_post)
    x1_bf = _bf(jnp.concatenate(x1, axis=1))
    out = [x1[j] + _sigmoid(_dot(x1_bf, _wblk(w_gate_ref, j))) * _dot(p_bf, _wblk(w_ple_ref, j)) for j in range(nblk)]
    return jnp.concatenate(out, axis=1)


def _chunk_masks(c):
    t = lax.broadcasted_iota(jnp.int32, (c, c), 0)
    s = lax.broadcasted_iota(jnp.int32, (c, c), 1)
    tb = t // SPAN
    sb = s // SPAN
    mask_d = (tb == sb) & (s <= t)
    mask_p = ((tb & 1) == 1) & (sb == tb - 1)
    return mask_d, mask_p


def _sub_step_head(stages, r0, xa_ref, p_ref, npre_ref, w_ple_ref):
    rows = slice(r0, r0 + SUB_TILE)
    ple, h = [], None
    if 3 in stages:
        p_bf = _bf(p_ref[rows, :])
        ple = [_dot(p_bf, _wblk(w_ple_ref, j)) for j in range(2)]
    if 1 in stages:
        xa = xa_ref[rows, :]
        h = _bf(xa * _rms_scale(xa) * npre_ref[...])
    return ple, h


def _sub_step(stages, first_pos, r0, xa_ref, xc_ref, p_ref, lbl_ref, w_in_ref, w_pool_ref, scale_ref, hgn_ref,
              w_out_ref, npre_ref, npost_ref, w_ple_ref, w_gate_ref, y_ref,
              u_w, z_w, mix_w, u_r, z_r, mix_r, carry, states, head=None, before_tail=None):
    tt = SUB_TILE
    rows = slice(r0, r0 + tt)
    states = list(states)

    def in_proj(h, c0):
        blk = _dot(h, _wblk(w_in_ref, c0 // COLS))
        if c0 == C_U:
            u_w[HDR:HDR + tt, :] = blk
        else:
            z_w[:, c0 - C_GP:c0 - C_GP + D_POOL] = blk

    if 2 not in stages:
        if 1 in stages:
            xa = xa_ref[rows, :]
            h = _bf(xa * _rms_scale(xa) * npre_ref[...])
            for c0 in (C_U, C_GP, C_Q, C_F, C_V, C_GH):
                in_proj(h, c0)
        if 3 in stages:
            y_ref[rows, :] = _finish(xc_ref[rows, :], mix_r[...], _bf(p_ref[rows, :]), w_out_ref, w_ple_ref,
                                     w_gate_ref, npost_ref[...])
        return carry, states

    lb = _lower_bound(lbl_ref[...])
    mask_d, mask_p = _chunk_masks(CHUNK)

    def pool_group(gi):
        cs = slice(gi * POOL_GROUP, (gi + 1) * POOL_GROUP)
        pooled = _pool_group(u_r[:, cs], POOL_WINDOWS[gi], first_pos)
        mixed = _dot(_bf(pooled), _bf(w_pool_ref[gi])) * scale_ref[:, cs]
        gp = z_r[:, cs]
        mix_w[:, cs] = _bf(mixed * (gp * _sigmoid(gp)))

    def hgrn_unit(hd, c0):
        hs = slice(hd * HEAD, (hd + 1) * HEAD)
        rs = slice(c0, c0 + CHUNK)
        col = lambda base: slice(base - C_GP + hd * HEAD, base - C_GP + (hd + 1) * HEAD)
        unit = _HgrnChunk(lambda: (z_r[rs, col(C_Q)], z_r[rs, col(C_F)], z_r[rs, col(C_V)]), lb[:, hs])
        unit.head, unit.rows, unit.gate_cols = hd, rs, col(C_GH)
        return unit

    def hgrn_store(unit):
        gh = z_r[unit.rows, unit.gate_cols]
        mix_w[unit.rows, D_POOL + unit.head * HEAD:D_POOL + (unit.head + 1) * HEAD] = _bf(
            _head_out(unit.o, gh, hgn_ref[...]))

    units = [hgrn_unit(hd, c0) for c0 in range(0, tt, CHUNK) for hd in range(N_HEADS)]
    dense = [("out", 1), ("in", C_U), ("in", C_GP), ("gate", 0), ("gate", 1), ("in", C_Q), ("in", C_F),
             ("in", C_V)]
    assert len(units) == len(dense)
    pool_at = {0: (0,), 2: (1,), 4: (2,), 7: (3,)}
    post_norm_at = 1
    final_at = {5: (0,), 6: (1,)}
    h = mix_prev = x1 = x1_bf = None
    y, gate, ple = {}, {}, []
    if head is None:
        head = _sub_step_head(stages, r0, xa_ref, p_ref, npre_ref, w_ple_ref)
    ple, h = head
    units[0].prepare()
    if 3 in stages:
        mix_prev = mix_r[...]
        y[0] = _dot(mix_prev, _wblk(w_out_ref, 0))
    u_r[0:HDR, :] = carry
    for k, unit in enumerate(units):
        unit.issue()
        if k + 1 < len(units):
            units[k + 1].prepare()
        kind, arg = dense[k]
        if kind == "in" and 1 in stages:
            in_proj(h, arg)
        elif kind == "out" and 3 in stages:
            y[arg] = _dot(mix_prev, _wblk(w_out_ref, arg))
        elif kind == "gate" and 3 in stages:
            gate[arg] = _dot(x1_bf, _wblk(w_gate_ref, arg))
        states[unit.head] = unit.combine(states[unit.head], mask_d, mask_p)
        if k > 0:
            hgrn_store(units[k - 1])
        if k == post_norm_at and 3 in stages:
            x1 = _post_norm(xc_ref[rows, :], [y[0], y[1]], npost_ref[...])
            x1_bf = _bf(jnp.concatenate(x1, axis=1))
        for gi in pool_at.get(k, ()):
            pool_group(gi)
        if 3 in stages:
            for j in final_at.get(k, ()):
                y_ref[rows, j * COLS:(j + 1) * COLS] = x1[j] + _sigmoid(gate[j]) * ple[j]
    if before_tail is not None:
        before_tail()
    if 1 in stages:
        in_proj(h, C_GH)
    hgrn_store(units[-1])
    return u_r[tt:tt + HDR, :], states


def _prompt_kernel(tiles_per_stream, xa_ref, xc_ref, p_ref, lbl_ref, w_in_ref, w_pool_ref, scale_ref, hgn_ref,
                   w_out_ref, npre_ref, npost_ref, w_ple_ref, w_gate_ref,
                   y_ref, pool_ref, hg_ref,
                   u_a, z_a, mix_a, u_b, z_b, mix_b, carry_ref, st_ref, carry_end, st_end):
    s = pl.program_id(0)
    last = pl.num_programs(0) - 1
    consts = (xa_ref, xc_ref, p_ref, lbl_ref, w_in_ref, w_pool_ref, scale_ref, hgn_ref, w_out_ref, npre_ref,
              npost_ref, w_ple_ref, w_gate_ref, y_ref)
    set_a, set_b = (u_a, z_a, mix_a), (u_b, z_b, mix_b)
    pos_first = lax.rem(2 * s - 1 + tiles_per_stream, tiles_per_stream) * SUB_TILE
    pos_second = lax.rem(2 * s, tiles_per_stream) * SUB_TILE
    new_stream = pos_second == 0

    def run(first_stages, second_stages):
        carry = carry_ref[...]
        states = [st_ref[hd] for hd in range(N_HEADS)]
        head_second = []
        before_tail = None
        if 2 in first_stages and 2 in second_stages:
            before_tail = lambda: head_second.append(
                _sub_step_head(second_stages, SUB_TILE, xa_ref, p_ref, npre_ref, w_ple_ref))
        carry, states = _sub_step(first_stages, pos_first, 0, *consts, *set_a, *set_b, carry, states,
                                  before_tail=before_tail)
        carry_end[...] = carry
        for hd in range(N_HEADS):
            st_end[hd] = states[hd]
        carry = jnp.where(new_stream, 0.0, carry)
        states = [jnp.where(new_stream, 0.0, st) for st in states]
        carry, states = _sub_step(second_stages, pos_second, SUB_TILE, *consts, *set_b, *set_a, carry, states,
                                  head=head_second[0] if head_second else None)
        carry_ref[...] = carry
        for hd in range(N_HEADS):
            st_ref[hd] = states[hd]

    @pl.when(s == 0)
    def _():
        carry_ref[...] = jnp.zeros(carry_ref.shape, jnp.float32)
        st_ref[...] = jnp.zeros(st_ref.shape, jnp.float32)
        run({1}, {1, 2})

    @pl.when((s > 0) & (s < last))
    def _():
        run({1, 2, 3}, {1, 2, 3})

    @pl.when(s == last)
    def _():
        run({2, 3}, {3})

    @pl.when(new_stream & (s > 0))
    def _():
        pool_ref[...] = carry_end[HDR - POOL_BUF:HDR, :]
        for hd in range(N_HEADS):
            hg_ref[hd] = st_end[hd].T


def _const_spec(shape):
    nd = len(shape)
    return pl.BlockSpec(shape, lambda *_: (0,) * nd, pipeline_mode=pl.Buffered(1))


def _prompt_call(x, p, lbl, w_in, w_pool, scale, hgn, w_out, npre, npost, w_ple, w_gate):
    b, t, _ = x.shape
    tt = 2 * SUB_TILE
    assert t % tt == 0 and SUB_TILE % CHUNK == 0
    nt = t // SUB_TILE
    nblk = b * t // tt
    weights = (lbl, w_in, w_pool, scale, hgn, w_out, npre, npost, w_ple, w_gate)
    x2 = x.reshape(b * t, D_MODEL)
    p2 = p.reshape(b * t, D_PLE)
    ahead = lambda s: (jnp.minimum(s, nblk - 1), 0)
    behind = lambda s: (jnp.clip(s - 1, 0, nblk - 1), 0)
    stream = lambda s: (jnp.clip((2 * s - 2) // nt, 0, b - 1),)
    y, pool, hg = pl.pallas_call(
        functools.partial(_prompt_kernel, nt),
        grid=(nblk + 1,),
        in_specs=[pl.BlockSpec((tt, D_MODEL), ahead),
                  pl.BlockSpec((tt, D_MODEL), behind),
                  pl.BlockSpec((tt, D_PLE), behind)]
                 + [_const_spec(w.shape) for w in weights],
        out_specs=[pl.BlockSpec((tt, D_MODEL), behind),
                   pl.BlockSpec((None, POOL_BUF, D_POOL), lambda s: stream(s) + (0, 0)),
                   pl.BlockSpec((None, N_HEADS, HEAD, HEAD), lambda s: stream(s) + (0, 0, 0))],
        out_shape=[jax.ShapeDtypeStruct((b * t, D_MODEL), jnp.float32),
                   jax.ShapeDtypeStruct((b, POOL_BUF, D_POOL), jnp.float32),
                   jax.ShapeDtypeStruct((b, N_HEADS, HEAD, HEAD), jnp.float32)],
        scratch_shapes=[pltpu.VMEM((HDR + SUB_TILE, D_POOL), jnp.float32),
                        pltpu.VMEM((SUB_TILE, D_IN - D_POOL), jnp.float32),
                        pltpu.VMEM((SUB_TILE, D_MODEL), jnp.bfloat16)] * 2
                       + [pltpu.VMEM((HDR, D_POOL), jnp.float32),
                          pltpu.VMEM((N_HEADS, HEAD, HEAD), jnp.float32)] * 2,
        compiler_params=pltpu.CompilerParams(dimension_semantics=("arbitrary",),
                                             vmem_limit_bytes=VMEM_LIMIT_BYTES),
        name="prompt_layer",
    )(x2, x2, p2, *weights)
    return y.reshape(b, t, D_MODEL), pool, hg


STREAMS_PER_ITER = 8
SAMPLE_VMEM_LIMIT_BYTES = 58 * 1024 * 1024


def _sample_kernel(start_pos, x_ref, p_ref, cache_ref, s0_ref, lbl_ref, w_in_ref, w_pool_ref, scale_ref,
                   hgn_ref, w_out_ref, npre_ref, npost_ref, w_ple_ref, w_gate_ref,
                   y_ref, pool_ref, hg_ref, w_in_b, w_out_b, w_ple_b, w_gate_b,
                   z_ref, ext_ref, mix_ref):
    n_streams, ts = pool_ref.shape[0], pool_ref.shape[1] + 1

    for src, dst in ((w_in_ref, w_in_b), (w_out_ref, w_out_b), (w_ple_ref, w_ple_b), (w_gate_ref, w_gate_b)):
        for j in range(dst.shape[0]):
            dst[j] = _bf(src[:, j * COLS:(j + 1) * COLS])

    x = x_ref[...]
    h = _bf(x * _rms_scale(x) * npre_ref[...])
    for j in range(D_IN // COLS):
        z_ref[:, j * COLS:(j + 1) * COLS] = _dot(h, w_in_b[j])

    lb = _lower_bound(lbl_ref[...])
    causal = (lax.broadcasted_iota(jnp.int32, (ts, ts), 1) <= lax.broadcasted_iota(jnp.int32, (ts, ts), 0))
    w_pool = [_bf(w_pool_ref[gi]) for gi in range(len(POOL_WINDOWS))]

    def prepare(b, slot):
        c = {"b": b, "rows": pl.ds(pl.multiple_of(b * ts, ts), ts)}
        rows = c["rows"]
        ext = ext_ref.at[slot]
        u = z_ref[rows, C_U:C_U + D_POOL]
        ext[0:HDR, :] = jnp.zeros((HDR, D_POOL), jnp.float32)
        ext[HDR - POOL_BUF:HDR, :] = cache_ref[b]
        ext[HDR:HDR + ts, :] = u
        pool_ref[b] = u[ts - POOL_BUF:ts, :]
        c["pooled"] = [_bf(_pool_group(ext[:, gi * POOL_GROUP:(gi + 1) * POOL_GROUP], w, start_pos))
                       for gi, w in enumerate(POOL_WINDOWS)]
        gated = []
        for hd in range(N_HEADS):
            q = z_ref[rows, C_Q + hd * HEAD:C_Q + (hd + 1) * HEAD]
            fl = z_ref[rows, C_F + hd * HEAD:C_F + (hd + 1) * HEAD]
            gated.append(_gates(q, fl, lb[:, hd * HEAD:(hd + 1) * HEAD]))
        decay_rows = [F[ts - 1:ts, :] for _, _, F in gated]
        c["decay_cols"] = jnp.concatenate(
            decay_rows + [jnp.zeros((8 - N_HEADS, HEAD), jnp.float32)], axis=0).T
        c["qe"] = [_bf(qe) for qe, _, _ in gated]
        c["ke"] = [_bf(ke) for _, ke, _ in gated]
        c["kd"] = [_bf(ke * decay_rows[hd]) for hd, (_, ke, _) in enumerate(gated)]
        c["v"] = [_bf(z_ref[rows, C_V + hd * HEAD:C_V + (hd + 1) * HEAD]) for hd in range(N_HEADS)]
        return c

    def issue(c):
        c["mixed"] = [_dot(c["pooled"][gi], w_pool[gi]) for gi in range(len(POOL_WINDOWS))]
        c["att"] = [_dot_nt(c["qe"][hd], c["ke"][hd]) for hd in range(N_HEADS)]
        c["upd"] = [_dot_tn(c["kd"][hd], c["v"][hd]) for hd in range(N_HEADS)]

    def combine(c):
        b, rows = c["b"], c["rows"]
        for gi in range(len(POOL_WINDOWS)):
            cs = slice(gi * POOL_GROUP, (gi + 1) * POOL_GROUP)
            gp = z_ref[rows, C_GP + gi * POOL_GROUP:C_GP + (gi + 1) * POOL_GROUP]
            mix_ref[rows, cs] = _bf(c["mixed"][gi] * scale_ref[:, cs] * (gp * _sigmoid(gp)))
        c["o"] = []
        for hd in range(N_HEADS):
            s0 = s0_ref[b, hd]
            att = _bf(jnp.where(causal, c["att"][hd], 0.0))
            c["o"].append(_dot(att, c["v"][hd]) + _dot(c["qe"][hd], _bf(s0)))
            hg_ref[b, hd] = s0 * c["decay_cols"][:, hd:hd + 1] + c["upd"][hd]

    def store(c):
        rows = c["rows"]
        for hd in range(N_HEADS):
            gh = z_ref[rows, C_GH + hd * HEAD:C_GH + (hd + 1) * HEAD]
            mix_ref[rows, D_POOL + hd * HEAD:D_POOL + (hd + 1) * HEAD] = _bf(
                _head_out(c["o"][hd], gh, hgn_ref[...]))

    def stream_group(g, carry):
        ctx = [prepare(g * STREAMS_PER_ITER + slot, slot) for slot in range(STREAMS_PER_ITER)]
        for phase in (issue, combine, store):
            for c in ctx:
                phase(c)
        return carry

    lax.fori_loop(0, n_streams // STREAMS_PER_ITER, stream_group, 0)

    y_ref[...] = _finish(x, mix_ref[...], _bf(p_ref[...]), w_out_b, w_ple_b, w_gate_b, npost_ref[...])


def _sample_call(start_pos, x, p, cache, s0, lbl, w_in, w_pool, scale, hgn, w_out, npre, npost, w_ple, w_gate):
    b, ts, _ = x.shape
    assert ts == BLK and ts == POOL_BUF + 1 and b % STREAMS_PER_ITER == 0
    n = b * ts
    blocks = lambda w: jax.ShapeDtypeStruct((w.shape[1] // COLS, w.shape[0], COLS), jnp.bfloat16)
    y, pool, hg, w_in_b, w_out_b, w_ple_b, w_gate_b = pl.pallas_call(
        functools.partial(_sample_kernel, start_pos),
        out_shape=[jax.ShapeDtypeStruct((n, D_MODEL), jnp.float32),
                   jax.ShapeDtypeStruct((b, POOL_BUF, D_POOL), jnp.float32),
                   jax.ShapeDtypeStruct((b, N_HEADS, HEAD, HEAD), jnp.float32),
                   blocks(w_in), blocks(w_out), blocks(w_ple), blocks(w_gate)],
        scratch_shapes=[pltpu.VMEM((n, D_IN), jnp.float32),
                        pltpu.VMEM((STREAMS_PER_ITER, HDR + ts, D_POOL), jnp.float32),
                        pltpu.VMEM((n, D_MODEL), jnp.bfloat16)],
        compiler_params=pltpu.CompilerParams(vmem_limit_bytes=SAMPLE_VMEM_LIMIT_BYTES),
        name="sample_layer",
    )(x.reshape(n, D_MODEL), p.reshape(n, D_PLE), cache, s0, lbl, w_in, w_pool, scale, hgn, w_out, npre, npost,
      w_ple, w_gate)
    return (y.reshape(b, ts, D_MODEL), pool, hg), (w_in_b, w_out_b, w_ple_b, w_gate_b)


def kernel(x_prompt, x_sample, cache_pool, state_hgrn, p_prompt, p_sample, lb_logits, w_in, w_pool, pool_scale,
           hg_norm, w_out, norm_pre, norm_post, w_ple, w_ple_gate):
    depth = w_in.shape[0]
    assert depth == 1 and lb_logits.shape[0] == 2
    past_len = 1024
    (y_s, pool_s, hg_s), (w_in_b, w_out_b, w_ple_b, w_gate_b) = _sample_call(
        past_len, x_sample, p_sample[0], cache_pool[0], state_hgrn[0], lb_logits, w_in[0], w_pool[0], pool_scale,
        hg_norm, w_out[0], norm_pre, norm_post, w_ple[0], w_ple_gate[0])
    y_p, pool_p, hg_p = _prompt_call(x_prompt, p_prompt[0], lb_logits, w_in_b, w_pool[0], pool_scale, hg_norm,
                                     w_out_b, norm_pre, norm_post, w_ple_b, w_gate_b)
    return (y_p, y_s, pool_p[None], hg_p[None], pool_s[None], hg_s[None])
```

```python
import functools

import jax
import jax.numpy as jnp
from jax import lax
from jax.experimental import pallas as pl
from jax.experimental.pallas import tpu as pltpu

D_MODEL = 1024
D_POOL = 512
POOL_WINDOWS = (2, 4, 8, 16)
POOL_GROUP = 128
POOL_BUF = 15
N_HEADS = 4
HEAD = 128
D_HG = N_HEADS * HEAD
D_IN = 2 * D_POOL + 4 * D_HG
D_PLE = 256
EPS = 1e-6
LOG2_E = 1.4426950408889634
COLS = 512
SPAN = 32
BLK = 16
CHUNK = 128
SUBLANES = 8
HDR = 16
SUB_TILE = 256
POSITIONS = 2
VMEM_LIMIT_BYTES = 60 * 1024 * 1024

C_U, C_GP, C_Q, C_F, C_V, C_GH = 0, 512, 1024, 1536, 2048, 2560

_NT = (((1,), (1,)), ((), ()))
_TN = (((0,), (0,)), ((), ()))


def _dot(a, b):
    return jnp.dot(a, b, preferred_element_type=jnp.float32)


def _dot_nt(a, b):
    return lax.dot_general(a, b, _NT, preferred_element_type=jnp.float32)


def _dot_tn(a, b):
    return lax.dot_general(a, b, _TN, preferred_element_type=jnp.float32)


def _bf(x):
    return x.astype(jnp.bfloat16)


def _wblk(w_ref, j):
    return w_ref[j]


def _sigmoid(x):
    return 1.0 / (1.0 + jnp.exp2(x * (-LOG2_E)))


def _rms_scale(x):
    return lax.rsqrt(jnp.mean(x * x, axis=-1, keepdims=True) + EPS)


def _lower_bound(lb_logits):
    l0 = lb_logits[0:1, :]
    l1 = lb_logits[1:2, :]
    m = jnp.maximum(l0, l1)
    e0 = jnp.exp(l0 - m)
    e1 = jnp.exp(l1 - m)
    return e0 / (e0 + e1)


def _window_sum(e, w):
    s = e
    d = 1
    while d < w:
        s = s + pltpu.roll(s, d, axis=0)
        d *= 2
    return s


def _pool_group(e, w, first_pos):
    s = _window_sum(e, w)[HDR:]
    u = e[HDR:]
    rows = lax.broadcasted_iota(jnp.int32, (HDR, POOL_GROUP), 0)
    cnt = jnp.minimum(w, first_pos + rows + 1).astype(jnp.float32)
    head = s[:HDR] / cnt - u[:HDR]
    if s.shape[0] == HDR:
        return head
    tail = s[HDR:] * (1.0 / w) - u[HDR:]
    return jnp.concatenate([head, tail], axis=0)


def _block_cumprod(f, reverse=False):
    n, lanes = f.shape
    x = f.reshape(n // SUBLANES, SUBLANES, lanes)
    row = lax.broadcasted_iota(jnp.int32, x.shape, 1)
    for d in (1, 2, 4):
        if reverse:
            x = x * jnp.where(row + d < SUBLANES, pltpu.roll(x, SUBLANES - d, axis=1), 1.0)
        else:
            x = x * jnp.where(row >= d, pltpu.roll(x, d, axis=1), 1.0)
    x = x.reshape(n // BLK, BLK // SUBLANES, SUBLANES, lanes)
    lo, hi = x[:, 0], x[:, 1]
    if reverse:
        lo = lo * hi[:, 0:1, :]
    else:
        hi = hi * lo[:, SUBLANES - 1:SUBLANES, :]
    return jnp.concatenate([lo[:, None], hi[:, None]], axis=1).reshape(n, lanes)


def _gates(q, fl, lb):
    f = lb + (1.0 - lb) * _sigmoid(fl)
    k = 1.0 - f
    F = _block_cumprod(f)
    return q * F, k / F, F


def _as_column(row):
    tile = jnp.concatenate([row, jnp.zeros((7, row.shape[1]), row.dtype)], axis=0)
    return tile.T[:, 0:1]


def _stack_blocks(pieces, n_blocks):
    zero = jnp.zeros((SPAN, HEAD), jnp.float32)
    return jnp.concatenate([pieces.get(j, zero) for j in range(n_blocks)], axis=0)


class _HgrnChunk:
    def __init__(self, load, lb):
        self.load, self.lb = load, lb

    def prepare(self):
        q, fl, v = self.load()
        c = self.c = q.shape[0]
        nb = c // SPAN
        f = self.lb + (1.0 - self.lb) * _sigmoid(fl)
        k = 1.0 - f
        halves = lambda x: [jnp.concatenate([x[SPAN * j + BLK * w:SPAN * j + BLK * (w + 1)] for j in range(nb)], axis=0)
                            for w in (0, 1)]
        (f_a, f_b), (q_a, q_b), (k_a, k_b) = halves(f), halves(q), halves(k)
        fwd = _block_cumprod(f_b)
        rev_incl = _block_cumprod(f_a, reverse=True)
        row = lax.broadcasted_iota(jnp.int32, f_a.shape, 0) & (BLK - 1)
        rev = jnp.where(row < BLK - 1, pltpu.roll(rev_incl, f_a.shape[0] - 1, axis=0), 1.0)
        qm = (q_a / rev, q_b * fwd)
        km = (k_a * rev, k_b / fwd)
        span = lambda pair, j: jnp.concatenate([pair[0][BLK * j:BLK * (j + 1)], pair[1][BLK * j:BLK * (j + 1)]], axis=0)
        t_head = [rev_incl[BLK * j:BLK * j + 1, :] for j in range(nb)]
        t_tail = [fwd[BLK * j + BLK - 1:BLK * (j + 1), :] for j in range(nb)]
        T = [t_head[j] * t_tail[j] for j in range(nb)]
        qm_b = [span(qm, j) for j in range(nb)]
        km_b = [span(km, j) for j in range(nb)]
        qe_b = [qm_b[j] * t_head[j] for j in range(nb)]
        kd_b = [km_b[j] * t_tail[j] for j in range(nb)]
        zero = jnp.zeros((SPAN, HEAD), jnp.float32)
        k_pair = [kd_b[j] * t_head[j + 1] if j % 2 == 0 else zero for j in range(nb)]
        self.qe = _bf(jnp.concatenate(qm_b, axis=0))
        self.ke_kd = _bf(jnp.concatenate(km_b + k_pair, axis=0))

        q_slots, k_slots = [], []
        gs = 4
        while gs <= nb:
            for a in range(0, nb, gs):
                mid = a + gs // 2
                qs, ks = {}, {}
                dec = None
                for j in range(mid, a + gs):
                    qs[j] = qe_b[j] if dec is None else qe_b[j] * dec
                    dec = T[j] if dec is None else dec * T[j]
                dec = None
                for j in range(mid - 1, a - 1, -1):
                    ks[j] = kd_b[j] if dec is None else kd_b[j] * dec
                    dec = T[j] if dec is None else dec * T[j]
                q_slots.append(_stack_blocks(qs, nb))
                k_slots.append(_stack_blocks(ks, nb))
            gs *= 2
        self.q_far = _bf(jnp.concatenate(q_slots, axis=1)) if q_slots else None
        self.k_far = _bf(jnp.concatenate(k_slots, axis=1)) if k_slots else None

        e_in = [None] * nb
        dec = None
        for j in range(nb):
            e_in[j] = dec
            dec = T[j] if dec is None else dec * T[j]
        self.e_tot = dec
        d_out = [None] * nb
        dec = None
        for j in range(nb - 1, -1, -1):
            d_out[j] = dec
            dec = T[j] if dec is None else dec * T[j]
        self.q_in = _bf(jnp.concatenate(
            [qe_b[j] if e_in[j] is None else qe_b[j] * e_in[j] for j in range(nb)], axis=0))
        self.k_out = _bf(jnp.concatenate(
            [kd_b[j] if d_out[j] is None else kd_b[j] * d_out[j] for j in range(nb)], axis=0))
        self.v_bf = _bf(v)

    def issue(self):
        self.r1 = _dot_nt(self.qe, self.ke_kd)
        self.far = _dot_nt(self.q_far, self.k_far) if self.q_far is not None else None
        self.upd = _dot_tn(self.v_bf, self.k_out)

    def combine(self, st, mask_d, mask_p):
        c = self.c
        p = jnp.where(mask_d, self.r1[:, :c], jnp.where(mask_p, self.r1[:, c:], 0.0))
        if self.far is not None:
            p = p + self.far
        self.o = _dot(_bf(p), self.v_bf) + _dot_nt(self.q_in, _bf(st))
        return st * self.e_tot + self.upd


def _head_out(o, gate, hg_norm):
    y = o * _rms_scale(o) * hg_norm
    return y * (gate * _sigmoid(gate))


def _post_norm(x, y_blocks, norm_post):
    ms = sum(jnp.sum(y * y, axis=-1, keepdims=True) for y in y_blocks) / D_MODEL
    r = lax.rsqrt(ms + EPS)
    return [x[:, j * COLS:(j + 1) * COLS] + y * r * norm_post[:, j * COLS:(j + 1) * COLS]
            for j, y in enumerate(y_blocks)]


def _finish(x, mix_bf, p_bf, w_out_ref, w_ple_ref, w_gate_ref, norm_post):
    nblk = D_MODEL // COLS
    x1 = _post_norm(x, [_dot(mix_bf, _wblk(w_out_ref, j)) for j in range(nblk)], norm_post)
    x1_bf = _bf(jnp.concatenate(x1, axis=1))
    out = [x1[j] + _sigmoid(_dot(x1_bf, _wblk(w_gate_ref, j))) * _dot(p_bf, _wblk(w_ple_ref, j)) for j in range(nblk)]
    return jnp.concatenate(out, axis=1)


def _chunk_masks(c):
    t = lax.broadcasted_iota(jnp.int32, (c, c), 0)
    s = lax.broadcasted_iota(jnp.int32, (c, c), 1)
    tb = t // SPAN
    sb = s // SPAN
    mask_d = (tb == sb) & (s <= t)
    mask_p = ((tb & 1) == 1) & (sb == tb - 1)
    return mask_d, mask_p


def _sub_step_head(stages, r0, xa_ref, p_ref, npre_ref, w_ple_ref):
    rows = slice(r0, r0 + SUB_TILE)
    ple, h = [], None
    if 3 in stages:
        p_bf = _bf(p_ref[rows, :])
        ple = [_dot(p_bf, _wblk(w_ple_ref, j)) for j in range(2)]
    if 1 in stages:
        xa = xa_ref[rows, :]
        h = _bf(xa * _rms_scale(xa) * npre_ref[...])
    return ple, h


def _sub_step(stages, first_pos, r0, xa_ref, xc_ref, p_ref, lbl_ref, w_in_ref, w_pool_ref, scale_ref, hgn_ref,
              w_out_ref, npre_ref, npost_ref, w_ple_ref, w_gate_ref, y_ref,
              u_w, z_w, mix_w, u_r, z_r, mix_r, carry, states, head=None, before_tail=None):
    tt = SUB_TILE
    rows = slice(r0, r0 + tt)
    states = list(states)

    def in_proj(h, c0):
        blk = _dot(h, _wblk(w_in_ref, c0 // COLS))
        if c0 == C_U:
            u_w[HDR:HDR + tt, :] = blk
        else:
            z_w[:, c0 - C_GP:c0 - C_GP + D_POOL] = blk

    if 2 not in stages:
        if 1 in stages:
            xa = xa_ref[rows, :]
            h = _bf(xa * _rms_scale(xa) * npre_ref[...])
            for c0 in (C_U, C_GP, C_Q, C_F, C_V, C_GH):
                in_proj(h, c0)
        if 3 in stages:
            y_ref[rows, :] = _finish(xc_ref[rows, :], mix_r[...], _bf(p_ref[rows, :]), w_out_ref, w_ple_ref,
                                     w_gate_ref, npost_ref[...])
        return carry, states

    lb = _lower_bound(lbl_ref[...])
    mask_d, mask_p = _chunk_masks(CHUNK)

    def pool_group(gi):
        cs = slice(gi * POOL_GROUP, (gi + 1) * POOL_GROUP)
        pooled = _pool_group(u_r[:, cs], POOL_WINDOWS[gi], first_pos)
        mixed = _dot(_bf(pooled), _bf(w_pool_ref[gi])) * scale_ref[:, cs]
        gp = z_r[:, cs]
        mix_w[:, cs] = _bf(mixed * (gp * _sigmoid(gp)))

    def hgrn_unit(hd, c0):
        hs = slice(hd * HEAD, (hd + 1) * HEAD)
        rs = slice(c0, c0 + CHUNK)
        col = lambda base: slice(base - C_GP + hd * HEAD, base - C_GP + (hd + 1) * HEAD)
        unit = _HgrnChunk(lambda: (z_r[rs, col(C_Q)], z_r[rs, col(C_F)], z_r[rs, col(C_V)]), lb[:, hs])
        unit.head, unit.rows, unit.gate_cols = hd, rs, col(C_GH)
        return unit

    def hgrn_store(unit):
        gh = z_r[unit.rows, unit.gate_cols]
        mix_w[unit.rows, D_POOL + unit.head * HEAD:D_POOL + (unit.head + 1) * HEAD] = _bf(
            _head_out(unit.o, gh, hgn_ref[...]))

    units = [hgrn_unit(hd, c0) for c0 in range(0, tt, CHUNK) for hd in range(N_HEADS)]
    dense = [("out", 1), ("in", C_U), ("in", C_GP), ("gate", 0), ("gate", 1), ("in", C_Q), ("in", C_F),
             ("in", C_V)]
    assert len(units) == len(dense)
    pool_at = {0: (0,), 2: (1,), 4: (2,), 7: (3,)}
    post_norm_at = 1
    final_at = {5: (0,), 6: (1,)}
    h = mix_prev = x1 = x1_bf = None
    y, gate, ple = {}, {}, []
    if head is None:
        head = _sub_step_head(stages, r0, xa_ref, p_ref, npre_ref, w_ple_ref)
    ple, h = head
    units[0].prepare()
    if 3 in stages:
        mix_prev = mix_r[...]
        y[0] = _dot(mix_prev, _wblk(w_out_ref, 0))
    u_r[0:HDR, :] = carry
    for k, unit in enumerate(units):
        unit.issue()
        if k + 1 < len(units):
            units[k + 1].prepare()
        kind, arg = dense[k]
        if kind == "in" and 1 in stages:
            in_proj(h, arg)
        elif kind == "out" and 3 in stages:
            y[arg] = _dot(mix_prev, _wblk(w_out_ref, arg))
        elif kind == "gate" and 3 in stages:
            gate[arg] = _dot(x1_bf, _wblk(w_gate_ref, arg))
        states[unit.head] = unit.combine(states[unit.head], mask_d, mask_p)
        if k > 0:
            hgrn_store(units[k - 1])
        if k == post_norm_at and 3 in stages:
            x1 = _post_norm(xc_ref[rows, :], [y[0], y[1]], npost_ref[...])
            x1_bf = _bf(jnp.concatenate(x1, axis=1))
        for gi in pool_at.get(k, ()):
            pool_group(gi)
        if 3 in stages:
            for j in final_at.get(k, ()):
                y_ref[rows, j * COLS:(j + 1) * COLS] = x1[j] + _sigmoid(gate[j]) * ple[j]
    if before_tail is not None:
        before_tail()
    if 1 in stages:
        in_proj(h, C_GH)
    hgrn_store(units[-1])
    return u_r[tt:tt + HDR, :], states


def _position_stages(n, n_tiles):
    stages = set()
    if n < n_tiles:
        stages.add(1)
    if 1 <= n <= n_tiles:
        stages.add(2)
    if POSITIONS <= n < n_tiles + POSITIONS:
        stages.add(3)
    return stages


def _prompt_kernel(tiles_per_stream, n_tiles, xa_ref, xc_ref, p_ref, lbl_ref, w_in_ref, w_pool_ref, scale_ref,
                   hgn_ref, w_out_ref, npre_ref, npost_ref, w_ple_ref, w_gate_ref,
                   y_ref, pool_ref, hg_ref, *scratch):
    uz = (scratch[0:2], scratch[2:4])
    mix = scratch[4:4 + POSITIONS]
    carry_ref, st_ref, carry_end, st_end = scratch[4 + POSITIONS:]
    s = pl.program_id(0)
    last = pl.num_programs(0) - 1
    consts = (xa_ref, xc_ref, p_ref, lbl_ref, w_in_ref, w_pool_ref, scale_ref, hgn_ref, w_out_ref, npre_ref,
              npost_ref, w_ple_ref, w_gate_ref, y_ref)
    offsets = [lax.rem(POSITIONS * s + i - 1 + tiles_per_stream, tiles_per_stream) * SUB_TILE
               for i in range(POSITIONS)]
    new_stream = offsets[1] == 0

    def run(first_position):
        stage_sets = [_position_stages(first_position + i, n_tiles) for i in range(POSITIONS)]
        carry = carry_ref[...]
        states = [st_ref[hd] for hd in range(N_HEADS)]
        heads = {}
        for i, stages in enumerate(stage_sets):
            if not stages:
                continue
            r0 = i * SUB_TILE
            before_tail = None
            if i + 1 < POSITIONS and 2 in stages and 2 in stage_sets[i + 1]:
                def before_tail(i=i):
                    heads[i + 1] = _sub_step_head(stage_sets[i + 1], (i + 1) * SUB_TILE, xa_ref, p_ref, npre_ref,
                                                  w_ple_ref)
            if i == 1:
                carry = jnp.where(new_stream, 0.0, carry)
                states = [jnp.where(new_stream, 0.0, st) for st in states]
            u_w, z_w = uz[i % 2]
            u_r, z_r = uz[(i + 1) % 2]
            carry, states = _sub_step(stages, offsets[i], r0, *consts, u_w, z_w, mix[(i - 1) % POSITIONS],
                                      u_r, z_r, mix[i], carry, states, head=heads.get(i), before_tail=before_tail)
            if i == 0:
                carry_end[...] = carry
                for hd in range(N_HEADS):
                    st_end[hd] = states[hd]
        carry_ref[...] = carry
        for hd in range(N_HEADS):
            st_ref[hd] = states[hd]

    @pl.when(s == 0)
    def _():
        carry_ref[...] = jnp.zeros(carry_ref.shape, jnp.float32)
        st_ref[...] = jnp.zeros(st_ref.shape, jnp.float32)
        run(0)

    @pl.when((s > 0) & (s < last))
    def _():
        run(POSITIONS)

    @pl.when(s == last)
    def _():
        run(n_tiles)

    @pl.when(new_stream & (s > 0))
    def _():
        pool_ref[...] = carry_end[HDR - POOL_BUF:HDR, :]
        for hd in range(N_HEADS):
            hg_ref[hd] = st_end[hd].T


def _const_spec(shape):
    nd = len(shape)
    return pl.BlockSpec(shape, lambda *_: (0,) * nd, pipeline_mode=pl.Buffered(1))


def _prompt_call(x, p, lbl, w_in, w_pool, scale, hgn, w_out, npre, npost, w_ple, w_gate):
    b, t, _ = x.shape
    tt = POSITIONS * SUB_TILE
    assert t % tt == 0 and SUB_TILE % CHUNK == 0 and POSITIONS % 2 == 0
    nt = t // SUB_TILE
    nblk = b * t // tt
    weights = (lbl, w_in, w_pool, scale, hgn, w_out, npre, npost, w_ple, w_gate)
    x2 = x.reshape(b * t, D_MODEL)
    p2 = p.reshape(b * t, D_PLE)
    ahead = lambda s: (jnp.minimum(s, nblk - 1), 0)
    behind = lambda s: (jnp.clip(s - 1, 0, nblk - 1), 0)
    stream = lambda s: (jnp.clip((POSITIONS * s - 2) // nt, 0, b - 1),)
    y, pool, hg = pl.pallas_call(
        functools.partial(_prompt_kernel, nt, b * nt),
        grid=(nblk + 1,),
        in_specs=[pl.BlockSpec((tt, D_MODEL), ahead),
                  pl.BlockSpec((tt, D_MODEL), behind),
                  pl.BlockSpec((tt, D_PLE), behind)]
                 + [_const_spec(w.shape) for w in weights],
        out_specs=[pl.BlockSpec((tt, D_MODEL), behind),
                   pl.BlockSpec((None, POOL_BUF, D_POOL), lambda s: stream(s) + (0, 0)),
                   pl.BlockSpec((None, N_HEADS, HEAD, HEAD), lambda s: stream(s) + (0, 0, 0))],
        out_shape=[jax.ShapeDtypeStruct((b * t, D_MODEL), jnp.float32),
                   jax.ShapeDtypeStruct((b, POOL_BUF, D_POOL), jnp.float32),
                   jax.ShapeDtypeStruct((b, N_HEADS, HEAD, HEAD), jnp.float32)],
        scratch_shapes=[pltpu.VMEM((HDR + SUB_TILE, D_POOL), jnp.float32),
                        pltpu.VMEM((SUB_TILE, D_IN - D_POOL), jnp.float32)] * 2
                       + [pltpu.VMEM((SUB_TILE, D_MODEL), jnp.bfloat16)] * POSITIONS
                       + [pltpu.VMEM((HDR, D_POOL), jnp.float32),
                          pltpu.VMEM((N_HEADS, HEAD, HEAD), jnp.float32)] * 2,
        compiler_params=pltpu.CompilerParams(dimension_semantics=("arbitrary",),
                                             vmem_limit_bytes=VMEM_LIMIT_BYTES),
        name="prompt_layer",
    )(x2, x2, p2, *weights)
    return y.reshape(b, t, D_MODEL), pool, hg


STREAMS_PER_ITER = 8
SAMPLE_VMEM_LIMIT_BYTES = 58 * 1024 * 1024


def _sample_kernel(start_pos, x_ref, p_ref, cache_ref, s0_ref, lbl_ref, w_in_ref, w_pool_ref, scale_ref,
                   hgn_ref, w_out_ref, npre_ref, npost_ref, w_ple_ref, w_gate_ref,
                   y_ref, pool_ref, hg_ref, w_in_b, w_out_b, w_ple_b, w_gate_b,
                   z_ref, ext_ref, mix_ref):
    n_streams, ts = pool_ref.shape[1], pool_ref.shape[0] + 1
    ext_rows = HDR + ts

    for src, dst in ((w_in_ref, w_in_b), (w_out_ref, w_out_b), (w_ple_ref, w_ple_b), (w_gate_ref, w_gate_b)):
        for j in range(dst.shape[0]):
            dst[j] = _bf(src[:, j * COLS:(j + 1) * COLS])

    ext_ref[...] = jnp.zeros(ext_ref.shape, jnp.float32)
    for r in range(POOL_BUF):
        for gi in range(len(POOL_WINDOWS)):
            ext_ref[gi, pl.ds(HDR - POOL_BUF + r, n_streams, stride=ext_rows), :] = (
                cache_ref[r, :, gi * POOL_GROUP:(gi + 1) * POOL_GROUP])

    x = x_ref[...]
    h = _bf(x * _rms_scale(x) * npre_ref[...])
    for j in range(D_IN // COLS):
        z_ref[:, j * COLS:(j + 1) * COLS] = _dot(h, w_in_b[j])

    lb = _lower_bound(lbl_ref[...])
    causal = (lax.broadcasted_iota(jnp.int32, (ts, ts), 1) <= lax.broadcasted_iota(jnp.int32, (ts, ts), 0))
    w_pool = [_bf(w_pool_ref[gi]) for gi in range(len(POOL_WINDOWS))]

    def prepare(b, slot):
        c = {"b": b, "rows": pl.ds(pl.multiple_of(b * ts, ts), ts)}
        rows = c["rows"]
        base = pl.multiple_of(b * ext_rows, ext_rows)
        c["pooled"] = []
        for gi, w in enumerate(POOL_WINDOWS):
            ext_ref[gi, pl.ds(base + HDR, ts), :] = z_ref[rows, C_U + gi * POOL_GROUP:C_U + (gi + 1) * POOL_GROUP]
            c["pooled"].append(_bf(_pool_group(ext_ref[gi, pl.ds(base, ext_rows), :], w, start_pos)))
        gated = []
        for hd in range(N_HEADS):
            q = z_ref[rows, C_Q + hd * HEAD:C_Q + (hd + 1) * HEAD]
            fl = z_ref[rows, C_F + hd * HEAD:C_F + (hd + 1) * HEAD]
            gated.append(_gates(q, fl, lb[:, hd * HEAD:(hd + 1) * HEAD]))
        decay_rows = [F[ts - 1:ts, :] for _, _, F in gated]
        c["decay_cols"] = jnp.concatenate(
            decay_rows + [jnp.zeros((8 - N_HEADS, HEAD), jnp.float32)], axis=0).T
        c["qe"] = [_bf(qe) for qe, _, _ in gated]
        c["ke"] = [_bf(ke) for _, ke, _ in gated]
        c["kd"] = [_bf(ke * decay_rows[hd]) for hd, (_, ke, _) in enumerate(gated)]
        c["v"] = [_bf(z_ref[rows, C_V + hd * HEAD:C_V + (hd + 1) * HEAD]) for hd in range(N_HEADS)]
        return c

    def issue(c):
        c["mixed"] = [_dot(c["pooled"][gi], w_pool[gi]) for gi in range(len(POOL_WINDOWS))]
        c["att"] = [_dot_nt(c["qe"][hd], c["ke"][hd]) for hd in range(N_HEADS)]
        c["upd"] = [_dot_tn(c["kd"][hd], c["v"][hd]) for hd in range(N_HEADS)]

    def combine(c):
        b, rows = c["b"], c["rows"]
        for gi in range(len(POOL_WINDOWS)):
            cs = slice(gi * POOL_GROUP, (gi + 1) * POOL_GROUP)
            gp = z_ref[rows, C_GP + gi * POOL_GROUP:C_GP + (gi + 1) * POOL_GROUP]
            mix_ref[rows, cs] = _bf(c["mixed"][gi] * scale_ref[:, cs] * (gp * _sigmoid(gp)))
        c["o"] = []
        for hd in range(N_HEADS):
            s0 = s0_ref[b, hd]
            att = _bf(jnp.where(causal, c["att"][hd], 0.0))
            c["o"].append(_dot(att, c["v"][hd]) + _dot(c["qe"][hd], _bf(s0)))
            hg_ref[b, hd] = s0 * c["decay_cols"][:, hd:hd + 1] + c["upd"][hd]

    def store(c):
        rows = c["rows"]
        for hd in range(N_HEADS):
            gh = z_ref[rows, C_GH + hd * HEAD:C_GH + (hd + 1) * HEAD]
            mix_ref[rows, D_POOL + hd * HEAD:D_POOL + (hd + 1) * HEAD] = _bf(
                _head_out(c["o"][hd], gh, hgn_ref[...]))

    def stream_group(g, carry):
        ctx = [prepare(g * STREAMS_PER_ITER + slot, slot) for slot in range(STREAMS_PER_ITER)]
        for phase in (issue, combine, store):
            for c in ctx:
                phase(c)
        return carry

    lax.fori_loop(0, n_streams // STREAMS_PER_ITER, stream_group, 0)

    for r in range(POOL_BUF):
        for gi in range(len(POOL_WINDOWS)):
            pool_ref[r, :, gi * POOL_GROUP:(gi + 1) * POOL_GROUP] = (
                ext_ref[gi, pl.ds(ext_rows - POOL_BUF + r, n_streams, stride=ext_rows), :])

    y_ref[...] = _finish(x, mix_ref[...], _bf(p_ref[...]), w_out_b, w_ple_b, w_gate_b, npost_ref[...])


def _sample_call(start_pos, x, p, cache, s0, lbl, w_in, w_pool, scale, hgn, w_out, npre, npost, w_ple, w_gate):
    b, ts, _ = x.shape
    assert ts == BLK and ts == POOL_BUF + 1 and b % STREAMS_PER_ITER == 0
    n = b * ts
    blocks = lambda w: jax.ShapeDtypeStruct((w.shape[1] // COLS, w.shape[0], COLS), jnp.bfloat16)
    y, pool, hg, w_in_b, w_out_b, w_ple_b, w_gate_b = pl.pallas_call(
        functools.partial(_sample_kernel, start_pos),
        out_shape=[jax.ShapeDtypeStruct((n, D_MODEL), jnp.float32),
                   jax.ShapeDtypeStruct((POOL_BUF, b, D_POOL), jnp.float32),
                   jax.ShapeDtypeStruct((b, N_HEADS, HEAD, HEAD), jnp.float32),
                   blocks(w_in), blocks(w_out), blocks(w_ple), blocks(w_gate)],
        scratch_shapes=[pltpu.VMEM((n, D_IN), jnp.float32),
                        pltpu.VMEM((len(POOL_WINDOWS), b * (HDR + ts), POOL_GROUP), jnp.float32),
                        pltpu.VMEM((n, D_MODEL), jnp.bfloat16)],
        compiler_params=pltpu.CompilerParams(vmem_limit_bytes=SAMPLE_VMEM_LIMIT_BYTES),
        name="sample_layer",
    )(x.reshape(n, D_MODEL), p.reshape(n, D_PLE), jnp.transpose(cache, (1, 0, 2)), s0, lbl, w_in, w_pool, scale,
      hgn, w_out, npre, npost, w_ple, w_gate)
    return (y.reshape(b, ts, D_MODEL), jnp.transpose(pool, (1, 0, 2)), hg), (w_in_b, w_out_b, w_ple_b, w_gate_b)


def kernel(x_prompt, x_sample, cache_pool, state_hgrn, p_prompt, p_sample, lb_logits, w_in, w_pool, pool_scale,
           hg_norm, w_out, norm_pre, norm_post, w_ple, w_ple_gate):
    depth = w_in.shape[0]
    assert depth == 1 and lb_logits.shape[0] == 2
    past_len = 1024
    (y_s, pool_s, hg_s), (w_in_b, w_out_b, w_ple_b, w_gate_b) = _sample_call(
        past_len, x_sample, p_sample[0], cache_pool[0], state_hgrn[0], lb_logits, w_in[0], w_pool[0], pool_scale,
        hg_norm, w_out[0], norm_pre, norm_post, w_ple[0], w_ple_gate[0])
    y_p, pool_p, hg_p = _prompt_call(x_prompt, p_prompt[0], lb_logits, w_in_b, w_pool[0], pool_scale, hg_norm,
                                     w_out_b, norm_pre, norm_post, w_ple_b, w_gate_b)
    return (y_p, y_s, pool_p[None], hg_p[None], pool_s[None], hg_s[None])
```

```python
import functools

import jax
import jax.numpy as jnp
from jax import lax
from jax.experimental import pallas as pl
from jax.experimental.pallas import tpu as pltpu

D_MODEL = 1024
D_POOL = 512
POOL_WINDOWS = (2, 4, 8, 16)
POOL_GROUP = 128
POOL_BUF = 15
N_HEADS = 4
HEAD = 128
D_HG = N_HEADS * HEAD
D_IN = 2 * D_POOL + 4 * D_HG
D_PLE = 256
EPS = 1e-6
LOG2_E = 1.4426950408889634
COLS = 512
SPAN = 32
BLK = 16
CHUNK = 128
SUBLANES = 8
HDR = 16
SUB_TILE = 256
POSITIONS = 2
VMEM_LIMIT_BYTES = 60 * 1024 * 1024

C_U, C_GP, C_Q, C_F, C_V, C_GH = 0, 512, 1024, 1536, 2048, 2560

_NT = (((1,), (1,)), ((), ()))
_TN = (((0,), (0,)), ((), ()))


def _dot(a, b):
    return jnp.dot(a, b, preferred_element_type=jnp.float32)


def _dot_nt(a, b):
    return lax.dot_general(a, b, _NT, preferred_element_type=jnp.float32)


def _dot_tn(a, b):
    return lax.dot_general(a, b, _TN, preferred_element_type=jnp.float32)


def _bf(x):
    return x.astype(jnp.bfloat16)


def _wblk(w_ref, j):
    return w_ref[j]


def _sigmoid(x):
    return 1.0 / (1.0 + jnp.exp2(x * (-LOG2_E)))


def _rms_scale(x):
    return lax.rsqrt(jnp.mean(x * x, axis=-1, keepdims=True) + EPS)


def _lower_bound(lb_logits):
    l0 = lb_logits[0:1, :]
    l1 = lb_logits[1:2, :]
    m = jnp.maximum(l0, l1)
    e0 = jnp.exp(l0 - m)
    e1 = jnp.exp(l1 - m)
    return e0 / (e0 + e1)


def _window_sum(e, w):
    s = e
    d = 1
    while d < w:
        s = s + pltpu.roll(s, d, axis=0)
        d *= 2
    return s


def _pool_group(e, w, first_pos):
    s = _window_sum(e, w)[HDR:]
    u = e[HDR:]
    rows = lax.broadcasted_iota(jnp.int32, (HDR, POOL_GROUP), 0)
    cnt = jnp.minimum(w, first_pos + rows + 1).astype(jnp.float32)
    head = s[:HDR] / cnt - u[:HDR]
    if s.shape[0] == HDR:
        return head
    tail = s[HDR:] * (1.0 / w) - u[HDR:]
    return jnp.concatenate([head, tail], axis=0)


def _block_cumprod(f, reverse=False):
    n, lanes = f.shape
    x = f.reshape(n // SUBLANES, SUBLANES, lanes)
    row = lax.broadcasted_iota(jnp.int32, x.shape, 1)
    for d in (1, 2, 4):
        if reverse:
            x = x * jnp.where(row + d < SUBLANES, pltpu.roll(x, SUBLANES - d, axis=1), 1.0)
        else:
            x = x * jnp.where(row >= d, pltpu.roll(x, d, axis=1), 1.0)
    x = x.reshape(n // BLK, BLK // SUBLANES, SUBLANES, lanes)
    lo, hi = x[:, 0], x[:, 1]
    if reverse:
        lo = lo * hi[:, 0:1, :]
    else:
        hi = hi * lo[:, SUBLANES - 1:SUBLANES, :]
    return jnp.concatenate([lo[:, None], hi[:, None]], axis=1).reshape(n, lanes)


def _gates(q, fl, lb):
    f = lb + (1.0 - lb) * _sigmoid(fl)
    k = 1.0 - f
    F = _block_cumprod(f)
    return q * F, k / F, F


class _HgrnChunk:
    def __init__(self, load, lb):
        self.load, self.lb = load, lb

    def prepare(self):
        q, fl, v = self.load()
        c = self.c = q.shape[0]
        nb = c // SPAN
        f = self.lb + (1.0 - self.lb) * _sigmoid(fl)
        k = 1.0 - f
        halves = lambda x: [jnp.concatenate([x[SPAN * j + BLK * w:SPAN * j + BLK * (w + 1)] for j in range(nb)], axis=0)
                            for w in (0, 1)]
        (f_a, f_b), (q_a, q_b), (k_a, k_b) = halves(f), halves(q), halves(k)
        fwd = _block_cumprod(f_b)
        rev_incl = _block_cumprod(f_a, reverse=True)
        row = lax.broadcasted_iota(jnp.int32, f_a.shape, 0) & (BLK - 1)
        rev = jnp.where(row < BLK - 1, pltpu.roll(rev_incl, f_a.shape[0] - 1, axis=0), 1.0)
        qm = (q_a / rev, q_b * fwd)
        km = (k_a * rev, k_b / fwd)
        span = lambda pair, j: jnp.concatenate([pair[0][BLK * j:BLK * (j + 1)], pair[1][BLK * j:BLK * (j + 1)]], axis=0)
        t_head = [rev_incl[BLK * j:BLK * j + 1, :] for j in range(nb)]
        t_tail = [fwd[BLK * j + BLK - 1:BLK * (j + 1), :] for j in range(nb)]
        T = [t_head[j] * t_tail[j] for j in range(nb)]
        qm_b = [span(qm, j) for j in range(nb)]
        km_b = [span(km, j) for j in range(nb)]
        qe_b = [qm_b[j] * t_head[j] for j in range(nb)]
        kd_b = [km_b[j] * t_tail[j] for j in range(nb)]
        zero = jnp.zeros((SPAN, HEAD), jnp.float32)
        k_pair = [kd_b[j] * t_head[j + 1] if j % 2 == 0 else zero for j in range(nb)]
        self.qe = _bf(jnp.concatenate(qm_b, axis=0))
        self.ke_kd = _bf(jnp.concatenate(km_b + k_pair, axis=0))

        assert nb == 4
        self.q_far = _bf(jnp.concatenate([qe_b[2], qe_b[3] * T[2]], axis=0))
        self.k_far = _bf(jnp.concatenate([kd_b[0] * T[1], kd_b[1], zero, zero], axis=0))

        e_in = [None] * nb
        dec = None
        for j in range(nb):
            e_in[j] = dec
            dec = T[j] if dec is None else dec * T[j]
        self.e_tot = dec
        d_out = [None] * nb
        dec = None
        for j in range(nb - 1, -1, -1):
            d_out[j] = dec
            dec = T[j] if dec is None else dec * T[j]
        self.q_in = _bf(jnp.concatenate(
            [qe_b[j] if e_in[j] is None else qe_b[j] * e_in[j] for j in range(nb)], axis=0))
        self.k_out = _bf(jnp.concatenate(
            [kd_b[j] if d_out[j] is None else kd_b[j] * d_out[j] for j in range(nb)], axis=0))
        self.v_bf = _bf(v)

    def issue(self):
        self.r1 = _dot_nt(self.qe, self.ke_kd)
        self.far = _dot_nt(self.q_far, self.k_far)
        self.upd = _dot_tn(self.v_bf, self.k_out)

    def combine(self, st, mask_d, mask_p):
        c = self.c
        p = jnp.where(mask_d, self.r1[:, :c], jnp.where(mask_p, self.r1[:, c:], 0.0))
        p = jnp.concatenate([p[:c // 2], p[c // 2:] + self.far], axis=0)
        self.o = _dot(_bf(p), self.v_bf) + _dot_nt(self.q_in, _bf(st))
        return st * self.e_tot + self.upd


def _head_out(o, gate, hg_norm):
    y = o * _rms_scale(o) * hg_norm
    return y * (gate * _sigmoid(gate))


def _post_norm(x, y_blocks, norm_post):
    ms = sum(jnp.sum(y * y, axis=-1, keepdims=True) for y in y_blocks) / D_MODEL
    r = lax.rsqrt(ms + EPS)
    return [x[:, j * COLS:(j + 1) * COLS] + y * r * norm_post[:, j * COLS:(j + 1) * COLS]
            for j, y in enumerate(y_blocks)]


def _finish(x, mix_bf, p_bf, w_out_ref, w_ple_ref, w_gate_ref, norm_post):
    nblk = D_MODEL // COLS
    x1 = _post_norm(x, [_dot(mix_bf, _wblk(w_out_ref, j)) for j in range(nblk)], norm_post)
    x1_bf = _bf(jnp.concatenate(x1, axis=1))
    out = [x1[j] + _sigmoid(_dot(x1_bf, _wblk(w_gate_ref, j))) * _dot(p_bf, _wblk(w_ple_ref, j)) for j in range(nblk)]
    return jnp.concatenate(out, axis=1)


def _chunk_masks(c):
    t = lax.broadcasted_iota(jnp.int32, (c, c), 0)
    s = lax.broadcasted_iota(jnp.int32, (c, c), 1)
    tb = t // SPAN
    sb = s // SPAN
    mask_d = (tb == sb) & (s <= t)
    mask_p = ((tb & 1) == 1) & (sb == tb - 1)
    return mask_d, mask_p


def _sub_step_head(stages, r0, xa_ref, p_ref, npre_ref, w_ple_ref):
    rows = slice(r0, r0 + SUB_TILE)
    ple, h = [], None
    if 3 in stages:
        p_bf = _bf(p_ref[rows, :])
        ple = [_dot(p_bf, _wblk(w_ple_ref, j)) for j in range(2)]
    if 1 in stages:
        xa = xa_ref[rows, :]
        h = _bf(xa * _rms_scale(xa) * npre_ref[...])
    return ple, h


def _sub_step(stages, first_pos, r0, xa_ref, xc_ref, p_ref, lbl_ref, w_in_ref, w_pool_ref, scale_ref, hgn_ref,
              w_out_ref, npre_ref, npost_ref, w_ple_ref, w_gate_ref, y_ref,
              u_w, z_w, mix_w, u_r, z_r, mix_r, carry, states, head=None, before_tail=None):
    tt = SUB_TILE
    rows = slice(r0, r0 + tt)
    states = list(states)

    def in_proj(h, c0):
        blk = _dot(h, _wblk(w_in_ref, c0 // COLS))
        if c0 == C_U:
            u_w[HDR:HDR + tt, :] = blk
        else:
            z_w[:, c0 - C_GP:c0 - C_GP + D_POOL] = blk

    if 2 not in stages:
        if 1 in stages:
            xa = xa_ref[rows, :]
            h = _bf(xa * _rms_scale(xa) * npre_ref[...])
            for c0 in (C_U, C_GP, C_Q, C_F, C_V, C_GH):
                in_proj(h, c0)
        if 3 in stages:
            y_ref[rows, :] = _finish(xc_ref[rows, :], mix_r[...], _bf(p_ref[rows, :]), w_out_ref, w_ple_ref,
                                     w_gate_ref, npost_ref[...])
        return carry, states

    lb = _lower_bound(lbl_ref[...])
    mask_d, mask_p = _chunk_masks(CHUNK)

    def pool_group(gi):
        cs = slice(gi * POOL_GROUP, (gi + 1) * POOL_GROUP)
        pooled = _pool_group(u_r[:, cs], POOL_WINDOWS[gi], first_pos)
        mixed = _dot(_bf(pooled), _bf(w_pool_ref[gi])) * scale_ref[:, cs]
        gp = z_r[:, cs]
        mix_w[:, cs] = _bf(mixed * (gp * _sigmoid(gp)))

    def hgrn_unit(hd, c0):
        hs = slice(hd * HEAD, (hd + 1) * HEAD)
        rs = slice(c0, c0 + CHUNK)
        col = lambda base: slice(base - C_GP + hd * HEAD, base - C_GP + (hd + 1) * HEAD)
        unit = _HgrnChunk(lambda: (z_r[rs, col(C_Q)], z_r[rs, col(C_F)], z_r[rs, col(C_V)]), lb[:, hs])
        unit.head, unit.rows, unit.gate_cols = hd, rs, col(C_GH)
        return unit

    def hgrn_store(unit):
        gh = z_r[unit.rows, unit.gate_cols]
        mix_w[unit.rows, D_POOL + unit.head * HEAD:D_POOL + (unit.head + 1) * HEAD] = _bf(
            _head_out(unit.o, gh, hgn_ref[...]))

    units = [hgrn_unit(hd, c0) for c0 in range(0, tt, CHUNK) for hd in range(N_HEADS)]
    dense = [("out", 1), ("in", C_U), ("in", C_GP), ("gate", 0), ("gate", 1), ("in", C_Q), ("in", C_F),
             ("in", C_V)]
    assert len(units) == len(dense)
    pool_at = {0: (0,), 2: (1,), 4: (2,), 7: (3,)}
    post_norm_at = 1
    final_at = {5: (0,), 6: (1,)}
    h = mix_prev = x1 = x1_bf = None
    y, gate, ple = {}, {}, []
    if head is None:
        head = _sub_step_head(stages, r0, xa_ref, p_ref, npre_ref, w_ple_ref)
    ple, h = head
    units[0].prepare()
    if 3 in stages:
        mix_prev = mix_r[...]
        y[0] = _dot(mix_prev, _wblk(w_out_ref, 0))
    u_r[0:HDR, :] = carry
    for k, unit in enumerate(units):
        unit.issue()
        if k + 1 < len(units):
            units[k + 1].prepare()
        kind, arg = dense[k]
        if kind == "in" and 1 in stages:
            in_proj(h, arg)
        elif kind == "out" and 3 in stages:
            y[arg] = _dot(mix_prev, _wblk(w_out_ref, arg))
        elif kind == "gate" and 3 in stages:
            gate[arg] = _dot(x1_bf, _wblk(w_gate_ref, arg))
        states[unit.head] = unit.combine(states[unit.head], mask_d, mask_p)
        if k > 0:
            hgrn_store(units[k - 1])
        if k == post_norm_at and 3 in stages:
            x1 = _post_norm(xc_ref[rows, :], [y[0], y[1]], npost_ref[...])
            x1_bf = _bf(jnp.concatenate(x1, axis=1))
        for gi in pool_at.get(k, ()):
            pool_group(gi)
        if 3 in stages:
            for j in final_at.get(k, ()):
                y_ref[rows, j * COLS:(j + 1) * COLS] = x1[j] + _sigmoid(gate[j]) * ple[j]
    if before_tail is not None:
        before_tail()
    if 1 in stages:
        in_proj(h, C_GH)
    hgrn_store(units[-1])
    return u_r[tt:tt + HDR, :], states


def _position_stages(n, n_tiles):
    stages = set()
    if n < n_tiles:
        stages.add(1)
    if 1 <= n <= n_tiles:
        stages.add(2)
    if POSITIONS <= n < n_tiles + POSITIONS:
        stages.add(3)
    return stages


def _prompt_kernel(tiles_per_stream, n_tiles, xa_ref, xc_ref, p_ref, lbl_ref, w_in_ref, w_pool_ref, scale_ref,
                   hgn_ref, w_out_ref, npre_ref, npost_ref, w_ple_ref, w_gate_ref,
                   y_ref, pool_ref, hg_ref, *scratch):
    uz = (scratch[0:2], scratch[2:4])
    mix = scratch[4:4 + POSITIONS]
    carry_ref, st_ref, carry_end, st_end = scratch[4 + POSITIONS:]
    s = pl.program_id(0)
    last = pl.num_programs(0) - 1
    consts = (xa_ref, xc_ref, p_ref, lbl_ref, w_in_ref, w_pool_ref, scale_ref, hgn_ref, w_out_ref, npre_ref,
              npost_ref, w_ple_ref, w_gate_ref, y_ref)
    offsets = [lax.rem(POSITIONS * s + i - 1 + tiles_per_stream, tiles_per_stream) * SUB_TILE
               for i in range(POSITIONS)]
    new_stream = offsets[1] == 0

    def run(first_position):
        stage_sets = [_position_stages(first_position + i, n_tiles) for i in range(POSITIONS)]
        carry = carry_ref[...]
        states = [st_ref[hd] for hd in range(N_HEADS)]
        heads = {}
        for i, stages in enumerate(stage_sets):
            if not stages:
                continue
            r0 = i * SUB_TILE
            before_tail = None
            if i + 1 < POSITIONS and 2 in stages and 2 in stage_sets[i + 1]:
                def before_tail(i=i):
                    heads[i + 1] = _sub_step_head(stage_sets[i + 1], (i + 1) * SUB_TILE, xa_ref, p_ref, npre_ref,
                                                  w_ple_ref)
            if i == 1:
                carry = jnp.where(new_stream, 0.0, carry)
                states = [jnp.where(new_stream, 0.0, st) for st in states]
            u_w, z_w = uz[i % 2]
            u_r, z_r = uz[(i + 1) % 2]
            carry, states = _sub_step(stages, offsets[i], r0, *consts, u_w, z_w, mix[(i - 1) % POSITIONS],
                                      u_r, z_r, mix[i], carry, states, head=heads.get(i), before_tail=before_tail)
            if i == 0:
                carry_end[...] = carry
                for hd in range(N_HEADS):
                    st_end[hd] = states[hd]
        carry_ref[...] = carry
        for hd in range(N_HEADS):
            st_ref[hd] = states[hd]

    @pl.when(s == 0)
    def _():
        carry_ref[...] = jnp.zeros(carry_ref.shape, jnp.float32)
        st_ref[...] = jnp.zeros(st_ref.shape, jnp.float32)
        run(0)

    @pl.when((s > 0) & (s < last))
    def _():
        run(POSITIONS)

    @pl.when(s == last)
    def _():
        run(n_tiles)

    @pl.when(new_stream & (s > 0))
    def _():
        pool_ref[...] = carry_end[HDR - POOL_BUF:HDR, :]
        for hd in range(N_HEADS):
            hg_ref[hd] = st_end[hd].T


def _const_spec(shape):
    nd = len(shape)
    return pl.BlockSpec(shape, lambda *_: (0,) * nd, pipeline_mode=pl.Buffered(1))


def _prompt_call(x, p, lbl, w_in, w_pool, scale, hgn, w_out, npre, npost, w_ple, w_gate):
    b, t, _ = x.shape
    tt = POSITIONS * SUB_TILE
    assert t % tt == 0 and SUB_TILE % CHUNK == 0 and POSITIONS % 2 == 0
    nt = t // SUB_TILE
    nblk = b * t // tt
    weights = (lbl, w_in, w_pool, scale, hgn, w_out, npre, npost, w_ple, w_gate)
    x2 = x.reshape(b * t, D_MODEL)
    p2 = p.reshape(b * t, D_PLE)
    ahead = lambda s: (jnp.minimum(s, nblk - 1), 0)
    behind = lambda s: (jnp.clip(s - 1, 0, nblk - 1), 0)
    stream = lambda s: (jnp.clip((POSITIONS * s - 2) // nt, 0, b - 1),)
    y, pool, hg = pl.pallas_call(
        functools.partial(_prompt_kernel, nt, b * nt),
        grid=(nblk + 1,),
        in_specs=[pl.BlockSpec((tt, D_MODEL), ahead),
                  pl.BlockSpec((tt, D_MODEL), behind),
                  pl.BlockSpec((tt, D_PLE), behind)]
                 + [_const_spec(w.shape) for w in weights],
        out_specs=[pl.BlockSpec((tt, D_MODEL), behind),
                   pl.BlockSpec((None, POOL_BUF, D_POOL), lambda s: stream(s) + (0, 0)),
                   pl.BlockSpec((None, N_HEADS, HEAD, HEAD), lambda s: stream(s) + (0, 0, 0))],
        out_shape=[jax.ShapeDtypeStruct((b * t, D_MODEL), jnp.float32),
                   jax.ShapeDtypeStruct((b, POOL_BUF, D_POOL), jnp.float32),
                   jax.ShapeDtypeStruct((b, N_HEADS, HEAD, HEAD), jnp.float32)],
        scratch_shapes=[pltpu.VMEM((HDR + SUB_TILE, D_POOL), jnp.float32),
                        pltpu.VMEM((SUB_TILE, D_IN - D_POOL), jnp.float32)] * 2
                       + [pltpu.VMEM((SUB_TILE, D_MODEL), jnp.bfloat16)] * POSITIONS
                       + [pltpu.VMEM((HDR, D_POOL), jnp.float32),
                          pltpu.VMEM((N_HEADS, HEAD, HEAD), jnp.float32)] * 2,
        compiler_params=pltpu.CompilerParams(dimension_semantics=("arbitrary",),
                                             vmem_limit_bytes=VMEM_LIMIT_BYTES),
        name="prompt_layer",
    )(x2, x2, p2, *weights)
    return y.reshape(b, t, D_MODEL), pool, hg


STREAMS_PER_ITER = 8
SAMPLE_VMEM_LIMIT_BYTES = 58 * 1024 * 1024


def _sample_kernel(start_pos, x_ref, p_ref, cache_ref, s0_ref, lbl_ref, w_in_ref, w_pool_ref, scale_ref,
                   hgn_ref, w_out_ref, npre_ref, npost_ref, w_ple_ref, w_gate_ref,
                   y_ref, pool_ref, hg_ref, w_in_b, w_out_b, w_ple_b, w_gate_b,
                   z_ref, ext_ref, mix_ref):
    n_streams, ts = pool_ref.shape[1], pool_ref.shape[0] + 1
    ext_rows = HDR + ts

    for src, dst in ((w_in_ref, w_in_b), (w_out_ref, w_out_b), (w_ple_ref, w_ple_b), (w_gate_ref, w_gate_b)):
        for j in range(dst.shape[0]):
            dst[j] = _bf(src[:, j * COLS:(j + 1) * COLS])

    ext_ref[...] = jnp.zeros(ext_ref.shape, jnp.float32)
    for r in range(POOL_BUF):
        for gi in range(len(POOL_WINDOWS)):
            ext_ref[gi, pl.ds(HDR - POOL_BUF + r, n_streams, stride=ext_rows), :] = (
                cache_ref[r, :, gi * POOL_GROUP:(gi + 1) * POOL_GROUP])

    x = x_ref[...]
    h = _bf(x * _rms_scale(x) * npre_ref[...])
    for j in range(D_IN // COLS):
        z_ref[:, j * COLS:(j + 1) * COLS] = _dot(h, w_in_b[j])

    lb = _lower_bound(lbl_ref[...])
    causal = (lax.broadcasted_iota(jnp.int32, (ts, ts), 1) <= lax.broadcasted_iota(jnp.int32, (ts, ts), 0))
    w_pool = [_bf(w_pool_ref[gi]) for gi in range(len(POOL_WINDOWS))]

    def prepare(b):
        c = {"b": b, "rows": pl.ds(pl.multiple_of(b * ts, ts), ts)}
        rows = c["rows"]
        base = pl.multiple_of(b * ext_rows, ext_rows)
        c["pooled"] = []
        for gi, w in enumerate(POOL_WINDOWS):
            ext_ref[gi, pl.ds(base + HDR, ts), :] = z_ref[rows, C_U + gi * POOL_GROUP:C_U + (gi + 1) * POOL_GROUP]
            c["pooled"].append(_bf(_pool_group(ext_ref[gi, pl.ds(base, ext_rows), :], w, start_pos)))
        gated = []
        for hd in range(N_HEADS):
            q = z_ref[rows, C_Q + hd * HEAD:C_Q + (hd + 1) * HEAD]
            fl = z_ref[rows, C_F + hd * HEAD:C_F + (hd + 1) * HEAD]
            gated.append(_gates(q, fl, lb[:, hd * HEAD:(hd + 1) * HEAD]))
        decay_rows = [F[ts - 1:ts, :] for _, _, F in gated]
        c["decay_cols"] = jnp.concatenate(
            decay_rows + [jnp.zeros((8 - N_HEADS, HEAD), jnp.float32)], axis=0).T
        c["qe"] = [_bf(qe) for qe, _, _ in gated]
        c["ke"] = [_bf(ke) for _, ke, _ in gated]
        c["kd"] = [_bf(ke * decay_rows[hd]) for hd, (_, ke, _) in enumerate(gated)]
        c["v"] = [_bf(z_ref[rows, C_V + hd * HEAD:C_V + (hd + 1) * HEAD]) for hd in range(N_HEADS)]
        return c

    def issue(c):
        c["mixed"] = [_dot(c["pooled"][gi], w_pool[gi]) for gi in range(len(POOL_WINDOWS))]
        c["att"] = [_dot_nt(c["qe"][hd], c["ke"][hd]) for hd in range(N_HEADS)]
        c["upd"] = [_dot_tn(c["kd"][hd], c["v"][hd]) for hd in range(N_HEADS)]

    def combine(c):
        b, rows = c["b"], c["rows"]
        for gi in range(len(POOL_WINDOWS)):
            cs = slice(gi * POOL_GROUP, (gi + 1) * POOL_GROUP)
            gp = z_ref[rows, C_GP + gi * POOL_GROUP:C_GP + (gi + 1) * POOL_GROUP]
            mix_ref[rows, cs] = _bf(c["mixed"][gi] * scale_ref[:, cs] * (gp * _sigmoid(gp)))
        c["o"] = []
        for hd in range(N_HEADS):
            s0 = s0_ref[b, hd]
            att = _bf(jnp.where(causal, c["att"][hd], 0.0))
            c["o"].append(_dot(att, c["v"][hd]) + _dot(c["qe"][hd], _bf(s0)))
            hg_ref[b, hd] = s0 * c["decay_cols"][:, hd:hd + 1] + c["upd"][hd]

    def store(c):
        rows = c["rows"]
        for hd in range(N_HEADS):
            gh = z_ref[rows, C_GH + hd * HEAD:C_GH + (hd + 1) * HEAD]
            mix_ref[rows, D_POOL + hd * HEAD:D_POOL + (hd + 1) * HEAD] = _bf(
                _head_out(c["o"][hd], gh, hgn_ref[...]))

    def stream_group(g, carry):
        ctx = [prepare(g * STREAMS_PER_ITER + i) for i in range(STREAMS_PER_ITER)]
        for phase in (issue, combine, store):
            for c in ctx:
                phase(c)
        return carry

    lax.fori_loop(0, n_streams // STREAMS_PER_ITER, stream_group, 0)

    for r in range(POOL_BUF):
        for gi in range(len(POOL_WINDOWS)):
            pool_ref[r, :, gi * POOL_GROUP:(gi + 1) * POOL_GROUP] = (
                ext_ref[gi, pl.ds(ext_rows - POOL_BUF + r, n_streams, stride=ext_rows), :])

    y_ref[...] = _finish(x, mix_ref[...], _bf(p_ref[...]), w_out_b, w_ple_b, w_gate_b, npost_ref[...])


def _sample_call(start_pos, x, p, cache, s0, lbl, w_in, w_pool, scale, hgn, w_out, npre, npost, w_ple, w_gate):
    b, ts, _ = x.shape
    assert ts == BLK and ts == POOL_BUF + 1 and b % STREAMS_PER_ITER == 0
    n = b * ts
    blocks = lambda w: jax.ShapeDtypeStruct((w.shape[1] // COLS, w.shape[0], COLS), jnp.bfloat16)
    y, pool, hg, w_in_b, w_out_b, w_ple_b, w_gate_b = pl.pallas_call(
        functools.partial(_sample_kernel, start_pos),
        out_shape=[jax.ShapeDtypeStruct((n, D_MODEL), jnp.float32),
                   jax.ShapeDtypeStruct((POOL_BUF, b, D_POOL), jnp.float32),
                   jax.ShapeDtypeStruct((b, N_HEADS, HEAD, HEAD), jnp.float32),
                   blocks(w_in), blocks(w_out), blocks(w_ple), blocks(w_gate)],
        scratch_shapes=[pltpu.VMEM((n, D_IN), jnp.float32),
                        pltpu.VMEM((len(POOL_WINDOWS), b * (HDR + ts), POOL_GROUP), jnp.float32),
                        pltpu.VMEM((n, D_MODEL), jnp.bfloat16)],
        compiler_params=pltpu.CompilerParams(vmem_limit_bytes=SAMPLE_VMEM_LIMIT_BYTES),
        name="sample_layer",
    )(x.reshape(n, D_MODEL), p.reshape(n, D_PLE), jnp.transpose(cache, (1, 0, 2)), s0, lbl, w_in, w_pool, scale,
      hgn, w_out, npre, npost, w_ple, w_gate)
    return (y.reshape(b, ts, D_MODEL), jnp.transpose(pool, (1, 0, 2)), hg), (w_in_b, w_out_b, w_ple_b, w_gate_b)


def kernel(x_prompt, x_sample, cache_pool, state_hgrn, p_prompt, p_sample, lb_logits, w_in, w_pool, pool_scale,
           hg_norm, w_out, norm_pre, norm_post, w_ple, w_ple_gate):
    depth = w_in.shape[0]
    assert depth == 1 and lb_logits.shape[0] == 2
    past_len = 1024
    (y_s, pool_s, hg_s), (w_in_b, w_out_b, w_ple_b, w_gate_b) = _sample_call(
        past_len, x_sample, p_sample[0], cache_pool[0], state_hgrn[0], lb_logits, w_in[0], w_pool[0], pool_scale,
        hg_norm, w_out[0], norm_pre, norm_post, w_ple[0], w_ple_gate[0])
    y_p, pool_p, hg_p = _prompt_call(x_prompt, p_prompt[0], lb_logits, w_in_b, w_pool[0], pool_scale, hg_norm,
                                     w_out_b, norm_pre, norm_post, w_ple_b, w_gate_b)
    return (y_p, y_s, pool_p[None], hg_p[None], pool_s[None], hg_s[None])
```

```python
import functools

import jax
import jax.numpy as jnp
from jax import lax
from jax.experimental import pallas as pl
from jax.experimental.pallas import tpu as pltpu

D_MODEL = 1024
D_POOL = 512
POOL_WINDOWS = (2, 4, 8, 16)
POOL_GROUP = 128
POOL_BUF = 15
N_HEADS = 4
HEAD = 128
D_HG = N_HEADS * HEAD
D_IN = 2 * D_POOL + 4 * D_HG
D_PLE = 256
EPS = 1e-6
LOG2_E = 1.4426950408889634
COLS = 512
SPAN = 32
BLK = 16
CHUNK = 128
SUBLANES = 8
HDR = 16
SUB_TILE = 256
POSITIONS = 2
VMEM_LIMIT_BYTES = 60 * 1024 * 1024

C_U, C_GP, C_Q, C_F, C_V, C_GH = 0, 512, 1024, 1536, 2048, 2560

_NT = (((1,), (1,)), ((), ()))
_TN = (((0,), (0,)), ((), ()))


def _dot(a, b):
    return jnp.dot(a, b, preferred_element_type=jnp.float32)


def _dot_nt(a, b):
    return lax.dot_general(a, b, _NT, preferred_element_type=jnp.float32)


def _dot_tn(a, b):
    return lax.dot_general(a, b, _TN, preferred_element_type=jnp.float32)


def _bf(x):
    return x.astype(jnp.bfloat16)


def _wblk(w_ref, j):
    return w_ref[j]


def _sigmoid(x):
    return 1.0 / (1.0 + jnp.exp2(x * (-LOG2_E)))


def _rms_scale(x):
    return lax.rsqrt(jnp.mean(x * x, axis=-1, keepdims=True) + EPS)


def _lower_bound(lb_logits):
    l0 = lb_logits[0:1, :]
    l1 = lb_logits[1:2, :]
    m = jnp.maximum(l0, l1)
    e0 = jnp.exp(l0 - m)
    e1 = jnp.exp(l1 - m)
    return e0 / (e0 + e1)


def _window_sum(e, w):
    s = e
    d = 1
    while d < w:
        s = s + pltpu.roll(s, d, axis=0)
        d *= 2
    return s


def _pool_group(e, w, first_pos):
    s = _window_sum(e, w)[HDR:]
    u = e[HDR:]
    rows = lax.broadcasted_iota(jnp.int32, (HDR, POOL_GROUP), 0)
    cnt = jnp.minimum(w, first_pos + rows + 1).astype(jnp.float32)
    head = s[:HDR] / cnt - u[:HDR]
    if s.shape[0] == HDR:
        return head
    tail = s[HDR:] * (1.0 / w) - u[HDR:]
    return jnp.concatenate([head, tail], axis=0)


def _block_cumprod(f, reverse=False):
    n, lanes = f.shape
    x = f.reshape(n // SUBLANES, SUBLANES, lanes)
    row = lax.broadcasted_iota(jnp.int32, x.shape, 1)
    for d in (1, 2, 4):
        if reverse:
            x = x * jnp.where(row + d < SUBLANES, pltpu.roll(x, SUBLANES - d, axis=1), 1.0)
        else:
            x = x * jnp.where(row >= d, pltpu.roll(x, d, axis=1), 1.0)
    x = x.reshape(n // BLK, BLK // SUBLANES, SUBLANES, lanes)
    lo, hi = x[:, 0], x[:, 1]
    if reverse:
        lo = lo * hi[:, 0:1, :]
    else:
        hi = hi * lo[:, SUBLANES - 1:SUBLANES, :]
    return jnp.concatenate([lo[:, None], hi[:, None]], axis=1).reshape(n, lanes)


def _gates(q, fl, lb):
    f = lb + (1.0 - lb) * _sigmoid(fl)
    k = 1.0 - f
    F = _block_cumprod(f)
    return q * F, k / F, F


class _HgrnChunk:
    def __init__(self, load, lb):
        self.load, self.lb = load, lb

    def prepare(self):
        q, fl, v = self.load()
        c = self.c = q.shape[0]
        nb = c // SPAN
        f = self.lb + (1.0 - self.lb) * _sigmoid(fl)
        k = 1.0 - f
        halves = lambda x: [jnp.concatenate([x[SPAN * j + BLK * w:SPAN * j + BLK * (w + 1)] for j in range(nb)], axis=0)
                            for w in (0, 1)]
        (f_a, f_b), (q_a, q_b), (k_a, k_b) = halves(f), halves(q), halves(k)
        fwd = _block_cumprod(f_b)
        rev_incl = _block_cumprod(f_a, reverse=True)
        row = lax.broadcasted_iota(jnp.int32, f_a.shape, 0) & (BLK - 1)
        rev = jnp.where(row < BLK - 1, pltpu.roll(rev_incl, f_a.shape[0] - 1, axis=0), 1.0)
        qm = (q_a / rev, q_b * fwd)
        km = (k_a * rev, k_b / fwd)
        span = lambda pair, j: jnp.concatenate([pair[0][BLK * j:BLK * (j + 1)], pair[1][BLK * j:BLK * (j + 1)]], axis=0)
        t_head = [rev_incl[BLK * j:BLK * j + 1, :] for j in range(nb)]
        t_tail = [fwd[BLK * j + BLK - 1:BLK * (j + 1), :] for j in range(nb)]
        T = [t_head[j] * t_tail[j] for j in range(nb)]
        qm_b = [span(qm, j) for j in range(nb)]
        km_b = [span(km, j) for j in range(nb)]
        qe_b = [qm_b[j] * t_head[j] for j in range(nb)]
        kd_b = [km_b[j] * t_tail[j] for j in range(nb)]
        zero = jnp.zeros((SPAN, HEAD), jnp.float32)
        k_pair = [kd_b[j] * t_head[j + 1] if j % 2 == 0 else zero for j in range(nb)]
        self.qe = _bf(jnp.concatenate(qm_b, axis=0))
        self.ke_kd = _bf(jnp.concatenate(km_b + k_pair, axis=0))

        assert nb == 4
        self.q_far = _bf(jnp.concatenate([zero, zero, qe_b[2], qe_b[3] * T[2]], axis=0))
        self.k_far = _bf(jnp.concatenate([kd_b[0] * T[1], kd_b[1], zero, zero], axis=0))

        e_in = [None] * nb
        dec = None
        for j in range(nb):
            e_in[j] = dec
            dec = T[j] if dec is None else dec * T[j]
        self.e_tot = dec
        d_out = [None] * nb
        dec = None
        for j in range(nb - 1, -1, -1):
            d_out[j] = dec
            dec = T[j] if dec is None else dec * T[j]
        self.q_in = _bf(jnp.concatenate(
            [qe_b[j] if e_in[j] is None else qe_b[j] * e_in[j] for j in range(nb)], axis=0))
        self.k_out = _bf(jnp.concatenate(
            [kd_b[j] if d_out[j] is None else kd_b[j] * d_out[j] for j in range(nb)], axis=0))
        self.v_bf = _bf(v)

    def issue(self):
        self.r1 = _dot_nt(self.qe, self.ke_kd)
        self.far = _dot_nt(self.q_far, self.k_far)
        self.upd = _dot_tn(self.v_bf, self.k_out)

    def combine(self, st, mask_d, mask_p):
        c = self.c
        p = jnp.where(mask_d, self.r1[:, :c], jnp.where(mask_p, self.r1[:, c:], 0.0)) + self.far
        self.o = _dot(_bf(p), self.v_bf) + _dot_nt(self.q_in, _bf(st))
        return st * self.e_tot + self.upd


def _head_out(o, gate, hg_norm):
    y = o * _rms_scale(o) * hg_norm
    return y * (gate * _sigmoid(gate))


def _post_norm(x, y_blocks, norm_post):
    ms = sum(jnp.sum(y * y, axis=-1, keepdims=True) for y in y_blocks) / D_MODEL
    r = lax.rsqrt(ms + EPS)
    return [x[:, j * COLS:(j + 1) * COLS] + y * r * norm_post[:, j * COLS:(j + 1) * COLS]
            for j, y in enumerate(y_blocks)]


def _finish(x, mix_bf, p_bf, w_out_ref, w_ple_ref, w_gate_ref, norm_post):
    nblk = D_MODEL // COLS
    x1 = _post_norm(x, [_dot(mix_bf, _wblk(w_out_ref, j)) for j in range(nblk)], norm_post)
    x1_bf = _bf(jnp.concatenate(x1, axis=1))
    out = [x1[j] + _sigmoid(_dot(x1_bf, _wblk(w_gate_ref, j))) * _dot(p_bf, _wblk(w_ple_ref, j)) for j in range(nblk)]
    return jnp.concatenate(out, axis=1)


def _chunk_masks(c):
    t = lax.broadcasted_iota(jnp.int32, (c, c), 0)
    s = lax.broadcasted_iota(jnp.int32, (c, c), 1)
    tb = t // SPAN
    sb = s // SPAN
    mask_d = (tb == sb) & (s <= t)
    mask_p = ((tb & 1) == 1) & (sb == tb - 1)
    return mask_d, mask_p


def _sub_step_head(stages, r0, xa_ref, p_ref, npre_ref, w_ple_ref):
    rows = slice(r0, r0 + SUB_TILE)
    ple, h = [], None
    if 3 in stages:
        p_bf = _bf(p_ref[rows, :])
        ple = [_dot(p_bf, _wblk(w_ple_ref, j)) for j in range(2)]
    if 1 in stages:
        xa = xa_ref[rows, :]
        h = _bf(xa * _rms_scale(xa) * npre_ref[...])
    return ple, h


def _sub_step(stages, first_pos, r0, xa_ref, xc_ref, p_ref, lbl_ref, w_in_ref, w_pool_ref, scale_ref, hgn_ref,
              w_out_ref, npre_ref, npost_ref, w_ple_ref, w_gate_ref, y_ref,
              u_w, z_w, mix_w, u_r, z_r, mix_r, carry, states, head=None, before_tail=None):
    tt = SUB_TILE
    rows = slice(r0, r0 + tt)
    states = list(states)

    def in_proj(h, c0):
        blk = _dot(h, _wblk(w_in_ref, c0 // COLS))
        if c0 == C_U:
            u_w[HDR:HDR + tt, :] = blk
        else:
            z_w[:, c0 - C_GP:c0 - C_GP + D_POOL] = blk

    if 2 not in stages:
        if 1 in stages:
            xa = xa_ref[rows, :]
            h = _bf(xa * _rms_scale(xa) * npre_ref[...])
            for c0 in (C_U, C_GP, C_Q, C_F, C_V, C_GH):
                in_proj(h, c0)
        if 3 in stages:
            y_ref[rows, :] = _finish(xc_ref[rows, :], mix_r[...], _bf(p_ref[rows, :]), w_out_ref, w_ple_ref,
                                     w_gate_ref, npost_ref[...])
        return carry, states

    lb = _lower_bound(lbl_ref[...])
    mask_d, mask_p = _chunk_masks(CHUNK)

    def pool_group(gi):
        cs = slice(gi * POOL_GROUP, (gi + 1) * POOL_GROUP)
        pooled = _pool_group(u_r[:, cs], POOL_WINDOWS[gi], first_pos)
        mixed = _dot(_bf(pooled), _bf(w_pool_ref[gi])) * scale_ref[:, cs]
        gp = z_r[:, cs]
        mix_w[:, cs] = _bf(mixed * (gp * _sigmoid(gp)))

    def hgrn_unit(hd, c0):
        hs = slice(hd * HEAD, (hd + 1) * HEAD)
        rs = slice(c0, c0 + CHUNK)
        col = lambda base: slice(base - C_GP + hd * HEAD, base - C_GP + (hd + 1) * HEAD)
        unit = _HgrnChunk(lambda: (z_r[rs, col(C_Q)], z_r[rs, col(C_F)], z_r[rs, col(C_V)]), lb[:, hs])
        unit.head, unit.rows, unit.gate_cols = hd, rs, col(C_GH)
        return unit

    def hgrn_store(unit):
        gh = z_r[unit.rows, unit.gate_cols]
        mix_w[unit.rows, D_POOL + unit.head * HEAD:D_POOL + (unit.head + 1) * HEAD] = _bf(
            _head_out(unit.o, gh, hgn_ref[...]))

    units = [hgrn_unit(hd, c0) for c0 in range(0, tt, CHUNK) for hd in range(N_HEADS)]
    dense = [("out", 1), ("in", C_U), ("in", C_GP), ("gate", 0), ("gate", 1), ("in", C_Q), ("in", C_F),
             ("in", C_V)]
    assert len(units) == len(dense)
    pool_at = {0: (0,), 2: (1,), 4: (2,), 7: (3,)}
    post_norm_at = 1
    final_at = {5: (0,), 6: (1,)}
    h = mix_prev = x1 = x1_bf = None
    y, gate, ple = {}, {}, []
    if head is None:
        head = _sub_step_head(stages, r0, xa_ref, p_ref, npre_ref, w_ple_ref)
    ple, h = head
    units[0].prepare()
    if 3 in stages:
        mix_prev = mix_r[...]
        y[0] = _dot(mix_prev, _wblk(w_out_ref, 0))
    u_r[0:HDR, :] = carry
    for k, unit in enumerate(units):
        unit.issue()
        if k + 1 < len(units):
            units[k + 1].prepare()
        kind, arg = dense[k]
        if kind == "in" and 1 in stages:
            in_proj(h, arg)
        elif kind == "out" and 3 in stages:
            y[arg] = _dot(mix_prev, _wblk(w_out_ref, arg))
        elif kind == "gate" and 3 in stages:
            gate[arg] = _dot(x1_bf, _wblk(w_gate_ref, arg))
        states[unit.head] = unit.combine(states[unit.head], mask_d, mask_p)
        if k > 0:
            hgrn_store(units[k - 1])
        if k == post_norm_at and 3 in stages:
            x1 = _post_norm(xc_ref[rows, :], [y[0], y[1]], npost_ref[...])
            x1_bf = _bf(jnp.concatenate(x1, axis=1))
        for gi in pool_at.get(k, ()):
            pool_group(gi)
        if 3 in stages:
            for j in final_at.get(k, ()):
                y_ref[rows, j * COLS:(j + 1) * COLS] = x1[j] + _sigmoid(gate[j]) * ple[j]
    if before_tail is not None:
        before_tail()
    if 1 in stages:
        in_proj(h, C_GH)
    hgrn_store(units[-1])
    return u_r[tt:tt + HDR, :], states


def _position_stages(n, n_tiles):
    stages = set()
    if n < n_tiles:
        stages.add(1)
    if 1 <= n <= n_tiles:
        stages.add(2)
    if POSITIONS <= n < n_tiles + POSITIONS:
        stages.add(3)
    return stages


def _prompt_kernel(tiles_per_stream, n_tiles, xa_ref, xc_ref, p_ref, lbl_ref, w_in_ref, w_pool_ref, scale_ref,
                   hgn_ref, w_out_ref, npre_ref, npost_ref, w_ple_ref, w_gate_ref,
                   y_ref, pool_ref, hg_ref, *scratch):
    uz = (scratch[0:2], scratch[2:4])
    mix = scratch[4:4 + POSITIONS]
    carry_ref, st_ref, carry_end, st_end = scratch[4 + POSITIONS:]
    s = pl.program_id(0)
    last = pl.num_programs(0) - 1
    consts = (xa_ref, xc_ref, p_ref, lbl_ref, w_in_ref, w_pool_ref, scale_ref, hgn_ref, w_out_ref, npre_ref,
              npost_ref, w_ple_ref, w_gate_ref, y_ref)
    offsets = [lax.rem(POSITIONS * s + i - 1 + tiles_per_stream, tiles_per_stream) * SUB_TILE
               for i in range(POSITIONS)]
    new_stream = offsets[1] == 0

    def run(first_position):
        stage_sets = [_position_stages(first_position + i, n_tiles) for i in range(POSITIONS)]
        carry = carry_ref[...]
        states = [st_ref[hd] for hd in range(N_HEADS)]
        heads = {}
        for i, stages in enumerate(stage_sets):
            if not stages:
                continue
            r0 = i * SUB_TILE
            before_tail = None
            if i + 1 < POSITIONS and 2 in stages and 2 in stage_sets[i + 1]:
                def before_tail(i=i):
                    heads[i + 1] = _sub_step_head(stage_sets[i + 1], (i + 1) * SUB_TILE, xa_ref, p_ref, npre_ref,
                                                  w_ple_ref)
            if i == 1:
                carry = jnp.where(new_stream, 0.0, carry)
                states = [jnp.where(new_stream, 0.0, st) for st in states]
            u_w, z_w = uz[i % 2]
            u_r, z_r = uz[(i + 1) % 2]
            carry, states = _sub_step(stages, offsets[i], r0, *consts, u_w, z_w, mix[(i - 1) % POSITIONS],
                                      u_r, z_r, mix[i], carry, states, head=heads.get(i), before_tail=before_tail)
            if i == 0:
                carry_end[...] = carry
                for hd in range(N_HEADS):
                    st_end[hd] = states[hd]
        carry_ref[...] = carry
        for hd in range(N_HEADS):
            st_ref[hd] = states[hd]

    @pl.when(s == 0)
    def _():
        carry_ref[...] = jnp.zeros(carry_ref.shape, jnp.float32)
        st_ref[...] = jnp.zeros(st_ref.shape, jnp.float32)
        run(0)

    @pl.when((s > 0) & (s < last))
    def _():
        run(POSITIONS)

    @pl.when(s == last)
    def _():
        run(n_tiles)

    @pl.when(new_stream & (s > 0))
    def _():
        pool_ref[...] = carry_end[HDR - POOL_BUF:HDR, :]
        for hd in range(N_HEADS):
            hg_ref[hd] = st_end[hd].T


def _const_spec(shape):
    nd = len(shape)
    return pl.BlockSpec(shape, lambda *_: (0,) * nd, pipeline_mode=pl.Buffered(1))


def _prompt_call(x, p, lbl, w_in, w_pool, scale, hgn, w_out, npre, npost, w_ple, w_gate):
    b, t, _ = x.shape
    tt = POSITIONS * SUB_TILE
    assert t % tt == 0 and SUB_TILE % CHUNK == 0 and POSITIONS % 2 == 0
    nt = t // SUB_TILE
    nblk = b * t // tt
    weights = (lbl, w_in, w_pool, scale, hgn, w_out, npre, npost, w_ple, w_gate)
    x2 = x.reshape(b * t, D_MODEL)
    p2 = p.reshape(b * t, D_PLE)
    ahead = lambda s: (jnp.minimum(s, nblk - 1), 0)
    behind = lambda s: (jnp.clip(s - 1, 0, nblk - 1), 0)
    stream = lambda s: (jnp.clip((POSITIONS * s - 2) // nt, 0, b - 1),)
    y, pool, hg = pl.pallas_call(
        functools.partial(_prompt_kernel, nt, b * nt),
        grid=(nblk + 1,),
        in_specs=[pl.BlockSpec((tt, D_MODEL), ahead),
                  pl.BlockSpec((tt, D_MODEL), behind),
                  pl.BlockSpec((tt, D_PLE), behind)]
                 + [_const_spec(w.shape) for w in weights],
        out_specs=[pl.BlockSpec((tt, D_MODEL), behind),
                   pl.BlockSpec((None, POOL_BUF, D_POOL), lambda s: stream(s) + (0, 0)),
                   pl.BlockSpec((None, N_HEADS, HEAD, HEAD), lambda s: stream(s) + (0, 0, 0))],
        out_shape=[jax.ShapeDtypeStruct((b * t, D_MODEL), jnp.float32),
                   jax.ShapeDtypeStruct((b, POOL_BUF, D_POOL), jnp.float32),
                   jax.ShapeDtypeStruct((b, N_HEADS, HEAD, HEAD), jnp.float32)],
        scratch_shapes=[pltpu.VMEM((HDR + SUB_TILE, D_POOL), jnp.float32),
                        pltpu.VMEM((SUB_TILE, D_IN - D_POOL), jnp.float32)] * 2
                       + [pltpu.VMEM((SUB_TILE, D_MODEL), jnp.bfloat16)] * POSITIONS
                       + [pltpu.VMEM((HDR, D_POOL), jnp.float32),
                          pltpu.VMEM((N_HEADS, HEAD, HEAD), jnp.float32)] * 2,
        compiler_params=pltpu.CompilerParams(dimension_semantics=("arbitrary",),
                                             vmem_limit_bytes=VMEM_LIMIT_BYTES),
        name="prompt_layer",
    )(x2, x2, p2, *weights)
    return y.reshape(b, t, D_MODEL), pool, hg


STREAMS_PER_ITER = 8
SAMPLE_VMEM_LIMIT_BYTES = 58 * 1024 * 1024


def _sample_kernel(start_pos, x_ref, p_ref, cache_ref, s0_ref, lbl_ref, w_in_ref, w_pool_ref, scale_ref,
                   hgn_ref, w_out_ref, npre_ref, npost_ref, w_ple_ref, w_gate_ref,
                   y_ref, pool_ref, hg_ref, w_in_b, w_out_b, w_ple_b, w_gate_b,
                   z_ref, ext_ref, mix_ref):
    n_streams, ts = pool_ref.shape[1], pool_ref.shape[0] + 1
    ext_rows = HDR + ts

    for src, dst in ((w_in_ref, w_in_b), (w_out_ref, w_out_b), (w_ple_ref, w_ple_b), (w_gate_ref, w_gate_b)):
        for j in range(dst.shape[0]):
            dst[j] = _bf(src[:, j * COLS:(j + 1) * COLS])

    ext_ref[...] = jnp.zeros(ext_ref.shape, jnp.float32)
    for r in range(POOL_BUF):
        for gi in range(len(POOL_WINDOWS)):
            ext_ref[gi, pl.ds(HDR - POOL_BUF + r, n_streams, stride=ext_rows), :] = (
                cache_ref[r, :, gi * POOL_GROUP:(gi + 1) * POOL_GROUP])

    x = x_ref[...]
    h = _bf(x * _rms_scale(x) * npre_ref[...])
    for j in range(D_IN // COLS):
        z_ref[:, j * COLS:(j + 1) * COLS] = _dot(h, w_in_b[j])

    lb = _lower_bound(lbl_ref[...])
    causal = (lax.broadcasted_iota(jnp.int32, (ts, ts), 1) <= lax.broadcasted_iota(jnp.int32, (ts, ts), 0))
    w_pool = [_bf(w_pool_ref[gi]) for gi in range(len(POOL_WINDOWS))]

    def prepare(b):
        c = {"b": b, "rows": pl.ds(pl.multiple_of(b * ts, ts), ts)}
        rows = c["rows"]
        base = pl.multiple_of(b * ext_rows, ext_rows)
        c["pooled"] = []
        for gi, w in enumerate(POOL_WINDOWS):
            ext_ref[gi, pl.ds(base + HDR, ts), :] = z_ref[rows, C_U + gi * POOL_GROUP:C_U + (gi + 1) * POOL_GROUP]
            c["pooled"].append(_bf(_pool_group(ext_ref[gi, pl.ds(base, ext_rows), :], w, start_pos)))
        gated = []
        for hd in range(N_HEADS):
            q = z_ref[rows, C_Q + hd * HEAD:C_Q + (hd + 1) * HEAD]
            fl = z_ref[rows, C_F + hd * HEAD:C_F + (hd + 1) * HEAD]
            gated.append(_gates(q, fl, lb[:, hd * HEAD:(hd + 1) * HEAD]))
        decay_rows = [F[ts - 1:ts, :] for _, _, F in gated]
        c["decay_cols"] = jnp.concatenate(
            decay_rows + [jnp.zeros((8 - N_HEADS, HEAD), jnp.float32)], axis=0).T
        c["qe"] = [_bf(qe) for qe, _, _ in gated]
        c["ke"] = [_bf(ke) for _, ke, _ in gated]
        c["kd"] = [_bf(ke * decay_rows[hd]) for hd, (_, ke, _) in enumerate(gated)]
        c["v"] = [_bf(z_ref[rows, C_V + hd * HEAD:C_V + (hd + 1) * HEAD]) for hd in range(N_HEADS)]
        return c

    def issue(c):
        c["mixed"] = [_dot(c["pooled"][gi], w_pool[gi]) for gi in range(len(POOL_WINDOWS))]
        c["att"] = [_dot_nt(c["qe"][hd], c["ke"][hd]) for hd in range(N_HEADS)]
        c["upd"] = [_dot_tn(c["kd"][hd], c["v"][hd]) for hd in range(N_HEADS)]

    def combine(c):
        b, rows = c["b"], c["rows"]
        for gi in range(len(POOL_WINDOWS)):
            cs = slice(gi * POOL_GROUP, (gi + 1) * POOL_GROUP)
            gp = z_ref[rows, C_GP + gi * POOL_GROUP:C_GP + (gi + 1) * POOL_GROUP]
            mix_ref[rows, cs] = _bf(c["mixed"][gi] * scale_ref[:, cs] * (gp * _sigmoid(gp)))
        c["o"] = []
        for hd in range(N_HEADS):
            s0 = s0_ref[b, hd]
            att = _bf(jnp.where(causal, c["att"][hd], 0.0))
            c["o"].append(_dot(att, c["v"][hd]) + _dot(c["qe"][hd], _bf(s0)))
            hg_ref[b, hd] = s0 * c["decay_cols"][:, hd:hd + 1] + c["upd"][hd]

    def store(c):
        rows = c["rows"]
        for hd in range(N_HEADS):
            gh = z_ref[rows, C_GH + hd * HEAD:C_GH + (hd + 1) * HEAD]
            mix_ref[rows, D_POOL + hd * HEAD:D_POOL + (hd + 1) * HEAD] = _bf(
                _head_out(c["o"][hd], gh, hgn_ref[...]))

    def stream_group(g, carry):
        ctx = [prepare(g * STREAMS_PER_ITER + i) for i in range(STREAMS_PER_ITER)]
        for phase in (issue, combine, store):
            for c in ctx:
                phase(c)
        return carry

    lax.fori_loop(0, n_streams // STREAMS_PER_ITER, stream_group, 0)

    for r in range(POOL_BUF):
        for gi in range(len(POOL_WINDOWS)):
            pool_ref[r, :, gi * POOL_GROUP:(gi + 1) * POOL_GROUP] = (
                ext_ref[gi, pl.ds(ext_rows - POOL_BUF + r, n_streams, stride=ext_rows), :])

    y_ref[...] = _finish(x, mix_ref[...], _bf(p_ref[...]), w_out_b, w_ple_b, w_gate_b, npost_ref[...])


def _sample_call(start_pos, x, p, cache, s0, lbl, w_in, w_pool, scale, hgn, w_out, npre, npost, w_ple, w_gate):
    b, ts, _ = x.shape
    assert ts == BLK and ts == POOL_BUF + 1 and b % STREAMS_PER_ITER == 0
    n = b * ts
    blocks = lambda w: jax.ShapeDtypeStruct((w.shape[1] // COLS, w.shape[0], COLS), jnp.bfloat16)
    y, pool, hg, w_in_b, w_out_b, w_ple_b, w_gate_b = pl.pallas_call(
        functools.partial(_sample_kernel, start_pos),
        out_shape=[jax.ShapeDtypeStruct((n, D_MODEL), jnp.float32),
                   jax.ShapeDtypeStruct((POOL_BUF, b, D_POOL), jnp.float32),
                   jax.ShapeDtypeStruct((b, N_HEADS, HEAD, HEAD), jnp.float32),
                   blocks(w_in), blocks(w_out), blocks(w_ple), blocks(w_gate)],
        scratch_shapes=[pltpu.VMEM((n, D_IN), jnp.float32),
                        pltpu.VMEM((len(POOL_WINDOWS), b * (HDR + ts), POOL_GROUP), jnp.float32),
                        pltpu.VMEM((n, D_MODEL), jnp.bfloat16)],
        compiler_params=pltpu.CompilerParams(vmem_limit_bytes=SAMPLE_VMEM_LIMIT_BYTES),
        name="sample_layer",
    )(x.reshape(n, D_MODEL), p.reshape(n, D_PLE), jnp.transpose(cache, (1, 0, 2)), s0, lbl, w_in, w_pool, scale,
      hgn, w_out, npre, npost, w_ple, w_gate)
    return (y.reshape(b, ts, D_MODEL), jnp.transpose(pool, (1, 0, 2)), hg), (w_in_b, w_out_b, w_ple_b, w_gate_b)


def kernel(x_prompt, x_sample, cache_pool, state_hgrn, p_prompt, p_sample, lb_logits, w_in, w_pool, pool_scale,
           hg_norm, w_out, norm_pre, norm_post, w_ple, w_ple_gate):
    depth = w_in.shape[0]
    assert depth == 1 and lb_logits.shape[0] == 2
    past_len = 1024
    (y_s, pool_s, hg_s), (w_in_b, w_out_b, w_ple_b, w_gate_b) = _sample_call(
        past_len, x_sample, p_sample[0], cache_pool[0], state_hgrn[0], lb_logits, w_in[0], w_pool[0], pool_scale,
        hg_norm, w_out[0], norm_pre, norm_post, w_ple[0], w_ple_gate[0])
    y_p, pool_p, hg_p = _prompt_call(x_prompt, p_prompt[0], lb_logits, w_in_b, w_pool[0], pool_scale, hg_norm,
                                     w_out_b, norm_pre, norm_post, w_ple_b, w_gate_b)
    return (y_p, y_s, pool_p[None], hg_p[None], pool_s[None], hg_s[None])
```

```python
import functools

import jax
import jax.numpy as jnp
from jax import lax
from jax.experimental import pallas as pl
from jax.experimental.pallas import tpu as pltpu

D_MODEL = 1024
D_POOL = 512
POOL_WINDOWS = (2, 4, 8, 16)
POOL_GROUP = 128
POOL_BUF = 15
N_HEADS = 4
HEAD = 128
D_HG = N_HEADS * HEAD
D_IN = 2 * D_POOL + 4 * D_HG
D_PLE = 256
EPS = 1e-6
LOG2_E = 1.4426950408889634
COLS = 512
SPAN = 32
BLK = 16
CHUNK = 128
SUBLANES = 8
HDR = 16
SUB_TILE = 256
POSITIONS = 2
VMEM_LIMIT_BYTES = 60 * 1024 * 1024

C_U, C_GP, C_Q, C_F, C_V, C_GH = 0, 512, 1024, 1536, 2048, 2560

_NT = (((1,), (1,)), ((), ()))
_TN = (((0,), (0,)), ((), ()))


def _dot(a, b):
    return jnp.dot(a, b, preferred_element_type=jnp.float32)


def _dot_nt(a, b):
    return lax.dot_general(a, b, _NT, preferred_element_type=jnp.float32)


def _dot_tn(a, b):
    return lax.dot_general(a, b, _TN, preferred_element_type=jnp.float32)


def _bf(x):
    return x.astype(jnp.bfloat16)


def _wblk(w_ref, j):
    return w_ref[j]


def _sigmoid(x):
    return 1.0 / (1.0 + jnp.exp2(x * (-LOG2_E)))


def _rms_scale(x):
    return lax.rsqrt(jnp.mean(x * x, axis=-1, keepdims=True) + EPS)


def _lower_bound(lb_logits):
    l0 = lb_logits[0:1, :]
    l1 = lb_logits[1:2, :]
    m = jnp.maximum(l0, l1)
    e0 = jnp.exp(l0 - m)
    e1 = jnp.exp(l1 - m)
    return e0 / (e0 + e1)


def _window_sum(e, w):
    s = e
    d = 1
    while d < w:
        s = s + pltpu.roll(s, d, axis=0)
        d *= 2
    return s


def _pool_group(e, w, first_pos):
    s = _window_sum(e, w)[HDR:]
    u = e[HDR:]
    rows = lax.broadcasted_iota(jnp.int32, (HDR, POOL_GROUP), 0)
    cnt = jnp.minimum(w, first_pos + rows + 1).astype(jnp.float32)
    head = s[:HDR] / cnt - u[:HDR]
    if s.shape[0] == HDR:
        return head
    tail = s[HDR:] * (1.0 / w) - u[HDR:]
    return jnp.concatenate([head, tail], axis=0)


def _block_cumprod(f, reverse=False):
    n, lanes = f.shape
    x = f.reshape(n // SUBLANES, SUBLANES, lanes)
    row = lax.broadcasted_iota(jnp.int32, x.shape, 1)
    for d in (1, 2, 4):
        if reverse:
            x = x * jnp.where(row + d < SUBLANES, pltpu.roll(x, SUBLANES - d, axis=1), 1.0)
        else:
            x = x * jnp.where(row >= d, pltpu.roll(x, d, axis=1), 1.0)
    x = x.reshape(n // BLK, BLK // SUBLANES, SUBLANES, lanes)
    lo, hi = x[:, 0], x[:, 1]
    if reverse:
        lo = lo * hi[:, 0:1, :]
    else:
        hi = hi * lo[:, SUBLANES - 1:SUBLANES, :]
    return jnp.concatenate([lo[:, None], hi[:, None]], axis=1).reshape(n, lanes)


def _gates(q, fl, lb):
    f = lb + (1.0 - lb) * _sigmoid(fl)
    k = 1.0 - f
    F = _block_cumprod(f)
    return q * F, k / F, F


class _HgrnChunk:
    def __init__(self, load, lb):
        self.load, self.lb = load, lb

    def prepare(self):
        q, fl, v = self.load()
        c = self.c = q.shape[0]
        nb = c // SPAN
        f = self.lb + (1.0 - self.lb) * _sigmoid(fl)
        k = 1.0 - f
        halves = lambda x: [jnp.concatenate([x[SPAN * j + BLK * w:SPAN * j + BLK * (w + 1)] for j in range(nb)], axis=0)
                            for w in (0, 1)]
        (f_a, f_b), (q_a, q_b), (k_a, k_b) = halves(f), halves(q), halves(k)
        fwd = _block_cumprod(f_b)
        rev_incl = _block_cumprod(f_a, reverse=True)
        row = lax.broadcasted_iota(jnp.int32, f_a.shape, 0) & (BLK - 1)
        rev = jnp.where(row < BLK - 1, pltpu.roll(rev_incl, f_a.shape[0] - 1, axis=0), 1.0)
        qm = (q_a / rev, q_b * fwd)
        km = (k_a * rev, k_b / fwd)
        span = lambda pair, j: jnp.concatenate([pair[0][BLK * j:BLK * (j + 1)], pair[1][BLK * j:BLK * (j + 1)]], axis=0)
        t_head = [rev_incl[BLK * j:BLK * j + 1, :] for j in range(nb)]
        t_tail = [fwd[BLK * j + BLK - 1:BLK * (j + 1), :] for j in range(nb)]
        T = [t_head[j] * t_tail[j] for j in range(nb)]
        qm_b = [span(qm, j) for j in range(nb)]
        km_b = [span(km, j) for j in range(nb)]
        qe_b = [qm_b[j] * t_head[j] for j in range(nb)]
        kd_b = [km_b[j] * t_tail[j] for j in range(nb)]
        zero = jnp.zeros((SPAN, HEAD), jnp.float32)
        k_pair = [kd_b[j] * t_head[j + 1] if j % 2 == 0 else zero for j in range(nb)]
        self.qe = _bf(jnp.concatenate(qm_b, axis=0))
        self.ke_kd = _bf(jnp.concatenate(km_b + k_pair, axis=0))

        assert nb == 4
        self.q_far = _bf(jnp.concatenate([zero, zero, qe_b[2], qe_b[3] * T[2]], axis=0))
        self.k_far = _bf(jnp.concatenate([kd_b[0] * T[1], kd_b[1], zero, zero], axis=0))

        e_in = [None] * nb
        dec = None
        for j in range(nb):
            e_in[j] = dec
            dec = T[j] if dec is None else dec * T[j]
        self.e_tot = dec
        d_out = [None] * nb
        dec = None
        for j in range(nb - 1, -1, -1):
            d_out[j] = dec
            dec = T[j] if dec is None else dec * T[j]
        self.q_in = _bf(jnp.concatenate(
            [qe_b[j] if e_in[j] is None else qe_b[j] * e_in[j] for j in range(nb)], axis=0))
        self.k_out = _bf(jnp.concatenate(
            [kd_b[j] if d_out[j] is None else kd_b[j] * d_out[j] for j in range(nb)], axis=0))
        self.v_bf = _bf(v)

    def issue(self):
        self.r1 = _dot_nt(self.qe, self.ke_kd)
        self.far = _dot_nt(self.q_far, self.k_far)
        self.upd = _dot_tn(self.v_bf, self.k_out)

    def combine(self, st, mask_d, mask_p):
        c = self.c
        p = jnp.where(mask_d, self.r1[:, :c], jnp.where(mask_p, self.r1[:, c:], 0.0)) + self.far
        self.o = _dot(_bf(p), self.v_bf) + _dot_nt(self.q_in, _bf(st))
        return st * self.e_tot + self.upd


def _head_out(o, gate, hg_norm):
    y = o * _rms_scale(o) * hg_norm
    return y * (gate * _sigmoid(gate))


def _post_norm(x, y_blocks, norm_post):
    ms = sum(jnp.sum(y * y, axis=-1, keepdims=True) for y in y_blocks) / D_MODEL
    r = lax.rsqrt(ms + EPS)
    return [x[:, j * COLS:(j + 1) * COLS] + y * r * norm_post[:, j * COLS:(j + 1) * COLS]
            for j, y in enumerate(y_blocks)]


def _finish(x, mix_bf, p_bf, w_out_ref, w_ple_ref, w_gate_ref, norm_post):
    nblk = D_MODEL // COLS
    x1 = _post_norm(x, [_dot(mix_bf, _wblk(w_out_ref, j)) for j in range(nblk)], norm_post)
    x1_bf = _bf(jnp.concatenate(x1, axis=1))
    out = [x1[j] + _sigmoid(_dot(x1_bf, _wblk(w_gate_ref, j))) * _dot(p_bf, _wblk(w_ple_ref, j)) for j in range(nblk)]
    return jnp.concatenate(out, axis=1)


def _chunk_masks(c):
    t = lax.broadcasted_iota(jnp.int32, (c, c), 0)
    s = lax.broadcasted_iota(jnp.int32, (c, c), 1)
    tb = t // SPAN
    sb = s // SPAN
    mask_d = (tb == sb) & (s <= t)
    mask_p = ((tb & 1) == 1) & (sb == tb - 1)
    return mask_d, mask_p


def _sub_step_head(stages, r0, xa_ref, p_ref, npre_ref, w_ple_ref):
    rows = slice(r0, r0 + SUB_TILE)
    ple, h = [], None
    if 3 in stages:
        p_bf = _bf(p_ref[rows, :])
        ple = [_dot(p_bf, _wblk(w_ple_ref, j)) for j in range(2)]
    if 1 in stages:
        xa = xa_ref[rows, :]
        h = _bf(xa * _rms_scale(xa) * npre_ref[...])
    return ple, h


def _sub_step(stages, first_pos, r0, xa_ref, xc_ref, p_ref, lbl_ref, w_in_ref, w_pool_ref, scale_ref, hgn_ref,
              w_out_ref, npre_ref, npost_ref, w_ple_ref, w_gate_ref, y_ref,
              u_w, z_w, mix_w, u_r, z_r, mix_r, carry, states, head=None, before_tail=None):
    tt = SUB_TILE
    rows = slice(r0, r0 + tt)
    states = list(states)

    def in_proj(h, c0):
        blk = _dot(h, _wblk(w_in_ref, c0 // COLS))
        if c0 == C_U:
            u_w[HDR:HDR + tt, :] = blk
        else:
            z_w[:, c0 - C_GP:c0 - C_GP + D_POOL] = blk

    if 2 not in stages:
        if 1 in stages:
            xa = xa_ref[rows, :]
            h = _bf(xa * _rms_scale(xa) * npre_ref[...])
            for c0 in (C_U, C_GP, C_Q, C_F, C_V, C_GH):
                in_proj(h, c0)
        if 3 in stages:
            y_ref[rows, :] = _finish(xc_ref[rows, :], mix_r[...], _bf(p_ref[rows, :]), w_out_ref, w_ple_ref,
                                     w_gate_ref, npost_ref[...])
        return carry, states

    lb = _lower_bound(lbl_ref[...])
    mask_d, mask_p = _chunk_masks(CHUNK)

    def pool_group(gi):
        cs = slice(gi * POOL_GROUP, (gi + 1) * POOL_GROUP)
        pooled = _pool_group(u_r[:, cs], POOL_WINDOWS[gi], first_pos)
        mixed = _dot(_bf(pooled), _bf(w_pool_ref[gi])) * scale_ref[:, cs]
        gp = z_r[:, cs]
        mix_w[:, cs] = _bf(mixed * (gp * _sigmoid(gp)))

    def hgrn_unit(hd, c0):
        hs = slice(hd * HEAD, (hd + 1) * HEAD)
        rs = slice(c0, c0 + CHUNK)
        col = lambda base: slice(base - C_GP + hd * HEAD, base - C_GP + (hd + 1) * HEAD)
        unit = _HgrnChunk(lambda: (z_r[rs, col(C_Q)], z_r[rs, col(C_F)], z_r[rs, col(C_V)]), lb[:, hs])
        unit.head, unit.rows, unit.gate_cols = hd, rs, col(C_GH)
        return unit

    def hgrn_store(unit):
        gh = z_r[unit.rows, unit.gate_cols]
        mix_w[unit.rows, D_POOL + unit.head * HEAD:D_POOL + (unit.head + 1) * HEAD] = _bf(
            _head_out(unit.o, gh, hgn_ref[...]))

    units = [hgrn_unit(hd, c0) for c0 in range(0, tt, CHUNK) for hd in range(N_HEADS)]
    dense = [("out", 1), ("in", C_U), ("in", C_GP), ("gate", 0), ("gate", 1), ("in", C_Q), ("in", C_F),
             ("in", C_V)]
    assert len(units) == len(dense)
    pool_at = {0: (0,), 2: (1,), 4: (2,), 7: (3,)}
    post_norm_at = 1
    final_at = {5: (0,), 6: (1,)}
    h = mix_prev = x1 = x1_bf = None
    y, gate, ple = {}, {}, []
    if head is None:
        head = _sub_step_head(stages, r0, xa_ref, p_ref, npre_ref, w_ple_ref)
    ple, h = head
    units[0].prepare()
    if 3 in stages:
        mix_prev = mix_r[...]
        y[0] = _dot(mix_prev, _wblk(w_out_ref, 0))
    u_r[0:HDR, :] = carry
    for k, unit in enumerate(units):
        unit.issue()
        if k + 1 < len(units):
            units[k + 1].prepare()
        kind, arg = dense[k]
        if kind == "in" and 1 in stages:
            in_proj(h, arg)
        elif kind == "out" and 3 in stages:
            y[arg] = _dot(mix_prev, _wblk(w_out_ref, arg))
        elif kind == "gate" and 3 in stages:
            gate[arg] = _dot(x1_bf, _wblk(w_gate_ref, arg))
        states[unit.head] = unit.combine(states[unit.head], mask_d, mask_p)
        if k > 0:
            hgrn_store(units[k - 1])
        if k == post_norm_at and 3 in stages:
            x1 = _post_norm(xc_ref[rows, :], [y[0], y[1]], npost_ref[...])
            x1_bf = _bf(jnp.concatenate(x1, axis=1))
        for gi in pool_at.get(k, ()):
            pool_group(gi)
        if 3 in stages:
            for j in final_at.get(k, ()):
                y_ref[rows, j * COLS:(j + 1) * COLS] = x1[j] + _sigmoid(gate[j]) * ple[j]
    if before_tail is not None:
        before_tail()
    if 1 in stages:
        in_proj(h, C_GH)
    hgrn_store(units[-1])
    return u_r[tt:tt + HDR, :], states


def _position_stages(n, n_tiles):
    stages = set()
    if n < n_tiles:
        stages.add(1)
    if 1 <= n <= n_tiles:
        stages.add(2)
    if POSITIONS <= n < n_tiles + POSITIONS:
        stages.add(3)
    return stages


def _unpack_params(params_ref):
    return dict(npre_ref=params_ref.at[0:1, :], npost_ref=params_ref.at[1:2, :],
                scale_ref=params_ref.at[2:3, 0:D_POOL], hgn_ref=params_ref.at[2:3, D_POOL:D_POOL + HEAD],
                lbl_ref=params_ref.at[3:5, 0:D_HG])


def _prompt_kernel(tiles_per_stream, n_tiles, xa_ref, xc_ref, p_ref, params_ref, w_in_ref, w_pool_ref, w_out_ref,
                   w_ple_ref, w_gate_ref,
                   y_ref, pool_ref, hg_ref, *scratch):
    uz = (scratch[0:2], scratch[2:4])
    mix = scratch[4:4 + POSITIONS]
    carry_ref, st_ref, carry_end, st_end = scratch[4 + POSITIONS:]
    s = pl.program_id(0)
    last = pl.num_programs(0) - 1
    small = _unpack_params(params_ref)
    lbl_ref, scale_ref, hgn_ref = small["lbl_ref"], small["scale_ref"], small["hgn_ref"]
    npre_ref, npost_ref = small["npre_ref"], small["npost_ref"]
    consts = (xa_ref, xc_ref, p_ref, lbl_ref, w_in_ref, w_pool_ref, scale_ref, hgn_ref, w_out_ref, npre_ref,
              npost_ref, w_ple_ref, w_gate_ref, y_ref)
    offsets = [lax.rem(POSITIONS * s + i - 1 + tiles_per_stream, tiles_per_stream) * SUB_TILE
               for i in range(POSITIONS)]
    new_stream = offsets[1] == 0

    def run(first_position):
        stage_sets = [_position_stages(first_position + i, n_tiles) for i in range(POSITIONS)]
        carry = carry_ref[...]
        states = [st_ref[hd] for hd in range(N_HEADS)]
        heads = {}
        for i, stages in enumerate(stage_sets):
            if not stages:
                continue
            r0 = i * SUB_TILE
            before_tail = None
            if i + 1 < POSITIONS and 2 in stages and 2 in stage_sets[i + 1]:
                def before_tail(i=i):
                    heads[i + 1] = _sub_step_head(stage_sets[i + 1], (i + 1) * SUB_TILE, xa_ref, p_ref, npre_ref,
                                                  w_ple_ref)
            if i == 1:
                carry = jnp.where(new_stream, 0.0, carry)
                states = [jnp.where(new_stream, 0.0, st) for st in states]
            u_w, z_w = uz[i % 2]
            u_r, z_r = uz[(i + 1) % 2]
            carry, states = _sub_step(stages, offsets[i], r0, *consts, u_w, z_w, mix[(i - 1) % POSITIONS],
                                      u_r, z_r, mix[i], carry, states, head=heads.get(i), before_tail=before_tail)
            if i == 0:
                carry_end[...] = carry
                for hd in range(N_HEADS):
                    st_end[hd] = states[hd]
        carry_ref[...] = carry
        for hd in range(N_HEADS):
            st_ref[hd] = states[hd]

    @pl.when(s == 0)
    def _():
        carry_ref[...] = jnp.zeros(carry_ref.shape, jnp.float32)
        st_ref[...] = jnp.zeros(st_ref.shape, jnp.float32)
        run(0)

    @pl.when((s > 0) & (s < last))
    def _():
        run(POSITIONS)

    @pl.when(s == last)
    def _():
        run(n_tiles)

    @pl.when(new_stream & (s > 0))
    def _():
        pool_ref[...] = carry_end[HDR - POOL_BUF:HDR, :]
        for hd in range(N_HEADS):
            hg_ref[hd] = st_end[hd].T


def _const_spec(shape):
    nd = len(shape)
    return pl.BlockSpec(shape, lambda *_: (0,) * nd, pipeline_mode=pl.Buffered(1))


def _prompt_call(x, p, params, w_in, w_pool, w_out, w_ple, w_gate):
    b, t, _ = x.shape
    tt = POSITIONS * SUB_TILE
    assert t % tt == 0 and SUB_TILE % CHUNK == 0 and POSITIONS % 2 == 0
    nt = t // SUB_TILE
    nblk = b * t // tt
    weights = (params, w_in, w_pool, w_out, w_ple, w_gate)
    x2 = x.reshape(b * t, D_MODEL)
    p2 = p.reshape(b * t, D_PLE)
    ahead = lambda s: (jnp.minimum(s, nblk - 1), 0)
    behind = lambda s: (jnp.clip(s - 1, 0, nblk - 1), 0)
    stream = lambda s: (jnp.clip((POSITIONS * s - 2) // nt, 0, b - 1),)
    y, pool, hg = pl.pallas_call(
        functools.partial(_prompt_kernel, nt, b * nt),
        grid=(nblk + 1,),
        in_specs=[pl.BlockSpec((tt, D_MODEL), ahead),
                  pl.BlockSpec((tt, D_MODEL), behind),
                  pl.BlockSpec((tt, D_PLE), behind)]
                 + [_const_spec(w.shape) for w in weights],
        out_specs=[pl.BlockSpec((tt, D_MODEL), behind),
                   pl.BlockSpec((None, POOL_BUF, D_POOL), lambda s: stream(s) + (0, 0)),
                   pl.BlockSpec((None, N_HEADS, HEAD, HEAD), lambda s: stream(s) + (0, 0, 0))],
        out_shape=[jax.ShapeDtypeStruct((b * t, D_MODEL), jnp.float32),
                   jax.ShapeDtypeStruct((b, POOL_BUF, D_POOL), jnp.float32),
                   jax.ShapeDtypeStruct((b, N_HEADS, HEAD, HEAD), jnp.float32)],
        scratch_shapes=[pltpu.VMEM((HDR + SUB_TILE, D_POOL), jnp.float32),
                        pltpu.VMEM((SUB_TILE, D_IN - D_POOL), jnp.float32)] * 2
                       + [pltpu.VMEM((SUB_TILE, D_MODEL), jnp.bfloat16)] * POSITIONS
                       + [pltpu.VMEM((HDR, D_POOL), jnp.float32),
                          pltpu.VMEM((N_HEADS, HEAD, HEAD), jnp.float32)] * 2,
        compiler_params=pltpu.CompilerParams(dimension_semantics=("arbitrary",),
                                             vmem_limit_bytes=VMEM_LIMIT_BYTES),
        name="prompt_layer",
    )(x2, x2, p2, *weights)
    return y.reshape(b, t, D_MODEL), pool, hg


STREAMS_PER_ITER = 8
SAMPLE_VMEM_LIMIT_BYTES = 58 * 1024 * 1024


def _sample_kernel(start_pos, x_ref, p_ref, cache_ref, s0_ref, lbl_ref, w_in_ref, w_pool_ref, scale_ref,
                   hgn_ref, w_out_ref, npre_ref, npost_ref, w_ple_ref, w_gate_ref,
                   y_ref, pool_ref, hg_ref, w_in_b, w_out_b, w_ple_b, w_gate_b, params_out,
                   z_ref, ext_ref, mix_ref):
    n_streams, ts = pool_ref.shape[1], pool_ref.shape[0] + 1
    ext_rows = HDR + ts

    for src, dst in ((w_in_ref, w_in_b), (w_out_ref, w_out_b), (w_ple_ref, w_ple_b), (w_gate_ref, w_gate_b)):
        for j in range(dst.shape[0]):
            dst[j] = _bf(src[:, j * COLS:(j + 1) * COLS])

    params_out[...] = jnp.zeros(params_out.shape, jnp.float32)
    packed = _unpack_params(params_out)
    for name, src in (("npre_ref", npre_ref), ("npost_ref", npost_ref), ("scale_ref", scale_ref),
                      ("hgn_ref", hgn_ref), ("lbl_ref", lbl_ref)):
        packed[name][...] = src[...]

    ext_ref[...] = jnp.zeros(ext_ref.shape, jnp.float32)
    for r in range(POOL_BUF):
        for gi in range(len(POOL_WINDOWS)):
            ext_ref[gi, pl.ds(HDR - POOL_BUF + r, n_streams, stride=ext_rows), :] = (
                cache_ref[r, :, gi * POOL_GROUP:(gi + 1) * POOL_GROUP])

    x = x_ref[...]
    h = _bf(x * _rms_scale(x) * npre_ref[...])
    for j in range(D_IN // COLS):
        z_ref[:, j * COLS:(j + 1) * COLS] = _dot(h, w_in_b[j])

    lb = _lower_bound(lbl_ref[...])
    causal = (lax.broadcasted_iota(jnp.int32, (ts, ts), 1) <= lax.broadcasted_iota(jnp.int32, (ts, ts), 0))
    w_pool = [_bf(w_pool_ref[gi]) for gi in range(len(POOL_WINDOWS))]

    def prepare(b):
        c = {"b": b, "rows": pl.ds(pl.multiple_of(b * ts, ts), ts)}
        rows = c["rows"]
        base = pl.multiple_of(b * ext_rows, ext_rows)
        c["pooled"] = []
        for gi, w in enumerate(POOL_WINDOWS):
            ext_ref[gi, pl.ds(base + HDR, ts), :] = z_ref[rows, C_U + gi * POOL_GROUP:C_U + (gi + 1) * POOL_GROUP]
            c["pooled"].append(_bf(_pool_group(ext_ref[gi, pl.ds(base, ext_rows), :], w, start_pos)))
        gated = []
        for hd in range(N_HEADS):
            q = z_ref[rows, C_Q + hd * HEAD:C_Q + (hd + 1) * HEAD]
            fl = z_ref[rows, C_F + hd * HEAD:C_F + (hd + 1) * HEAD]
            gated.append(_gates(q, fl, lb[:, hd * HEAD:(hd + 1) * HEAD]))
        decay_rows = [F[ts - 1:ts, :] for _, _, F in gated]
        c["decay_cols"] = jnp.concatenate(
            decay_rows + [jnp.zeros((8 - N_HEADS, HEAD), jnp.float32)], axis=0).T
        c["qe"] = [_bf(qe) for qe, _, _ in gated]
        c["ke"] = [_bf(ke) for _, ke, _ in gated]
        c["kd"] = [_bf(ke * decay_rows[hd]) for hd, (_, ke, _) in enumerate(gated)]
        c["v"] = [_bf(z_ref[rows, C_V + hd * HEAD:C_V + (hd + 1) * HEAD]) for hd in range(N_HEADS)]
        return c

    def issue(c):
        c["mixed"] = [_dot(c["pooled"][gi], w_pool[gi]) for gi in range(len(POOL_WINDOWS))]
        c["att"] = [_dot_nt(c["qe"][hd], c["ke"][hd]) for hd in range(N_HEADS)]
        c["upd"] = [_dot_tn(c["kd"][hd], c["v"][hd]) for hd in range(N_HEADS)]

    def combine(c):
        b, rows = c["b"], c["rows"]
        for gi in range(len(POOL_WINDOWS)):
            cs = slice(gi * POOL_GROUP, (gi + 1) * POOL_GROUP)
            gp = z_ref[rows, C_GP + gi * POOL_GROUP:C_GP + (gi + 1) * POOL_GROUP]
            mix_ref[rows, cs] = _bf(c["mixed"][gi] * scale_ref[:, cs] * (gp * _sigmoid(gp)))
        c["o"] = []
        for hd in range(N_HEADS):
            s0 = s0_ref[b, hd]
            att = _bf(jnp.where(causal, c["att"][hd], 0.0))
            c["o"].append(_dot(att, c["v"][hd]) + _dot(c["qe"][hd], _bf(s0)))
            hg_ref[b, hd] = s0 * c["decay_cols"][:, hd:hd + 1] + c["upd"][hd]

    def store(c):
        rows = c["rows"]
        for hd in range(N_HEADS):
            gh = z_ref[rows, C_GH + hd * HEAD:C_GH + (hd + 1) * HEAD]
            mix_ref[rows, D_POOL + hd * HEAD:D_POOL + (hd + 1) * HEAD] = _bf(
                _head_out(c["o"][hd], gh, hgn_ref[...]))

    def stream_group(g, carry):
        ctx = [prepare(g * STREAMS_PER_ITER + i) for i in range(STREAMS_PER_ITER)]
        for phase in (issue, combine, store):
            for c in ctx:
                phase(c)
        return carry

    lax.fori_loop(0, n_streams // STREAMS_PER_ITER, stream_group, 0)

    for r in range(POOL_BUF):
        for gi in range(len(POOL_WINDOWS)):
            pool_ref[r, :, gi * POOL_GROUP:(gi + 1) * POOL_GROUP] = (
                ext_ref[gi, pl.ds(ext_rows - POOL_BUF + r, n_streams, stride=ext_rows), :])

    y_ref[...] = _finish(x, mix_ref[...], _bf(p_ref[...]), w_out_b, w_ple_b, w_gate_b, npost_ref[...])


def _sample_call(start_pos, x, p, cache, s0, lbl, w_in, w_pool, scale, hgn, w_out, npre, npost, w_ple, w_gate):
    b, ts, _ = x.shape
    assert ts == BLK and ts == POOL_BUF + 1 and b % STREAMS_PER_ITER == 0
    n = b * ts
    blocks = lambda w: jax.ShapeDtypeStruct((w.shape[1] // COLS, w.shape[0], COLS), jnp.bfloat16)
    y, pool, hg, w_in_b, w_out_b, w_ple_b, w_gate_b, params = pl.pallas_call(
        functools.partial(_sample_kernel, start_pos),
        out_shape=[jax.ShapeDtypeStruct((n, D_MODEL), jnp.float32),
                   jax.ShapeDtypeStruct((POOL_BUF, b, D_POOL), jnp.float32),
                   jax.ShapeDtypeStruct((b, N_HEADS, HEAD, HEAD), jnp.float32),
                   blocks(w_in), blocks(w_out), blocks(w_ple), blocks(w_gate),
                   jax.ShapeDtypeStruct((SUBLANES, D_MODEL), jnp.float32)],
        scratch_shapes=[pltpu.VMEM((n, D_IN), jnp.float32),
                        pltpu.VMEM((len(POOL_WINDOWS), b * (HDR + ts), POOL_GROUP), jnp.float32),
                        pltpu.VMEM((n, D_MODEL), jnp.bfloat16)],
        compiler_params=pltpu.CompilerParams(vmem_limit_bytes=SAMPLE_VMEM_LIMIT_BYTES),
        name="sample_layer",
    )(x.reshape(n, D_MODEL), p.reshape(n, D_PLE), jnp.transpose(cache, (1, 0, 2)), s0, lbl, w_in, w_pool, scale,
      hgn, w_out, npre, npost, w_ple, w_gate)
    return (y.reshape(b, ts, D_MODEL), jnp.transpose(pool, (1, 0, 2)), hg), (params, w_in_b, w_out_b, w_ple_b, w_gate_b)


def kernel(x_prompt, x_sample, cache_pool, state_hgrn, p_prompt, p_sample, lb_logits, w_in, w_pool, pool_scale,
           hg_norm, w_out, norm_pre, norm_post, w_ple, w_ple_gate):
    depth = w_in.shape[0]
    assert depth == 1 and lb_logits.shape[0] == 2
    past_len = 1024
    (y_s, pool_s, hg_s), (params, w_in_b, w_out_b, w_ple_b, w_gate_b) = _sample_call(
        past_len, x_sample, p_sample[0], cache_pool[0], state_hgrn[0], lb_logits, w_in[0], w_pool[0], pool_scale,
        hg_norm, w_out[0], norm_pre, norm_post, w_ple[0], w_ple_gate[0])
    y_p, pool_p, hg_p = _prompt_call(x_prompt, p_prompt[0], params, w_in_b, w_pool[0], w_out_b, w_ple_b, w_gate_b)
    return (y_p, y_s, pool_p[None], hg_p[None], pool_s[None], hg_s[None])
```

```python
import functools

import jax
import jax.numpy as jnp
from jax import lax
from jax.experimental import pallas as pl
from jax.experimental.pallas import tpu as pltpu

D_MODEL = 1024
D_POOL = 512
POOL_WINDOWS = (2, 4, 8, 16)
POOL_GROUP = 128
POOL_BUF = 15
N_HEADS = 4
HEAD = 128
D_HG = N_HEADS * HEAD
D_IN = 2 * D_POOL + 4 * D_HG
D_PLE = 256
EPS = 1e-6
LOG2_E = 1.4426950408889634
COLS = 512
SPAN = 32
BLK = 16
CHUNK = 128
SUBLANES = 8
HDR = 16
SUB_TILE = 256
POSITIONS = 4
VMEM_LIMIT_BYTES = 60 * 1024 * 1024

C_U, C_GP, C_Q, C_F, C_V, C_GH = 0, 512, 1024, 1536, 2048, 2560

_NT = (((1,), (1,)), ((), ()))
_TN = (((0,), (0,)), ((), ()))


def _dot(a, b):
    return jnp.dot(a, b, preferred_element_type=jnp.float32)


def _dot_nt(a, b):
    return lax.dot_general(a, b, _NT, preferred_element_type=jnp.float32)


def _dot_tn(a, b):
    return lax.dot_general(a, b, _TN, preferred_element_type=jnp.float32)


def _bf(x):
    return x.astype(jnp.bfloat16)


def _wblk(w_ref, j):
    return w_ref[j]


def _sigmoid(x):
    return 1.0 / (1.0 + jnp.exp2(x * (-LOG2_E)))


def _rms_scale(x):
    return lax.rsqrt(jnp.mean(x * x, axis=-1, keepdims=True) + EPS)


def _lower_bound(lb_logits):
    l0 = lb_logits[0:1, :]
    l1 = lb_logits[1:2, :]
    m = jnp.maximum(l0, l1)
    e0 = jnp.exp(l0 - m)
    e1 = jnp.exp(l1 - m)
    return e0 / (e0 + e1)


def _window_sum(e, w):
    s = e
    d = 1
    while d < w:
        s = s + pltpu.roll(s, d, axis=0)
        d *= 2
    return s


def _pool_group(e, w, first_pos):
    s = _window_sum(e, w)[HDR:]
    u = e[HDR:]
    rows = lax.broadcasted_iota(jnp.int32, (HDR, POOL_GROUP), 0)
    cnt = jnp.minimum(w, first_pos + rows + 1).astype(jnp.float32)
    head = s[:HDR] / cnt - u[:HDR]
    if s.shape[0] == HDR:
        return head
    tail = s[HDR:] * (1.0 / w) - u[HDR:]
    return jnp.concatenate([head, tail], axis=0)


def _block_cumprod(f, reverse=False):
    n, lanes = f.shape
    x = f.reshape(n // SUBLANES, SUBLANES, lanes)
    row = lax.broadcasted_iota(jnp.int32, x.shape, 1)
    for d in (1, 2, 4):
        if reverse:
            x = x * jnp.where(row + d < SUBLANES, pltpu.roll(x, SUBLANES - d, axis=1), 1.0)
        else:
            x = x * jnp.where(row >= d, pltpu.roll(x, d, axis=1), 1.0)
    x = x.reshape(n // BLK, BLK // SUBLANES, SUBLANES, lanes)
    lo, hi = x[:, 0], x[:, 1]
    if reverse:
        lo = lo * hi[:, 0:1, :]
    else:
        hi = hi * lo[:, SUBLANES - 1:SUBLANES, :]
    return jnp.concatenate([lo[:, None], hi[:, None]], axis=1).reshape(n, lanes)


def _gates(q, fl, lb):
    f = lb + (1.0 - lb) * _sigmoid(fl)
    k = 1.0 - f
    F = _block_cumprod(f)
    return q * F, k / F, F


class _HgrnChunk:
    def __init__(self, load, lb):
        self.load, self.lb = load, lb

    def prepare(self):
        q, fl, v = self.load()
        c = self.c = q.shape[0]
        nb = c // SPAN
        f = self.lb + (1.0 - self.lb) * _sigmoid(fl)
        k = 1.0 - f
        halves = lambda x: [jnp.concatenate([x[SPAN * j + BLK * w:SPAN * j + BLK * (w + 1)] for j in range(nb)], axis=0)
                            for w in (0, 1)]
        (f_a, f_b), (q_a, q_b), (k_a, k_b) = halves(f), halves(q), halves(k)
        fwd = _block_cumprod(f_b)
        rev_incl = _block_cumprod(f_a, reverse=True)
        row = lax.broadcasted_iota(jnp.int32, f_a.shape, 0) & (BLK - 1)
        rev = jnp.where(row < BLK - 1, pltpu.roll(rev_incl, f_a.shape[0] - 1, axis=0), 1.0)
        qm = (q_a / rev, q_b * fwd)
        km = (k_a * rev, k_b / fwd)
        span = lambda pair, j: jnp.concatenate([pair[0][BLK * j:BLK * (j + 1)], pair[1][BLK * j:BLK * (j + 1)]], axis=0)
        t_head = [rev_incl[BLK * j:BLK * j + 1, :] for j in range(nb)]
        t_tail = [fwd[BLK * j + BLK - 1:BLK * (j + 1), :] for j in range(nb)]
        T = [t_head[j] * t_tail[j] for j in range(nb)]
        qm_b = [span(qm, j) for j in range(nb)]
        km_b = [span(km, j) for j in range(nb)]
        qe_b = [qm_b[j] * t_head[j] for j in range(nb)]
        kd_b = [km_b[j] * t_tail[j] for j in range(nb)]
        zero = jnp.zeros((SPAN, HEAD), jnp.float32)
        k_pair = [kd_b[j] * t_head[j + 1] if j % 2 == 0 else zero for j in range(nb)]
        self.qe = _bf(jnp.concatenate(qm_b, axis=0))
        self.ke_kd = _bf(jnp.concatenate(km_b + k_pair, axis=0))

        assert nb == 4
        self.q_far = _bf(jnp.concatenate([zero, zero, qe_b[2], qe_b[3] * T[2]], axis=0))
        self.k_far = _bf(jnp.concatenate([kd_b[0] * T[1], kd_b[1], zero, zero], axis=0))

        e_in = [None] * nb
        dec = None
        for j in range(nb):
            e_in[j] = dec
            dec = T[j] if dec is None else dec * T[j]
        self.e_tot = dec
        d_out = [None] * nb
        dec = None
        for j in range(nb - 1, -1, -1):
            d_out[j] = dec
            dec = T[j] if dec is None else dec * T[j]
        self.q_in = _bf(jnp.concatenate(
            [qe_b[j] if e_in[j] is None else qe_b[j] * e_in[j] for j in range(nb)], axis=0))
        self.k_out = _bf(jnp.concatenate(
            [kd_b[j] if d_out[j] is None else kd_b[j] * d_out[j] for j in range(nb)], axis=0))
        self.v_bf = _bf(v)

    def issue(self):
        self.r1 = _dot_nt(self.qe, self.ke_kd)
        self.far = _dot_nt(self.q_far, self.k_far)
        self.upd = _dot_tn(self.v_bf, self.k_out)

    def combine(self, st, mask_d, mask_p):
        c = self.c
        p = jnp.where(mask_d, self.r1[:, :c], jnp.where(mask_p, self.r1[:, c:], 0.0)) + self.far
        self.o = _dot(_bf(p), self.v_bf) + _dot_nt(self.q_in, _bf(st))
        return st * self.e_tot + self.upd


def _head_out(o, gate, hg_norm):
    y = o * _rms_scale(o) * hg_norm
    return y * (gate * _sigmoid(gate))


def _post_norm(x, y_blocks, norm_post):
    ms = sum(jnp.sum(y * y, axis=-1, keepdims=True) for y in y_blocks) / D_MODEL
    r = lax.rsqrt(ms + EPS)
    return [x[:, j * COLS:(j + 1) * COLS] + y * r * norm_post[:, j * COLS:(j + 1) * COLS]
            for j, y in enumerate(y_blocks)]


def _finish(x, mix_bf, p_bf, w_out_ref, w_ple_ref, w_gate_ref, norm_post):
    nblk = D_MODEL // COLS
    x1 = _post_norm(x, [_dot(mix_bf, _wblk(w_out_ref, j)) for j in range(nblk)], norm_post)
    x1_bf = _bf(jnp.concatenate(x1, axis=1))
    out = [x1[j] + _sigmoid(_dot(x1_bf, _wblk(w_gate_ref, j))) * _dot(p_bf, _wblk(w_ple_ref, j)) for j in range(nblk)]
    return jnp.concatenate(out, axis=1)


def _chunk_masks(c):
    t = lax.broadcasted_iota(jnp.int32, (c, c), 0)
    s = lax.broadcasted_iota(jnp.int32, (c, c), 1)
    tb = t // SPAN
    sb = s // SPAN
    mask_d = (tb == sb) & (s <= t)
    mask_p = ((tb & 1) == 1) & (sb == tb - 1)
    return mask_d, mask_p


def _sub_step_head(stages, r0, xa_ref, p_ref, npre_ref, w_ple_ref):
    rows = slice(r0, r0 + SUB_TILE)
    ple, h = [], None
    if 3 in stages:
        p_bf = _bf(p_ref[rows, :])
        ple = [_dot(p_bf, _wblk(w_ple_ref, j)) for j in range(2)]
    if 1 in stages:
        xa = xa_ref[rows, :]
        h = _bf(xa * _rms_scale(xa) * npre_ref[...])
    return ple, h


def _sub_step(stages, first_pos, r0, xa_ref, xc_ref, p_ref, lbl_ref, w_in_ref, w_pool_ref, scale_ref, hgn_ref,
              w_out_ref, npre_ref, npost_ref, w_ple_ref, w_gate_ref, y_ref,
              u_w, z_w, mix_w, u_r, z_r, mix_r, carry, states, head=None, before_tail=None):
    tt = SUB_TILE
    rows = slice(r0, r0 + tt)
    states = list(states)

    def in_proj(h, c0):
        blk = _dot(h, _wblk(w_in_ref, c0 // COLS))
        if c0 == C_U:
            u_w[HDR:HDR + tt, :] = blk
        else:
            z_w[:, c0 - C_GP:c0 - C_GP + D_POOL] = blk

    if 2 not in stages:
        if 1 in stages:
            xa = xa_ref[rows, :]
            h = _bf(xa * _rms_scale(xa) * npre_ref[...])
            for c0 in (C_U, C_GP, C_Q, C_F, C_V, C_GH):
                in_proj(h, c0)
        if 3 in stages:
            y_ref[rows, :] = _finish(xc_ref[rows, :], mix_r[...], _bf(p_ref[rows, :]), w_out_ref, w_ple_ref,
                                     w_gate_ref, npost_ref[...])
        return carry, states

    lb = _lower_bound(lbl_ref[...])
    mask_d, mask_p = _chunk_masks(CHUNK)

    def pool_group(gi):
        cs = slice(gi * POOL_GROUP, (gi + 1) * POOL_GROUP)
        pooled = _pool_group(u_r[:, cs], POOL_WINDOWS[gi], first_pos)
        mixed = _dot(_bf(pooled), _bf(w_pool_ref[gi])) * scale_ref[:, cs]
        gp = z_r[:, cs]
        mix_w[:, cs] = _bf(mixed * (gp * _sigmoid(gp)))

    def hgrn_unit(hd, c0):
        hs = slice(hd * HEAD, (hd + 1) * HEAD)
        rs = slice(c0, c0 + CHUNK)
        col = lambda base: slice(base - C_GP + hd * HEAD, base - C_GP + (hd + 1) * HEAD)
        unit = _HgrnChunk(lambda: (z_r[rs, col(C_Q)], z_r[rs, col(C_F)], z_r[rs, col(C_V)]), lb[:, hs])
        unit.head, unit.rows, unit.gate_cols = hd, rs, col(C_GH)
        return unit

    def hgrn_store(unit):
        gh = z_r[unit.rows, unit.gate_cols]
        mix_w[unit.rows, D_POOL + unit.head * HEAD:D_POOL + (unit.head + 1) * HEAD] = _bf(
            _head_out(unit.o, gh, hgn_ref[...]))

    units = [hgrn_unit(hd, c0) for c0 in range(0, tt, CHUNK) for hd in range(N_HEADS)]
    dense = [("out", 1), ("in", C_U), ("in", C_GP), ("gate", 0), ("gate", 1), ("in", C_Q), ("in", C_F),
             ("in", C_V)]
    assert len(units) == len(dense)
    pool_at = {0: (0,), 2: (1,), 4: (2,), 7: (3,)}
    post_norm_at = 1
    final_at = {5: (0,), 6: (1,)}
    h = mix_prev = x1 = x1_bf = None
    y, gate, ple = {}, {}, []
    if head is None:
        head = _sub_step_head(stages, r0, xa_ref, p_ref, npre_ref, w_ple_ref)
    ple, h = head
    units[0].prepare()
    if 3 in stages:
        mix_prev = mix_r[...]
        y[0] = _dot(mix_prev, _wblk(w_out_ref, 0))
    u_r[0:HDR, :] = carry
    for k, unit in enumerate(units):
        unit.issue()
        if k + 1 < len(units):
            units[k + 1].prepare()
        kind, arg = dense[k]
        if kind == "in" and 1 in stages:
            in_proj(h, arg)
        elif kind == "out" and 3 in stages:
            y[arg] = _dot(mix_prev, _wblk(w_out_ref, arg))
        elif kind == "gate" and 3 in stages:
            gate[arg] = _dot(x1_bf, _wblk(w_gate_ref, arg))
        states[unit.head] = unit.combine(states[unit.head], mask_d, mask_p)
        if k > 0:
            hgrn_store(units[k - 1])
        if k == post_norm_at and 3 in stages:
            x1 = _post_norm(xc_ref[rows, :], [y[0], y[1]], npost_ref[...])
            x1_bf = _bf(jnp.concatenate(x1, axis=1))
        for gi in pool_at.get(k, ()):
            pool_group(gi)
        if 3 in stages:
            for j in final_at.get(k, ()):
                y_ref[rows, j * COLS:(j + 1) * COLS] = x1[j] + _sigmoid(gate[j]) * ple[j]
    if before_tail is not None:
        before_tail()
    if 1 in stages:
        in_proj(h, C_GH)
    hgrn_store(units[-1])
    return u_r[tt:tt + HDR, :], states


def _position_stages(n, n_tiles):
    stages = set()
    if n < n_tiles:
        stages.add(1)
    if 1 <= n <= n_tiles:
        stages.add(2)
    if POSITIONS <= n < n_tiles + POSITIONS:
        stages.add(3)
    return stages


def _unpack_params(params_ref):
    return dict(npre_ref=params_ref.at[0:1, :], npost_ref=params_ref.at[1:2, :],
                scale_ref=params_ref.at[2:3, 0:D_POOL], hgn_ref=params_ref.at[2:3, D_POOL:D_POOL + HEAD],
                lbl_ref=params_ref.at[3:5, 0:D_HG])


def _prompt_kernel(tiles_per_stream, n_tiles, xa_ref, xc_ref, p_ref, params_ref, w_in_ref, w_pool_ref, w_out_ref,
                   w_ple_ref, w_gate_ref,
                   y_ref, pool_ref, hg_ref, *scratch):
    uz = (scratch[0:2], scratch[2:4])
    mix = scratch[4:4 + POSITIONS]
    carry_ref, st_ref, carry_end, st_end = scratch[4 + POSITIONS:]
    s = pl.program_id(0)
    last = pl.num_programs(0) - 1
    small = _unpack_params(params_ref)
    lbl_ref, scale_ref, hgn_ref = small["lbl_ref"], small["scale_ref"], small["hgn_ref"]
    npre_ref, npost_ref = small["npre_ref"], small["npost_ref"]
    consts = (xa_ref, xc_ref, p_ref, lbl_ref, w_in_ref, w_pool_ref, scale_ref, hgn_ref, w_out_ref, npre_ref,
              npost_ref, w_ple_ref, w_gate_ref, y_ref)
    offsets = [lax.rem(POSITIONS * s + i - 1 + tiles_per_stream, tiles_per_stream) * SUB_TILE
               for i in range(POSITIONS)]
    new_stream = offsets[1] == 0

    def run(first_position):
        stage_sets = [_position_stages(first_position + i, n_tiles) for i in range(POSITIONS)]
        carry = carry_ref[...]
        states = [st_ref[hd] for hd in range(N_HEADS)]
        heads = {}
        for i, stages in enumerate(stage_sets):
            if not stages:
                continue
            r0 = i * SUB_TILE
            before_tail = None
            if i + 1 < POSITIONS and 2 in stages and 2 in stage_sets[i + 1]:
                def before_tail(i=i):
                    heads[i + 1] = _sub_step_head(stage_sets[i + 1], (i + 1) * SUB_TILE, xa_ref, p_ref, npre_ref,
                                                  w_ple_ref)
            if i == 1:
                carry = jnp.where(new_stream, 0.0, carry)
                states = [jnp.where(new_stream, 0.0, st) for st in states]
            u_w, z_w = uz[i % 2]
            u_r, z_r = uz[(i + 1) % 2]
            carry, states = _sub_step(stages, offsets[i], r0, *consts, u_w, z_w, mix[(i - 1) % POSITIONS],
                                      u_r, z_r, mix[i], carry, states, head=heads.get(i), before_tail=before_tail)
            if i == 0:
                carry_end[...] = carry
                for hd in range(N_HEADS):
                    st_end[hd] = states[hd]
        carry_ref[...] = carry
        for hd in range(N_HEADS):
            st_ref[hd] = states[hd]

    @pl.when(s == 0)
    def _():
        carry_ref[...] = jnp.zeros(carry_ref.shape, jnp.float32)
        st_ref[...] = jnp.zeros(st_ref.shape, jnp.float32)
        run(0)

    @pl.when((s > 0) & (s < last))
    def _():
        run(POSITIONS)

    @pl.when(s == last)
    def _():
        run(n_tiles)

    @pl.when(new_stream & (s > 0))
    def _():
        pool_ref[...] = carry_end[HDR - POOL_BUF:HDR, :]
        for hd in range(N_HEADS):
            hg_ref[hd] = st_end[hd].T


def _const_spec(shape):
    nd = len(shape)
    return pl.BlockSpec(shape, lambda *_: (0,) * nd, pipeline_mode=pl.Buffered(1))


def _prompt_call(x, p, params, w_in, w_pool, w_out, w_ple, w_gate):
    b, t, _ = x.shape
    tt = POSITIONS * SUB_TILE
    assert t % tt == 0 and SUB_TILE % CHUNK == 0 and POSITIONS % 2 == 0
    nt = t // SUB_TILE
    nblk = b * t // tt
    weights = (params, w_in, w_pool, w_out, w_ple, w_gate)
    x2 = x.reshape(b * t, D_MODEL)
    p2 = p.reshape(b * t, D_PLE)
    ahead = lambda s: (jnp.minimum(s, nblk - 1), 0)
    behind = lambda s: (jnp.clip(s - 1, 0, nblk - 1), 0)
    stream = lambda s: (jnp.clip((POSITIONS * s - 2) // nt, 0, b - 1),)
    y, pool, hg = pl.pallas_call(
        functools.partial(_prompt_kernel, nt, b * nt),
        grid=(nblk + 1,),
        in_specs=[pl.BlockSpec((tt, D_MODEL), ahead),
                  pl.BlockSpec((tt, D_MODEL), behind),
                  pl.BlockSpec((tt, D_PLE), behind)]
                 + [_const_spec(w.shape) for w in weights],
        out_specs=[pl.BlockSpec((tt, D_MODEL), behind),
                   pl.BlockSpec((None, POOL_BUF, D_POOL), lambda s: stream(s) + (0, 0)),
                   pl.BlockSpec((None, N_HEADS, HEAD, HEAD), lambda s: stream(s) + (0, 0, 0))],
        out_shape=[jax.ShapeDtypeStruct((b * t, D_MODEL), jnp.float32),
                   jax.ShapeDtypeStruct((b, POOL_BUF, D_POOL), jnp.float32),
                   jax.ShapeDtypeStruct((b, N_HEADS, HEAD, HEAD), jnp.float32)],
        scratch_shapes=[pltpu.VMEM((HDR + SUB_TILE, D_POOL), jnp.float32),
                        pltpu.VMEM((SUB_TILE, D_IN - D_POOL), jnp.float32)] * 2
                       + [pltpu.VMEM((SUB_TILE, D_MODEL), jnp.bfloat16)] * POSITIONS
                       + [pltpu.VMEM((HDR, D_POOL), jnp.float32),
                          pltpu.VMEM((N_HEADS, HEAD, HEAD), jnp.float32)] * 2,
        compiler_params=pltpu.CompilerParams(dimension_semantics=("arbitrary",),
                                             vmem_limit_bytes=VMEM_LIMIT_BYTES),
        name="prompt_layer",
    )(x2, x2, p2, *weights)
    return y.reshape(b, t, D_MODEL), pool, hg


STREAMS_PER_ITER = 8
SAMPLE_VMEM_LIMIT_BYTES = 58 * 1024 * 1024


def _sample_kernel(start_pos, x_ref, p_ref, cache_ref, s0_ref, lbl_ref, w_in_ref, w_pool_ref, scale_ref,
                   hgn_ref, w_out_ref, npre_ref, npost_ref, w_ple_ref, w_gate_ref,
                   y_ref, pool_ref, hg_ref, w_in_b, w_out_b, w_ple_b, w_gate_b, params_out,
                   z_ref, ext_ref, mix_ref):
    n_streams, ts = pool_ref.shape[1], pool_ref.shape[0] + 1
    ext_rows = HDR + ts

    for src, dst in ((w_in_ref, w_in_b), (w_out_ref, w_out_b), (w_ple_ref, w_ple_b), (w_gate_ref, w_gate_b)):
        for j in range(dst.shape[0]):
            dst[j] = _bf(src[:, j * COLS:(j + 1) * COLS])

    params_out[...] = jnp.zeros(params_out.shape, jnp.float32)
    packed = _unpack_params(params_out)
    for name, src in (("npre_ref", npre_ref), ("npost_ref", npost_ref), ("scale_ref", scale_ref),
                      ("hgn_ref", hgn_ref), ("lbl_ref", lbl_ref)):
        packed[name][...] = src[...]

    ext_ref[...] = jnp.zeros(ext_ref.shape, jnp.float32)
    for r in range(POOL_BUF):
        for gi in range(len(POOL_WINDOWS)):
            ext_ref[gi, pl.ds(HDR - POOL_BUF + r, n_streams, stride=ext_rows), :] = (
                cache_ref[r, :, gi * POOL_GROUP:(gi + 1) * POOL_GROUP])

    x = x_ref[...]
    h = _bf(x * _rms_scale(x) * npre_ref[...])
    for j in range(D_IN // COLS):
        z_ref[:, j * COLS:(j + 1) * COLS] = _dot(h, w_in_b[j])

    lb = _lower_bound(lbl_ref[...])
    causal = (lax.broadcasted_iota(jnp.int32, (ts, ts), 1) <= lax.broadcasted_iota(jnp.int32, (ts, ts), 0))
    w_pool = [_bf(w_pool_ref[gi]) for gi in range(len(POOL_WINDOWS))]

    def prepare(b):
        c = {"b": b, "rows": pl.ds(pl.multiple_of(b * ts, ts), ts)}
        rows = c["rows"]
        base = pl.multiple_of(b * ext_rows, ext_rows)
        c["pooled"] = []
        for gi, w in enumerate(POOL_WINDOWS):
            ext_ref[gi, pl.ds(base + HDR, ts), :] = z_ref[rows, C_U + gi * POOL_GROUP:C_U + (gi + 1) * POOL_GROUP]
            c["pooled"].append(_bf(_pool_group(ext_ref[gi, pl.ds(base, ext_rows), :], w, start_pos)))
        gated = []
        for hd in range(N_HEADS):
            q = z_ref[rows, C_Q + hd * HEAD:C_Q + (hd + 1) * HEAD]
            fl = z_ref[rows, C_F + hd * HEAD:C_F + (hd + 1) * HEAD]
            gated.append(_gates(q, fl, lb[:, hd * HEAD:(hd + 1) * HEAD]))
        decay_rows = [F[ts - 1:ts, :] for _, _, F in gated]
        c["decay_cols"] = jnp.concatenate(
            decay_rows + [jnp.zeros((8 - N_HEADS, HEAD), jnp.float32)], axis=0).T
        c["qe"] = [_bf(qe) for qe, _, _ in gated]
        c["ke"] = [_bf(ke) for _, ke, _ in gated]
        c["kd"] = [_bf(ke * decay_rows[hd]) for hd, (_, ke, _) in enumerate(gated)]
        c["v"] = [_bf(z_ref[rows, C_V + hd * HEAD:C_V + (hd + 1) * HEAD]) for hd in range(N_HEADS)]
        return c

    def issue(c):
        c["mixed"] = [_dot(c["pooled"][gi], w_pool[gi]) for gi in range(len(POOL_WINDOWS))]
        c["att"] = [_dot_nt(c["qe"][hd], c["ke"][hd]) for hd in range(N_HEADS)]
        c["upd"] = [_dot_tn(c["kd"][hd], c["v"][hd]) for hd in range(N_HEADS)]

    def combine(c):
        b, rows = c["b"], c["rows"]
        for gi in range(len(POOL_WINDOWS)):
            cs = slice(gi * POOL_GROUP, (gi + 1) * POOL_GROUP)
            gp = z_ref[rows, C_GP + gi * POOL_GROUP:C_GP + (gi + 1) * POOL_GROUP]
            mix_ref[rows, cs] = _bf(c["mixed"][gi] * scale_ref[:, cs] * (gp * _sigmoid(gp)))
        c["o"] = []
        for hd in range(N_HEADS):
            s0 = s0_ref[b, hd]
            att = _bf(jnp.where(causal, c["att"][hd], 0.0))
            c["o"].append(_dot(att, c["v"][hd]) + _dot(c["qe"][hd], _bf(s0)))
            hg_ref[b, hd] = s0 * c["decay_cols"][:, hd:hd + 1] + c["upd"][hd]

    def store(c):
        rows = c["rows"]
        for hd in range(N_HEADS):
            gh = z_ref[rows, C_GH + hd * HEAD:C_GH + (hd + 1) * HEAD]
            mix_ref[rows, D_POOL + hd * HEAD:D_POOL + (hd + 1) * HEAD] = _bf(
                _head_out(c["o"][hd], gh, hgn_ref[...]))

    def stream_group(g, carry):
        ctx = [prepare(g * STREAMS_PER_ITER + i) for i in range(STREAMS_PER_ITER)]
        for phase in (issue, combine, store):
            for c in ctx:
                phase(c)
        return carry

    lax.fori_loop(0, n_streams // STREAMS_PER_ITER, stream_group, 0)

    for r in range(POOL_BUF):
        for gi in range(len(POOL_WINDOWS)):
            pool_ref[r, :, gi * POOL_GROUP:(gi + 1) * POOL_GROUP] = (
                ext_ref[gi, pl.ds(ext_rows - POOL_BUF + r, n_streams, stride=ext_rows), :])

    y_ref[...] = _finish(x, mix_ref[...], _bf(p_ref[...]), w_out_b, w_ple_b, w_gate_b, npost_ref[...])


def _sample_call(start_pos, x, p, cache, s0, lbl, w_in, w_pool, scale, hgn, w_out, npre, npost, w_ple, w_gate):
    b, ts, _ = x.shape
    assert ts == BLK and ts == POOL_BUF + 1 and b % STREAMS_PER_ITER == 0
    n = b * ts
    blocks = lambda w: jax.ShapeDtypeStruct((w.shape[1] // COLS, w.shape[0], COLS), jnp.bfloat16)
    y, pool, hg, w_in_b, w_out_b, w_ple_b, w_gate_b, params = pl.pallas_call(
        functools.partial(_sample_kernel, start_pos),
        out_shape=[jax.ShapeDtypeStruct((n, D_MODEL), jnp.float32),
                   jax.ShapeDtypeStruct((POOL_BUF, b, D_POOL), jnp.float32),
                   jax.ShapeDtypeStruct((b, N_HEADS, HEAD, HEAD), jnp.float32),
                   blocks(w_in), blocks(w_out), blocks(w_ple), blocks(w_gate),
                   jax.ShapeDtypeStruct((SUBLANES, D_MODEL), jnp.float32)],
        scratch_shapes=[pltpu.VMEM((n, D_IN), jnp.float32),
                        pltpu.VMEM((len(POOL_WINDOWS), b * (HDR + ts), POOL_GROUP), jnp.float32),
                        pltpu.VMEM((n, D_MODEL), jnp.bfloat16)],
        compiler_params=pltpu.CompilerParams(vmem_limit_bytes=SAMPLE_VMEM_LIMIT_BYTES),
        name="sample_layer",
    )(x.reshape(n, D_MODEL), p.reshape(n, D_PLE), jnp.transpose(cache, (1, 0, 2)), s0, lbl, w_in, w_pool, scale,
      hgn, w_out, npre, npost, w_ple, w_gate)
    return (y.reshape(b, ts, D_MODEL), jnp.transpose(pool, (1, 0, 2)), hg), (params, w_in_b, w_out_b, w_ple_b, w_gate_b)


def kernel(x_prompt, x_sample, cache_pool, state_hgrn, p_prompt, p_sample, lb_logits, w_in, w_pool, pool_scale,
           hg_norm, w_out, norm_pre, norm_post, w_ple, w_ple_gate):
    depth = w_in.shape[0]
    assert depth == 1 and lb_logits.shape[0] == 2
    past_len = 1024
    (y_s, pool_s, hg_s), (params, w_in_b, w_out_b, w_ple_b, w_gate_b) = _sample_call(
        past_len, x_sample, p_sample[0], cache_pool[0], state_hgrn[0], lb_logits, w_in[0], w_pool[0], pool_scale,
        hg_norm, w_out[0], norm_pre, norm_post, w_ple[0], w_ple_gate[0])
    y_p, pool_p, hg_p = _prompt_call(x_prompt, p_prompt[0], params, w_in_b, w_pool[0], w_out_b, w_ple_b, w_gate_b)
    return (y_p, y_s, pool_p[None], hg_p[None], pool_s[None], hg_s[None])
```

```python
import functools

import jax
import jax.numpy as jnp
from jax import lax
from jax.experimental import pallas as pl
from jax.experimental.pallas import tpu as pltpu

D_MODEL = 1024
D_POOL = 512
POOL_WINDOWS = (2, 4, 8, 16)
POOL_GROUP = 128
POOL_BUF = 15
N_HEADS = 4
HEAD = 128
D_HG = N_HEADS * HEAD
D_IN = 2 * D_POOL + 4 * D_HG
D_PLE = 256
EPS = 1e-6
LOG2_E = 1.4426950408889634
COLS = 512
SPAN = 32
BLK = 16
CHUNK = 128
SUBLANES = 8
HDR = 16
SUB_TILE = 256
POSITIONS = 2
VMEM_LIMIT_BYTES = 60 * 1024 * 1024

C_U, C_GP, C_Q, C_F, C_V, C_GH = 0, 512, 1024, 1536, 2048, 2560

_NT = (((1,), (1,)), ((), ()))
_TN = (((0,), (0,)), ((), ()))


def _dot(a, b):
    return jnp.dot(a, b, preferred_element_type=jnp.float32)


def _dot_nt(a, b):
    return lax.dot_general(a, b, _NT, preferred_element_type=jnp.float32)


def _dot_tn(a, b):
    return lax.dot_general(a, b, _TN, preferred_element_type=jnp.float32)


def _bf(x):
    return x.astype(jnp.bfloat16)


def _wblk(w_ref, j):
    return w_ref[j]


def _sigmoid(x):
    return 1.0 / (1.0 + jnp.exp2(x * (-LOG2_E)))


def _rms_scale(x):
    return lax.rsqrt(jnp.mean(x * x, axis=-1, keepdims=True) + EPS)


def _lower_bound(lb_logits):
    l0 = lb_logits[0:1, :]
    l1 = lb_logits[1:2, :]
    m = jnp.maximum(l0, l1)
    e0 = jnp.exp(l0 - m)
    e1 = jnp.exp(l1 - m)
    return e0 / (e0 + e1)


def _window_sum(e, w):
    s = e
    d = 1
    while d < w:
        s = s + pltpu.roll(s, d, axis=0)
        d *= 2
    return s


def _pool_group(e, w, first_pos):
    s = _window_sum(e, w)[HDR:]
    u = e[HDR:]
    rows = lax.broadcasted_iota(jnp.int32, (HDR, POOL_GROUP), 0)
    cnt = jnp.minimum(w, first_pos + rows + 1).astype(jnp.float32)
    head = s[:HDR] / cnt - u[:HDR]
    if s.shape[0] == HDR:
        return head
    tail = s[HDR:] * (1.0 / w) - u[HDR:]
    return jnp.concatenate([head, tail], axis=0)


def _block_cumprod(f, reverse=False):
    n, lanes = f.shape
    x = f.reshape(n // SUBLANES, SUBLANES, lanes)
    row = lax.broadcasted_iota(jnp.int32, x.shape, 1)
    for d in (1, 2, 4):
        if reverse:
            x = x * jnp.where(row + d < SUBLANES, pltpu.roll(x, SUBLANES - d, axis=1), 1.0)
        else:
            x = x * jnp.where(row >= d, pltpu.roll(x, d, axis=1), 1.0)
    x = x.reshape(n // BLK, BLK // SUBLANES, SUBLANES, lanes)
    lo, hi = x[:, 0], x[:, 1]
    if reverse:
        lo = lo * hi[:, 0:1, :]
    else:
        hi = hi * lo[:, SUBLANES - 1:SUBLANES, :]
    return jnp.concatenate([lo[:, None], hi[:, None]], axis=1).reshape(n, lanes)


def _gates(q, fl, lb):
    f = lb + (1.0 - lb) * _sigmoid(fl)
    k = 1.0 - f
    F = _block_cumprod(f)
    return q * F, k / F, F


class _HgrnChunk:
    def __init__(self, load, lb):
        self.load, self.lb = load, lb

    def prepare(self):
        q, fl, v = self.load()
        c = self.c = q.shape[0]
        nb = c // SPAN
        f = self.lb + (1.0 - self.lb) * _sigmoid(fl)
        k = 1.0 - f
        halves = lambda x: [jnp.concatenate([x[SPAN * j + BLK * w:SPAN * j + BLK * (w + 1)] for j in range(nb)], axis=0)
                            for w in (0, 1)]
        (f_a, f_b), (q_a, q_b), (k_a, k_b) = halves(f), halves(q), halves(k)
        fwd = _block_cumprod(f_b)
        rev_incl = _block_cumprod(f_a, reverse=True)
        row = lax.broadcasted_iota(jnp.int32, f_a.shape, 0) & (BLK - 1)
        rev = jnp.where(row < BLK - 1, pltpu.roll(rev_incl, f_a.shape[0] - 1, axis=0), 1.0)
        qm = (q_a / rev, q_b * fwd)
        km = (k_a * rev, k_b / fwd)
        span = lambda pair, j: jnp.concatenate([pair[0][BLK * j:BLK * (j + 1)], pair[1][BLK * j:BLK * (j + 1)]], axis=0)
        t_head = [rev_incl[BLK * j:BLK * j + 1, :] for j in range(nb)]
        t_tail = [fwd[BLK * j + BLK - 1:BLK * (j + 1), :] for j in range(nb)]
        T = [t_head[j] * t_tail[j] for j in range(nb)]
        qm_b = [span(qm, j) for j in range(nb)]
        km_b = [span(km, j) for j in range(nb)]
        qe_b = [qm_b[j] * t_head[j] for j in range(nb)]
        kd_b = [km_b[j] * t_tail[j] for j in range(nb)]
        zero = jnp.zeros((SPAN, HEAD), jnp.float32)
        k_pair = [kd_b[j] * t_head[j + 1] if j % 2 == 0 else zero for j in range(nb)]
        self.qe = _bf(jnp.concatenate(qm_b, axis=0))
        self.ke_kd = _bf(jnp.concatenate(km_b + k_pair, axis=0))

        assert nb == 4
        self.q_far = _bf(jnp.concatenate([zero, zero, qe_b[2], qe_b[3] * T[2]], axis=0))
        self.k_far = _bf(jnp.concatenate([kd_b[0] * T[1], kd_b[1], zero, zero], axis=0))

        e_in = [None] * nb
        dec = None
        for j in range(nb):
            e_in[j] = dec
            dec = T[j] if dec is None else dec * T[j]
        self.e_tot = dec
        d_out = [None] * nb
        dec = None
        for j in range(nb - 1, -1, -1):
            d_out[j] = dec
            dec = T[j] if dec is None else dec * T[j]
        self.q_in = _bf(jnp.concatenate(
            [qe_b[j] if e_in[j] is None else qe_b[j] * e_in[j] for j in range(nb)], axis=0))
        self.k_out = _bf(jnp.concatenate(
            [kd_b[j] if d_out[j] is None else kd_b[j] * d_out[j] for j in range(nb)], axis=0))
        self.v_bf = _bf(v)

    def issue(self):
        self.r1 = _dot_nt(self.qe, self.ke_kd)
        self.far = _dot_nt(self.q_far, self.k_far)
        self.upd = _dot_tn(self.v_bf, self.k_out)

    def combine(self, st, mask_d, mask_p):
        c = self.c
        p = jnp.where(mask_d, self.r1[:, :c], jnp.where(mask_p, self.r1[:, c:], 0.0)) + self.far
        self.o = _dot(_bf(p), self.v_bf) + _dot_nt(self.q_in, _bf(st))
        return st * self.e_tot + self.upd


def _head_out(o, gate, hg_norm):
    y = o * _rms_scale(o) * hg_norm
    return y * (gate * _sigmoid(gate))


def _post_norm(x, y_blocks, norm_post):
    ms = sum(jnp.sum(y * y, axis=-1, keepdims=True) for y in y_blocks) / D_MODEL
    r = lax.rsqrt(ms + EPS)
    return [x[:, j * COLS:(j + 1) * COLS] + y * r * norm_post[:, j * COLS:(j + 1) * COLS]
            for j, y in enumerate(y_blocks)]


def _finish(x, mix_bf, p_bf, w_out_ref, w_ple_ref, w_gate_ref, norm_post):
    nblk = D_MODEL // COLS
    x1 = _post_norm(x, [_dot(mix_bf, _wblk(w_out_ref, j)) for j in range(nblk)], norm_post)
    x1_bf = _bf(jnp.concatenate(x1, axis=1))
    out = [x1[j] + _sigmoid(_dot(x1_bf, _wblk(w_gate_ref, j))) * _dot(p_bf, _wblk(w_ple_ref, j)) for j in range(nblk)]
    return jnp.concatenate(out, axis=1)


def _chunk_masks(c):
    t = lax.broadcasted_iota(jnp.int32, (c, c), 0)
    s = lax.broadcasted_iota(jnp.int32, (c, c), 1)
    tb = t // SPAN
    sb = s // SPAN
    mask_d = (tb == sb) & (s <= t)
    mask_p = ((tb & 1) == 1) & (sb == tb - 1)
    return mask_d, mask_p


def _sub_step_head(stages, r0, xa_ref, p_ref, npre_ref, w_ple_ref):
    rows = slice(r0, r0 + SUB_TILE)
    ple, h = [], None
    if 3 in stages:
        p_bf = _bf(p_ref[rows, :])
        ple = [_dot(p_bf, _wblk(w_ple_ref, j)) for j in range(2)]
    if 1 in stages:
        xa = xa_ref[rows, :]
        h = _bf(xa * _rms_scale(xa) * npre_ref[...])
    return ple, h


def _sub_step(stages, first_pos, r0, xa_ref, xc_ref, p_ref, lbl_ref, w_in_ref, w_pool_ref, scale_ref, hgn_ref,
              w_out_ref, npre_ref, npost_ref, w_ple_ref, w_gate_ref, y_ref,
              u_w, z_w, mix_w, u_r, z_r, mix_r, carry, states, head=None, before_tail=None):
    tt = SUB_TILE
    rows = slice(r0, r0 + tt)
    states = list(states)

    def in_proj(h, c0):
        blk = _dot(h, _wblk(w_in_ref, c0 // COLS))
        if c0 == C_U:
            u_w[HDR:HDR + tt, :] = blk
        else:
            z_w[:, c0 - C_GP:c0 - C_GP + D_POOL] = blk

    if 2 not in stages:
        if 1 in stages:
            xa = xa_ref[rows, :]
            h = _bf(xa * _rms_scale(xa) * npre_ref[...])
            for c0 in (C_U, C_GP, C_Q, C_F, C_V, C_GH):
                in_proj(h, c0)
        if 3 in stages:
            y_ref[rows, :] = _finish(xc_ref[rows, :], mix_r[...], _bf(p_ref[rows, :]), w_out_ref, w_ple_ref,
                                     w_gate_ref, npost_ref[...])
        return carry, states

    lb = _lower_bound(lbl_ref[...])
    mask_d, mask_p = _chunk_masks(CHUNK)

    def pool_group(gi):
        cs = slice(gi * POOL_GROUP, (gi + 1) * POOL_GROUP)
        pooled = _pool_group(u_r[:, cs], POOL_WINDOWS[gi], first_pos)
        mixed = _dot(_bf(pooled), _bf(w_pool_ref[gi])) * scale_ref[:, cs]
        gp = z_r[:, cs]
        mix_w[:, cs] = _bf(mixed * (gp * _sigmoid(gp)))

    def hgrn_unit(hd, c0):
        hs = slice(hd * HEAD, (hd + 1) * HEAD)
        rs = slice(c0, c0 + CHUNK)
        col = lambda base: slice(base - C_GP + hd * HEAD, base - C_GP + (hd + 1) * HEAD)
        unit = _HgrnChunk(lambda: (z_r[rs, col(C_Q)], z_r[rs, col(C_F)], z_r[rs, col(C_V)]), lb[:, hs])
        unit.head, unit.rows, unit.gate_cols = hd, rs, col(C_GH)
        return unit

    def hgrn_store(unit):
        gh = z_r[unit.rows, unit.gate_cols]
        mix_w[unit.rows, D_POOL + unit.head * HEAD:D_POOL + (unit.head + 1) * HEAD] = _bf(
            _head_out(unit.o, gh, hgn_ref[...]))

    units = [hgrn_unit(hd, c0) for c0 in range(0, tt, CHUNK) for hd in range(N_HEADS)]
    dense = [("out", 1), ("in", C_U), ("in", C_GP), ("gate", 0), ("gate", 1), ("in", C_Q), ("in", C_F),
             ("in", C_V)]
    assert len(units) == len(dense)
    pool_at = {0: (0,), 2: (1,), 4: (2,), 7: (3,)}
    post_norm_at = 1
    final_at = {5: (0,), 6: (1,)}
    h = mix_prev = x1 = x1_bf = None
    y, gate, ple = {}, {}, []
    if head is None:
        head = _sub_step_head(stages, r0, xa_ref, p_ref, npre_ref, w_ple_ref)
    ple, h = head
    units[0].prepare()
    if 3 in stages:
        mix_prev = mix_r[...]
        y[0] = _dot(mix_prev, _wblk(w_out_ref, 0))
    u_r[0:HDR, :] = carry
    for k, unit in enumerate(units):
        unit.issue()
        if k + 1 < len(units):
            units[k + 1].prepare()
        kind, arg = dense[k]
        if kind == "in" and 1 in stages:
            in_proj(h, arg)
        elif kind == "out" and 3 in stages:
            y[arg] = _dot(mix_prev, _wblk(w_out_ref, arg))
        elif kind == "gate" and 3 in stages:
            gate[arg] = _dot(x1_bf, _wblk(w_gate_ref, arg))
        states[unit.head] = unit.combine(states[unit.head], mask_d, mask_p)
        if k > 0:
            hgrn_store(units[k - 1])
        if k == post_norm_at and 3 in stages:
            x1 = _post_norm(xc_ref[rows, :], [y[0], y[1]], npost_ref[...])
            x1_bf = _bf(jnp.concatenate(x1, axis=1))
        for gi in pool_at.get(k, ()):
            pool_group(gi)
        if 3 in stages:
            for j in final_at.get(k, ()):
                y_ref[rows, j * COLS:(j + 1) * COLS] = x1[j] + _sigmoid(gate[j]) * ple[j]
    if before_tail is not None:
        before_tail()
    if 1 in stages:
        in_proj(h, C_GH)
    hgrn_store(units[-1])
    return u_r[tt:tt + HDR, :], states


def _position_stages(n, n_tiles):
    stages = set()
    if n < n_tiles:
        stages.add(1)
    if 1 <= n <= n_tiles:
        stages.add(2)
    if POSITIONS <= n < n_tiles + POSITIONS:
        stages.add(3)
    return stages


def _unpack_params(params_ref):
    return dict(npre_ref=params_ref.at[0:1, :], npost_ref=params_ref.at[1:2, :],
                scale_ref=params_ref.at[2:3, 0:D_POOL], hgn_ref=params_ref.at[2:3, D_POOL:D_POOL + HEAD],
                lbl_ref=params_ref.at[3:5, 0:D_HG])


def _prompt_kernel(tiles_per_stream, n_tiles, xa_ref, xc_ref, p_ref, params_ref, w_in_ref, w_pool_ref, w_out_ref,
                   w_ple_ref, w_gate_ref,
                   y_ref, pool_ref, hg_ref, *scratch):
    uz = (scratch[0:2], scratch[2:4])
    mix = scratch[4:4 + POSITIONS]
    carry_ref, st_ref, carry_end, st_end = scratch[4 + POSITIONS:]
    s = pl.program_id(0)
    last = pl.num_programs(0) - 1
    small = _unpack_params(params_ref)
    lbl_ref, scale_ref, hgn_ref = small["lbl_ref"], small["scale_ref"], small["hgn_ref"]
    npre_ref, npost_ref = small["npre_ref"], small["npost_ref"]
    consts = (xa_ref, xc_ref, p_ref, lbl_ref, w_in_ref, w_pool_ref, scale_ref, hgn_ref, w_out_ref, npre_ref,
              npost_ref, w_ple_ref, w_gate_ref, y_ref)
    offsets = [lax.rem(POSITIONS * s + i - 1 + tiles_per_stream, tiles_per_stream) * SUB_TILE
               for i in range(POSITIONS)]
    new_stream = offsets[1] == 0

    def run(first_position):
        stage_sets = [_position_stages(first_position + i, n_tiles) for i in range(POSITIONS)]
        carry = carry_ref[...]
        states = [st_ref[hd] for hd in range(N_HEADS)]
        heads = {}
        for i, stages in enumerate(stage_sets):
            if not stages:
                continue
            r0 = i * SUB_TILE
            before_tail = None
            if i + 1 < POSITIONS and 2 in stages and 2 in stage_sets[i + 1]:
                def before_tail(i=i):
                    heads[i + 1] = _sub_step_head(stage_sets[i + 1], (i + 1) * SUB_TILE, xa_ref, p_ref, npre_ref,
                                                  w_ple_ref)
            if i == 1:
                carry = jnp.where(new_stream, 0.0, carry)
                states = [jnp.where(new_stream, 0.0, st) for st in states]
            u_w, z_w = uz[i % 2]
            u_r, z_r = uz[(i + 1) % 2]
            carry, states = _sub_step(stages, offsets[i], r0, *consts, u_w, z_w, mix[(i - 1) % POSITIONS],
                                      u_r, z_r, mix[i], carry, states, head=heads.get(i), before_tail=before_tail)
            if i == 0:
                carry_end[...] = carry
                for hd in range(N_HEADS):
                    st_end[hd] = states[hd]
        carry_ref[...] = carry
        for hd in range(N_HEADS):
            st_ref[hd] = states[hd]

    @pl.when(s == 0)
    def _():
        carry_ref[...] = jnp.zeros(carry_ref.shape, jnp.float32)
        st_ref[...] = jnp.zeros(st_ref.shape, jnp.float32)
        run(0)

    @pl.when((s > 0) & (s < last))
    def _():
        run(POSITIONS)

    @pl.when(s == last)
    def _():
        run(n_tiles)

    @pl.when(new_stream & (s > 0))
    def _():
        pool_ref[...] = carry_end[HDR - POOL_BUF:HDR, :]
        for hd in range(N_HEADS):
            hg_ref[hd] = st_end[hd].T


def _const_spec(shape):
    nd = len(shape)
    return pl.BlockSpec(shape, lambda *_: (0,) * nd, pipeline_mode=pl.Buffered(1))


def _prompt_call(x, p, params, w_in, w_pool, w_out, w_ple, w_gate):
    b, t, _ = x.shape
    tt = POSITIONS * SUB_TILE
    assert t % tt == 0 and SUB_TILE % CHUNK == 0 and POSITIONS % 2 == 0
    nt = t // SUB_TILE
    nblk = b * t // tt
    weights = (params, w_in, w_pool, w_out, w_ple, w_gate)
    x2 = x.reshape(b * t, D_MODEL)
    p2 = p.reshape(b * t, D_PLE)
    ahead = lambda s: (jnp.minimum(s, nblk - 1), 0)
    behind = lambda s: (jnp.clip(s - 1, 0, nblk - 1), 0)
    stream = lambda s: (jnp.clip((POSITIONS * s - 2) // nt, 0, b - 1),)
    y, pool, hg = pl.pallas_call(
        functools.partial(_prompt_kernel, nt, b * nt),
        grid=(nblk + 1,),
        in_specs=[pl.BlockSpec((tt, D_MODEL), ahead),
                  pl.BlockSpec((tt, D_MODEL), behind),
                  pl.BlockSpec((tt, D_PLE), behind)]
                 + [_const_spec(w.shape) for w in weights],
        out_specs=[pl.BlockSpec((tt, D_MODEL), behind),
                   pl.BlockSpec((None, POOL_BUF, D_POOL), lambda s: stream(s) + (0, 0)),
                   pl.BlockSpec((None, N_HEADS, HEAD, HEAD), lambda s: stream(s) + (0, 0, 0))],
        out_shape=[jax.ShapeDtypeStruct((b * t, D_MODEL), jnp.float32),
                   jax.ShapeDtypeStruct((b, POOL_BUF, D_POOL), jnp.float32),
                   jax.ShapeDtypeStruct((b, N_HEADS, HEAD, HEAD), jnp.float32)],
        scratch_shapes=[pltpu.VMEM((HDR + SUB_TILE, D_POOL), jnp.float32),
                        pltpu.VMEM((SUB_TILE, D_IN - D_POOL), jnp.float32)] * 2
                       + [pltpu.VMEM((SUB_TILE, D_MODEL), jnp.bfloat16)] * POSITIONS
                       + [pltpu.VMEM((HDR, D_POOL), jnp.float32),
                          pltpu.VMEM((N_HEADS, HEAD, HEAD), jnp.float32)] * 2,
        compiler_params=pltpu.CompilerParams(dimension_semantics=("arbitrary",),
                                             vmem_limit_bytes=VMEM_LIMIT_BYTES),
        name="prompt_layer",
    )(x2, x2, p2, *weights)
    return y.reshape(b, t, D_MODEL), pool, hg


STREAMS_PER_ITER = 8
SAMPLE_VMEM_LIMIT_BYTES = 58 * 1024 * 1024


def _weight_block_copies(w_hbm, stage, sems):
    return [pltpu.make_async_copy(w_hbm.at[:, pl.ds(j * COLS, COLS)], stage.at[j], sems.at[j])
            for j in range(stage.shape[0])]


def _sample_kernel(start_pos, x_ref, p_ref, cache_ref, s0_ref, lbl_ref, w_in_hbm, w_pool_ref, scale_ref,
                   hgn_ref, w_out_hbm, npre_ref, npost_ref, w_ple_hbm, w_gate_hbm,
                   y_ref, pool_ref, hg_ref, w_in_b, w_out_b, w_ple_b, w_gate_b, params_out,
                   z_ref, ext_ref, mix_ref, w_in_f, w_out_f, w_ple_f, w_gate_f, sem_in, sem_out, sem_ple, sem_gate):
    n_streams, ts = pool_ref.shape[1], pool_ref.shape[0] + 1
    ext_rows = HDR + ts

    in_copies = _weight_block_copies(w_in_hbm, w_in_f, sem_in)
    late_weights = ((_weight_block_copies(w_out_hbm, w_out_f, sem_out), w_out_f, w_out_b),
                    (_weight_block_copies(w_ple_hbm, w_ple_f, sem_ple), w_ple_f, w_ple_b),
                    (_weight_block_copies(w_gate_hbm, w_gate_f, sem_gate), w_gate_f, w_gate_b))
    for copy in in_copies + [c for copies, _, _ in late_weights for c in copies]:
        copy.start()

    params_out[...] = jnp.zeros(params_out.shape, jnp.float32)
    packed = _unpack_params(params_out)
    for name, src in (("npre_ref", npre_ref), ("npost_ref", npost_ref), ("scale_ref", scale_ref),
                      ("hgn_ref", hgn_ref), ("lbl_ref", lbl_ref)):
        packed[name][...] = src[...]

    ext_ref[...] = jnp.zeros(ext_ref.shape, jnp.float32)
    for r in range(POOL_BUF):
        for gi in range(len(POOL_WINDOWS)):
            ext_ref[gi, pl.ds(HDR - POOL_BUF + r, n_streams, stride=ext_rows), :] = (
                cache_ref[r, :, gi * POOL_GROUP:(gi + 1) * POOL_GROUP])

    x = x_ref[...]
    h = _bf(x * _rms_scale(x) * npre_ref[...])
    for j in range(D_IN // COLS):
        in_copies[j].wait()
        w_in_b[j] = _bf(w_in_f[j])
        z_ref[:, j * COLS:(j + 1) * COLS] = _dot(h, w_in_b[j])

    lb = _lower_bound(lbl_ref[...])
    causal = (lax.broadcasted_iota(jnp.int32, (ts, ts), 1) <= lax.broadcasted_iota(jnp.int32, (ts, ts), 0))
    w_pool = [_bf(w_pool_ref[gi]) for gi in range(len(POOL_WINDOWS))]

    def prepare(b):
        c = {"b": b, "rows": pl.ds(pl.multiple_of(b * ts, ts), ts)}
        rows = c["rows"]
        base = pl.multiple_of(b * ext_rows, ext_rows)
        c["pooled"] = []
        for gi, w in enumerate(POOL_WINDOWS):
            ext_ref[gi, pl.ds(base + HDR, ts), :] = z_ref[rows, C_U + gi * POOL_GROUP:C_U + (gi + 1) * POOL_GROUP]
            c["pooled"].append(_bf(_pool_group(ext_ref[gi, pl.ds(base, ext_rows), :], w, start_pos)))
        gated = []
        for hd in range(N_HEADS):
            q = z_ref[rows, C_Q + hd * HEAD:C_Q + (hd + 1) * HEAD]
            fl = z_ref[rows, C_F + hd * HEAD:C_F + (hd + 1) * HEAD]
            gated.append(_gates(q, fl, lb[:, hd * HEAD:(hd + 1) * HEAD]))
        decay_rows = [F[ts - 1:ts, :] for _, _, F in gated]
        c["decay_cols"] = jnp.concatenate(
            decay_rows + [jnp.zeros((8 - N_HEADS, HEAD), jnp.float32)], axis=0).T
        c["qe"] = [_bf(qe) for qe, _, _ in gated]
        c["ke"] = [_bf(ke) for _, ke, _ in gated]
        c["kd"] = [_bf(ke * decay_rows[hd]) for hd, (_, ke, _) in enumerate(gated)]
        c["v"] = [_bf(z_ref[rows, C_V + hd * HEAD:C_V + (hd + 1) * HEAD]) for hd in range(N_HEADS)]
        return c

    def issue(c):
        c["mixed"] = [_dot(c["pooled"][gi], w_pool[gi]) for gi in range(len(POOL_WINDOWS))]
        c["att"] = [_dot_nt(c["qe"][hd], c["ke"][hd]) for hd in range(N_HEADS)]
        c["upd"] = [_dot_tn(c["kd"][hd], c["v"][hd]) for hd in range(N_HEADS)]

    def combine(c):
        b, rows = c["b"], c["rows"]
        for gi in range(len(POOL_WINDOWS)):
            cs = slice(gi * POOL_GROUP, (gi + 1) * POOL_GROUP)
            gp = z_ref[rows, C_GP + gi * POOL_GROUP:C_GP + (gi + 1) * POOL_GROUP]
            mix_ref[rows, cs] = _bf(c["mixed"][gi] * scale_ref[:, cs] * (gp * _sigmoid(gp)))
        c["o"] = []
        for hd in range(N_HEADS):
            s0 = s0_ref[b, hd]
            att = _bf(jnp.where(causal, c["att"][hd], 0.0))
            c["o"].append(_dot(att, c["v"][hd]) + _dot(c["qe"][hd], _bf(s0)))
            hg_ref[b, hd] = s0 * c["decay_cols"][:, hd:hd + 1] + c["upd"][hd]

    def store(c):
        rows = c["rows"]
        for hd in range(N_HEADS):
            gh = z_ref[rows, C_GH + hd * HEAD:C_GH + (hd + 1) * HEAD]
            mix_ref[rows, D_POOL + hd * HEAD:D_POOL + (hd + 1) * HEAD] = _bf(
                _head_out(c["o"][hd], gh, hgn_ref[...]))

    def stream_group(g, carry):
        ctx = [prepare(g * STREAMS_PER_ITER + i) for i in range(STREAMS_PER_ITER)]
        for phase in (issue, combine, store):
            for c in ctx:
                phase(c)
        return carry

    lax.fori_loop(0, n_streams // STREAMS_PER_ITER, stream_group, 0)

    for r in range(POOL_BUF):
        for gi in range(len(POOL_WINDOWS)):
            pool_ref[r, :, gi * POOL_GROUP:(gi + 1) * POOL_GROUP] = (
                ext_ref[gi, pl.ds(ext_rows - POOL_BUF + r, n_streams, stride=ext_rows), :])

    for copies, stage, dst in late_weights:
        for j, copy in enumerate(copies):
            copy.wait()
            dst[j] = _bf(stage[j])
    y_ref[...] = _finish(x, mix_ref[...], _bf(p_ref[...]), w_out_b, w_ple_b, w_gate_b, npost_ref[...])


def _sample_call(start_pos, x, p, cache, s0, lbl, w_in, w_pool, scale, hgn, w_out, npre, npost, w_ple, w_gate):
    b, ts, _ = x.shape
    assert ts == BLK and ts == POOL_BUF + 1 and b % STREAMS_PER_ITER == 0
    n = b * ts
    block_shape = lambda w: (w.shape[1] // COLS, w.shape[0], COLS)
    blocks = lambda w: jax.ShapeDtypeStruct(block_shape(w), jnp.bfloat16)
    dense = (w_in, w_out, w_ple, w_gate)
    vmem, hbm = pl.BlockSpec(memory_space=pltpu.VMEM), pl.BlockSpec(memory_space=pl.ANY)
    y, pool, hg, w_in_b, w_out_b, w_ple_b, w_gate_b, params = pl.pallas_call(
        functools.partial(_sample_kernel, start_pos),
        in_specs=[vmem, vmem, vmem, vmem, vmem, hbm, vmem, vmem, vmem, hbm, vmem, vmem, hbm, hbm],
        out_shape=[jax.ShapeDtypeStruct((n, D_MODEL), jnp.float32),
                   jax.ShapeDtypeStruct((POOL_BUF, b, D_POOL), jnp.float32),
                   jax.ShapeDtypeStruct((b, N_HEADS, HEAD, HEAD), jnp.float32),
                   blocks(w_in), blocks(w_out), blocks(w_ple), blocks(w_gate),
                   jax.ShapeDtypeStruct((SUBLANES, D_MODEL), jnp.float32)],
        scratch_shapes=[pltpu.VMEM((n, D_IN), jnp.float32),
                        pltpu.VMEM((len(POOL_WINDOWS), b * (HDR + ts), POOL_GROUP), jnp.float32),
                        pltpu.VMEM((n, D_MODEL), jnp.bfloat16)]
                       + [pltpu.VMEM(block_shape(w), jnp.float32) for w in dense]
                       + [pltpu.SemaphoreType.DMA((block_shape(w)[0],)) for w in dense],
        compiler_params=pltpu.CompilerParams(vmem_limit_bytes=SAMPLE_VMEM_LIMIT_BYTES),
        name="sample_layer",
    )(x.reshape(n, D_MODEL), p.reshape(n, D_PLE), jnp.transpose(cache, (1, 0, 2)), s0, lbl, w_in, w_pool, scale,
      hgn, w_out, npre, npost, w_ple, w_gate)
    return (y.reshape(b, ts, D_MODEL), jnp.transpose(pool, (1, 0, 2)), hg), (params, w_in_b, w_out_b, w_ple_b, w_gate_b)


def kernel(x_prompt, x_sample, cache_pool, state_hgrn, p_prompt, p_sample, lb_logits, w_in, w_pool, pool_scale,
           hg_norm, w_out, norm_pre, norm_post, w_ple, w_ple_gate):
    depth = w_in.shape[0]
    assert depth == 1 and lb_logits.shape[0] == 2
    past_len = 1024
    (y_s, pool_s, hg_s), (params, w_in_b, w_out_b, w_ple_b, w_gate_b) = _sample_call(
        past_len, x_sample, p_sample[0], cache_pool[0], state_hgrn[0], lb_logits, w_in[0], w_pool[0], pool_scale,
        hg_norm, w_out[0], norm_pre, norm_post, w_ple[0], w_ple_gate[0])
    y_p, pool_p, hg_p = _prompt_call(x_prompt, p_prompt[0], params, w_in_b, w_pool[0], w_out_b, w_ple_b, w_gate_b)
    return (y_p, y_s, pool_p[None], hg_p[None], pool_s[None], hg_s[None])
```

```python
import functools

import jax
import jax.numpy as jnp
from jax import lax
from jax.experimental import pallas as pl
from jax.experimental.pallas import tpu as pltpu

D_MODEL = 1024
D_POOL = 512
POOL_WINDOWS = (2, 4, 8, 16)
POOL_GROUP = 128
POOL_BUF = 15
N_HEADS = 4
HEAD = 128
D_HG = N_HEADS * HEAD
D_IN = 2 * D_POOL + 4 * D_HG
D_PLE = 256
EPS = 1e-6
LOG2_E = 1.4426950408889634
COLS = 512
SPAN = 32
BLK = 16
CHUNK = 128
SUBLANES = 8
HDR = 16
SUB_TILE = 256
POSITIONS = 2
VMEM_LIMIT_BYTES = 60 * 1024 * 1024

C_U, C_GP, C_Q, C_F, C_V, C_GH = 0, 512, 1024, 1536, 2048, 2560

_NT = (((1,), (1,)), ((), ()))
_TN = (((0,), (0,)), ((), ()))


def _dot(a, b):
    return jnp.dot(a, b, preferred_element_type=jnp.float32)


def _dot_nt(a, b):
    return lax.dot_general(a, b, _NT, preferred_element_type=jnp.float32)


def _dot_tn(a, b):
    return lax.dot_general(a, b, _TN, preferred_element_type=jnp.float32)


def _bf(x):
    return x.astype(jnp.bfloat16)


def _wblk(w_ref, j):
    return w_ref[j]


def _sigmoid(x):
    return 1.0 / (1.0 + jnp.exp2(x * (-LOG2_E)))


def _rms_scale(x):
    return lax.rsqrt(jnp.mean(x * x, axis=-1, keepdims=True) + EPS)


def _lower_bound(lb_logits):
    l0 = lb_logits[0:1, :]
    l1 = lb_logits[1:2, :]
    m = jnp.maximum(l0, l1)
    e0 = jnp.exp(l0 - m)
    e1 = jnp.exp(l1 - m)
    return e0 / (e0 + e1)


def _window_sum(e, w):
    s = e
    d = 1
    while d < w:
        s = s + pltpu.roll(s, d, axis=0)
        d *= 2
    return s


def _pool_group(e, w, first_pos):
    s = _window_sum(e, w)[HDR:]
    u = e[HDR:]
    rows = lax.broadcasted_iota(jnp.int32, (HDR, POOL_GROUP), 0)
    cnt = jnp.minimum(w, first_pos + rows + 1).astype(jnp.float32)
    head = s[:HDR] / cnt - u[:HDR]
    if s.shape[0] == HDR:
        return head
    tail = s[HDR:] * (1.0 / w) - u[HDR:]
    return jnp.concatenate([head, tail], axis=0)


def _block_cumprod(f, reverse=False):
    n, lanes = f.shape
    x = f.reshape(n // SUBLANES, SUBLANES, lanes)
    row = lax.broadcasted_iota(jnp.int32, x.shape, 1)
    for d in (1, 2, 4):
        if reverse:
            x = x * jnp.where(row + d < SUBLANES, pltpu.roll(x, SUBLANES - d, axis=1), 1.0)
        else:
            x = x * jnp.where(row >= d, pltpu.roll(x, d, axis=1), 1.0)
    x = x.reshape(n // BLK, BLK // SUBLANES, SUBLANES, lanes)
    lo, hi = x[:, 0], x[:, 1]
    if reverse:
        lo = lo * hi[:, 0:1, :]
    else:
        hi = hi * lo[:, SUBLANES - 1:SUBLANES, :]
    return jnp.concatenate([lo[:, None], hi[:, None]], axis=1).reshape(n, lanes)


def _gates(q, fl, lb):
    f = lb + (1.0 - lb) * _sigmoid(fl)
    k = 1.0 - f
    F = _block_cumprod(f)
    return q * F, k / F, F


class _HgrnChunk:
    def __init__(self, load, lb):
        self.load, self.lb = load, lb

    def prepare(self):
        q, fl, v = self.load()
        c = self.c = q.shape[0]
        nb = c // SPAN
        f = self.lb + (1.0 - self.lb) * _sigmoid(fl)
        k = 1.0 - f
        halves = lambda x: [jnp.concatenate([x[SPAN * j + BLK * w:SPAN * j + BLK * (w + 1)] for j in range(nb)], axis=0)
                            for w in (0, 1)]
        (f_a, f_b), (q_a, q_b), (k_a, k_b) = halves(f), halves(q), halves(k)
        fwd = _block_cumprod(f_b)
        rev_incl = _block_cumprod(f_a, reverse=True)
        row = lax.broadcasted_iota(jnp.int32, f_a.shape, 0) & (BLK - 1)
        rev = jnp.where(row < BLK - 1, pltpu.roll(rev_incl, f_a.shape[0] - 1, axis=0), 1.0)
        qm = (q_a / rev, q_b * fwd)
        km = (k_a * rev, k_b / fwd)
        span = lambda pair, j: jnp.concatenate([pair[0][BLK * j:BLK * (j + 1)], pair[1][BLK * j:BLK * (j + 1)]], axis=0)
        t_head = [rev_incl[BLK * j:BLK * j + 1, :] for j in range(nb)]
        t_tail = [fwd[BLK * j + BLK - 1:BLK * (j + 1), :] for j in range(nb)]
        T = [t_head[j] * t_tail[j] for j in range(nb)]
        qm_b = [span(qm, j) for j in range(nb)]
        km_b = [span(km, j) for j in range(nb)]
        qe_b = [qm_b[j] * t_head[j] for j in range(nb)]
        kd_b = [km_b[j] * t_tail[j] for j in range(nb)]
        zero = jnp.zeros((SPAN, HEAD), jnp.float32)
        k_pair = [kd_b[j] * t_head[j + 1] if j % 2 == 0 else zero for j in range(nb)]
        self.qe = _bf(jnp.concatenate(qm_b, axis=0))
        self.ke_kd = _bf(jnp.concatenate(km_b + k_pair, axis=0))

        assert nb == 4
        self.q_far = _bf(jnp.concatenate([zero, zero, qe_b[2], qe_b[3] * T[2]], axis=0))
        self.k_far = _bf(jnp.concatenate([kd_b[0] * T[1], kd_b[1], zero, zero], axis=0))

        e_in = [None] * nb
        dec = None
        for j in range(nb):
            e_in[j] = dec
            dec = T[j] if dec is None else dec * T[j]
        self.e_tot = dec
        d_out = [None] * nb
        dec = None
        for j in range(nb - 1, -1, -1):
            d_out[j] = dec
            dec = T[j] if dec is None else dec * T[j]
        self.q_in = _bf(jnp.concatenate(
            [qe_b[j] if e_in[j] is None else qe_b[j] * e_in[j] for j in range(nb)], axis=0))
        self.k_out = _bf(jnp.concatenate(
            [kd_b[j] if d_out[j] is None else kd_b[j] * d_out[j] for j in range(nb)], axis=0))
        self.v_bf = _bf(v)

    def issue(self):
        self.r1 = _dot_nt(self.qe, self.ke_kd)
        self.far = _dot_nt(self.q_far, self.k_far)
        self.upd = _dot_tn(self.v_bf, self.k_out)

    def combine(self, st, mask_d, mask_p):
        c = self.c
        p = jnp.where(mask_d, self.r1[:, :c], jnp.where(mask_p, self.r1[:, c:], 0.0)) + self.far
        self.o = _dot(_bf(p), self.v_bf) + _dot_nt(self.q_in, _bf(st))
        return st * self.e_tot + self.upd


def _head_out(o, gate, hg_norm):
    y = o * _rms_scale(o) * hg_norm
    return y * (gate * _sigmoid(gate))


def _post_norm(x, y_blocks, norm_post):
    ms = sum(jnp.sum(y * y, axis=-1, keepdims=True) for y in y_blocks) / D_MODEL
    r = lax.rsqrt(ms + EPS)
    return [x[:, j * COLS:(j + 1) * COLS] + y * r * norm_post[:, j * COLS:(j + 1) * COLS]
            for j, y in enumerate(y_blocks)]


def _finish(x, mix_bf, p_bf, w_out_ref, w_ple_ref, w_gate_ref, norm_post):
    nblk = D_MODEL // COLS
    x1 = _post_norm(x, [_dot(mix_bf, _wblk(w_out_ref, j)) for j in range(nblk)], norm_post)
    x1_bf = _bf(jnp.concatenate(x1, axis=1))
    out = [x1[j] + _sigmoid(_dot(x1_bf, _wblk(w_gate_ref, j))) * _dot(p_bf, _wblk(w_ple_ref, j)) for j in range(nblk)]
    return jnp.concatenate(out, axis=1)


def _chunk_masks(c):
    t = lax.broadcasted_iota(jnp.int32, (c, c), 0)
    s = lax.broadcasted_iota(jnp.int32, (c, c), 1)
    tb = t // SPAN
    sb = s // SPAN
    mask_d = (tb == sb) & (s <= t)
    mask_p = ((tb & 1) == 1) & (sb == tb - 1)
    return mask_d, mask_p


def _sub_step_head(stages, r0, xa_ref, p_ref, npre_ref, w_ple_ref):
    rows = slice(r0, r0 + SUB_TILE)
    ple, h = [], None
    if 3 in stages:
        p_bf = _bf(p_ref[rows, :])
        ple = [_dot(p_bf, _wblk(w_ple_ref, j)) for j in range(2)]
    if 1 in stages:
        xa = xa_ref[rows, :]
        h = _bf(xa * _rms_scale(xa) * npre_ref[...])
    return ple, h


def _sub_step(stages, first_pos, r0, xa_ref, xc_ref, p_ref, lbl_ref, w_in_ref, w_pool_ref, scale_ref, hgn_ref,
              w_out_ref, npre_ref, npost_ref, w_ple_ref, w_gate_ref, y_ref,
              u_w, z_w, mix_w, u_r, z_r, mix_r, carry, states, head=None, before_tail=None):
    tt = SUB_TILE
    rows = slice(r0, r0 + tt)
    states = list(states)

    def in_proj(h, c0):
        blk = _dot(h, _wblk(w_in_ref, c0 // COLS))
        if c0 == C_U:
            u_w[HDR:HDR + tt, :] = blk
        else:
            z_w[:, c0 - C_GP:c0 - C_GP + D_POOL] = blk

    if 2 not in stages:
        if 1 in stages:
            xa = xa_ref[rows, :]
            h = _bf(xa * _rms_scale(xa) * npre_ref[...])
            for c0 in (C_U, C_GP, C_Q, C_F, C_V, C_GH):
                in_proj(h, c0)
        if 3 in stages:
            y_ref[rows, :] = _finish(xc_ref[rows, :], mix_r[...], _bf(p_ref[rows, :]), w_out_ref, w_ple_ref,
                                     w_gate_ref, npost_ref[...])
        return carry, states

    lb = _lower_bound(lbl_ref[...])
    mask_d, mask_p = _chunk_masks(CHUNK)

    def pool_group(gi):
        cs = slice(gi * POOL_GROUP, (gi + 1) * POOL_GROUP)
        pooled = _pool_group(u_r[:, cs], POOL_WINDOWS[gi], first_pos)
        mixed = _dot(_bf(pooled), _bf(w_pool_ref[gi])) * scale_ref[:, cs]
        gp = z_r[:, cs]
        mix_w[:, cs] = _bf(mixed * (gp * _sigmoid(gp)))

    def hgrn_unit(hd, c0):
        hs = slice(hd * HEAD, (hd + 1) * HEAD)
        rs = slice(c0, c0 + CHUNK)
        col = lambda base: slice(base - C_GP + hd * HEAD, base - C_GP + (hd + 1) * HEAD)
        unit = _HgrnChunk(lambda: (z_r[rs, col(C_Q)], z_r[rs, col(C_F)], z_r[rs, col(C_V)]), lb[:, hs])
        unit.head, unit.rows, unit.gate_cols = hd, rs, col(C_GH)
        return unit

    def hgrn_store(unit):
        gh = z_r[unit.rows, unit.gate_cols]
        mix_w[unit.rows, D_POOL + unit.head * HEAD:D_POOL + (unit.head + 1) * HEAD] = _bf(
            _head_out(unit.o, gh, hgn_ref[...]))

    units = [hgrn_unit(hd, c0) for c0 in range(0, tt, CHUNK) for hd in range(N_HEADS)]
    dense = [("out", 1), ("in", C_U), ("in", C_GP), ("gate", 0), ("gate", 1), ("in", C_Q), ("in", C_F),
             ("in", C_V)]
    assert len(units) == len(dense)
    pool_at = {0: (0,), 2: (1,), 4: (2,), 7: (3,)}
    post_norm_at = 1
    final_at = {5: (0,), 6: (1,)}
    h = mix_prev = x1 = x1_bf = None
    y, gate, ple = {}, {}, []
    if head is None:
        head = _sub_step_head(stages, r0, xa_ref, p_ref, npre_ref, w_ple_ref)
    ple, h = head
    units[0].prepare()
    if 3 in stages:
        mix_prev = mix_r[...]
        y[0] = _dot(mix_prev, _wblk(w_out_ref, 0))
    u_r[0:HDR, :] = carry
    for k, unit in enumerate(units):
        unit.issue()
        if k + 1 < len(units):
            units[k + 1].prepare()
        kind, arg = dense[k]
        if kind == "in" and 1 in stages:
            in_proj(h, arg)
        elif kind == "out" and 3 in stages:
            y[arg] = _dot(mix_prev, _wblk(w_out_ref, arg))
        elif kind == "gate" and 3 in stages:
            gate[arg] = _dot(x1_bf, _wblk(w_gate_ref, arg))
        states[unit.head] = unit.combine(states[unit.head], mask_d, mask_p)
        if k > 0:
            hgrn_store(units[k - 1])
        if k == post_norm_at and 3 in stages:
            x1 = _post_norm(xc_ref[rows, :], [y[0], y[1]], npost_ref[...])
            x1_bf = _bf(jnp.concatenate(x1, axis=1))
        for gi in pool_at.get(k, ()):
            pool_group(gi)
        if 3 in stages:
            for j in final_at.get(k, ()):
                y_ref[rows, j * COLS:(j + 1) * COLS] = x1[j] + _sigmoid(gate[j]) * ple[j]
    if before_tail is not None:
        before_tail()
    if 1 in stages:
        in_proj(h, C_GH)
    hgrn_store(units[-1])
    return u_r[tt:tt + HDR, :], states


def _position_stages(n, n_tiles):
    stages = set()
    if n < n_tiles:
        stages.add(1)
    if 1 <= n <= n_tiles:
        stages.add(2)
    if POSITIONS <= n < n_tiles + POSITIONS:
        stages.add(3)
    return stages


def _unpack_params(params_ref):
    return dict(npre_ref=params_ref.at[0:1, :], npost_ref=params_ref.at[1:2, :],
                scale_ref=params_ref.at[2:3, 0:D_POOL], hgn_ref=params_ref.at[2:3, D_POOL:D_POOL + HEAD],
                lbl_ref=params_ref.at[3:5, 0:D_HG])


def _prompt_kernel(tiles_per_stream, n_tiles, xa_ref, xc_ref, p_ref, params_ref, w_in_ref, w_pool_ref, w_out_ref,
                   w_ple_ref, w_gate_ref,
                   y_ref, pool_ref, hg_ref, *scratch):
    uz = (scratch[0:2], scratch[2:4])
    mix = scratch[4:4 + POSITIONS]
    carry_ref, st_ref, carry_end, st_end = scratch[4 + POSITIONS:]
    s = pl.program_id(0)
    last = pl.num_programs(0) - 1
    small = _unpack_params(params_ref)
    lbl_ref, scale_ref, hgn_ref = small["lbl_ref"], small["scale_ref"], small["hgn_ref"]
    npre_ref, npost_ref = small["npre_ref"], small["npost_ref"]
    consts = (xa_ref, xc_ref, p_ref, lbl_ref, w_in_ref, w_pool_ref, scale_ref, hgn_ref, w_out_ref, npre_ref,
              npost_ref, w_ple_ref, w_gate_ref, y_ref)
    offsets = [lax.rem(POSITIONS * s + i - 1 + tiles_per_stream, tiles_per_stream) * SUB_TILE
               for i in range(POSITIONS)]
    new_stream = offsets[1] == 0

    def run(first_position):
        stage_sets = [_position_stages(first_position + i, n_tiles) for i in range(POSITIONS)]
        carry = carry_ref[...]
        states = [st_ref[hd] for hd in range(N_HEADS)]
        heads = {}
        for i, stages in enumerate(stage_sets):
            if not stages:
                continue
            r0 = i * SUB_TILE
            before_tail = None
            if i + 1 < POSITIONS and 2 in stages and 2 in stage_sets[i + 1]:
                def before_tail(i=i):
                    heads[i + 1] = _sub_step_head(stage_sets[i + 1], (i + 1) * SUB_TILE, xa_ref, p_ref, npre_ref,
                                                  w_ple_ref)
            if i == 1:
                carry = jnp.where(new_stream, 0.0, carry)
                states = [jnp.where(new_stream, 0.0, st) for st in states]
            u_w, z_w = uz[i % 2]
            u_r, z_r = uz[(i + 1) % 2]
            carry, states = _sub_step(stages, offsets[i], r0, *consts, u_w, z_w, mix[(i - 1) % POSITIONS],
                                      u_r, z_r, mix[i], carry, states, head=heads.get(i), before_tail=before_tail)
            if i == 0:
                carry_end[...] = carry
                for hd in range(N_HEADS):
                    st_end[hd] = states[hd]
        carry_ref[...] = carry
        for hd in range(N_HEADS):
            st_ref[hd] = states[hd]

    @pl.when(s == 0)
    def _():
        carry_ref[...] = jnp.zeros(carry_ref.shape, jnp.float32)
        st_ref[...] = jnp.zeros(st_ref.shape, jnp.float32)
        run(0)

    @pl.when((s > 0) & (s < last))
    def _():
        run(POSITIONS)

    @pl.when(s == last)
    def _():
        run(n_tiles)

    @pl.when(new_stream & (s > 0))
    def _():
        pool_ref[...] = carry_end[HDR - POOL_BUF:HDR, :]
        for hd in range(N_HEADS):
            hg_ref[hd] = st_end[hd].T


def _const_spec(shape):
    nd = len(shape)
    return pl.BlockSpec(shape, lambda *_: (0,) * nd, pipeline_mode=pl.Buffered(1))


def _prompt_call(x, p, params, w_in, w_pool, w_out, w_ple, w_gate):
    b, t, _ = x.shape
    tt = POSITIONS * SUB_TILE
    assert t % tt == 0 and SUB_TILE % CHUNK == 0 and POSITIONS % 2 == 0
    nt = t // SUB_TILE
    nblk = b * t // tt
    weights = (params, w_in, w_pool, w_out, w_ple, w_gate)
    x2 = x.reshape(b * t, D_MODEL)
    p2 = p.reshape(b * t, D_PLE)
    ahead = lambda s: (jnp.minimum(s, nblk - 1), 0)
    behind = lambda s: (jnp.clip(s - 1, 0, nblk - 1), 0)
    stream = lambda s: (jnp.clip((POSITIONS * s - 2) // nt, 0, b - 1),)
    y, pool, hg = pl.pallas_call(
        functools.partial(_prompt_kernel, nt, b * nt),
        grid=(nblk + 1,),
        in_specs=[pl.BlockSpec((tt, D_MODEL), ahead),
                  pl.BlockSpec((tt, D_MODEL), behind),
                  pl.BlockSpec((tt, D_PLE), behind)]
                 + [_const_spec(w.shape) for w in weights],
        out_specs=[pl.BlockSpec((tt, D_MODEL), behind),
                   pl.BlockSpec((None, POOL_BUF, D_POOL), lambda s: stream(s) + (0, 0)),
                   pl.BlockSpec((None, N_HEADS, HEAD, HEAD), lambda s: stream(s) + (0, 0, 0))],
        out_shape=[jax.ShapeDtypeStruct((b * t, D_MODEL), jnp.float32),
                   jax.ShapeDtypeStruct((b, POOL_BUF, D_POOL), jnp.float32),
                   jax.ShapeDtypeStruct((b, N_HEADS, HEAD, HEAD), jnp.float32)],
        scratch_shapes=[pltpu.VMEM((HDR + SUB_TILE, D_POOL), jnp.float32),
                        pltpu.VMEM((SUB_TILE, D_IN - D_POOL), jnp.float32)] * 2
                       + [pltpu.VMEM((SUB_TILE, D_MODEL), jnp.bfloat16)] * POSITIONS
                       + [pltpu.VMEM((HDR, D_POOL), jnp.float32),
                          pltpu.VMEM((N_HEADS, HEAD, HEAD), jnp.float32)] * 2,
        compiler_params=pltpu.CompilerParams(dimension_semantics=("arbitrary",),
                                             vmem_limit_bytes=VMEM_LIMIT_BYTES),
        name="prompt_layer",
    )(x2, x2, p2, *weights)
    return y.reshape(b, t, D_MODEL), pool, hg


STREAMS_PER_ITER = 8
SAMPLE_VMEM_LIMIT_BYTES = 58 * 1024 * 1024
FETCH_WINDOW = 2


def _weight_block_copies(w_hbm, stage, sems):
    return [pltpu.make_async_copy(w_hbm.at[:, pl.ds(j * COLS, COLS)], stage.at[j], sems.at[j])
            for j in range(stage.shape[0])]


def _write_back_copies(w_b, w_hbm, sems):
    return [pltpu.make_async_copy(w_b.at[j], w_hbm.at[j], sems.at[j]) for j in range(w_b.shape[0])]


def _stream_group_copy(src, dst, sems, g):
    streams = pl.ds(g * STREAMS_PER_ITER, STREAMS_PER_ITER)
    return pltpu.make_async_copy(src.at[streams], dst.at[streams], sems.at[g])


def _sample_kernel(start_pos, x_ref, p_ref, cache_ref, s0_hbm, lbl_ref, w_in_hbm, w_pool_ref, scale_ref,
                   hgn_ref, w_out_hbm, npre_ref, npost_ref, w_ple_hbm, w_gate_hbm,
                   y_ref, pool_ref, hg_hbm, w_in_o, w_out_o, w_ple_o, w_gate_o, params_out,
                   z_ref, ext_ref, mix_ref, s0_ref, hg_ref, w_in_f, w_out_f, w_ple_f, w_gate_f,
                   w_in_b, w_out_b, w_ple_b, w_gate_b, sem_s0, sem_hg, sem_in, sem_out, sem_ple, sem_gate,
                   sem_in_o, sem_out_o, sem_ple_o, sem_gate_o):
    n_streams, ts = pool_ref.shape[1], pool_ref.shape[0] + 1
    ext_rows = HDR + ts
    n_groups = n_streams // STREAMS_PER_ITER

    in_copies = _weight_block_copies(w_in_hbm, w_in_f, sem_in)
    in_write_back = _write_back_copies(w_in_b, w_in_o, sem_in_o)
    late_weights = (
        (_weight_block_copies(w_out_hbm, w_out_f, sem_out), w_out_f, w_out_b, _write_back_copies(w_out_b, w_out_o, sem_out_o)),
        (_weight_block_copies(w_ple_hbm, w_ple_f, sem_ple), w_ple_f, w_ple_b, _write_back_copies(w_ple_b, w_ple_o, sem_ple_o)),
        (_weight_block_copies(w_gate_hbm, w_gate_f, sem_gate), w_gate_f, w_gate_b,
         _write_back_copies(w_gate_b, w_gate_o, sem_gate_o)))
    fetches = (in_copies + [_stream_group_copy(s0_hbm, s0_ref, sem_s0, g) for g in range(n_groups)]
               + [c for copies, _, _, _ in late_weights for c in copies])
    for copy in fetches[:FETCH_WINDOW]:
        copy.start()

    params_out[...] = jnp.zeros(params_out.shape, jnp.float32)
    packed = _unpack_params(params_out)
    for name, src in (("npre_ref", npre_ref), ("npost_ref", npost_ref), ("scale_ref", scale_ref),
                      ("hgn_ref", hgn_ref), ("lbl_ref", lbl_ref)):
        packed[name][...] = src[...]

    ext_ref[...] = jnp.zeros(ext_ref.shape, jnp.float32)
    for r in range(POOL_BUF):
        for gi in range(len(POOL_WINDOWS)):
            ext_ref[gi, pl.ds(HDR - POOL_BUF + r, n_streams, stride=ext_rows), :] = (
                cache_ref[r, :, gi * POOL_GROUP:(gi + 1) * POOL_GROUP])

    x = x_ref[...]
    h = _bf(x * _rms_scale(x) * npre_ref[...])
    for j in range(D_IN // COLS):
        in_copies[j].wait()
        last = j == len(in_copies) - 1
        for copy in fetches[FETCH_WINDOW + j:] if last else fetches[FETCH_WINDOW + j:FETCH_WINDOW + j + 1]:
            copy.start()
        w_in_b[j] = _bf(w_in_f[j])
        in_write_back[j].start()
        z_ref[:, j * COLS:(j + 1) * COLS] = _dot(h, w_in_b[j])

    lb = _lower_bound(lbl_ref[...])
    causal = (lax.broadcasted_iota(jnp.int32, (ts, ts), 1) <= lax.broadcasted_iota(jnp.int32, (ts, ts), 0))
    w_pool = [_bf(w_pool_ref[gi]) for gi in range(len(POOL_WINDOWS))]

    def prepare(b):
        c = {"b": b, "rows": pl.ds(pl.multiple_of(b * ts, ts), ts)}
        rows = c["rows"]
        base = pl.multiple_of(b * ext_rows, ext_rows)
        c["pooled"] = []
        for gi, w in enumerate(POOL_WINDOWS):
            ext_ref[gi, pl.ds(base + HDR, ts), :] = z_ref[rows, C_U + gi * POOL_GROUP:C_U + (gi + 1) * POOL_GROUP]
            c["pooled"].append(_bf(_pool_group(ext_ref[gi, pl.ds(base, ext_rows), :], w, start_pos)))
        gated = []
        for hd in range(N_HEADS):
            q = z_ref[rows, C_Q + hd * HEAD:C_Q + (hd + 1) * HEAD]
            fl = z_ref[rows, C_F + hd * HEAD:C_F + (hd + 1) * HEAD]
            gated.append(_gates(q, fl, lb[:, hd * HEAD:(hd + 1) * HEAD]))
        decay_rows = [F[ts - 1:ts, :] for _, _, F in gated]
        c["decay_cols"] = jnp.concatenate(
            decay_rows + [jnp.zeros((8 - N_HEADS, HEAD), jnp.float32)], axis=0).T
        c["qe"] = [_bf(qe) for qe, _, _ in gated]
        c["ke"] = [_bf(ke) for _, ke, _ in gated]
        c["kd"] = [_bf(ke * decay_rows[hd]) for hd, (_, ke, _) in enumerate(gated)]
        c["v"] = [_bf(z_ref[rows, C_V + hd * HEAD:C_V + (hd + 1) * HEAD]) for hd in range(N_HEADS)]
        return c

    def issue(c):
        c["mixed"] = [_dot(c["pooled"][gi], w_pool[gi]) for gi in range(len(POOL_WINDOWS))]
        c["att"] = [_dot_nt(c["qe"][hd], c["ke"][hd]) for hd in range(N_HEADS)]
        c["upd"] = [_dot_tn(c["kd"][hd], c["v"][hd]) for hd in range(N_HEADS)]

    def combine(c):
        b, rows = c["b"], c["rows"]
        for gi in range(len(POOL_WINDOWS)):
            cs = slice(gi * POOL_GROUP, (gi + 1) * POOL_GROUP)
            gp = z_ref[rows, C_GP + gi * POOL_GROUP:C_GP + (gi + 1) * POOL_GROUP]
            mix_ref[rows, cs] = _bf(c["mixed"][gi] * scale_ref[:, cs] * (gp * _sigmoid(gp)))
        c["o"] = []
        for hd in range(N_HEADS):
            s0 = s0_ref[b, hd]
            att = _bf(jnp.where(causal, c["att"][hd], 0.0))
            c["o"].append(_dot(att, c["v"][hd]) + _dot(c["qe"][hd], _bf(s0)))
            hg_ref[b, hd] = s0 * c["decay_cols"][:, hd:hd + 1] + c["upd"][hd]

    def store(c):
        rows = c["rows"]
        for hd in range(N_HEADS):
            gh = z_ref[rows, C_GH + hd * HEAD:C_GH + (hd + 1) * HEAD]
            mix_ref[rows, D_POOL + hd * HEAD:D_POOL + (hd + 1) * HEAD] = _bf(
                _head_out(c["o"][hd], gh, hgn_ref[...]))

    def stream_group(g, carry):
        ctx = [prepare(g * STREAMS_PER_ITER + i) for i in range(STREAMS_PER_ITER)]
        for c in ctx:
            issue(c)
        _stream_group_copy(s0_hbm, s0_ref, sem_s0, g).wait()
        for phase in (combine, store):
            for c in ctx:
                phase(c)
        _stream_group_copy(hg_ref, hg_hbm, sem_hg, g).start()
        return carry

    lax.fori_loop(0, n_groups, stream_group, 0)

    for r in range(POOL_BUF):
        for gi in range(len(POOL_WINDOWS)):
            pool_ref[r, :, gi * POOL_GROUP:(gi + 1) * POOL_GROUP] = (
                ext_ref[gi, pl.ds(ext_rows - POOL_BUF + r, n_streams, stride=ext_rows), :])

    for copies, stage, dst, write_back in late_weights:
        for j, copy in enumerate(copies):
            copy.wait()
            dst[j] = _bf(stage[j])
            write_back[j].start()
    y_ref[...] = _finish(x, mix_ref[...], _bf(p_ref[...]), w_out_b, w_ple_b, w_gate_b, npost_ref[...])

    for copy in (in_write_back + [c for _, _, _, write_back in late_weights for c in write_back]
                 + [_stream_group_copy(hg_ref, hg_hbm, sem_hg, g) for g in range(n_groups)]):
        copy.wait()


def _sample_call(start_pos, x, p, cache, s0, lbl, w_in, w_pool, scale, hgn, w_out, npre, npost, w_ple, w_gate):
    b, ts, _ = x.shape
    assert ts == BLK and ts == POOL_BUF + 1 and b % STREAMS_PER_ITER == 0
    n = b * ts
    block_shape = lambda w: (w.shape[1] // COLS, w.shape[0], COLS)
    blocks = lambda w: jax.ShapeDtypeStruct(block_shape(w), jnp.bfloat16)
    dense = (w_in, w_out, w_ple, w_gate)
    vmem, hbm = pl.BlockSpec(memory_space=pltpu.VMEM), pl.BlockSpec(memory_space=pl.ANY)
    y, pool, hg, w_in_b, w_out_b, w_ple_b, w_gate_b, params = pl.pallas_call(
        functools.partial(_sample_kernel, start_pos),
        in_specs=[vmem, vmem, vmem, hbm, vmem, hbm, vmem, vmem, vmem, hbm, vmem, vmem, hbm, hbm],
        out_specs=[vmem, vmem, hbm, hbm, hbm, hbm, hbm, vmem],
        out_shape=[jax.ShapeDtypeStruct((n, D_MODEL), jnp.float32),
                   jax.ShapeDtypeStruct((POOL_BUF, b, D_POOL), jnp.float32),
                   jax.ShapeDtypeStruct(s0.shape, jnp.float32),
                   blocks(w_in), blocks(w_out), blocks(w_ple), blocks(w_gate),
                   jax.ShapeDtypeStruct((SUBLANES, D_MODEL), jnp.float32)],
        scratch_shapes=[pltpu.VMEM((n, D_IN), jnp.float32),
                        pltpu.VMEM((len(POOL_WINDOWS), b * (HDR + ts), POOL_GROUP), jnp.float32),
                        pltpu.VMEM((n, D_MODEL), jnp.bfloat16),
                        pltpu.VMEM(s0.shape, jnp.float32), pltpu.VMEM(s0.shape, jnp.float32)]
                       + [pltpu.VMEM(block_shape(w), jnp.float32) for w in dense]
                       + [pltpu.VMEM(block_shape(w), jnp.bfloat16) for w in dense]
                       + [pltpu.SemaphoreType.DMA((b // STREAMS_PER_ITER,))] * 2
                       + [pltpu.SemaphoreType.DMA((block_shape(w)[0],)) for w in dense] * 2,
        compiler_params=pltpu.CompilerParams(vmem_limit_bytes=SAMPLE_VMEM_LIMIT_BYTES),
        name="sample_layer",
    )(x.reshape(n, D_MODEL), p.reshape(n, D_PLE), jnp.transpose(cache, (1, 0, 2)), s0, lbl, w_in, w_pool, scale,
      hgn, w_out, npre, npost, w_ple, w_gate)
    return (y.reshape(b, ts, D_MODEL), jnp.transpose(pool, (1, 0, 2)), hg), (params, w_in_b, w_out_b, w_ple_b, w_gate_b)


def kernel(x_prompt, x_sample, cache_pool, state_hgrn, p_prompt, p_sample, lb_logits, w_in, w_pool, pool_scale,
           hg_norm, w_out, norm_pre, norm_post, w_ple, w_ple_gate):
    depth = w_in.shape[0]
    assert depth == 1 and lb_logits.shape[0] == 2
    past_len = 1024
    (y_s, pool_s, hg_s), (params, w_in_b, w_out_b, w_ple_b, w_gate_b) = _sample_call(
        past_len, x_sample, p_sample[0], cache_pool[0], state_hgrn[0], lb_logits, w_in[0], w_pool[0], pool_scale,
        hg_norm, w_out[0], norm_pre, norm_post, w_ple[0], w_ple_gate[0])
    y_p, pool_p, hg_p = _prompt_call(x_prompt, p_prompt[0], params, w_in_b, w_pool[0], w_out_b, w_ple_b, w_gate_b)
    return (y_p, y_s, pool_p[None], hg_p[None], pool_s[None], hg_s[None])
```

```python
import functools

import jax
import jax.numpy as jnp
from jax import lax
from jax.experimental import pallas as pl
from jax.experimental.pallas import tpu as pltpu

D_MODEL = 1024
D_POOL = 512
POOL_WINDOWS = (2, 4, 8, 16)
POOL_GROUP = 128
POOL_BUF = 15
N_HEADS = 4
HEAD = 128
D_HG = N_HEADS * HEAD
D_IN = 2 * D_POOL + 4 * D_HG
D_PLE = 256
EPS = 1e-6
LOG2_E = 1.4426950408889634
COLS = 512
SPAN = 32
BLK = 16
CHUNK = 128
SUBLANES = 8
HDR = 16
SUB_TILE = 256
POSITIONS = 2
VMEM_LIMIT_BYTES = 60 * 1024 * 1024

C_U, C_GP, C_Q, C_F, C_V, C_GH = 0, 512, 1024, 1536, 2048, 2560

_NT = (((1,), (1,)), ((), ()))
_TN = (((0,), (0,)), ((), ()))


def _dot(a, b):
    return jnp.dot(a, b, preferred_element_type=jnp.float32)


def _dot_nt(a, b):
    return lax.dot_general(a, b, _NT, preferred_element_type=jnp.float32)


def _dot_tn(a, b):
    return lax.dot_general(a, b, _TN, preferred_element_type=jnp.float32)


def _bf(x):
    return x.astype(jnp.bfloat16)


def _wblk(w_ref, j):
    return w_ref[j]


def _sigmoid(x):
    return 1.0 / (1.0 + jnp.exp2(x * (-LOG2_E)))


def _rms_scale(x):
    return lax.rsqrt(jnp.mean(x * x, axis=-1, keepdims=True) + EPS)


def _lower_bound(lb_logits):
    l0 = lb_logits[0:1, :]
    l1 = lb_logits[1:2, :]
    m = jnp.maximum(l0, l1)
    e0 = jnp.exp(l0 - m)
    e1 = jnp.exp(l1 - m)
    return e0 / (e0 + e1)


def _window_sum(e, w):
    s = e
    d = 1
    while d < w:
        s = s + pltpu.roll(s, d, axis=0)
        d *= 2
    return s


def _pool_group(e, w, first_pos):
    s = _window_sum(e, w)[HDR:]
    u = e[HDR:]
    rows = lax.broadcasted_iota(jnp.int32, (HDR, POOL_GROUP), 0)
    cnt = jnp.minimum(w, first_pos + rows + 1).astype(jnp.float32)
    head = s[:HDR] / cnt - u[:HDR]
    if s.shape[0] == HDR:
        return head
    tail = s[HDR:] * (1.0 / w) - u[HDR:]
    return jnp.concatenate([head, tail], axis=0)


def _block_cumprod(f, reverse=False):
    n, lanes = f.shape
    x = f.reshape(n // SUBLANES, SUBLANES, lanes)
    row = lax.broadcasted_iota(jnp.int32, x.shape, 1)
    for d in (1, 2, 4):
        if reverse:
            x = x * jnp.where(row + d < SUBLANES, pltpu.roll(x, SUBLANES - d, axis=1), 1.0)
        else:
            x = x * jnp.where(row >= d, pltpu.roll(x, d, axis=1), 1.0)
    x = x.reshape(n // BLK, BLK // SUBLANES, SUBLANES, lanes)
    lo, hi = x[:, 0], x[:, 1]
    if reverse:
        lo = lo * hi[:, 0:1, :]
    else:
        hi = hi * lo[:, SUBLANES - 1:SUBLANES, :]
    return jnp.concatenate([lo[:, None], hi[:, None]], axis=1).reshape(n, lanes)


def _gates(q, fl, lb):
    f = lb + (1.0 - lb) * _sigmoid(fl)
    k = 1.0 - f
    F = _block_cumprod(f)
    return q * F, k / F, F


class _HgrnChunk:
    def __init__(self, load, lb):
        self.load, self.lb = load, lb

    def prepare(self):
        q, fl, v = self.load()
        c = self.c = q.shape[0]
        nb = c // SPAN
        f = self.lb + (1.0 - self.lb) * _sigmoid(fl)
        k = 1.0 - f
        halves = lambda x: [jnp.concatenate([x[SPAN * j + BLK * w:SPAN * j + BLK * (w + 1)] for j in range(nb)], axis=0)
                            for w in (0, 1)]
        (f_a, f_b), (q_a, q_b), (k_a, k_b) = halves(f), halves(q), halves(k)
        fwd = _block_cumprod(f_b)
        rev_incl = _block_cumprod(f_a, reverse=True)
        row = lax.broadcasted_iota(jnp.int32, f_a.shape, 0) & (BLK - 1)
        rev = jnp.where(row < BLK - 1, pltpu.roll(rev_incl, f_a.shape[0] - 1, axis=0), 1.0)
        qm = (q_a / rev, q_b * fwd)
        km = (k_a * rev, k_b / fwd)
        span = lambda pair, j: jnp.concatenate([pair[0][BLK * j:BLK * (j + 1)], pair[1][BLK * j:BLK * (j + 1)]], axis=0)
        t_head = [rev_incl[BLK * j:BLK * j + 1, :] for j in range(nb)]
        t_tail = [fwd[BLK * j + BLK - 1:BLK * (j + 1), :] for j in range(nb)]
        T = [t_head[j] * t_tail[j] for j in range(nb)]
        qm_b = [span(qm, j) for j in range(nb)]
        km_b = [span(km, j) for j in range(nb)]
        qe_b = [qm_b[j] * t_head[j] for j in range(nb)]
        kd_b = [km_b[j] * t_tail[j] for j in range(nb)]
        zero = jnp.zeros((SPAN, HEAD), jnp.float32)
        k_pair = [kd_b[j] * t_head[j + 1] if j % 2 == 0 else zero for j in range(nb)]
        self.qe = _bf(jnp.concatenate(qm_b, axis=0))
        self.ke_kd = _bf(jnp.concatenate(km_b + k_pair, axis=0))

        assert nb == 4
        self.q_far = _bf(jnp.concatenate([zero, zero, qe_b[2], qe_b[3] * T[2]], axis=0))
        self.k_far = _bf(jnp.concatenate([kd_b[0] * T[1], kd_b[1], zero, zero], axis=0))

        e_in = [None] * nb
        dec = None
        for j in range(nb):
            e_in[j] = dec
            dec = T[j] if dec is None else dec * T[j]
        self.e_tot = dec
        d_out = [None] * nb
        dec = None
        for j in range(nb - 1, -1, -1):
            d_out[j] = dec
            dec = T[j] if dec is None else dec * T[j]
        self.q_in = _bf(jnp.concatenate(
            [qe_b[j] if e_in[j] is None else qe_b[j] * e_in[j] for j in range(nb)], axis=0))
        self.k_out = _bf(jnp.concatenate(
            [kd_b[j] if d_out[j] is None else kd_b[j] * d_out[j] for j in range(nb)], axis=0))
        self.v_bf = _bf(v)

    def issue(self):
        self.r1 = _dot_nt(self.qe, self.ke_kd)
        self.far = _dot_nt(self.q_far, self.k_far)
        self.upd = _dot_tn(self.v_bf, self.k_out)

    def combine(self, st, mask_d, mask_p):
        c = self.c
        p = jnp.where(mask_d, self.r1[:, :c], jnp.where(mask_p, self.r1[:, c:], 0.0)) + self.far
        self.o = _dot(_bf(p), self.v_bf) + _dot_nt(self.q_in, _bf(st))
        return st * self.e_tot + self.upd


def _head_out(o, gate, hg_norm):
    y = o * _rms_scale(o) * hg_norm
    return y * (gate * _sigmoid(gate))


def _post_norm(x, y_blocks, norm_post):
    ms = sum(jnp.sum(y * y, axis=-1, keepdims=True) for y in y_blocks) / D_MODEL
    r = lax.rsqrt(ms + EPS)
    return [x[:, j * COLS:(j + 1) * COLS] + y * r * norm_post[:, j * COLS:(j + 1) * COLS]
            for j, y in enumerate(y_blocks)]


def _finish(x, mix_bf, p_bf, w_out_ref, w_ple_ref, w_gate_ref, norm_post):
    nblk = D_MODEL // COLS
    x1 = _post_norm(x, [_dot(mix_bf, _wblk(w_out_ref, j)) for j in range(nblk)], norm_post)
    x1_bf = _bf(jnp.concatenate(x1, axis=1))
    out = [x1[j] + _sigmoid(_dot(x1_bf, _wblk(w_gate_ref, j))) * _dot(p_bf, _wblk(w_ple_ref, j)) for j in range(nblk)]
    return jnp.concatenate(out, axis=1)


def _chunk_masks(c):
    t = lax.broadcasted_iota(jnp.int32, (c, c), 0)
    s = lax.broadcasted_iota(jnp.int32, (c, c), 1)
    tb = t // SPAN
    sb = s // SPAN
    mask_d = (tb == sb) & (s <= t)
    mask_p = ((tb & 1) == 1) & (sb == tb - 1)
    return mask_d, mask_p


def _sub_step_head(stages, r0, xa_ref, p_ref, npre_ref, w_ple_ref):
    rows = slice(r0, r0 + SUB_TILE)
    ple, h = [], None
    if 3 in stages:
        p_bf = _bf(p_ref[rows, :])
        ple = [_dot(p_bf, _wblk(w_ple_ref, j)) for j in range(2)]
    if 1 in stages:
        xa = xa_ref[rows, :]
        h = _bf(xa * _rms_scale(xa) * npre_ref[...])
    return ple, h


def _sub_step(stages, first_pos, r0, xa_ref, xc_ref, p_ref, lbl_ref, w_in_ref, w_pool_ref, scale_ref, hgn_ref,
              w_out_ref, npre_ref, npost_ref, w_ple_ref, w_gate_ref, y_ref,
              u_w, z_w, mix_w, u_r, z_r, mix_r, carry, states, head=None, before_tail=None):
    tt = SUB_TILE
    rows = slice(r0, r0 + tt)
    states = list(states)

    def in_proj(h, c0):
        blk = _dot(h, _wblk(w_in_ref, c0 // COLS))
        if c0 == C_U:
            u_w[HDR:HDR + tt, :] = blk
        else:
            z_w[:, c0 - C_GP:c0 - C_GP + D_POOL] = blk

    if 2 not in stages:
        if 1 in stages:
            xa = xa_ref[rows, :]
            h = _bf(xa * _rms_scale(xa) * npre_ref[...])
            for c0 in (C_U, C_GP, C_Q, C_F, C_V, C_GH):
                in_proj(h, c0)
        if 3 in stages:
            y_ref[rows, :] = _finish(xc_ref[rows, :], mix_r[...], _bf(p_ref[rows, :]), w_out_ref, w_ple_ref,
                                     w_gate_ref, npost_ref[...])
        return carry, states

    lb = _lower_bound(lbl_ref[...])
    mask_d, mask_p = _chunk_masks(CHUNK)

    def pool_group(gi):
        cs = slice(gi * POOL_GROUP, (gi + 1) * POOL_GROUP)
        pooled = _pool_group(u_r[:, cs], POOL_WINDOWS[gi], first_pos)
        mixed = _dot(_bf(pooled), _bf(w_pool_ref[gi])) * scale_ref[:, cs]
        gp = z_r[:, cs]
        mix_w[:, cs] = _bf(mixed * (gp * _sigmoid(gp)))

    def hgrn_unit(hd, c0):
        hs = slice(hd * HEAD, (hd + 1) * HEAD)
        rs = slice(c0, c0 + CHUNK)
        col = lambda base: slice(base - C_GP + hd * HEAD, base - C_GP + (hd + 1) * HEAD)
        unit = _HgrnChunk(lambda: (z_r[rs, col(C_Q)], z_r[rs, col(C_F)], z_r[rs, col(C_V)]), lb[:, hs])
        unit.head, unit.rows, unit.gate_cols = hd, rs, col(C_GH)
        return unit

    def hgrn_store(unit):
        gh = z_r[unit.rows, unit.gate_cols]
        mix_w[unit.rows, D_POOL + unit.head * HEAD:D_POOL + (unit.head + 1) * HEAD] = _bf(
            _head_out(unit.o, gh, hgn_ref[...]))

    units = [hgrn_unit(hd, c0) for c0 in range(0, tt, CHUNK) for hd in range(N_HEADS)]
    dense = [("out", 1), ("in", C_U), ("in", C_GP), ("gate", 0), ("gate", 1), ("in", C_Q), ("in", C_F),
             ("in", C_V)]
    assert len(units) == len(dense)
    pool_at = {0: (0,), 2: (1,), 4: (2,), 7: (3,)}
    post_norm_at = 1
    final_at = {5: (0,), 6: (1,)}
    h = mix_prev = x1 = x1_bf = None
    y, gate, ple = {}, {}, []
    if head is None:
        head = _sub_step_head(stages, r0, xa_ref, p_ref, npre_ref, w_ple_ref)
    ple, h = head
    units[0].prepare()
    if 3 in stages:
        mix_prev = mix_r[...]
        y[0] = _dot(mix_prev, _wblk(w_out_ref, 0))
    u_r[0:HDR, :] = carry
    for k, unit in enumerate(units):
        unit.issue()
        if k + 1 < len(units):
            units[k + 1].prepare()
        kind, arg = dense[k]
        if kind == "in" and 1 in stages:
            in_proj(h, arg)
        elif kind == "out" and 3 in stages:
            y[arg] = _dot(mix_prev, _wblk(w_out_ref, arg))
        elif kind == "gate" and 3 in stages:
            gate[arg] = _dot(x1_bf, _wblk(w_gate_ref, arg))
        states[unit.head] = unit.combine(states[unit.head], mask_d, mask_p)
        if k > 0:
            hgrn_store(units[k - 1])
        if k == post_norm_at and 3 in stages:
            x1 = _post_norm(xc_ref[rows, :], [y[0], y[1]], npost_ref[...])
            x1_bf = _bf(jnp.concatenate(x1, axis=1))
        for gi in pool_at.get(k, ()):
            pool_group(gi)
        if 3 in stages:
            for j in final_at.get(k, ()):
                y_ref[rows, j * COLS:(j + 1) * COLS] = x1[j] + _sigmoid(gate[j]) * ple[j]
    if before_tail is not None:
        before_tail()
    if 1 in stages:
        in_proj(h, C_GH)
    hgrn_store(units[-1])
    return u_r[tt:tt + HDR, :], states


def _position_stages(n, n_tiles):
    stages = set()
    if n < n_tiles:
        stages.add(1)
    if 1 <= n <= n_tiles:
        stages.add(2)
    if POSITIONS <= n < n_tiles + POSITIONS:
        stages.add(3)
    return stages


def _unpack_params(params_ref):
    return dict(npre_ref=params_ref.at[0:1, :], npost_ref=params_ref.at[1:2, :],
                scale_ref=params_ref.at[2:3, 0:D_POOL], hgn_ref=params_ref.at[2:3, D_POOL:D_POOL + HEAD],
                lbl_ref=params_ref.at[3:5, 0:D_HG])


def _prompt_kernel(tiles_per_stream, n_tiles, xa_ref, xc_ref, p_ref, params_ref, w_in_ref, w_pool_ref, w_out_ref,
                   w_ple_ref, w_gate_ref,
                   y_ref, pool_ref, hg_ref, *scratch):
    uz = (scratch[0:2], scratch[2:4])
    mix = scratch[4:4 + POSITIONS]
    carry_ref, st_ref, carry_end, st_end = scratch[4 + POSITIONS:]
    s = pl.program_id(0)
    last = pl.num_programs(0) - 1
    small = _unpack_params(params_ref)
    lbl_ref, scale_ref, hgn_ref = small["lbl_ref"], small["scale_ref"], small["hgn_ref"]
    npre_ref, npost_ref = small["npre_ref"], small["npost_ref"]
    consts = (xa_ref, xc_ref, p_ref, lbl_ref, w_in_ref, w_pool_ref, scale_ref, hgn_ref, w_out_ref, npre_ref,
              npost_ref, w_ple_ref, w_gate_ref, y_ref)
    offsets = [lax.rem(POSITIONS * s + i - 1 + tiles_per_stream, tiles_per_stream) * SUB_TILE
               for i in range(POSITIONS)]
    new_stream = offsets[1] == 0

    def run(first_position):
        stage_sets = [_position_stages(first_position + i, n_tiles) for i in range(POSITIONS)]
        carry = carry_ref[...]
        states = [st_ref[hd] for hd in range(N_HEADS)]
        heads = {}
        for i, stages in enumerate(stage_sets):
            if not stages:
                continue
            r0 = i * SUB_TILE
            before_tail = None
            if i + 1 < POSITIONS and 2 in stages and 2 in stage_sets[i + 1]:
                def before_tail(i=i):
                    heads[i + 1] = _sub_step_head(stage_sets[i + 1], (i + 1) * SUB_TILE, xa_ref, p_ref, npre_ref,
                                                  w_ple_ref)
            if i == 1:
                carry = jnp.where(new_stream, 0.0, carry)
                states = [jnp.where(new_stream, 0.0, st) for st in states]
            u_w, z_w = uz[i % 2]
            u_r, z_r = uz[(i + 1) % 2]
            carry, states = _sub_step(stages, offsets[i], r0, *consts, u_w, z_w, mix[(i - 1) % POSITIONS],
                                      u_r, z_r, mix[i], carry, states, head=heads.get(i), before_tail=before_tail)
            if i == 0:
                carry_end[...] = carry
                for hd in range(N_HEADS):
                    st_end[hd] = states[hd]
        carry_ref[...] = carry
        for hd in range(N_HEADS):
            st_ref[hd] = states[hd]

    @pl.when(s == 0)
    def _():
        carry_ref[...] = jnp.zeros(carry_ref.shape, jnp.float32)
        st_ref[...] = jnp.zeros(st_ref.shape, jnp.float32)
        run(0)

    @pl.when((s > 0) & (s < last))
    def _():
        run(POSITIONS)

    @pl.when(s == last)
    def _():
        run(n_tiles)

    @pl.when(new_stream & (s > 0))
    def _():
        pool_ref[...] = carry_end[HDR - POOL_BUF:HDR, :]
        for hd in range(N_HEADS):
            hg_ref[hd] = st_end[hd].T


def _const_spec(shape):
    nd = len(shape)
    return pl.BlockSpec(shape, lambda *_: (0,) * nd, pipeline_mode=pl.Buffered(1))


def _prompt_call(x, p, params, w_in, w_pool, w_out, w_ple, w_gate):
    b, t, _ = x.shape
    tt = POSITIONS * SUB_TILE
    assert t % tt == 0 and SUB_TILE % CHUNK == 0 and POSITIONS % 2 == 0
    nt = t // SUB_TILE
    nblk = b * t // tt
    weights = (params, w_in, w_pool, w_out, w_ple, w_gate)
    x2 = x.reshape(b * t, D_MODEL)
    p2 = p.reshape(b * t, D_PLE)
    ahead = lambda s: (jnp.minimum(s, nblk - 1), 0)
    behind = lambda s: (jnp.clip(s - 1, 0, nblk - 1), 0)
    stream = lambda s: (jnp.clip((POSITIONS * s - 2) // nt, 0, b - 1),)
    y, pool, hg = pl.pallas_call(
        functools.partial(_prompt_kernel, nt, b * nt),
        grid=(nblk + 1,),
        in_specs=[pl.BlockSpec((tt, D_MODEL), ahead),
                  pl.BlockSpec((tt, D_MODEL), behind),
                  pl.BlockSpec((tt, D_PLE), behind)]
                 + [_const_spec(w.shape) for w in weights],
        out_specs=[pl.BlockSpec((tt, D_MODEL), behind),
                   pl.BlockSpec((None, POOL_BUF, D_POOL), lambda s: stream(s) + (0, 0)),
                   pl.BlockSpec((None, N_HEADS, HEAD, HEAD), lambda s: stream(s) + (0, 0, 0))],
        out_shape=[jax.ShapeDtypeStruct((b * t, D_MODEL), jnp.float32),
                   jax.ShapeDtypeStruct((b, POOL_BUF, D_POOL), jnp.float32),
                   jax.ShapeDtypeStruct((b, N_HEADS, HEAD, HEAD), jnp.float32)],
        scratch_shapes=[pltpu.VMEM((HDR + SUB_TILE, D_POOL), jnp.float32),
                        pltpu.VMEM((SUB_TILE, D_IN - D_POOL), jnp.float32)] * 2
                       + [pltpu.VMEM((SUB_TILE, D_MODEL), jnp.bfloat16)] * POSITIONS
                       + [pltpu.VMEM((HDR, D_POOL), jnp.float32),
                          pltpu.VMEM((N_HEADS, HEAD, HEAD), jnp.float32)] * 2,
        compiler_params=pltpu.CompilerParams(dimension_semantics=("arbitrary",),
                                             vmem_limit_bytes=VMEM_LIMIT_BYTES),
        name="prompt_layer",
    )(x2, x2, p2, *weights)
    return y.reshape(b, t, D_MODEL), pool, hg


STREAMS_PER_ITER = 8
SAMPLE_VMEM_LIMIT_BYTES = 58 * 1024 * 1024
IN_ROW_BLOCKS = 2


def _row_block_copies(w_hbm, stage, sems, n=1):
    rows = w_hbm.shape[0] // n
    return [pltpu.make_async_copy(w_hbm.at[pl.ds(k * rows, rows)], stage.at[pl.ds(k * rows, rows)], sems.at[k])
            for k in range(n)]


def _write_back_copies(w_b, w_hbm, sems):
    return [pltpu.make_async_copy(w_b.at[j], w_hbm.at[j], sems.at[j]) for j in range(w_b.shape[0])]


def _stream_group_copy(src, dst, sems, g):
    streams = pl.ds(g * STREAMS_PER_ITER, STREAMS_PER_ITER)
    return pltpu.make_async_copy(src.at[streams], dst.at[streams], sems.at[g])


def _sample_kernel(start_pos, x_ref, p_ref, cache_ref, s0_hbm, lbl_ref, w_in_hbm, w_pool_ref, scale_ref,
                   hgn_ref, w_out_hbm, npre_ref, npost_ref, w_ple_hbm, w_gate_hbm,
                   y_ref, pool_ref, hg_hbm, w_in_o, w_out_o, w_ple_o, w_gate_o, params_out,
                   z_ref, ext_ref, mix_ref, s0_ref, hg_ref, w_in_f, w_out_f, w_ple_f, w_gate_f,
                   w_in_b, w_out_b, w_ple_b, w_gate_b, sem_s0, sem_hg, sem_in, sem_out, sem_ple, sem_gate,
                   sem_in_o, sem_out_o, sem_ple_o, sem_gate_o):
    n_streams, ts = pool_ref.shape[1], pool_ref.shape[0] + 1
    ext_rows = HDR + ts
    n_groups = n_streams // STREAMS_PER_ITER

    in_copies = _row_block_copies(w_in_hbm, w_in_f, sem_in, IN_ROW_BLOCKS)
    in_write_back = _write_back_copies(w_in_b, w_in_o, sem_in_o)
    late_weights = (
        (_row_block_copies(w_out_hbm, w_out_f, sem_out), w_out_f, w_out_b, _write_back_copies(w_out_b, w_out_o, sem_out_o)),
        (_row_block_copies(w_ple_hbm, w_ple_f, sem_ple), w_ple_f, w_ple_b, _write_back_copies(w_ple_b, w_ple_o, sem_ple_o)),
        (_row_block_copies(w_gate_hbm, w_gate_f, sem_gate), w_gate_f, w_gate_b,
         _write_back_copies(w_gate_b, w_gate_o, sem_gate_o)))
    for copy in (in_copies + [_stream_group_copy(s0_hbm, s0_ref, sem_s0, g) for g in range(n_groups)]
                 + [c for copies, _, _, _ in late_weights for c in copies]):
        copy.start()

    params_out[...] = jnp.zeros(params_out.shape, jnp.float32)
    packed = _unpack_params(params_out)
    for name, src in (("npre_ref", npre_ref), ("npost_ref", npost_ref), ("scale_ref", scale_ref),
                      ("hgn_ref", hgn_ref), ("lbl_ref", lbl_ref)):
        packed[name][...] = src[...]

    ext_ref[...] = jnp.zeros(ext_ref.shape, jnp.float32)
    for r in range(POOL_BUF):
        for gi in range(len(POOL_WINDOWS)):
            ext_ref[gi, pl.ds(HDR - POOL_BUF + r, n_streams, stride=ext_rows), :] = (
                cache_ref[r, :, gi * POOL_GROUP:(gi + 1) * POOL_GROUP])

    x = x_ref[...]
    h = _bf(x * _rms_scale(x) * npre_ref[...])
    rows_in = D_MODEL // IN_ROW_BLOCKS
    for k, copy in enumerate(in_copies):
        copy.wait()
        krows = slice(k * rows_in, (k + 1) * rows_in)
        for j in range(D_IN // COLS):
            cols = slice(j * COLS, (j + 1) * COLS)
            w_in_b[j, krows, :] = _bf(w_in_f[krows, cols])
            if k == IN_ROW_BLOCKS - 1:
                in_write_back[j].start()
            part = _dot(h[:, krows], w_in_b[j, krows, :])
            z_ref[:, cols] = part if k == 0 else z_ref[:, cols] + part

    lb = _lower_bound(lbl_ref[...])
    causal = (lax.broadcasted_iota(jnp.int32, (ts, ts), 1) <= lax.broadcasted_iota(jnp.int32, (ts, ts), 0))
    w_pool = [_bf(w_pool_ref[gi]) for gi in range(len(POOL_WINDOWS))]

    def prepare(b):
        c = {"b": b, "rows": pl.ds(pl.multiple_of(b * ts, ts), ts)}
        rows = c["rows"]
        base = pl.multiple_of(b * ext_rows, ext_rows)
        c["pooled"] = []
        for gi, w in enumerate(POOL_WINDOWS):
            ext_ref[gi, pl.ds(base + HDR, ts), :] = z_ref[rows, C_U + gi * POOL_GROUP:C_U + (gi + 1) * POOL_GROUP]
            c["pooled"].append(_bf(_pool_group(ext_ref[gi, pl.ds(base, ext_rows), :], w, start_pos)))
        gated = []
        for hd in range(N_HEADS):
            q = z_ref[rows, C_Q + hd * HEAD:C_Q + (hd + 1) * HEAD]
            fl = z_ref[rows, C_F + hd * HEAD:C_F + (hd + 1) * HEAD]
            gated.append(_gates(q, fl, lb[:, hd * HEAD:(hd + 1) * HEAD]))
        decay_rows = [F[ts - 1:ts, :] for _, _, F in gated]
        c["decay_cols"] = jnp.concatenate(
            decay_rows + [jnp.zeros((8 - N_HEADS, HEAD), jnp.float32)], axis=0).T
        c["qe"] = [_bf(qe) for qe, _, _ in gated]
        c["ke"] = [_bf(ke) for _, ke, _ in gated]
        c["kd"] = [_bf(ke * decay_rows[hd]) for hd, (_, ke, _) in enumerate(gated)]
        c["v"] = [_bf(z_ref[rows, C_V + hd * HEAD:C_V + (hd + 1) * HEAD]) for hd in range(N_HEADS)]
        return c

    def issue(c):
        c["mixed"] = [_dot(c["pooled"][gi], w_pool[gi]) for gi in range(len(POOL_WINDOWS))]
        c["att"] = [_dot_nt(c["qe"][hd], c["ke"][hd]) for hd in range(N_HEADS)]
        c["upd"] = [_dot_tn(c["kd"][hd], c["v"][hd]) for hd in range(N_HEADS)]

    def combine(c):
        b, rows = c["b"], c["rows"]
        for gi in range(len(POOL_WINDOWS)):
            cs = slice(gi * POOL_GROUP, (gi + 1) * POOL_GROUP)
            gp = z_ref[rows, C_GP + gi * POOL_GROUP:C_GP + (gi + 1) * POOL_GROUP]
            mix_ref[rows, cs] = _bf(c["mixed"][gi] * scale_ref[:, cs] * (gp * _sigmoid(gp)))
        c["o"] = []
        for hd in range(N_HEADS):
            s0 = s0_ref[b, hd]
            att = _bf(jnp.where(causal, c["att"][hd], 0.0))
            c["o"].append(_dot(att, c["v"][hd]) + _dot(c["qe"][hd], _bf(s0)))
            hg_ref[b, hd] = s0 * c["decay_cols"][:, hd:hd + 1] + c["upd"][hd]

    def store(c):
        rows = c["rows"]
        for hd in range(N_HEADS):
            gh = z_ref[rows, C_GH + hd * HEAD:C_GH + (hd + 1) * HEAD]
            mix_ref[rows, D_POOL + hd * HEAD:D_POOL + (hd + 1) * HEAD] = _bf(
                _head_out(c["o"][hd], gh, hgn_ref[...]))

    def stream_group(g, carry):
        ctx = [prepare(g * STREAMS_PER_ITER + i) for i in range(STREAMS_PER_ITER)]
        for c in ctx:
            issue(c)
        _stream_group_copy(s0_hbm, s0_ref, sem_s0, g).wait()
        for phase in (combine, store):
            for c in ctx:
                phase(c)
        _stream_group_copy(hg_ref, hg_hbm, sem_hg, g).start()
        return carry

    lax.fori_loop(0, n_groups, stream_group, 0)

    for r in range(POOL_BUF):
        for gi in range(len(POOL_WINDOWS)):
            pool_ref[r, :, gi * POOL_GROUP:(gi + 1) * POOL_GROUP] = (
                ext_ref[gi, pl.ds(ext_rows - POOL_BUF + r, n_streams, stride=ext_rows), :])

    for copies, stage, dst, write_back in late_weights:
        copies[0].wait()
        for j in range(dst.shape[0]):
            dst[j] = _bf(stage[:, j * COLS:(j + 1) * COLS])
            write_back[j].start()
    y_ref[...] = _finish(x, mix_ref[...], _bf(p_ref[...]), w_out_b, w_ple_b, w_gate_b, npost_ref[...])

    for copy in (in_write_back + [c for _, _, _, write_back in late_weights for c in write_back]
                 + [_stream_group_copy(hg_ref, hg_hbm, sem_hg, g) for g in range(n_groups)]):
        copy.wait()


def _sample_call(start_pos, x, p, cache, s0, lbl, w_in, w_pool, scale, hgn, w_out, npre, npost, w_ple, w_gate):
    b, ts, _ = x.shape
    assert ts == BLK and ts == POOL_BUF + 1 and b % STREAMS_PER_ITER == 0
    n = b * ts
    block_shape = lambda w: (w.shape[1] // COLS, w.shape[0], COLS)
    blocks = lambda w: jax.ShapeDtypeStruct(block_shape(w), jnp.bfloat16)
    dense = (w_in, w_out, w_ple, w_gate)
    vmem, hbm = pl.BlockSpec(memory_space=pltpu.VMEM), pl.BlockSpec(memory_space=pl.ANY)
    y, pool, hg, w_in_b, w_out_b, w_ple_b, w_gate_b, params = pl.pallas_call(
        functools.partial(_sample_kernel, start_pos),
        in_specs=[vmem, vmem, vmem, hbm, vmem, hbm, vmem, vmem, vmem, hbm, vmem, vmem, hbm, hbm],
        out_specs=[vmem, vmem, hbm, hbm, hbm, hbm, hbm, vmem],
        out_shape=[jax.ShapeDtypeStruct((n, D_MODEL), jnp.float32),
                   jax.ShapeDtypeStruct((POOL_BUF, b, D_POOL), jnp.float32),
                   jax.ShapeDtypeStruct(s0.shape, jnp.float32),
                   blocks(w_in), blocks(w_out), blocks(w_ple), blocks(w_gate),
                   jax.ShapeDtypeStruct((SUBLANES, D_MODEL), jnp.float32)],
        scratch_shapes=[pltpu.VMEM((n, D_IN), jnp.float32),
                        pltpu.VMEM((len(POOL_WINDOWS), b * (HDR + ts), POOL_GROUP), jnp.float32),
                        pltpu.VMEM((n, D_MODEL), jnp.bfloat16),
                        pltpu.VMEM(s0.shape, jnp.float32), pltpu.VMEM(s0.shape, jnp.float32)]
                       + [pltpu.VMEM(w.shape, jnp.float32) for w in dense]
                       + [pltpu.VMEM(block_shape(w), jnp.bfloat16) for w in dense]
                       + [pltpu.SemaphoreType.DMA((b // STREAMS_PER_ITER,))] * 2
                       + [pltpu.SemaphoreType.DMA((IN_ROW_BLOCKS,))] + [pltpu.SemaphoreType.DMA((1,))] * 3
                       + [pltpu.SemaphoreType.DMA((block_shape(w)[0],)) for w in dense],
        compiler_params=pltpu.CompilerParams(vmem_limit_bytes=SAMPLE_VMEM_LIMIT_BYTES),
        name="sample_layer",
    )(x.reshape(n, D_MODEL), p.reshape(n, D_PLE), jnp.transpose(cache, (1, 0, 2)), s0, lbl, w_in, w_pool, scale,
      hgn, w_out, npre, npost, w_ple, w_gate)
    return (y.reshape(b, ts, D_MODEL), jnp.transpose(pool, (1, 0, 2)), hg), (params, w_in_b, w_out_b, w_ple_b, w_gate_b)


def kernel(x_prompt, x_sample, cache_pool, state_hgrn, p_prompt, p_sample, lb_logits, w_in, w_pool, pool_scale,
           hg_norm, w_out, norm_pre, norm_post, w_ple, w_ple_gate):
    depth = w_in.shape[0]
    assert depth == 1 and lb_logits.shape[0] == 2
    past_len = 1024
    (y_s, pool_s, hg_s), (params, w_in_b, w_out_b, w_ple_b, w_gate_b) = _sample_call(
        past_len, x_sample, p_sample[0], cache_pool[0], state_hgrn[0], lb_logits, w_in[0], w_pool[0], pool_scale,
        hg_norm, w_out[0], norm_pre, norm_post, w_ple[0], w_ple_gate[0])
    y_p, pool_p, hg_p = _prompt_call(x_prompt, p_prompt[0], params, w_in_b, w_pool[0], w_out_b, w_ple_b, w_gate_b)
    return (y_p, y_s, pool_p[None], hg_p[None], pool_s[None], hg_s[None])
```

```python
import functools

import jax
import jax.numpy as jnp
from jax import lax
from jax.experimental import pallas as pl
from jax.experimental.pallas import tpu as pltpu

D_MODEL = 1024
D_POOL = 512
POOL_WINDOWS = (2, 4, 8, 16)
POOL_GROUP = 128
POOL_BUF = 15
N_HEADS = 4
HEAD = 128
D_HG = N_HEADS * HEAD
D_IN = 2 * D_POOL + 4 * D_HG
D_PLE = 256
EPS = 1e-6
LOG2_E = 1.4426950408889634
COLS = 512
SPAN = 32
BLK = 16
CHUNK = 128
SUBLANES = 8
HDR = 16
SUB_TILE = 256
POSITIONS = 2
VMEM_LIMIT_BYTES = 60 * 1024 * 1024

C_U, C_GP, C_Q, C_F, C_V, C_GH = 0, 512, 1024, 1536, 2048, 2560

_NT = (((1,), (1,)), ((), ()))
_TN = (((0,), (0,)), ((), ()))


def _dot(a, b):
    return jnp.dot(a, b, preferred_element_type=jnp.float32)


def _dot_nt(a, b):
    return lax.dot_general(a, b, _NT, preferred_element_type=jnp.float32)


def _dot_tn(a, b):
    return lax.dot_general(a, b, _TN, preferred_element_type=jnp.float32)


def _bf(x):
    return x.astype(jnp.bfloat16)


def _wblk(w_ref, j):
    return w_ref[j]


def _sigmoid(x):
    return 1.0 / (1.0 + jnp.exp2(x * (-LOG2_E)))


def _rms_scale(x):
    return lax.rsqrt(jnp.mean(x * x, axis=-1, keepdims=True) + EPS)


def _lower_bound(lb_logits):
    l0 = lb_logits[0:1, :]
    l1 = lb_logits[1:2, :]
    m = jnp.maximum(l0, l1)
    e0 = jnp.exp(l0 - m)
    e1 = jnp.exp(l1 - m)
    return e0 / (e0 + e1)


def _window_sum(e, w):
    s = e
    d = 1
    while d < w:
        s = s + pltpu.roll(s, d, axis=0)
        d *= 2
    return s


def _pool_group(e, w, first_pos):
    s = _window_sum(e, w)[HDR:]
    u = e[HDR:]
    rows = lax.broadcasted_iota(jnp.int32, (HDR, POOL_GROUP), 0)
    cnt = jnp.minimum(w, first_pos + rows + 1).astype(jnp.float32)
    head = s[:HDR] / cnt - u[:HDR]
    if s.shape[0] == HDR:
        return head
    tail = s[HDR:] * (1.0 / w) - u[HDR:]
    return jnp.concatenate([head, tail], axis=0)


def _block_cumprod(f, reverse=False):
    n, lanes = f.shape
    x = f.reshape(n // SUBLANES, SUBLANES, lanes)
    row = lax.broadcasted_iota(jnp.int32, x.shape, 1)
    for d in (1, 2, 4):
        if reverse:
            x = x * jnp.where(row + d < SUBLANES, pltpu.roll(x, SUBLANES - d, axis=1), 1.0)
        else:
            x = x * jnp.where(row >= d, pltpu.roll(x, d, axis=1), 1.0)
    x = x.reshape(n // BLK, BLK // SUBLANES, SUBLANES, lanes)
    lo, hi = x[:, 0], x[:, 1]
    if reverse:
        lo = lo * hi[:, 0:1, :]
    else:
        hi = hi * lo[:, SUBLANES - 1:SUBLANES, :]
    return jnp.concatenate([lo[:, None], hi[:, None]], axis=1).reshape(n, lanes)


def _gates(q, fl, lb):
    f = lb + (1.0 - lb) * _sigmoid(fl)
    k = 1.0 - f
    F = _block_cumprod(f)
    return q * F, k / F, F


class _HgrnChunk:
    def __init__(self, load, lb):
        self.load, self.lb = load, lb

    def prepare(self):
        q, fl, v = self.load()
        c = self.c = q.shape[0]
        nb = c // SPAN
        f = self.lb + (1.0 - self.lb) * _sigmoid(fl)
        k = 1.0 - f
        halves = lambda x: [jnp.concatenate([x[SPAN * j + BLK * w:SPAN * j + BLK * (w + 1)] for j in range(nb)], axis=0)
                            for w in (0, 1)]
        (f_a, f_b), (q_a, q_b), (k_a, k_b) = halves(f), halves(q), halves(k)
        fwd = _block_cumprod(f_b)
        rev_incl = _block_cumprod(f_a, reverse=True)
        row = lax.broadcasted_iota(jnp.int32, f_a.shape, 0) & (BLK - 1)
        rev = jnp.where(row < BLK - 1, pltpu.roll(rev_incl, f_a.shape[0] - 1, axis=0), 1.0)
        qm = (q_a / rev, q_b * fwd)
        km = (k_a * rev, k_b / fwd)
        span = lambda pair, j: jnp.concatenate([pair[0][BLK * j:BLK * (j + 1)], pair[1][BLK * j:BLK * (j + 1)]], axis=0)
        t_head = [rev_incl[BLK * j:BLK * j + 1, :] for j in range(nb)]
        t_tail = [fwd[BLK * j + BLK - 1:BLK * (j + 1), :] for j in range(nb)]
        T = [t_head[j] * t_tail[j] for j in range(nb)]
        qm_b = [span(qm, j) for j in range(nb)]
        km_b = [span(km, j) for j in range(nb)]
        qe_b = [qm_b[j] * t_head[j] for j in range(nb)]
        kd_b = [km_b[j] * t_tail[j] for j in range(nb)]
        zero = jnp.zeros((SPAN, HEAD), jnp.float32)
        k_pair = [kd_b[j] * t_head[j + 1] if j % 2 == 0 else zero for j in range(nb)]
        self.qe = _bf(jnp.concatenate(qm_b, axis=0))
        self.ke_kd = _bf(jnp.concatenate(km_b + k_pair, axis=0))

        assert nb == 4
        self.q_far = _bf(jnp.concatenate([zero, zero, qe_b[2], qe_b[3] * T[2]], axis=0))
        self.k_far = _bf(jnp.concatenate([kd_b[0] * T[1], kd_b[1], zero, zero], axis=0))

        e_in = [None] * nb
        dec = None
        for j in range(nb):
            e_in[j] = dec
            dec = T[j] if dec is None else dec * T[j]
        self.e_tot = dec
        d_out = [None] * nb
        dec = None
        for j in range(nb - 1, -1, -1):
            d_out[j] = dec
            dec = T[j] if dec is None else dec * T[j]
        self.q_in = _bf(jnp.concatenate(
            [qe_b[j] if e_in[j] is None else qe_b[j] * e_in[j] for j in range(nb)], axis=0))
        self.k_out = _bf(jnp.concatenate(
            [kd_b[j] if d_out[j] is None else kd_b[j] * d_out[j] for j in range(nb)], axis=0))
        self.v_bf = _bf(v)

    def issue(self):
        self.r1 = _dot_nt(self.qe, self.ke_kd)
        self.far = _dot_nt(self.q_far, self.k_far)
        self.upd = _dot_tn(self.v_bf, self.k_out)

    def combine(self, st, mask_d, mask_p):
        c = self.c
        p = jnp.where(mask_d, self.r1[:, :c], jnp.where(mask_p, self.r1[:, c:], 0.0)) + self.far
        self.o = _dot(_bf(p), self.v_bf) + _dot_nt(self.q_in, _bf(st))
        return st * self.e_tot + self.upd


def _head_out(o, gate, hg_norm):
    y = o * _rms_scale(o) * hg_norm
    return y * (gate * _sigmoid(gate))


def _post_norm(x, y_blocks, norm_post):
    ms = sum(jnp.sum(y * y, axis=-1, keepdims=True) for y in y_blocks) / D_MODEL
    r = lax.rsqrt(ms + EPS)
    return [x[:, j * COLS:(j + 1) * COLS] + y * r * norm_post[:, j * COLS:(j + 1) * COLS]
            for j, y in enumerate(y_blocks)]


def _finish(x, mix_bf, p_bf, w_out_ref, w_ple_ref, w_gate_ref, norm_post):
    nblk = D_MODEL // COLS
    x1 = _post_norm(x, [_dot(mix_bf, _wblk(w_out_ref, j)) for j in range(nblk)], norm_post)
    x1_bf = _bf(jnp.concatenate(x1, axis=1))
    out = [x1[j] + _sigmoid(_dot(x1_bf, _wblk(w_gate_ref, j))) * _dot(p_bf, _wblk(w_ple_ref, j)) for j in range(nblk)]
    return jnp.concatenate(out, axis=1)


def _chunk_masks(c):
    t = lax.broadcasted_iota(jnp.int32, (c, c), 0)
    s = lax.broadcasted_iota(jnp.int32, (c, c), 1)
    tb = t // SPAN
    sb = s // SPAN
    mask_d = (tb == sb) & (s <= t)
    mask_p = ((tb & 1) == 1) & (sb == tb - 1)
    return mask_d, mask_p


def _sub_step_head(stages, r0, xa_ref, p_ref, npre_ref, w_ple_ref):
    rows = slice(r0, r0 + SUB_TILE)
    ple, h = [], None
    if 3 in stages:
        p_bf = _bf(p_ref[rows, :])
        ple = [_dot(p_bf, _wblk(w_ple_ref, j)) for j in range(2)]
    if 1 in stages:
        xa = xa_ref[rows, :]
        h = _bf(xa * _rms_scale(xa) * npre_ref[...])
    return ple, h


def _sub_step(stages, first_pos, r0, xa_ref, xc_ref, p_ref, lbl_ref, w_in_ref, w_pool_ref, scale_ref, hgn_ref,
              w_out_ref, npre_ref, npost_ref, w_ple_ref, w_gate_ref, y_ref,
              u_w, z_w, mix_w, u_r, z_r, mix_r, carry, states, head=None, before_tail=None):
    tt = SUB_TILE
    rows = slice(r0, r0 + tt)
    states = list(states)

    def in_proj(h, c0):
        blk = _dot(h, _wblk(w_in_ref, c0 // COLS))
        if c0 == C_U:
            u_w[HDR:HDR + tt, :] = blk
        else:
            z_w[:, c0 - C_GP:c0 - C_GP + D_POOL] = blk

    if 2 not in stages:
        if 1 in stages:
            xa = xa_ref[rows, :]
            h = _bf(xa * _rms_scale(xa) * npre_ref[...])
            for c0 in (C_U, C_GP, C_Q, C_F, C_V, C_GH):
                in_proj(h, c0)
        if 3 in stages:
            y_ref[rows, :] = _finish(xc_ref[rows, :], mix_r[...], _bf(p_ref[rows, :]), w_out_ref, w_ple_ref,
                                     w_gate_ref, npost_ref[...])
        return carry, states

    lb = _lower_bound(lbl_ref[...])
    mask_d, mask_p = _chunk_masks(CHUNK)

    def pool_group(gi):
        cs = slice(gi * POOL_GROUP, (gi + 1) * POOL_GROUP)
        pooled = _pool_group(u_r[:, cs], POOL_WINDOWS[gi], first_pos)
        mixed = _dot(_bf(pooled), _bf(w_pool_ref[gi])) * scale_ref[:, cs]
        gp = z_r[:, cs]
        mix_w[:, cs] = _bf(mixed * (gp * _sigmoid(gp)))

    def hgrn_unit(hd, c0):
        hs = slice(hd * HEAD, (hd + 1) * HEAD)
        rs = slice(c0, c0 + CHUNK)
        col = lambda base: slice(base - C_GP + hd * HEAD, base - C_GP + (hd + 1) * HEAD)
        unit = _HgrnChunk(lambda: (z_r[rs, col(C_Q)], z_r[rs, col(C_F)], z_r[rs, col(C_V)]), lb[:, hs])
        unit.head, unit.rows, unit.gate_cols = hd, rs, col(C_GH)
        return unit

    def hgrn_store(unit):
        gh = z_r[unit.rows, unit.gate_cols]
        mix_w[unit.rows, D_POOL + unit.head * HEAD:D_POOL + (unit.head + 1) * HEAD] = _bf(
            _head_out(unit.o, gh, hgn_ref[...]))

    units = [hgrn_unit(hd, c0) for c0 in range(0, tt, CHUNK) for hd in range(N_HEADS)]
    dense = [("out", 1), ("in", C_U), ("in", C_GP), ("gate", 0), ("gate", 1), ("in", C_Q), ("in", C_F),
             ("in", C_V)]
    assert len(units) == len(dense)
    pool_at = {0: (0,), 2: (1,), 4: (2,), 7: (3,)}
    post_norm_at = 1
    final_at = {5: (0,), 6: (1,)}
    h = mix_prev = x1 = x1_bf = None
    y, gate, ple = {}, {}, []
    if head is None:
        head = _sub_step_head(stages, r0, xa_ref, p_ref, npre_ref, w_ple_ref)
    ple, h = head
    units[0].prepare()
    if 3 in stages:
        mix_prev = mix_r[...]
        y[0] = _dot(mix_prev, _wblk(w_out_ref, 0))
    u_r[0:HDR, :] = carry
    for k, unit in enumerate(units):
        unit.issue()
        if k + 1 < len(units):
            units[k + 1].prepare()
        kind, arg = dense[k]
        if kind == "in" and 1 in stages:
            in_proj(h, arg)
        elif kind == "out" and 3 in stages:
            y[arg] = _dot(mix_prev, _wblk(w_out_ref, arg))
        elif kind == "gate" and 3 in stages:
            gate[arg] = _dot(x1_bf, _wblk(w_gate_ref, arg))
        states[unit.head] = unit.combine(states[unit.head], mask_d, mask_p)
        if k > 0:
            hgrn_store(units[k - 1])
        if k == post_norm_at and 3 in stages:
            x1 = _post_norm(xc_ref[rows, :], [y[0], y[1]], npost_ref[...])
            x1_bf = _bf(jnp.concatenate(x1, axis=1))
        for gi in pool_at.get(k, ()):
            pool_group(gi)
        if 3 in stages:
            for j in final_at.get(k, ()):
                y_ref[rows, j * COLS:(j + 1) * COLS] = x1[j] + _sigmoid(gate[j]) * ple[j]
    if before_tail is not None:
        before_tail()
    if 1 in stages:
        in_proj(h, C_GH)
    hgrn_store(units[-1])
    return u_r[tt:tt + HDR, :], states


def _position_stages(n, n_tiles):
    stages = set()
    if n < n_tiles:
        stages.add(1)
    if 1 <= n <= n_tiles:
        stages.add(2)
    if POSITIONS <= n < n_tiles + POSITIONS:
        stages.add(3)
    return stages


def _unpack_params(params_ref):
    return dict(npre_ref=params_ref.at[0:1, :], npost_ref=params_ref.at[1:2, :],
                scale_ref=params_ref.at[2:3, 0:D_POOL], hgn_ref=params_ref.at[2:3, D_POOL:D_POOL + HEAD],
                lbl_ref=params_ref.at[3:5, 0:D_HG])


def _prompt_kernel(tiles_per_stream, n_tiles, xa_ref, xc_ref, p_ref, params_ref, w_in_ref, w_pool_ref, w_out_hbm,
                   w_ple_hbm, w_gate_hbm,
                   y_ref, pool_ref, hg_ref, *scratch):
    uz = (scratch[0:2], scratch[2:4])
    mix = scratch[4:4 + POSITIONS]
    carry_ref, st_ref, carry_end, st_end, w_out_ref, w_ple_ref, w_gate_ref, sem_w = scratch[4 + POSITIONS:]
    stage3_weights = [pltpu.make_async_copy(src, dst, sem_w.at[i]) for i, (src, dst) in enumerate(
        ((w_out_hbm, w_out_ref), (w_ple_hbm, w_ple_ref), (w_gate_hbm, w_gate_ref)))]
    s = pl.program_id(0)
    last = pl.num_programs(0) - 1
    small = _unpack_params(params_ref)
    lbl_ref, scale_ref, hgn_ref = small["lbl_ref"], small["scale_ref"], small["hgn_ref"]
    npre_ref, npost_ref = small["npre_ref"], small["npost_ref"]
    consts = (xa_ref, xc_ref, p_ref, lbl_ref, w_in_ref, w_pool_ref, scale_ref, hgn_ref, w_out_ref, npre_ref,
              npost_ref, w_ple_ref, w_gate_ref, y_ref)
    offsets = [lax.rem(POSITIONS * s + i - 1 + tiles_per_stream, tiles_per_stream) * SUB_TILE
               for i in range(POSITIONS)]
    new_stream = offsets[1] == 0

    def run(first_position):
        stage_sets = [_position_stages(first_position + i, n_tiles) for i in range(POSITIONS)]
        carry = carry_ref[...]
        states = [st_ref[hd] for hd in range(N_HEADS)]
        heads = {}
        for i, stages in enumerate(stage_sets):
            if not stages:
                continue
            r0 = i * SUB_TILE
            before_tail = None
            if i + 1 < POSITIONS and 2 in stages and 2 in stage_sets[i + 1]:
                def before_tail(i=i):
                    heads[i + 1] = _sub_step_head(stage_sets[i + 1], (i + 1) * SUB_TILE, xa_ref, p_ref, npre_ref,
                                                  w_ple_ref)
            if i == 1:
                carry = jnp.where(new_stream, 0.0, carry)
                states = [jnp.where(new_stream, 0.0, st) for st in states]
            u_w, z_w = uz[i % 2]
            u_r, z_r = uz[(i + 1) % 2]
            carry, states = _sub_step(stages, offsets[i], r0, *consts, u_w, z_w, mix[(i - 1) % POSITIONS],
                                      u_r, z_r, mix[i], carry, states, head=heads.get(i), before_tail=before_tail)
            if i == 0:
                carry_end[...] = carry
                for hd in range(N_HEADS):
                    st_end[hd] = states[hd]
        carry_ref[...] = carry
        for hd in range(N_HEADS):
            st_ref[hd] = states[hd]

    @pl.when(s == 0)
    def _():
        for copy in stage3_weights:
            copy.start()
        carry_ref[...] = jnp.zeros(carry_ref.shape, jnp.float32)
        st_ref[...] = jnp.zeros(st_ref.shape, jnp.float32)
        run(0)

    @pl.when(s == 1)
    def _():
        for copy in stage3_weights:
            copy.wait()

    @pl.when((s > 0) & (s < last))
    def _():
        run(POSITIONS)

    @pl.when(s == last)
    def _():
        run(n_tiles)

    @pl.when(new_stream & (s > 0))
    def _():
        pool_ref[...] = carry_end[HDR - POOL_BUF:HDR, :]
        for hd in range(N_HEADS):
            hg_ref[hd] = st_end[hd].T


def _const_spec(shape):
    nd = len(shape)
    return pl.BlockSpec(shape, lambda *_: (0,) * nd, pipeline_mode=pl.Buffered(1))


def _prompt_call(x, p, params, w_in, w_pool, w_out, w_ple, w_gate):
    b, t, _ = x.shape
    tt = POSITIONS * SUB_TILE
    assert t % tt == 0 and SUB_TILE % CHUNK == 0 and POSITIONS % 2 == 0
    nt = t // SUB_TILE
    nblk = b * t // tt
    weights = (params, w_in, w_pool)
    stage3_weights = (w_out, w_ple, w_gate)
    x2 = x.reshape(b * t, D_MODEL)
    p2 = p.reshape(b * t, D_PLE)
    ahead = lambda s: (jnp.minimum(s, nblk - 1), 0)
    behind = lambda s: (jnp.clip(s - 1, 0, nblk - 1), 0)
    stream = lambda s: (jnp.clip((POSITIONS * s - 2) // nt, 0, b - 1),)
    y, pool, hg = pl.pallas_call(
        functools.partial(_prompt_kernel, nt, b * nt),
        grid=(nblk + 1,),
        in_specs=[pl.BlockSpec((tt, D_MODEL), ahead),
                  pl.BlockSpec((tt, D_MODEL), behind),
                  pl.BlockSpec((tt, D_PLE), behind)]
                 + [_const_spec(w.shape) for w in weights]
                 + [pl.BlockSpec(memory_space=pl.ANY)] * len(stage3_weights),
        out_specs=[pl.BlockSpec((tt, D_MODEL), behind),
                   pl.BlockSpec((None, POOL_BUF, D_POOL), lambda s: stream(s) + (0, 0)),
                   pl.BlockSpec((None, N_HEADS, HEAD, HEAD), lambda s: stream(s) + (0, 0, 0))],
        out_shape=[jax.ShapeDtypeStruct((b * t, D_MODEL), jnp.float32),
                   jax.ShapeDtypeStruct((b, POOL_BUF, D_POOL), jnp.float32),
                   jax.ShapeDtypeStruct((b, N_HEADS, HEAD, HEAD), jnp.float32)],
        scratch_shapes=[pltpu.VMEM((HDR + SUB_TILE, D_POOL), jnp.float32),
                        pltpu.VMEM((SUB_TILE, D_IN - D_POOL), jnp.float32)] * 2
                       + [pltpu.VMEM((SUB_TILE, D_MODEL), jnp.bfloat16)] * POSITIONS
                       + [pltpu.VMEM((HDR, D_POOL), jnp.float32),
                          pltpu.VMEM((N_HEADS, HEAD, HEAD), jnp.float32)] * 2
                       + [pltpu.VMEM(w.shape, w.dtype) for w in stage3_weights]
                       + [pltpu.SemaphoreType.DMA((len(stage3_weights),))],
        compiler_params=pltpu.CompilerParams(dimension_semantics=("arbitrary",),
                                             vmem_limit_bytes=VMEM_LIMIT_BYTES),
        name="prompt_layer",
    )(x2, x2, p2, *weights, *stage3_weights)
    return y.reshape(b, t, D_MODEL), pool, hg


STREAMS_PER_ITER = 8
SAMPLE_VMEM_LIMIT_BYTES = 58 * 1024 * 1024
IN_ROW_BLOCKS = 2


def _row_block_copies(w_hbm, stage, sems, n=1):
    rows = w_hbm.shape[0] // n
    return [pltpu.make_async_copy(w_hbm.at[pl.ds(k * rows, rows)], stage.at[pl.ds(k * rows, rows)], sems.at[k])
            for k in range(n)]


def _write_back_copies(w_b, w_hbm, sems):
    return [pltpu.make_async_copy(w_b.at[j], w_hbm.at[j], sems.at[j]) for j in range(w_b.shape[0])]


def _stream_group_copy(src, dst, sems, g):
    streams = pl.ds(g * STREAMS_PER_ITER, STREAMS_PER_ITER)
    return pltpu.make_async_copy(src.at[streams], dst.at[streams], sems.at[g])


def _sample_kernel(start_pos, x_ref, p_ref, cache_ref, s0_hbm, lbl_ref, w_in_hbm, w_pool_ref, scale_ref,
                   hgn_ref, w_out_hbm, npre_ref, npost_ref, w_ple_hbm, w_gate_hbm,
                   y_ref, pool_ref, hg_hbm, w_in_o, w_out_o, w_ple_o, w_gate_o, params_out,
                   z_ref, ext_ref, mix_ref, s0_ref, hg_ref, w_in_f, w_out_f, w_ple_f, w_gate_f,
                   w_in_b, w_out_b, w_ple_b, w_gate_b, sem_s0, sem_hg, sem_in, sem_out, sem_ple, sem_gate,
                   sem_in_o, sem_out_o, sem_ple_o, sem_gate_o):
    n_streams, ts = pool_ref.shape[1], pool_ref.shape[0] + 1
    ext_rows = HDR + ts
    n_groups = n_streams // STREAMS_PER_ITER

    in_copies = _row_block_copies(w_in_hbm, w_in_f, sem_in, IN_ROW_BLOCKS)
    in_write_back = _write_back_copies(w_in_b, w_in_o, sem_in_o)
    late_weights = (
        (_row_block_copies(w_out_hbm, w_out_f, sem_out), w_out_f, w_out_b, _write_back_copies(w_out_b, w_out_o, sem_out_o)),
        (_row_block_copies(w_ple_hbm, w_ple_f, sem_ple), w_ple_f, w_ple_b, _write_back_copies(w_ple_b, w_ple_o, sem_ple_o)),
        (_row_block_copies(w_gate_hbm, w_gate_f, sem_gate), w_gate_f, w_gate_b,
         _write_back_copies(w_gate_b, w_gate_o, sem_gate_o)))
    for copy in (in_copies + [_stream_group_copy(s0_hbm, s0_ref, sem_s0, g) for g in range(n_groups)]
                 + [c for copies, _, _, _ in late_weights for c in copies]):
        copy.start()

    params_out[...] = jnp.zeros(params_out.shape, jnp.float32)
    packed = _unpack_params(params_out)
    for name, src in (("npre_ref", npre_ref), ("npost_ref", npost_ref), ("scale_ref", scale_ref),
                      ("hgn_ref", hgn_ref), ("lbl_ref", lbl_ref)):
        packed[name][...] = src[...]

    ext_ref[...] = jnp.zeros(ext_ref.shape, jnp.float32)
    for r in range(POOL_BUF):
        for gi in range(len(POOL_WINDOWS)):
            ext_ref[gi, pl.ds(HDR - POOL_BUF + r, n_streams, stride=ext_rows), :] = (
                cache_ref[r, :, gi * POOL_GROUP:(gi + 1) * POOL_GROUP])

    x = x_ref[...]
    h = _bf(x * _rms_scale(x) * npre_ref[...])
    rows_in = D_MODEL // IN_ROW_BLOCKS
    for k, copy in enumerate(in_copies):
        copy.wait()
        krows = slice(k * rows_in, (k + 1) * rows_in)
        for j in range(D_IN // COLS):
            cols = slice(j * COLS, (j + 1) * COLS)
            w_in_b[j, krows, :] = _bf(w_in_f[krows, cols])
            if k == IN_ROW_BLOCKS - 1:
                in_write_back[j].start()
            part = _dot(h[:, krows], w_in_b[j, krows, :])
            z_ref[:, cols] = part if k == 0 else z_ref[:, cols] + part

    lb = _lower_bound(lbl_ref[...])
    causal = (lax.broadcasted_iota(jnp.int32, (ts, ts), 1) <= lax.broadcasted_iota(jnp.int32, (ts, ts), 0))
    w_pool = [_bf(w_pool_ref[gi]) for gi in range(len(POOL_WINDOWS))]

    def prepare(b):
        c = {"b": b, "rows": pl.ds(pl.multiple_of(b * ts, ts), ts)}
        rows = c["rows"]
        base = pl.multiple_of(b * ext_rows, ext_rows)
        c["pooled"] = []
        for gi, w in enumerate(POOL_WINDOWS):
            ext_ref[gi, pl.ds(base + HDR, ts), :] = z_ref[rows, C_U + gi * POOL_GROUP:C_U + (gi + 1) * POOL_GROUP]
            c["pooled"].append(_bf(_pool_group(ext_ref[gi, pl.ds(base, ext_rows), :], w, start_pos)))
        gated = []
        for hd in range(N_HEADS):
            q = z_ref[rows, C_Q + hd * HEAD:C_Q + (hd + 1) * HEAD]
            fl = z_ref[rows, C_F + hd * HEAD:C_F + (hd + 1) * HEAD]
            gated.append(_gates(q, fl, lb[:, hd * HEAD:(hd + 1) * HEAD]))
        decay_rows = [F[ts - 1:ts, :] for _, _, F in gated]
        c["decay_cols"] = jnp.concatenate(
            decay_rows + [jnp.zeros((8 - N_HEADS, HEAD), jnp.float32)], axis=0).T
        c["qe"] = [_bf(qe) for qe, _, _ in gated]
        c["ke"] = [_bf(ke) for _, ke, _ in gated]
        c["kd"] = [_bf(ke * decay_rows[hd]) for hd, (_, ke, _) in enumerate(gated)]
        c["v"] = [_bf(z_ref[rows, C_V + hd * HEAD:C_V + (hd + 1) * HEAD]) for hd in range(N_HEADS)]
        return c

    def issue(c):
        c["mixed"] = [_dot(c["pooled"][gi], w_pool[gi]) for gi in range(len(POOL_WINDOWS))]
        c["att"] = [_dot_nt(c["qe"][hd], c["ke"][hd]) for hd in range(N_HEADS)]
        c["upd"] = [_dot_tn(c["kd"][hd], c["v"][hd]) for hd in range(N_HEADS)]

    def combine(c):
        b, rows = c["b"], c["rows"]
        for gi in range(len(POOL_WINDOWS)):
            cs = slice(gi * POOL_GROUP, (gi + 1) * POOL_GROUP)
            gp = z_ref[rows, C_GP + gi * POOL_GROUP:C_GP + (gi + 1) * POOL_GROUP]
            mix_ref[rows, cs] = _bf(c["mixed"][gi] * scale_ref[:, cs] * (gp * _sigmoid(gp)))
        c["o"] = []
        for hd in range(N_HEADS):
            s0 = s0_ref[b, hd]
            att = _bf(jnp.where(causal, c["att"][hd], 0.0))
            c["o"].append(_dot(att, c["v"][hd]) + _dot(c["qe"][hd], _bf(s0)))
            hg_ref[b, hd] = s0 * c["decay_cols"][:, hd:hd + 1] + c["upd"][hd]

    def store(c):
        rows = c["rows"]
        for hd in range(N_HEADS):
            gh = z_ref[rows, C_GH + hd * HEAD:C_GH + (hd + 1) * HEAD]
            mix_ref[rows, D_POOL + hd * HEAD:D_POOL + (hd + 1) * HEAD] = _bf(
                _head_out(c["o"][hd], gh, hgn_ref[...]))

    def stream_group(g, carry):
        ctx = [prepare(g * STREAMS_PER_ITER + i) for i in range(STREAMS_PER_ITER)]
        for c in ctx:
            issue(c)
        _stream_group_copy(s0_hbm, s0_ref, sem_s0, g).wait()
        for phase in (combine, store):
            for c in ctx:
                phase(c)
        _stream_group_copy(hg_ref, hg_hbm, sem_hg, g).start()
        return carry

    lax.fori_loop(0, n_groups, stream_group, 0)

    for r in range(POOL_BUF):
        for gi in range(len(POOL_WINDOWS)):
            pool_ref[r, :, gi * POOL_GROUP:(gi + 1) * POOL_GROUP] = (
                ext_ref[gi, pl.ds(ext_rows - POOL_BUF + r, n_streams, stride=ext_rows), :])

    for copies, stage, dst, write_back in late_weights:
        copies[0].wait()
        for j in range(dst.shape[0]):
            dst[j] = _bf(stage[:, j * COLS:(j + 1) * COLS])
            write_back[j].start()
    y_ref[...] = _finish(x, mix_ref[...], _bf(p_ref[...]), w_out_b, w_ple_b, w_gate_b, npost_ref[...])

    for copy in (in_write_back + [c for _, _, _, write_back in late_weights for c in write_back]
                 + [_stream_group_copy(hg_ref, hg_hbm, sem_hg, g) for g in range(n_groups)]):
        copy.wait()


def _sample_call(start_pos, x, p, cache, s0, lbl, w_in, w_pool, scale, hgn, w_out, npre, npost, w_ple, w_gate):
    b, ts, _ = x.shape
    assert ts == BLK and ts == POOL_BUF + 1 and b % STREAMS_PER_ITER == 0
    n = b * ts
    block_shape = lambda w: (w.shape[1] // COLS, w.shape[0], COLS)
    blocks = lambda w: jax.ShapeDtypeStruct(block_shape(w), jnp.bfloat16)
    dense = (w_in, w_out, w_ple, w_gate)
    vmem, hbm = pl.BlockSpec(memory_space=pltpu.VMEM), pl.BlockSpec(memory_space=pl.ANY)
    y, pool, hg, w_in_b, w_out_b, w_ple_b, w_gate_b, params = pl.pallas_call(
        functools.partial(_sample_kernel, start_pos),
        in_specs=[vmem, vmem, vmem, hbm, vmem, hbm, vmem, vmem, vmem, hbm, vmem, vmem, hbm, hbm],
        out_specs=[vmem, vmem, hbm, hbm, hbm, hbm, hbm, vmem],
        out_shape=[jax.ShapeDtypeStruct((n, D_MODEL), jnp.float32),
                   jax.ShapeDtypeStruct((POOL_BUF, b, D_POOL), jnp.float32),
                   jax.ShapeDtypeStruct(s0.shape, jnp.float32),
                   blocks(w_in), blocks(w_out), blocks(w_ple), blocks(w_gate),
                   jax.ShapeDtypeStruct((SUBLANES, D_MODEL), jnp.float32)],
        scratch_shapes=[pltpu.VMEM((n, D_IN), jnp.float32),
                        pltpu.VMEM((len(POOL_WINDOWS), b * (HDR + ts), POOL_GROUP), jnp.float32),
                        pltpu.VMEM((n, D_MODEL), jnp.bfloat16),
                        pltpu.VMEM(s0.shape, jnp.float32), pltpu.VMEM(s0.shape, jnp.float32)]
                       + [pltpu.VMEM(w.shape, jnp.float32) for w in dense]
                       + [pltpu.VMEM(block_shape(w), jnp.bfloat16) for w in dense]
                       + [pltpu.SemaphoreType.DMA((b // STREAMS_PER_ITER,))] * 2
                       + [pltpu.SemaphoreType.DMA((IN_ROW_BLOCKS,))] + [pltpu.SemaphoreType.DMA((1,))] * 3
                       + [pltpu.SemaphoreType.DMA((block_shape(w)[0],)) for w in dense],
        compiler_params=pltpu.CompilerParams(vmem_limit_bytes=SAMPLE_VMEM_LIMIT_BYTES),
        name="sample_layer",
    )(x.reshape(n, D_MODEL), p.reshape(n, D_PLE), jnp.transpose(cache, (1, 0, 2)), s0, lbl, w_in, w_pool, scale,
      hgn, w_out, npre, npost, w_ple, w_gate)
    return (y.reshape(b, ts, D_MODEL), jnp.transpose(pool, (1, 0, 2)), hg), (params, w_in_b, w_out_b, w_ple_b, w_gate_b)


def kernel(x_prompt, x_sample, cache_pool, state_hgrn, p_prompt, p_sample, lb_logits, w_in, w_pool, pool_scale,
           hg_norm, w_out, norm_pre, norm_post, w_ple, w_ple_gate):
    depth = w_in.shape[0]
    assert depth == 1 and lb_logits.shape[0] == 2
    past_len = 1024
    (y_s, pool_s, hg_s), (params, w_in_b, w_out_b, w_ple_b, w_gate_b) = _sample_call(
        past_len, x_sample, p_sample[0], cache_pool[0], state_hgrn[0], lb_logits, w_in[0], w_pool[0], pool_scale,
        hg_norm, w_out[0], norm_pre, norm_post, w_ple[0], w_ple_gate[0])
    y_p, pool_p, hg_p = _prompt_call(x_prompt, p_prompt[0], params, w_in_b, w_pool[0], w_out_b, w_ple_b, w_gate_b)
    return (y_p, y_s, pool_p[None], hg_p[None], pool_s[None], hg_s[None])
```

```python
import functools

import jax
import jax.numpy as jnp
from jax import lax
from jax.experimental import pallas as pl
from jax.experimental.pallas import tpu as pltpu

D_MODEL = 1024
D_POOL = 512
POOL_WINDOWS = (2, 4, 8, 16)
POOL_GROUP = 128
POOL_BUF = 15
N_HEADS = 4
HEAD = 128
D_HG = N_HEADS * HEAD
D_IN = 2 * D_POOL + 4 * D_HG
D_PLE = 256
EPS = 1e-6
LOG2_E = 1.4426950408889634
COLS = 512
SPAN = 32
BLK = 16
CHUNK = 128
SUBLANES = 8
HDR = 16
SUB_TILE = 256
POSITIONS = 2
VMEM_LIMIT_BYTES = 60 * 1024 * 1024

C_U, C_GP, C_Q, C_F, C_V, C_GH = 0, 512, 1024, 1536, 2048, 2560

_NT = (((1,), (1,)), ((), ()))
_TN = (((0,), (0,)), ((), ()))


def _dot(a, b):
    return jnp.dot(a, b, preferred_element_type=jnp.float32)


def _dot_nt(a, b):
    return lax.dot_general(a, b, _NT, preferred_element_type=jnp.float32)


def _dot_tn(a, b):
    return lax.dot_general(a, b, _TN, preferred_element_type=jnp.float32)


def _bf(x):
    return x.astype(jnp.bfloat16)


def _wblk(w_ref, j):
    return w_ref[j]


def _sigmoid(x):
    return 1.0 / (1.0 + jnp.exp2(x * (-LOG2_E)))


def _rms_scale(x):
    return lax.rsqrt(jnp.mean(x * x, axis=-1, keepdims=True) + EPS)


def _lower_bound(lb_logits):
    l0 = lb_logits[0:1, :]
    l1 = lb_logits[1:2, :]
    m = jnp.maximum(l0, l1)
    e0 = jnp.exp(l0 - m)
    e1 = jnp.exp(l1 - m)
    return e0 / (e0 + e1)


def _window_sum(e, w):
    s = e
    d = 1
    while d < w:
        s = s + pltpu.roll(s, d, axis=0)
        d *= 2
    return s


def _pool_group(e, w, first_pos):
    s = _window_sum(e, w)[HDR:]
    u = e[HDR:]
    rows = lax.broadcasted_iota(jnp.int32, (HDR, POOL_GROUP), 0)
    cnt = jnp.minimum(w, first_pos + rows + 1).astype(jnp.float32)
    head = s[:HDR] / cnt - u[:HDR]
    if s.shape[0] == HDR:
        return head
    tail = s[HDR:] * (1.0 / w) - u[HDR:]
    return jnp.concatenate([head, tail], axis=0)


def _block_cumprod(f, reverse=False):
    n, lanes = f.shape
    x = f.reshape(n // SUBLANES, SUBLANES, lanes)
    row = lax.broadcasted_iota(jnp.int32, x.shape, 1)
    for d in (1, 2, 4):
        if reverse:
            x = x * jnp.where(row + d < SUBLANES, pltpu.roll(x, SUBLANES - d, axis=1), 1.0)
        else:
            x = x * jnp.where(row >= d, pltpu.roll(x, d, axis=1), 1.0)
    x = x.reshape(n // BLK, BLK // SUBLANES, SUBLANES, lanes)
    lo, hi = x[:, 0], x[:, 1]
    if reverse:
        lo = lo * hi[:, 0:1, :]
    else:
        hi = hi * lo[:, SUBLANES - 1:SUBLANES, :]
    return jnp.concatenate([lo[:, None], hi[:, None]], axis=1).reshape(n, lanes)


def _gates(q, fl, lb):
    f = lb + (1.0 - lb) * _sigmoid(fl)
    k = 1.0 - f
    F = _block_cumprod(f)
    return q * F, k / F, F


class _HgrnChunk:
    def __init__(self, load, lb):
        self.load, self.lb = load, lb

    def prepare(self):
        q, fl, v = self.load()
        c = self.c = q.shape[0]
        nb = c // SPAN
        f = self.lb + (1.0 - self.lb) * _sigmoid(fl)
        k = 1.0 - f
        halves = lambda x: [jnp.concatenate([x[SPAN * j + BLK * w:SPAN * j + BLK * (w + 1)] for j in range(nb)], axis=0)
                            for w in (0, 1)]
        (f_a, f_b), (q_a, q_b), (k_a, k_b) = halves(f), halves(q), halves(k)
        fwd = _block_cumprod(f_b)
        rev_incl = _block_cumprod(f_a, reverse=True)
        row = lax.broadcasted_iota(jnp.int32, f_a.shape, 0) & (BLK - 1)
        rev = jnp.where(row < BLK - 1, pltpu.roll(rev_incl, f_a.shape[0] - 1, axis=0), 1.0)
        qm = (q_a / rev, q_b * fwd)
        km = (k_a * rev, k_b / fwd)
        span = lambda pair, j: jnp.concatenate([pair[0][BLK * j:BLK * (j + 1)], pair[1][BLK * j:BLK * (j + 1)]], axis=0)
        t_head = [rev_incl[BLK * j:BLK * j + 1, :] for j in range(nb)]
        t_tail = [fwd[BLK * j + BLK - 1:BLK * (j + 1), :] for j in range(nb)]
        T = [t_head[j] * t_tail[j] for j in range(nb)]
        qm_b = [span(qm, j) for j in range(nb)]
        km_b = [span(km, j) for j in range(nb)]
        qe_b = [qm_b[j] * t_head[j] for j in range(nb)]
        kd_b = [km_b[j] * t_tail[j] for j in range(nb)]
        zero = jnp.zeros((SPAN, HEAD), jnp.float32)
        k_pair = [kd_b[j] * t_head[j + 1] if j % 2 == 0 else zero for j in range(nb)]
        self.qe = _bf(jnp.concatenate(qm_b, axis=0))
        self.ke_kd = _bf(jnp.concatenate(km_b + k_pair, axis=0))

        assert nb == 4
        self.q_far = _bf(jnp.concatenate([zero, zero, qe_b[2], qe_b[3] * T[2]], axis=0))
        self.k_far = _bf(jnp.concatenate([kd_b[0] * T[1], kd_b[1], zero, zero], axis=0))

        e_in = [None] * nb
        dec = None
        for j in range(nb):
            e_in[j] = dec
            dec = T[j] if dec is None else dec * T[j]
        self.e_tot = dec
        d_out = [None] * nb
        dec = None
        for j in range(nb - 1, -1, -1):
            d_out[j] = dec
            dec = T[j] if dec is None else dec * T[j]
        self.q_in = _bf(jnp.concatenate(
            [qe_b[j] if e_in[j] is None else qe_b[j] * e_in[j] for j in range(nb)], axis=0))
        self.k_out = _bf(jnp.concatenate(
            [kd_b[j] if d_out[j] is None else kd_b[j] * d_out[j] for j in range(nb)], axis=0))
        self.v_bf = _bf(v)

    def issue(self):
        self.r1 = _dot_nt(self.qe, self.ke_kd)
        self.far = _dot_nt(self.q_far, self.k_far)
        self.upd = _dot_tn(self.v_bf, self.k_out)

    def combine(self, st, mask_d, mask_p):
        c = self.c
        p = jnp.where(mask_d, self.r1[:, :c], jnp.where(mask_p, self.r1[:, c:], 0.0)) + self.far
        self.o = _dot(_bf(p), self.v_bf) + _dot_nt(self.q_in, _bf(st))
        return st * self.e_tot + self.upd


def _head_out(o, gate, hg_norm):
    y = o * _rms_scale(o) * hg_norm
    return y * (gate * _sigmoid(gate))


def _post_norm(x, y_blocks, norm_post):
    ms = sum(jnp.sum(y * y, axis=-1, keepdims=True) for y in y_blocks) / D_MODEL
    r = lax.rsqrt(ms + EPS)
    return [x[:, j * COLS:(j + 1) * COLS] + y * r * norm_post[:, j * COLS:(j + 1) * COLS]
            for j, y in enumerate(y_blocks)]


def _finish(x, mix_bf, p_bf, w_out_ref, w_ple_ref, w_gate_ref, norm_post):
    nblk = D_MODEL // COLS
    x1 = _post_norm(x, [_dot(mix_bf, _wblk(w_out_ref, j)) for j in range(nblk)], norm_post)
    x1_bf = _bf(jnp.concatenate(x1, axis=1))
    out = [x1[j] + _sigmoid(_dot(x1_bf, _wblk(w_gate_ref, j))) * _dot(p_bf, _wblk(w_ple_ref, j)) for j in range(nblk)]
    return jnp.concatenate(out, axis=1)


def _chunk_masks(c):
    t = lax.broadcasted_iota(jnp.int32, (c, c), 0)
    s = lax.broadcasted_iota(jnp.int32, (c, c), 1)
    tb = t // SPAN
    sb = s // SPAN
    mask_d = (tb == sb) & (s <= t)
    mask_p = ((tb & 1) == 1) & (sb == tb - 1)
    return mask_d, mask_p


def _sub_step_head(stages, r0, xa_ref, p_ref, npre_ref, w_ple_ref):
    rows = slice(r0, r0 + SUB_TILE)
    ple, h = [], None
    if 3 in stages:
        p_bf = _bf(p_ref[rows, :])
        ple = [_dot(p_bf, _wblk(w_ple_ref, j)) for j in range(2)]
    if 1 in stages:
        xa = xa_ref[rows, :]
        h = _bf(xa * _rms_scale(xa) * npre_ref[...])
    return ple, h


def _sub_step(stages, first_pos, r0, xa_ref, xc_ref, p_ref, lbl_ref, w_in_ref, w_pool_ref, scale_ref, hgn_ref,
              w_out_ref, npre_ref, npost_ref, w_ple_ref, w_gate_ref, y_ref,
              u_w, z_w, mix_w, u_r, z_r, mix_r, carry, states, head=None, before_tail=None):
    tt = SUB_TILE
    rows = slice(r0, r0 + tt)
    states = list(states)

    def in_proj(h, c0):
        blk = _dot(h, _wblk(w_in_ref, c0 // COLS))
        if c0 == C_U:
            u_w[HDR:HDR + tt, :] = blk
        else:
            z_w[:, c0 - C_GP:c0 - C_GP + D_POOL] = blk

    if 2 not in stages:
        if 1 in stages:
            xa = xa_ref[rows, :]
            h = _bf(xa * _rms_scale(xa) * npre_ref[...])
            for c0 in (C_U, C_GP, C_Q, C_F, C_V, C_GH):
                in_proj(h, c0)
        if 3 in stages:
            y_ref[rows, :] = _finish(xc_ref[rows, :], mix_r[...], _bf(p_ref[rows, :]), w_out_ref, w_ple_ref,
                                     w_gate_ref, npost_ref[...])
        return carry, states

    lb = _lower_bound(lbl_ref[...])
    mask_d, mask_p = _chunk_masks(CHUNK)

    def pool_group(gi):
        cs = slice(gi * POOL_GROUP, (gi + 1) * POOL_GROUP)
        pooled = _pool_group(u_r[:, cs], POOL_WINDOWS[gi], first_pos)
        mixed = _dot(_bf(pooled), _bf(w_pool_ref[gi])) * scale_ref[:, cs]
        gp = z_r[:, cs]
        mix_w[:, cs] = _bf(mixed * (gp * _sigmoid(gp)))

    def hgrn_unit(hd, c0):
        hs = slice(hd * HEAD, (hd + 1) * HEAD)
        rs = slice(c0, c0 + CHUNK)
        col = lambda base: slice(base - C_GP + hd * HEAD, base - C_GP + (hd + 1) * HEAD)
        unit = _HgrnChunk(lambda: (z_r[rs, col(C_Q)], z_r[rs, col(C_F)], z_r[rs, col(C_V)]), lb[:, hs])
        unit.head, unit.rows, unit.gate_cols = hd, rs, col(C_GH)
        return unit

    def hgrn_store(unit):
        gh = z_r[unit.rows, unit.gate_cols]
        mix_w[unit.rows, D_POOL + unit.head * HEAD:D_POOL + (unit.head + 1) * HEAD] = _bf(
            _head_out(unit.o, gh, hgn_ref[...]))

    units = [hgrn_unit(hd, c0) for c0 in range(0, tt, CHUNK) for hd in range(N_HEADS)]
    dense = [("out", 1), ("in", C_U), ("in", C_GP), ("gate", 0), ("gate", 1), ("in", C_Q), ("in", C_F),
             ("in", C_V)]
    assert len(units) == len(dense)
    pool_at = {0: (0,), 2: (1,), 4: (2,), 7: (3,)}
    post_norm_at = 1
    final_at = {5: (0,), 6: (1,)}
    h = mix_prev = x1 = x1_bf = None
    y, gate, ple = {}, {}, []
    if head is None:
        head = _sub_step_head(stages, r0, xa_ref, p_ref, npre_ref, w_ple_ref)
    ple, h = head
    units[0].prepare()
    if 3 in stages:
        mix_prev = mix_r[...]
        y[0] = _dot(mix_prev, _wblk(w_out_ref, 0))
    u_r[0:HDR, :] = carry
    for k, unit in enumerate(units):
        unit.issue()
        if k + 1 < len(units):
            units[k + 1].prepare()
        kind, arg = dense[k]
        if kind == "in" and 1 in stages:
            in_proj(h, arg)
        elif kind == "out" and 3 in stages:
            y[arg] = _dot(mix_prev, _wblk(w_out_ref, arg))
        elif kind == "gate" and 3 in stages:
            gate[arg] = _dot(x1_bf, _wblk(w_gate_ref, arg))
        states[unit.head] = unit.combine(states[unit.head], mask_d, mask_p)
        if k > 0:
            hgrn_store(units[k - 1])
        if k == post_norm_at and 3 in stages:
            x1 = _post_norm(xc_ref[rows, :], [y[0], y[1]], npost_ref[...])
            x1_bf = _bf(jnp.concatenate(x1, axis=1))
        for gi in pool_at.get(k, ()):
            pool_group(gi)
        if 3 in stages:
            for j in final_at.get(k, ()):
                y_ref[rows, j * COLS:(j + 1) * COLS] = x1[j] + _sigmoid(gate[j]) * ple[j]
    if before_tail is not None:
        before_tail()
    if 1 in stages:
        in_proj(h, C_GH)
    hgrn_store(units[-1])
    return u_r[tt:tt + HDR, :], states


def _position_stages(n, n_tiles):
    stages = set()
    if n < n_tiles:
        stages.add(1)
    if 1 <= n <= n_tiles:
        stages.add(2)
    if POSITIONS <= n < n_tiles + POSITIONS:
        stages.add(3)
    return stages


def _unpack_params(params_ref):
    return dict(npre_ref=params_ref.at[0:1, :], npost_ref=params_ref.at[1:2, :],
                scale_ref=params_ref.at[2:3, 0:D_POOL], hgn_ref=params_ref.at[2:3, D_POOL:D_POOL + HEAD],
                lbl_ref=params_ref.at[3:5, 0:D_HG])


def _prompt_kernel(tiles_per_stream, n_tiles, xa_ref, xc_ref, p_ref, params_ref, w_in_ref, w_pool_ref, w_out_hbm,
                   w_ple_hbm, w_gate_hbm,
                   y_ref, pool_ref, hg_ref, *scratch):
    uz = (scratch[0:2], scratch[2:4])
    mix = scratch[4:4 + POSITIONS]
    carry_ref, st_ref, carry_end, st_end, w_out_ref, w_ple_ref, w_gate_ref, sem_w = scratch[4 + POSITIONS:]
    stage3_weights = [pltpu.make_async_copy(src, dst, sem_w.at[i]) for i, (src, dst) in enumerate(
        ((w_out_hbm, w_out_ref), (w_ple_hbm, w_ple_ref), (w_gate_hbm, w_gate_ref)))]
    s = pl.program_id(0)
    last = pl.num_programs(0) - 1
    small = _unpack_params(params_ref)
    lbl_ref, scale_ref, hgn_ref = small["lbl_ref"], small["scale_ref"], small["hgn_ref"]
    npre_ref, npost_ref = small["npre_ref"], small["npost_ref"]
    consts = (xa_ref, xc_ref, p_ref, lbl_ref, w_in_ref, w_pool_ref, scale_ref, hgn_ref, w_out_ref, npre_ref,
              npost_ref, w_ple_ref, w_gate_ref, y_ref)
    offsets = [lax.rem(POSITIONS * s + i - 1 + tiles_per_stream, tiles_per_stream) * SUB_TILE
               for i in range(POSITIONS)]
    new_stream = offsets[1] == 0

    def run(first_position):
        stage_sets = [_position_stages(first_position + i, n_tiles) for i in range(POSITIONS)]
        carry = carry_ref[...]
        states = [st_ref[hd] for hd in range(N_HEADS)]
        heads = {}
        for i, stages in enumerate(stage_sets):
            if not stages:
                continue
            r0 = i * SUB_TILE
            before_tail = None
            if i + 1 < POSITIONS and 2 in stages and 2 in stage_sets[i + 1]:
                def before_tail(i=i):
                    heads[i + 1] = _sub_step_head(stage_sets[i + 1], (i + 1) * SUB_TILE, xa_ref, p_ref, npre_ref,
                                                  w_ple_ref)
            if i == 1:
                carry = jnp.where(new_stream, 0.0, carry)
                states = [jnp.where(new_stream, 0.0, st) for st in states]
            u_w, z_w = uz[i % 2]
            u_r, z_r = uz[(i + 1) % 2]
            carry, states = _sub_step(stages, offsets[i], r0, *consts, u_w, z_w, mix[(i - 1) % POSITIONS],
                                      u_r, z_r, mix[i], carry, states, head=heads.get(i), before_tail=before_tail)
            if i == 0:
                carry_end[...] = carry
                for hd in range(N_HEADS):
                    st_end[hd] = states[hd]
        carry_ref[...] = carry
        for hd in range(N_HEADS):
            st_ref[hd] = states[hd]

    @pl.when(s == 0)
    def _():
        for copy in stage3_weights:
            copy.start()
        carry_ref[...] = jnp.zeros(carry_ref.shape, jnp.float32)
        st_ref[...] = jnp.zeros(st_ref.shape, jnp.float32)
        run(0)

    @pl.when(s == 1)
    def _():
        for copy in stage3_weights:
            copy.wait()

    @pl.when((s > 0) & (s < last))
    def _():
        run(POSITIONS)

    @pl.when(s == last)
    def _():
        run(n_tiles)

    @pl.when(new_stream & (s > 0))
    def _():
        pool_ref[...] = carry_end[HDR - POOL_BUF:HDR, :]
        for hd in range(N_HEADS):
            hg_ref[hd] = st_end[hd].T


def _const_spec(shape):
    nd = len(shape)
    return pl.BlockSpec(shape, lambda *_: (0,) * nd, pipeline_mode=pl.Buffered(1))


def _prompt_call(x, p, params, w_in, w_pool, w_out, w_ple, w_gate):
    b, t, _ = x.shape
    tt = POSITIONS * SUB_TILE
    assert t % tt == 0 and SUB_TILE % CHUNK == 0 and POSITIONS % 2 == 0
    nt = t // SUB_TILE
    nblk = b * t // tt
    weights = (params, w_in, w_pool)
    stage3_weights = (w_out, w_ple, w_gate)
    x2 = x.reshape(b * t, D_MODEL)
    p2 = p.reshape(b * t, D_PLE)
    ahead = lambda s: (jnp.minimum(s, nblk - 1), 0)
    behind = lambda s: (jnp.clip(s - 1, 0, nblk - 1), 0)
    stream = lambda s: (jnp.clip((POSITIONS * s - 2) // nt, 0, b - 1),)
    y, pool, hg = pl.pallas_call(
        functools.partial(_prompt_kernel, nt, b * nt),
        grid=(nblk + 1,),
        in_specs=[pl.BlockSpec((tt, D_MODEL), ahead),
                  pl.BlockSpec((tt, D_MODEL), behind),
                  pl.BlockSpec((tt, D_PLE), behind)]
                 + [_const_spec(w.shape) for w in weights]
                 + [pl.BlockSpec(memory_space=pl.ANY)] * len(stage3_weights),
        out_specs=[pl.BlockSpec((tt, D_MODEL), behind),
                   pl.BlockSpec((None, POOL_BUF, D_POOL), lambda s: stream(s) + (0, 0)),
                   pl.BlockSpec((None, N_HEADS, HEAD, HEAD), lambda s: stream(s) + (0, 0, 0))],
        out_shape=[jax.ShapeDtypeStruct((b * t, D_MODEL), jnp.float32),
                   jax.ShapeDtypeStruct((b, POOL_BUF, D_POOL), jnp.float32),
                   jax.ShapeDtypeStruct((b, N_HEADS, HEAD, HEAD), jnp.float32)],
        scratch_shapes=[pltpu.VMEM((HDR + SUB_TILE, D_POOL), jnp.float32),
                        pltpu.VMEM((SUB_TILE, D_IN - D_POOL), jnp.float32)] * 2
                       + [pltpu.VMEM((SUB_TILE, D_MODEL), jnp.bfloat16)] * POSITIONS
                       + [pltpu.VMEM((HDR, D_POOL), jnp.float32),
                          pltpu.VMEM((N_HEADS, HEAD, HEAD), jnp.float32)] * 2
                       + [pltpu.VMEM(w.shape, w.dtype) for w in stage3_weights]
                       + [pltpu.SemaphoreType.DMA((len(stage3_weights),))],
        compiler_params=pltpu.CompilerParams(dimension_semantics=("arbitrary",),
                                             vmem_limit_bytes=VMEM_LIMIT_BYTES),
        name="prompt_layer",
    )(x2, x2, p2, *weights, *stage3_weights)
    return y.reshape(b, t, D_MODEL), pool, hg


STREAMS_PER_ITER = 8
SAMPLE_VMEM_LIMIT_BYTES = 58 * 1024 * 1024
IN_ROW_BLOCKS = 4


def _row_block_copies(w_hbm, stage, sems, n=1):
    rows = w_hbm.shape[0] // n
    return [pltpu.make_async_copy(w_hbm.at[pl.ds(k * rows, rows)], stage.at[pl.ds(k * rows, rows)], sems.at[k])
            for k in range(n)]


def _write_back_copies(w_b, w_hbm, sems):
    return [pltpu.make_async_copy(w_b.at[j], w_hbm.at[j], sems.at[j]) for j in range(w_b.shape[0])]


def _stream_group_copy(src, dst, sems, g):
    streams = pl.ds(g * STREAMS_PER_ITER, STREAMS_PER_ITER)
    return pltpu.make_async_copy(src.at[streams], dst.at[streams], sems.at[g])


def _sample_kernel(start_pos, x_ref, p_ref, cache_ref, s0_hbm, lbl_ref, w_in_hbm, w_pool_ref, scale_ref,
                   hgn_ref, w_out_hbm, npre_ref, npost_ref, w_ple_hbm, w_gate_hbm,
                   y_ref, pool_ref, hg_hbm, w_in_o, w_out_o, w_ple_o, w_gate_o, params_out,
                   z_ref, ext_ref, mix_ref, s0_ref, hg_ref, w_in_f, w_out_f, w_ple_f, w_gate_f,
                   w_in_b, w_out_b, w_ple_b, w_gate_b, sem_s0, sem_hg, sem_in, sem_out, sem_ple, sem_gate,
                   sem_in_o, sem_out_o, sem_ple_o, sem_gate_o):
    n_streams, ts = pool_ref.shape[1], pool_ref.shape[0] + 1
    ext_rows = HDR + ts
    n_groups = n_streams // STREAMS_PER_ITER

    in_copies = _row_block_copies(w_in_hbm, w_in_f, sem_in, IN_ROW_BLOCKS)
    in_write_back = _write_back_copies(w_in_b, w_in_o, sem_in_o)
    late_weights = (
        (_row_block_copies(w_out_hbm, w_out_f, sem_out), w_out_f, w_out_b, _write_back_copies(w_out_b, w_out_o, sem_out_o)),
        (_row_block_copies(w_ple_hbm, w_ple_f, sem_ple), w_ple_f, w_ple_b, _write_back_copies(w_ple_b, w_ple_o, sem_ple_o)),
        (_row_block_copies(w_gate_hbm, w_gate_f, sem_gate), w_gate_f, w_gate_b,
         _write_back_copies(w_gate_b, w_gate_o, sem_gate_o)))
    for copy in (in_copies + [_stream_group_copy(s0_hbm, s0_ref, sem_s0, g) for g in range(n_groups)]
                 + [c for copies, _, _, _ in late_weights for c in copies]):
        copy.start()

    params_out[...] = jnp.zeros(params_out.shape, jnp.float32)
    packed = _unpack_params(params_out)
    for name, src in (("npre_ref", npre_ref), ("npost_ref", npost_ref), ("scale_ref", scale_ref),
                      ("hgn_ref", hgn_ref), ("lbl_ref", lbl_ref)):
        packed[name][...] = src[...]

    ext_ref[...] = jnp.zeros(ext_ref.shape, jnp.float32)
    for r in range(POOL_BUF):
        for gi in range(len(POOL_WINDOWS)):
            ext_ref[gi, pl.ds(HDR - POOL_BUF + r, n_streams, stride=ext_rows), :] = (
                cache_ref[r, :, gi * POOL_GROUP:(gi + 1) * POOL_GROUP])

    x = x_ref[...]
    h = _bf(x * _rms_scale(x) * npre_ref[...])
    rows_in = D_MODEL // IN_ROW_BLOCKS
    for k, copy in enumerate(in_copies):
        copy.wait()
        krows = slice(k * rows_in, (k + 1) * rows_in)
        for j in range(D_IN // COLS):
            cols = slice(j * COLS, (j + 1) * COLS)
            w_in_b[j, krows, :] = _bf(w_in_f[krows, cols])
            if k == IN_ROW_BLOCKS - 1:
                in_write_back[j].start()
            part = _dot(h[:, krows], w_in_b[j, krows, :])
            z_ref[:, cols] = part if k == 0 else z_ref[:, cols] + part

    lb = _lower_bound(lbl_ref[...])
    causal = (lax.broadcasted_iota(jnp.int32, (ts, ts), 1) <= lax.broadcasted_iota(jnp.int32, (ts, ts), 0))
    w_pool = [_bf(w_pool_ref[gi]) for gi in range(len(POOL_WINDOWS))]

    def prepare(b):
        c = {"b": b, "rows": pl.ds(pl.multiple_of(b * ts, ts), ts)}
        rows = c["rows"]
        base = pl.multiple_of(b * ext_rows, ext_rows)
        c["pooled"] = []
        for gi, w in enumerate(POOL_WINDOWS):
            ext_ref[gi, pl.ds(base + HDR, ts), :] = z_ref[rows, C_U + gi * POOL_GROUP:C_U + (gi + 1) * POOL_GROUP]
            c["pooled"].append(_bf(_pool_group(ext_ref[gi, pl.ds(base, ext_rows), :], w, start_pos)))
        gated = []
        for hd in range(N_HEADS):
            q = z_ref[rows, C_Q + hd * HEAD:C_Q + (hd + 1) * HEAD]
            fl = z_ref[rows, C_F + hd * HEAD:C_F + (hd + 1) * HEAD]
            gated.append(_gates(q, fl, lb[:, hd * HEAD:(hd + 1) * HEAD]))
        decay_rows = [F[ts - 1:ts, :] for _, _, F in gated]
        c["decay_cols"] = jnp.concatenate(
            decay_rows + [jnp.zeros((8 - N_HEADS, HEAD), jnp.float32)], axis=0).T
        c["qe"] = [_bf(qe) for qe, _, _ in gated]
        c["ke"] = [_bf(ke) for _, ke, _ in gated]
        c["kd"] = [_bf(ke * decay_rows[hd]) for hd, (_, ke, _) in enumerate(gated)]
        c["v"] = [_bf(z_ref[rows, C_V + hd * HEAD:C_V + (hd + 1) * HEAD]) for hd in range(N_HEADS)]
        return c

    def issue(c):
        c["mixed"] = [_dot(c["pooled"][gi], w_pool[gi]) for gi in range(len(POOL_WINDOWS))]
        c["att"] = [_dot_nt(c["qe"][hd], c["ke"][hd]) for hd in range(N_HEADS)]
        c["upd"] = [_dot_tn(c["kd"][hd], c["v"][hd]) for hd in range(N_HEADS)]

    def combine(c):
        b, rows = c["b"], c["rows"]
        for gi in range(len(POOL_WINDOWS)):
            cs = slice(gi * POOL_GROUP, (gi + 1) * POOL_GROUP)
            gp = z_ref[rows, C_GP + gi * POOL_GROUP:C_GP + (gi + 1) * POOL_GROUP]
            mix_ref[rows, cs] = _bf(c["mixed"][gi] * scale_ref[:, cs] * (gp * _sigmoid(gp)))
        c["o"] = []
        for hd in range(N_HEADS):
            s0 = s0_ref[b, hd]
            att = _bf(jnp.where(causal, c["att"][hd], 0.0))
            c["o"].append(_dot(att, c["v"][hd]) + _dot(c["qe"][hd], _bf(s0)))
            hg_ref[b, hd] = s0 * c["decay_cols"][:, hd:hd + 1] + c["upd"][hd]

    def store(c):
        rows = c["rows"]
        for hd in range(N_HEADS):
            gh = z_ref[rows, C_GH + hd * HEAD:C_GH + (hd + 1) * HEAD]
            mix_ref[rows, D_POOL + hd * HEAD:D_POOL + (hd + 1) * HEAD] = _bf(
                _head_out(c["o"][hd], gh, hgn_ref[...]))

    def stream_group(g, carry):
        ctx = [prepare(g * STREAMS_PER_ITER + i) for i in range(STREAMS_PER_ITER)]
        for c in ctx:
            issue(c)
        _stream_group_copy(s0_hbm, s0_ref, sem_s0, g).wait()
        for phase in (combine, store):
            for c in ctx:
                phase(c)
        _stream_group_copy(hg_ref, hg_hbm, sem_hg, g).start()
        return carry

    lax.fori_loop(0, n_groups, stream_group, 0)

    for r in range(POOL_BUF):
        for gi in range(len(POOL_WINDOWS)):
            pool_ref[r, :, gi * POOL_GROUP:(gi + 1) * POOL_GROUP] = (
                ext_ref[gi, pl.ds(ext_rows - POOL_BUF + r, n_streams, stride=ext_rows), :])

    for copies, stage, dst, write_back in late_weights:
        copies[0].wait()
        for j in range(dst.shape[0]):
            dst[j] = _bf(stage[:, j * COLS:(j + 1) * COLS])
            write_back[j].start()
    y_ref[...] = _finish(x, mix_ref[...], _bf(p_ref[...]), w_out_b, w_ple_b, w_gate_b, npost_ref[...])

    for copy in (in_write_back + [c for _, _, _, write_back in late_weights for c in write_back]
                 + [_stream_group_copy(hg_ref, hg_hbm, sem_hg, g) for g in range(n_groups)]):
        copy.wait()


def _sample_call(start_pos, x, p, cache, s0, lbl, w_in, w_pool, scale, hgn, w_out, npre, npost, w_ple, w_gate):
    b, ts, _ = x.shape
    assert ts == BLK and ts == POOL_BUF + 1 and b % STREAMS_PER_ITER == 0
    n = b * ts
    block_shape = lambda w: (w.shape[1] // COLS, w.shape[0], COLS)
    blocks = lambda w: jax.ShapeDtypeStruct(block_shape(w), jnp.bfloat16)
    dense = (w_in, w_out, w_ple, w_gate)
    vmem, hbm = pl.BlockSpec(memory_space=pltpu.VMEM), pl.BlockSpec(memory_space=pl.ANY)
    y, pool, hg, w_in_b, w_out_b, w_ple_b, w_gate_b, params = pl.pallas_call(
        functools.partial(_sample_kernel, start_pos),
        in_specs=[vmem, vmem, vmem, hbm, vmem, hbm, vmem, vmem, vmem, hbm, vmem, vmem, hbm, hbm],
        out_specs=[vmem, vmem, hbm, hbm, hbm, hbm, hbm, vmem],
        out_shape=[jax.ShapeDtypeStruct((n, D_MODEL), jnp.float32),
                   jax.ShapeDtypeStruct((POOL_BUF, b, D_POOL), jnp.float32),
                   jax.ShapeDtypeStruct(s0.shape, jnp.float32),
                   blocks(w_in), blocks(w_out), blocks(w_ple), blocks(w_gate),
                   jax.ShapeDtypeStruct((SUBLANES, D_MODEL), jnp.float32)],
        scratch_shapes=[pltpu.VMEM((n, D_IN), jnp.float32),
                        pltpu.VMEM((len(POOL_WINDOWS), b * (HDR + ts), POOL_GROUP), jnp.float32),
                        pltpu.VMEM((n, D_MODEL), jnp.bfloat16),
                        pltpu.VMEM(s0.shape, jnp.float32), pltpu.VMEM(s0.shape, jnp.float32)]
                       + [pltpu.VMEM(w.shape, jnp.float32) for w in dense]
                       + [pltpu.VMEM(block_shape(w), jnp.bfloat16) for w in dense]
                       + [pltpu.SemaphoreType.DMA((b // STREAMS_PER_ITER,))] * 2
                       + [pltpu.SemaphoreType.DMA((IN_ROW_BLOCKS,))] + [pltpu.SemaphoreType.DMA((1,))] * 3
                       + [pltpu.SemaphoreType.DMA((block_shape(w)[0],)) for w in dense],
        compiler_params=pltpu.CompilerParams(vmem_limit_bytes=SAMPLE_VMEM_LIMIT_BYTES),
        name="sample_layer",
    )(x.reshape(n, D_MODEL), p.reshape(n, D_PLE), jnp.transpose(cache, (1, 0, 2)), s0, lbl, w_in, w_pool, scale,
      hgn, w_out, npre, npost, w_ple, w_gate)
    return (y.reshape(b, ts, D_MODEL), jnp.transpose(pool, (1, 0, 2)), hg), (params, w_in_b, w_out_b, w_ple_b, w_gate_b)


def kernel(x_prompt, x_sample, cache_pool, state_hgrn, p_prompt, p_sample, lb_logits, w_in, w_pool, pool_scale,
           hg_norm, w_out, norm_pre, norm_post, w_ple, w_ple_gate):
    depth = w_in.shape[0]
    assert depth == 1 and lb_logits.shape[0] == 2
    past_len = 1024
    (y_s, pool_s, hg_s), (params, w_in_b, w_out_b, w_ple_b, w_gate_b) = _sample_call(
        past_len, x_sample, p_sample[0], cache_pool[0], state_hgrn[0], lb_logits, w_in[0], w_pool[0], pool_scale,
        hg_norm, w_out[0], norm_pre, norm_post, w_ple[0], w_ple_gate[0])
    y_p, pool_p, hg_p = _prompt_call(x_prompt, p_prompt[0], params, w_in_b, w_pool[0], w_out_b, w_ple_b, w_gate_b)
    return (y_p, y_s, pool_p[None], hg_p[None], pool_s[None], hg_s[None])
```
